```python
import jax, jax.numpy as jnp
from jax import lax
import numpy as np

D_MODEL = 1024
BATCH = 4
SEQ = 4096
DEPTH = 1

D_FF = 2816
ROPE_THETA = 10000.0
NORM_EPS = 1e-6
Q_BLOCK = 128
H_A = 8
Q_LORA = 384
KV_LORA = 256
D_NOPE = 64
D_ROPE_A = 32
D_V_A = 64
H_IDX = 8
D_IDX = 32
TOPK_MAX = 256
DIL_PAIRS = ((128, 1), (512, 4), (2048, 16))
N_GROUPS_B = 3
H_B = 4
D_HEAD_B = 64
N_MOD = 9
COLS = (Q_LORA, KV_LORA, D_ROPE_A, D_IDX, H_IDX, 3 * N_GROUPS_B * H_B * D_HEAD_B, 2 * D_MODEL)
D_IN = sum(COLS)
WIDTH_A = H_A * D_V_A
WIDTH_B = H_B * D_HEAD_B

kernel_name = "hybrid_dsa_dilated_macaron_block"


def rms_norm(x, g):
    xf = x.astype(jnp.float32)
    y = xf * lax.rsqrt(jnp.mean(xf * xf, axis=-1, keepdims=True) + NORM_EPS)
    return (y * g.astype(jnp.float32)).astype(x.dtype)


def rope(x, pos):
    d = x.shape[-1]
    half = d // 2
    inv = ROPE_THETA ** (-jnp.arange(half, dtype=jnp.float32) / half)
    ang = pos.astype(jnp.float32)[..., None] * inv
    ang = ang.reshape(ang.shape[:2] + (1,) * (x.ndim - 3) + (half,))
    cos, sin = jnp.cos(ang), jnp.sin(ang)
    xf = x.astype(jnp.float32)
    x1, x2 = xf[..., :half], xf[..., half:]
    return jnp.concatenate([x1 * cos - x2 * sin, x2 * cos + x1 * sin], axis=-1).astype(x.dtype)


def swiglu(h, w_gate, w_up, w_down):
    return (jax.nn.silu(h @ w_gate) * (h @ w_up)) @ w_down


def dsa_branch(c_q, c_kv, k_rope, k_idx, w_idx, pos, w_uq, w_uk, w_uv, w_iq):
    B, S, _ = c_q.shape
    topk = min(TOPK_MAX, S // 4)
    q = jnp.einsum('bsr,rhe->bshe', c_q, w_uq)
    q_nope, q_rope = q[..., :D_NOPE], rope(q[..., D_NOPE:], pos)
    k_rope = rope(k_rope[:, :, None, :], pos)[:, :, 0]
    q_lat = jnp.einsum('bshn,rhn->bshr', q_nope, w_uk)
    q_idx = rope(jnp.einsum('bsr,rhe->bshe', c_q, w_iq), pos)
    k_idx = rope(k_idx[:, :, None, :], pos)[:, :, 0]
    w_idx = w_idx * (H_IDX ** -0.5)
    attn_scale = (D_NOPE + D_ROPE_A) ** -0.5
    idx_scale = D_IDX ** -0.5
    key_pos = jnp.arange(S)
    gather = jax.vmap(lambda a, i: a[i])

    def block(i):
        t0 = i * Q_BLOCK
        tq = t0 + jnp.arange(Q_BLOCK)
        qi = lax.dynamic_slice_in_dim(q_idx, t0, Q_BLOCK, axis=1)
        wi = lax.dynamic_slice_in_dim(w_idx, t0, Q_BLOCK, axis=1)
        logits = jnp.einsum('bqhe,bse->bqhs', qi, k_idx).astype(jnp.float32) * idx_scale
        iscore = jnp.einsum('bqh,bqhs->bqs', wi.astype(jnp.float32), jax.nn.relu(logits))
        causal = key_pos[None, :] <= tq[:, None]
        iscore = jnp.where(causal[None], iscore, -jnp.inf)
        _, sel = lax.top_k(iscore, topk)
        valid = sel <= tq[None, :, None]
        ckv_sel = gather(c_kv, sel)
        kr_sel = gather(k_rope, sel)
        ql = lax.dynamic_slice_in_dim(q_lat, t0, Q_BLOCK, axis=1)
        qr = lax.dynamic_slice_in_dim(q_rope, t0, Q_BLOCK, axis=1)
        s = (jnp.einsum('bqhr,bqkr->bqhk', ql, ckv_sel)
             + jnp.einsum('bqhe,bqke->bqhk', qr, kr_sel)).astype(jnp.float32) * attn_scale
        s = jnp.where(valid[:, :, None, :], s, -jnp.inf)
        p = jax.nn.softmax(s, axis=-1)
        return jnp.einsum('bqhk,bqkr->bqhr', p.astype(c_kv.dtype), ckv_sel)

    o_lat = lax.map(block, jnp.arange(S // Q_BLOCK))
    o_lat = jnp.moveaxis(o_lat, 0, 1).reshape(B, S, H_A, KV_LORA)
    o = jnp.einsum('bshr,rhv->bshv', o_lat, w_uv)
    return o.reshape(B, S, WIDTH_A)


def dilated_group(q, k, v, window, dilation):
    B, S, H, dh = q.shape
    offs = jnp.arange(window // dilation + 1) * dilation
    scale = dh ** -0.5

    def block(i):
        t0 = i * Q_BLOCK
        tq = t0 + jnp.arange(Q_BLOCK)
        idx = tq[:, None] - offs[None, :]
        valid = idx >= 0
        idxc = jnp.maximum(idx, 0)
        kg = k[:, idxc]
        vg = v[:, idxc]
        qb = lax.dynamic_slice_in_dim(q, t0, Q_BLOCK, axis=1)
        s = jnp.einsum('bqhe,bqnhe->bqhn', qb, kg).astype(jnp.float32) * scale
        s = jnp.where(valid[None, :, None, :], s, -jnp.inf)
        lse = jax.nn.logsumexp(s, axis=-1)
        p = jnp.exp(s - lse[..., None])
        o = jnp.einsum('bqhn,bqnhe->bqhe', p.astype(v.dtype), vg)
        return o, lse

    o, lse = lax.map(block, jnp.arange(S // Q_BLOCK))
    o = jnp.moveaxis(o, 0, 1).reshape(B, S, H, dh)
    lse = jnp.moveaxis(lse, 0, 1).reshape(B, S, H)
    return o, lse


def dilated_branch(qkv, pos):
    B, S = qkv.shape[:2]
    q = rope(qkv[:, :, 0], pos)
    k = rope(qkv[:, :, 1], pos)
    v = qkv[:, :, 2]
    outs, lses = [], []
    for g, (window, dilation) in enumerate(DIL_PAIRS):
        o, lse = dilated_group(q[:, :, g], k[:, :, g], v[:, :, g], window, dilation)
        outs.append(o)
        lses.append(lse)
    wts = jax.nn.softmax(jnp.stack(lses, axis=0), axis=0)
    o = jnp.sum(wts[..., None].astype(v.dtype) * jnp.stack(outs, axis=0), axis=0)
    return o.reshape(B, S, WIDTH_B)


def hybrid_mixer(u, pos, w_in, g_cq, g_ckv, w_uq, w_uk, w_uv, w_iq, w_up_a, w_up_b, w_o):
    B, S, _ = u.shape
    proj = u @ w_in
    splits, acc = [], 0
    for n in COLS[:-1]:
        acc += n
        splits.append(acc)
    c_q, c_kv, k_rope, k_idx, w_idx, qkv_b, gates = jnp.split(proj, splits, axis=-1)
    c_q = rms_norm(c_q, g_cq)
    c_kv = rms_norm(c_kv, g_ckv)
    o_a = dsa_branch(c_q, c_kv, k_rope, k_idx, w_idx, pos, w_uq, w_uk, w_uv, w_iq)
    o_b = dilated_branch(qkv_b.reshape(B, S, 3, N_GROUPS_B, H_B, D_HEAD_B), pos)
    g_a, g_b = jnp.split(gates, 2, axis=-1)
    z = jax.nn.sigmoid(g_a) * (o_a @ w_up_a) + jax.nn.sigmoid(g_b) * (o_b @ w_up_b)
    return z @ w_o


def setup_inputs(seed: int = 0) -> dict:
    key = jax.random.key(seed)
    ks = jax.random.split(key, 32)
    L, D = DEPTH, D_MODEL

    def nrm(k, shape, fan_in):
        return jax.random.normal(k, shape, jnp.float32) * (fan_in ** -0.5)

    def gain(k, shape):
        return 1.0 + 0.05 * jax.random.normal(k, shape, jnp.float32)

    x = jax.random.normal(ks[0], (BATCH, SEQ, D), jnp.float32)
    c = jax.random.normal(ks[1], (BATCH, D), jnp.float32)
    positions = (jax.random.randint(ks[2], (BATCH, 1), 0, 512, dtype=jnp.int32)
                 + jnp.arange(SEQ, dtype=jnp.int32)[None, :])
    return {
        "x": x,
        "c": c,
        "positions": positions,
        "w_mod": 0.5 * nrm(ks[3], (L, D, N_MOD * D), D),
        "b_mod": 0.01 * jax.random.normal(ks[4], (L, N_MOD * D), jnp.float32),
        "g_pre_ffn1": gain(ks[5], (L, D)),
        "w_gate1": nrm(ks[6], (L, D, D_FF), D),
        "w_up1": nrm(ks[7], (L, D, D_FF), D),
        "w_down1": nrm(ks[8], (L, D_FF, D), D_FF),
        "g_post_ffn1": gain(ks[9], (L, D)),
        "g_pre_mix": gain(ks[10], (L, D)),
        "w_in": nrm(ks[11], (L, D, D_IN), D),
        "g_cq": gain(ks[12], (L, Q_LORA)),
        "g_ckv": gain(ks[13], (L, KV_LORA)),
        "w_uq": nrm(ks[14], (L, Q_LORA, H_A, D_NOPE + D_ROPE_A), Q_LORA),
        "w_uk": nrm(ks[15], (L, KV_LORA, H_A, D_NOPE), KV_LORA),
        "w_uv": nrm(ks[16], (L, KV_LORA, H_A, D_V_A), KV_LORA),
        "w_iq": nrm(ks[17], (L, Q_LORA, H_IDX, D_IDX), Q_LORA),
        "w_up_a": nrm(ks[18], (L, WIDTH_A, D), WIDTH_A),
        "w_up_b": nrm(ks[19], (L, WIDTH_B, D), WIDTH_B),
        "w_o": nrm(ks[20], (L, D, D), D),
        "g_post_mix": gain(ks[21], (L, D)),
        "g_pre_ffn2": gain(ks[22], (L, D)),
        "w_gate2": nrm(ks[23], (L, D, D_FF), D),
        "w_up2": nrm(ks[24], (L, D, D_FF), D),
        "w_down2": nrm(ks[25], (L, D_FF, D), D_FF),
        "g_post_ffn2": gain(ks[26], (L, D)),
    }


def reference(x, c, positions, w_mod, b_mod, g_pre_ffn1, w_gate1, w_up1, w_down1, g_post_ffn1,
              g_pre_mix, w_in, g_cq, g_ckv, w_uq, w_uk, w_uv, w_iq, w_up_a, w_up_b, w_o, g_post_mix,
              g_pre_ffn2, w_gate2, w_up2, w_down2, g_post_ffn2):
    B = x.shape[0]
    for l in range(DEPTH):
        mod = (jax.nn.silu(c) @ w_mod[l] + b_mod[l]).reshape(B, N_MOD, D_MODEL)
        sh1, sc1, gt1, sh2, sc2, gt2, sh3, sc3, gt3 = [mod[:, j, None, :] for j in range(N_MOD)]
        h = rms_norm(x, g_pre_ffn1[l]) * (1 + sc1) + sh1
        x = x + 0.5 * gt1 * rms_norm(swiglu(h, w_gate1[l], w_up1[l], w_down1[l]), g_post_ffn1[l])
        u = rms_norm(x, g_pre_mix[l]) * (1 + sc2) + sh2
        y = hybrid_mixer(u, positions, w_in[l], g_cq[l], g_ckv[l], w_uq[l], w_uk[l], w_uv[l], w_iq[l],
                         w_up_a[l], w_up_b[l], w_o[l])
        x = x + gt2 * rms_norm(y, g_post_mix[l])
        h = rms_norm(x, g_pre_ffn2[l]) * (1 + sc3) + sh3
        x = x + 0.5 * gt3 * rms_norm(swiglu(h, w_gate2[l], w_up2[l], w_down2[l]), g_post_ffn2[l])
    return x
```

```python
import functools

import numpy as np
import jax
import jax.numpy as jnp
from jax import lax
from jax.experimental import pallas as pl
from jax.experimental.pallas import tpu as pltpu

F32 = jnp.float32
BF16 = jnp.bfloat16

D_FF = 2816
ROPE_THETA = 10000.0
NORM_EPS = 1e-6
H_A = 8
Q_LORA = 384
KV_LORA = 256
D_NOPE = 64
D_ROPE_A = 32
D_V_A = 64
H_IDX = 8
D_IDX = 32
TOPK_MAX = 256
DIL_PAIRS = ((128, 1), (512, 4), (2048, 16))
N_GROUPS_B = 3
H_B = 4
D_HEAD_B = 64
N_MOD = 9
WIDTH_A = H_A * D_V_A
WIDTH_B = H_B * D_HEAD_B
N_QKV_B = 3 * N_GROUPS_B * WIDTH_B

LANES = 128
HEAD_PAD = 128
VMEM_LIMIT = 56 * 1024 * 1024

TM_FFN = 512
CH_FFN = 256
TP_PROJ = 256
TQ_DSA = 256
TB_DIL = 128
TM_MERGE = 256
BISECT_ITERS = 30
NEG_BIG = -1e30

C_CKV = 0
C_KR = 256
C_KR_SW = 288
C_KI = 320
C_KI_SW = 352
C_QKVB = 384
C_GATES = C_QKVB + N_QKV_B
N_STD = C_GATES + 2048
R_CQ = 0
R_CKV = Q_LORA
R_WI = Q_LORA + KV_LORA
N_TR = 768


def _params(n_axes):
    return pltpu.CompilerParams(dimension_semantics=("arbitrary",) * n_axes,
                                vmem_limit_bytes=VMEM_LIMIT)


def _rms_rows(x):
    return x * lax.rsqrt(jnp.mean(x * x, axis=-1, keepdims=True) + NORM_EPS)


def _mod_kernel(c_ref, w_ref, b_ref, o_ref):
    c = c_ref[...]
    o_ref[...] = jnp.dot(c * jax.nn.sigmoid(c), w_ref[...], preferred_element_type=F32) + b_ref[...]


def _mod_call(c8, w_mod, b_mod):
    d = c8.shape[1]
    return pl.pallas_call(
        _mod_kernel,
        out_shape=jax.ShapeDtypeStruct((8, N_MOD * d), F32),
        grid=(N_MOD,),
        in_specs=[pl.BlockSpec((8, d), lambda j: (0, 0)),
                  pl.BlockSpec((d, d), lambda j: (0, j)),
                  pl.BlockSpec((1, d), lambda j: (0, j))],
        out_specs=pl.BlockSpec((8, d), lambda j: (0, j)),
        compiler_params=_params(1),
        name="mod",
    )(c8, w_mod, b_mod)


def _rope_kernel(pos_ref, inv_ref, tt_ref, ca_ref, sa_ref, cb_ref, sb_ref):
    s = pos_ref.shape[1]
    ang = inv_ref[...] * pos_ref[...].astype(F32)
    cos = jnp.cos(ang)
    sin = jnp.sin(ang)
    cos_b, sin_b = cos[32:48], sin[32:48]
    tt_ref[0] = cos_b
    tt_ref[1] = sin_b
    cos_a = jnp.concatenate([cos[0:32]] * 4, axis=0)
    sin_a = jnp.concatenate([sin[0:32]] * 4, axis=0)
    cos_bt = jnp.concatenate([cos_b] * 8, axis=0)
    sin_bt = jnp.concatenate([-sin_b, sin_b] * 4, axis=0)
    for j in range(s // LANES):
        cols = slice(j * LANES, (j + 1) * LANES)
        ca_ref[cols, :] = cos_a[:, cols].T
        sa_ref[cols, :] = sin_a[:, cols].T
        cb_ref[cols, :] = cos_bt[:, cols].T
        sb_ref[cols, :] = sin_bt[:, cols].T


def _rope_call(pos3, inv):
    b, _, s = pos3.shape
    tab = jax.ShapeDtypeStruct((b, s, LANES), F32)
    tab_spec = pl.BlockSpec((None, s, LANES), lambda i: (i, 0, 0))
    return pl.pallas_call(
        _rope_kernel,
        out_shape=(jax.ShapeDtypeStruct((b, 2, 16, s), F32), tab, tab, tab, tab),
        grid=(b,),
        in_specs=[pl.BlockSpec((None, 1, s), lambda i: (i, 0, 0)),
                  pl.BlockSpec((48, 1), lambda i: (0, 0))],
        out_specs=(pl.BlockSpec((None, 2, 16, s), lambda i: (i, 0, 0, 0)),
                   tab_spec, tab_spec, tab_spec, tab_spec),
        compiler_params=_params(1),
        name="rope",
    )(pos3, inv)


def _ffn_kernel(x_ref, mod_ref, gpre_ref, wg_ref, wu_ref, wd_ref, gpost_ref, o_ref, acc_ref, *, j0):
    x = x_ref[...]
    sh, sc, gt = mod_ref[j0:j0 + 1, :], mod_ref[j0 + 1:j0 + 2, :], mod_ref[j0 + 2:j0 + 3, :]
    h = (_rms_rows(x) * gpre_ref[...] * (1.0 + sc) + sh).astype(BF16)
    n_ch = wg_ref.shape[1] // CH_FFN
    for ch in range(n_ch):
        cols = slice(ch * CH_FFN, (ch + 1) * CH_FFN)
        g = jnp.dot(h, wg_ref[:, cols], preferred_element_type=F32)
        u = jnp.dot(h, wu_ref[:, cols], preferred_element_type=F32)
        a = (g * jax.nn.sigmoid(g) * u).astype(BF16)
        part = jnp.dot(a, wd_ref[cols, :], preferred_element_type=F32)
        if ch == 0:
            acc_ref[...] = part
        else:
            acc_ref[...] += part
    o_ref[...] = x + 0.5 * gt * (_rms_rows(acc_ref[...]) * gpost_ref[...])


def _ffn_call(x, mod, g_pre, w_gate, w_up, w_down, g_post, j0):
    b, s, d = x.shape
    f = w_gate.shape[1]
    const = lambda shape: pl.BlockSpec(shape, lambda i, j: (0,) * len(shape))
    tile = pl.BlockSpec((None, TM_FFN, d), lambda i, j: (i, j, 0))
    return pl.pallas_call(
        functools.partial(_ffn_kernel, j0=j0),
        out_shape=jax.ShapeDtypeStruct((b, s, d), F32),
        grid=(b, s // TM_FFN),
        in_specs=[tile,
                  pl.BlockSpec((None, N_MOD, d), lambda i, j: (i, 0, 0)),
                  const((1, d)), const((d, f)), const((d, f)), const((f, d)), const((1, d))],
        out_specs=tile,
        scratch_shapes=[pltpu.VMEM((TM_FFN, d), F32)],
        compiler_params=_params(2),
        name="ffn",
    )(x, mod, g_pre, w_gate, w_up, w_down, g_post)


def _proj_kernel(x_ref, mod_ref, gpre_ref, wstd_ref, wtr_ref, gcq_ref, gckv_col_ref, gckv_row_ref,
                 wuq_ref, wiq_ref, wuk_ref, wuv_ref, ekr_ref, eki_ref,
                 tt_ref, ca_ref, sa_ref, cb_ref, sb_ref,
                 qt_ref, qit_ref, wit_ref, k_ref, ki_ref, vt_ref, qkvb_ref, gates_ref):
    x = x_ref[...]
    sh, sc = mod_ref[3:4, :], mod_ref[4:5, :]
    u = (_rms_rows(x) * gpre_ref[...] * (1.0 + sc) + sh).astype(BF16)
    std = jnp.dot(u, wstd_ref[...], preferred_element_type=F32)
    tr = lax.dot_general(wtr_ref[...], u, (((1,), (1,)), ((), ())),
                         preferred_element_type=F32)

    cq = tr[R_CQ:R_CQ + Q_LORA]
    cq = cq * lax.rsqrt(jnp.mean(cq * cq, axis=0, keepdims=True) + NORM_EPS) * gcq_ref[...]
    cq = cq.astype(BF16)
    ckv_t = tr[R_CKV:R_CKV + KV_LORA]
    ckv_t = ckv_t * lax.rsqrt(jnp.mean(ckv_t * ckv_t, axis=0, keepdims=True) + NORM_EPS) * gckv_col_ref[...]
    cos_t, sin_t = tt_ref[0], tt_ref[1]

    attn_scale = (D_NOPE + D_ROPE_A) ** -0.5
    q_t = jnp.dot(wuq_ref[...], cq, preferred_element_type=F32) * attn_scale
    idx_scale = D_IDX ** -0.5
    qi_t = jnp.dot(wiq_ref[...], cq, preferred_element_type=F32) * idx_scale
    for h in range(H_A):
        r0 = h * HEAD_PAD
        x1, x2 = q_t[r0 + 64:r0 + 80], q_t[r0 + 80:r0 + 96]
        qt_ref[r0:r0 + 64, :] = q_t[r0:r0 + 64].astype(BF16)
        qt_ref[r0 + 64:r0 + 80, :] = (x1 * cos_t - x2 * sin_t).astype(BF16)
        qt_ref[r0 + 80:r0 + 96, :] = (x2 * cos_t + x1 * sin_t).astype(BF16)
        qt_ref[r0 + 96:r0 + 128, :] = q_t[r0 + 96:r0 + 128].astype(BF16)
        y1, y2 = qi_t[r0:r0 + 16], qi_t[r0 + 16:r0 + 32]
        qit_ref[r0:r0 + 16, :] = (y1 * cos_t - y2 * sin_t).astype(BF16)
        qit_ref[r0 + 16:r0 + 32, :] = (y2 * cos_t + y1 * sin_t).astype(BF16)
        qit_ref[r0 + 32:r0 + 128, :] = qi_t[r0 + 32:r0 + 128].astype(BF16)
    wit_ref[...] = tr[R_WI:R_WI + H_IDX] * (H_IDX ** -0.5)
    vt_ref[...] = jnp.dot(wuv_ref[...], ckv_t.astype(BF16), preferred_element_type=F32).astype(BF16)

    ckv = std[:, C_CKV:C_CKV + KV_LORA]
    ckv = (_rms_rows(ckv) * gckv_row_ref[...]).astype(BF16)
    k_nope = jnp.dot(ckv, wuk_ref[...], preferred_element_type=F32)
    cos_k, sin_k = cb_ref[:, 0:32], sb_ref[:, 0:32]
    k_rope = std[:, C_KR:C_KR + 32] * cos_k + std[:, C_KR_SW:C_KR_SW + 32] * sin_k
    k_ref[...] = (k_nope + jnp.dot(k_rope.astype(BF16), ekr_ref[...],
                                   preferred_element_type=F32)).astype(BF16)
    k_idx = std[:, C_KI:C_KI + 32] * cos_k + std[:, C_KI_SW:C_KI_SW + 32] * sin_k
    ki_ref[...] = jnp.dot(k_idx.astype(BF16), eki_ref[...], preferred_element_type=F32).astype(BF16)

    cos_a, sin_a = ca_ref[...], sa_ref[...]
    for which in range(2):
        scale = (D_HEAD_B ** -0.5) if which == 0 else 1.0
        for g in range(N_GROUPS_B):
            c0 = (which * N_GROUPS_B + g) * WIDTH_B
            x1 = std[:, C_QKVB + c0:C_QKVB + c0 + 128]
            x2 = std[:, C_QKVB + c0 + 128:C_QKVB + c0 + 256]
            qkvb_ref[:, c0:c0 + 128] = ((x1 * cos_a - x2 * sin_a) * scale).astype(BF16)
            qkvb_ref[:, c0 + 128:c0 + 256] = ((x2 * cos_a + x1 * sin_a) * scale).astype(BF16)
    v0 = 2 * N_GROUPS_B * WIDTH_B
    qkvb_ref[:, v0:] = std[:, C_QKVB + v0:C_QKVB + N_QKV_B].astype(BF16)
    gates_ref[...] = jax.nn.sigmoid(std[:, C_GATES:]).astype(BF16)


def _proj_call(x, mod, g_pre, w, tabs):
    b, s, d = x.shape
    tt, ca, sa, cb, sb = tabs
    tp = TP_PROJ
    const = lambda a: pl.BlockSpec(a.shape, lambda i, j: (0,) * a.ndim)
    rows = lambda width: pl.BlockSpec((None, tp, width), lambda i, j: (i, j, 0))
    colsT = lambda height: pl.BlockSpec((None, height, tp), lambda i, j: (i, 0, j))
    consts = [g_pre, w["w_std"], w["w_tr"], w["g_cq_col"], w["g_ckv_col"], w["g_ckv_row"],
              w["w_uq_t"], w["w_iq_t"], w["w_uk_p"], w["w_uv_t"], w["e_kr"], w["e_ki"]]
    hq = H_A * HEAD_PAD
    return pl.pallas_call(
        _proj_kernel,
        out_shape=(jax.ShapeDtypeStruct((b, hq, s), BF16),
                   jax.ShapeDtypeStruct((b, hq, s), BF16),
                   jax.ShapeDtypeStruct((b, H_IDX, s), F32),
                   jax.ShapeDtypeStruct((b, s, hq), BF16),
                   jax.ShapeDtypeStruct((b, s, LANES), BF16),
                   jax.ShapeDtypeStruct((b, WIDTH_A, s), BF16),
                   jax.ShapeDtypeStruct((b, s, N_QKV_B), BF16),
                   jax.ShapeDtypeStruct((b, s, 2048), BF16)),
        grid=(b, s // tp),
        in_specs=[rows(d), pl.BlockSpec((None, N_MOD, d), lambda i, j: (i, 0, 0))]
                 + [const(a) for a in consts]
                 + [pl.BlockSpec((None, 2, 16, tp), lambda i, j: (i, 0, 0, j)),
                    rows(LANES), rows(LANES), rows(LANES), rows(LANES)],
        out_specs=(colsT(hq), colsT(hq), colsT(H_IDX), rows(hq), rows(LANES), colsT(WIDTH_A),
                   rows(N_QKV_B), rows(2048)),
        compiler_params=_params(2),
        name="proj",
    )(x, mod, *consts, tt, ca, sa, cb, sb)


def _dsa_kernel(qit_ref, wit_ref, qt_ref, ki_ref, k_ref, vt_ref, o_ref,
                sc_ref, m_ref, l_ref, acc_ref, *, topk):
    tq = qt_ref.shape[1]
    tk = tq
    s_len = k_ref.shape[0]
    i = pl.program_id(1)
    nkb = i + 1
    qpos = i * tq + lax.broadcasted_iota(jnp.int32, (1, tq), 1)
    row_iota = lax.broadcasted_iota(jnp.int32, (tk, tq), 0)

    def fold8(v):
        return jnp.sum(v.reshape(tk // 8, 8, tq), axis=0)

    def idx_body(kb, carry):
        smax, smin = carry
        k0 = pl.multiple_of(kb * tk, tk)
        ki = ki_ref[pl.ds(k0, tk), :]
        acc = jnp.zeros((tk, tq), F32)
        for h in range(H_IDX):
            lg = jnp.dot(ki, qit_ref[h * HEAD_PAD:(h + 1) * HEAD_PAD, :], preferred_element_type=F32)
            acc = acc + wit_ref[h:h + 1, :] * jnp.maximum(lg, 0.0)
        causal = (k0 + row_iota) <= qpos
        sc_ref[pl.ds(k0, tk), :] = jnp.where(causal, acc, -jnp.inf)
        smax = jnp.maximum(smax, jnp.max(jnp.where(causal, acc, -jnp.inf), axis=0, keepdims=True))
        smin = jnp.minimum(smin, jnp.min(jnp.where(causal, acc, jnp.inf), axis=0, keepdims=True))
        return smax, smin

    smax, smin = lax.fori_loop(0, nkb, idx_body,
                               (jnp.full((1, tq), -jnp.inf, F32), jnp.full((1, tq), jnp.inf, F32)))

    kf = jnp.minimum(qpos + 1, topk).astype(F32)

    def count_gt(thr):
        def body(kb, c):
            k0 = pl.multiple_of(kb * tk, tk)
            return c + fold8(jnp.where(sc_ref[pl.ds(k0, tk), :] > thr, 1.0, 0.0))
        c8 = lax.fori_loop(0, nkb, body, jnp.zeros((8, tq), F32))
        return jnp.sum(c8, axis=0, keepdims=True)

    def bisect(_, carry):
        lo, hi = carry
        mid = 0.5 * (lo + hi)
        ge = count_gt(mid) >= kf
        return jnp.where(ge, mid, lo), jnp.where(ge, hi, mid)

    lo, hi = lax.fori_loop(0, BISECT_ITERS, bisect, (smin - (1.0 + jnp.abs(smin)), smax))
    c_lo = count_gt(lo)
    need = kf - count_gt(hi)

    def tie_index():
        def count_le(jm):
            def body(kb, c):
                k0 = pl.multiple_of(kb * tk, tk)
                sv = sc_ref[pl.ds(k0, tk), :]
                kpos = (k0 + row_iota).astype(F32)
                ind = jnp.where(sv > lo, jnp.where(sv > hi, 0.0, jnp.where(kpos <= jm, 1.0, 0.0)), 0.0)
                return c + fold8(ind)
            c8 = lax.fori_loop(0, nkb, body, jnp.zeros((8, tq), F32))
            return jnp.sum(c8, axis=0, keepdims=True)

        def step(_, carry):
            jl, jh = carry
            jm = jnp.floor(0.5 * (jl + jh))
            ge = count_le(jm) >= need
            return jnp.where(ge, jl, jm), jnp.where(ge, jm, jh)

        n_steps = int(np.ceil(np.log2(s_len))) + 1
        _, jh = lax.fori_loop(0, n_steps, step,
                              (jnp.full((1, tq), -1.0, F32), jnp.full((1, tq), s_len - 1.0, F32)))
        return jh

    any_tie = jnp.max(jnp.where(c_lo > kf, 1.0, 0.0)) > 0.5
    j_cut = lax.cond(any_tie, tie_index, lambda: jnp.full((1, tq), float(s_len), F32))

    m_ref[...] = jnp.full(m_ref.shape, NEG_BIG, F32)
    l_ref[...] = jnp.zeros(l_ref.shape, F32)
    acc_ref[...] = jnp.zeros(acc_ref.shape, F32)

    def att_body(kb, carry):
        k0 = pl.multiple_of(kb * tk, tk)
        sv = sc_ref[pl.ds(k0, tk), :]
        kpos = (k0 + row_iota).astype(F32)
        bias = jnp.where(sv > hi, 0.0,
                         jnp.where(sv > lo, jnp.where(kpos <= j_cut, 0.0, NEG_BIG), NEG_BIG))
        for h in range(H_A):
            kh = k_ref[pl.ds(k0, tk), h * HEAD_PAD:(h + 1) * HEAD_PAD]
            s = jnp.dot(kh, qt_ref[h * HEAD_PAD:(h + 1) * HEAD_PAD, :], preferred_element_type=F32) + bias
            m_old = m_ref[h:h + 1, :]
            m_new = jnp.maximum(m_old, jnp.max(s, axis=0, keepdims=True))
            alpha = jnp.exp(m_old - m_new)
            p = jnp.exp(s - m_new)
            l_ref[h:h + 1, :] = alpha * l_ref[h:h + 1, :] + jnp.sum(p, axis=0, keepdims=True)
            rows = slice(h * D_V_A, (h + 1) * D_V_A)
            pv = jnp.dot(vt_ref[rows, pl.ds(k0, tk)], p.astype(BF16), preferred_element_type=F32)
            acc_ref[rows, :] = alpha * acc_ref[rows, :] + pv
            m_ref[h:h + 1, :] = m_new
        return carry

    lax.fori_loop(0, nkb, att_body, 0)
    for h in range(H_A):
        rows = slice(h * D_V_A, (h + 1) * D_V_A)
        acc_ref[rows, :] = acc_ref[rows, :] / l_ref[h:h + 1, :]
    o_ref[...] = acc_ref[...].T.astype(BF16)


def _dsa_call(qt, qit, wit, k, ki, vt):
    b, hq, s = qt.shape
    tq = TQ_DSA
    topk = min(TOPK_MAX, s // 4)
    colsT = lambda height: pl.BlockSpec((None, height, tq), lambda i, j: (i, 0, j))
    whole = lambda a: pl.BlockSpec((None,) + a.shape[1:], lambda i, j: (i, 0, 0))
    return pl.pallas_call(
        functools.partial(_dsa_kernel, topk=topk),
        out_shape=jax.ShapeDtypeStruct((b, s, WIDTH_A), BF16),
        grid=(b, s // tq),
        in_specs=[colsT(hq), colsT(H_IDX), colsT(hq), whole(ki), whole(k), whole(vt)],
        out_specs=pl.BlockSpec((None, tq, WIDTH_A), lambda i, j: (i, j, 0)),
        scratch_shapes=[pltpu.VMEM((s, tq), F32),
                        pltpu.VMEM((H_A, tq), F32),
                        pltpu.VMEM((H_A, tq), F32),
                        pltpu.VMEM((WIDTH_A, tq), F32)],
        compiler_params=_params(2),
        name="dsa",
    )(qit, wit, qt, ki, k, vt)


def _dil_kernel(q_ref, kc_ref, kp_ref, vc_ref, vp_ref, o_ref, lse_ref):
    tb = q_ref.shape[0]
    has_prev = pl.program_id(2) > 0
    q = q_ref[...]
    kc, kp, vc, vp = kc_ref[...], kp_ref[...], vc_ref[...], vp_ref[...]
    r = lax.broadcasted_iota(jnp.int32, (tb, tb), 0)
    c = lax.broadcasted_iota(jnp.int32, (tb, tb), 1)
    mask_c = c <= r
    mask_p = jnp.logical_and(c >= r, has_prev)
    lane = lax.broadcasted_iota(jnp.int32, (1, WIDTH_B), 1)
    head_qk = (lane % 128) // (D_HEAD_B // 2)
    head_v = lane // D_HEAD_B
    nt = (((1,), (1,)), ((), ()))
    o = jnp.zeros((tb, WIDTH_B), F32)
    lse = jnp.zeros((tb, WIDTH_B), F32)
    for h in range(H_B):
        qh = jnp.where(head_qk == h, q, jnp.zeros_like(q))
        sc = jnp.where(mask_c, lax.dot_general(qh, kc, nt, preferred_element_type=F32), -jnp.inf)
        sp = jnp.where(mask_p, lax.dot_general(qh, kp, nt, preferred_element_type=F32), -jnp.inf)
        m = jnp.maximum(jnp.max(sc, axis=1, keepdims=True), jnp.max(sp, axis=1, keepdims=True))
        pc = jnp.exp(sc - m)
        pp = jnp.exp(sp - m)
        l = jnp.sum(pc, axis=1, keepdims=True) + jnp.sum(pp, axis=1, keepdims=True)
        oh = (jnp.dot(pc.astype(BF16), vc, preferred_element_type=F32)
              + jnp.dot(pp.astype(BF16), vp, preferred_element_type=F32))
        o = jnp.where(head_v == h, oh / l, o)
        lse = jnp.where(head_v == h, m + jnp.log(l), lse)
    o_ref[...] = o
    lse_ref[...] = lse


def _dil_call(qkvb, g, dilation):
    b, s, _ = qkvb.shape
    n = s // dilation
    tb = TB_DIL
    view = qkvb.reshape(b, n, dilation * N_QKV_B)
    blocks_per_tok = N_QKV_B // WIDTH_B

    def spec(which, prev):
        def index(i, r, j):
            jj = jnp.maximum(j - 1, 0) if prev else j
            return (i, jj, r * blocks_per_tok + which * N_GROUPS_B + g)
        return pl.BlockSpec((None, tb, WIDTH_B), index)

    out_spec = pl.BlockSpec((None, tb, WIDTH_B), lambda i, r, j: (i, j, r))
    out = jax.ShapeDtypeStruct((b, n, dilation * WIDTH_B), F32)
    o, lse = pl.pallas_call(
        _dil_kernel,
        out_shape=(out, out),
        grid=(b, dilation, n // tb),
        in_specs=[spec(0, False), spec(1, False), spec(1, True), spec(2, False), spec(2, True)],
        out_specs=(out_spec, out_spec),
        compiler_params=_params(3),
        name="dil",
    )(view, view, view, view, view)
    return o.reshape(b, s, WIDTH_B), lse.reshape(b, s, WIDTH_B)


def _merge_kernel(x_ref, mod_ref, oa_ref, o0_ref, o1_ref, o2_ref, l0_ref, l1_ref, l2_ref, gates_ref,
                  wua_ref, wub_ref, wo_ref, gpost_ref, out_ref):
    l0, l1, l2 = l0_ref[...], l1_ref[...], l2_ref[...]
    m = jnp.maximum(jnp.maximum(l0, l1), l2)
    e0, e1, e2 = jnp.exp(l0 - m), jnp.exp(l1 - m), jnp.exp(l2 - m)
    ob = (e0 * o0_ref[...] + e1 * o1_ref[...] + e2 * o2_ref[...]) / (e0 + e1 + e2)
    za = jnp.dot(oa_ref[...], wua_ref[...], preferred_element_type=F32)
    zb = jnp.dot(ob.astype(BF16), wub_ref[...], preferred_element_type=F32)
    d = za.shape[1]
    z = gates_ref[:, 0:d].astype(F32) * za + gates_ref[:, d:2 * d].astype(F32) * zb
    y = jnp.dot(z.astype(BF16), wo_ref[...], preferred_element_type=F32)
    out_ref[...] = x_ref[...] + mod_ref[5:6, :] * (_rms_rows(y) * gpost_ref[...])


def _merge_call(x, mod, oa, dil, gates, w_up_a, w_up_b, w_o, g_post):
    b, s, d = x.shape
    tm = TM_MERGE
    rows = lambda width: pl.BlockSpec((None, tm, width), lambda i, j: (i, j, 0))
    const = lambda a: pl.BlockSpec(a.shape, lambda i, j: (0,) * a.ndim)
    (o0, l0), (o1, l1), (o2, l2) = dil
    return pl.pallas_call(
        _merge_kernel,
        out_shape=jax.ShapeDtypeStruct((b, s, d), F32),
        grid=(b, s // tm),
        in_specs=[rows(d), pl.BlockSpec((None, N_MOD, d), lambda i, j: (i, 0, 0)), rows(WIDTH_A)]
                 + [rows(WIDTH_B)] * 6 + [rows(2 * d), const(w_up_a), const(w_up_b), const(w_o), const(g_post)],
        out_specs=rows(d),
        compiler_params=_params(2),
        name="merge",
    )(x, mod, oa, o0, o1, o2, l0, l1, l2, gates, w_up_a, w_up_b, w_o, g_post)


def _pad_heads(w_t):
    h, r, n = w_t.shape
    return jnp.pad(w_t, ((0, 0), (0, HEAD_PAD - r), (0, 0))).reshape(h * HEAD_PAD, n)


def _mixer_weights(w_in, g_cq, g_ckv, w_uq, w_uk, w_uv, w_iq):
    d = w_in.shape[0]
    o_cq, o_ckv, o_kr, o_ki, o_wi = 0, Q_LORA, Q_LORA + KV_LORA, Q_LORA + KV_LORA + D_ROPE_A, \
        Q_LORA + KV_LORA + D_ROPE_A + D_IDX
    o_qkv = o_wi + H_IDX
    o_gates = o_qkv + N_QKV_B

    def swap_halves(w):
        half = w.shape[1] // 2
        return jnp.concatenate([w[:, half:], w[:, :half]], axis=1)

    w_kr, w_ki = w_in[:, o_kr:o_kr + D_ROPE_A], w_in[:, o_ki:o_ki + D_IDX]
    qkv = w_in[:, o_qkv:o_gates].reshape(d, 3, N_GROUPS_B, H_B, 2, D_HEAD_B // 2)
    qk_split = qkv[:, 0:2].transpose(0, 1, 2, 4, 3, 5).reshape(d, 2 * N_GROUPS_B * WIDTH_B)
    v_cols = qkv[:, 2].reshape(d, N_GROUPS_B * WIDTH_B)
    w_std = jnp.concatenate([w_in[:, o_ckv:o_ckv + KV_LORA], w_kr, swap_halves(w_kr), w_ki, swap_halves(w_ki),
                             qk_split, v_cols, w_in[:, o_gates:]], axis=1)
    w_tr = jnp.concatenate([w_in[:, o_cq:o_cq + Q_LORA], w_in[:, o_ckv:o_ckv + KV_LORA],
                            w_in[:, o_wi:o_wi + H_IDX]], axis=1).T
    w_tr = jnp.pad(w_tr, ((0, N_TR - w_tr.shape[0]), (0, 0)))

    e_kr = np.zeros((D_ROPE_A, H_A * HEAD_PAD), np.float32)
    for h in range(H_A):
        e_kr[np.arange(D_ROPE_A), h * HEAD_PAD + D_NOPE + np.arange(D_ROPE_A)] = 1.0
    e_ki = np.zeros((D_IDX, LANES), np.float32)
    e_ki[np.arange(D_IDX), np.arange(D_IDX)] = 1.0
    return {
        "w_std": w_std.astype(BF16),
        "w_tr": w_tr.astype(BF16),
        "g_cq_col": g_cq.reshape(Q_LORA, 1),
        "g_ckv_col": g_ckv.reshape(KV_LORA, 1),
        "g_ckv_row": g_ckv.reshape(1, KV_LORA),
        "w_uq_t": _pad_heads(w_uq.transpose(1, 2, 0)).astype(BF16),
        "w_iq_t": _pad_heads(w_iq.transpose(1, 2, 0)).astype(BF16),
        "w_uk_p": jnp.pad(w_uk, ((0, 0), (0, 0), (0, HEAD_PAD - D_NOPE))).reshape(KV_LORA, H_A * HEAD_PAD).astype(BF16),
        "w_uv_t": w_uv.transpose(1, 2, 0).reshape(WIDTH_A, KV_LORA).astype(BF16),
        "e_kr": jnp.asarray(e_kr, BF16),
        "e_ki": jnp.asarray(e_ki, BF16),
    }


def kernel(x, c, positions, w_mod, b_mod, g_pre_ffn1, w_gate1, w_up1, w_down1, g_post_ffn1, g_pre_mix, w_in, g_cq, g_ckv, w_uq, w_uk, w_uv, w_iq, w_up_a, w_up_b, w_o, g_post_mix, g_pre_ffn2, w_gate2, w_up2, w_down2, g_post_ffn2):
    b, s, d = x.shape
    assert s % (16 * TB_DIL) == 0 and s % TM_FFN == 0 and b <= 8
    c8 = jnp.pad(c, ((0, 8 - b), (0, 0)))
    half_b, half_a = D_HEAD_B // 2, D_ROPE_A // 2
    inv = jnp.concatenate([ROPE_THETA ** (-jnp.arange(half_b, dtype=F32) / half_b),
                           ROPE_THETA ** (-jnp.arange(half_a, dtype=F32) / half_a)]).reshape(48, 1)
    tabs = _rope_call(positions.reshape(b, 1, s), inv)
    row = lambda g: g.reshape(1, -1)
    for l in range(w_mod.shape[0]):
        mod = _mod_call(c8, w_mod[l], b_mod[l].reshape(1, -1))[:b].reshape(b, N_MOD, d)
        x = _ffn_call(x, mod, row(g_pre_ffn1[l]), w_gate1[l].astype(BF16), w_up1[l].astype(BF16),
                      w_down1[l].astype(BF16), row(g_post_ffn1[l]), 0)
        w = _mixer_weights(w_in[l], g_cq[l], g_ckv[l], w_uq[l], w_uk[l], w_uv[l], w_iq[l])
        qt, qit, wit, k, ki, vt, qkvb, gates = _proj_call(x, mod, row(g_pre_mix[l]), w, tabs)
        oa = _dsa_call(qt, qit, wit, k, ki, vt)
        dil = [_dil_call(qkvb, g, dilation) for g, (_, dilation) in enumerate(DIL_PAIRS)]
        x = _merge_call(x, mod, oa, dil, gates, w_up_a[l].astype(BF16), w_up_b[l].astype(BF16),
                        w_o[l].astype(BF16), row(g_post_mix[l]))
        x = _ffn_call(x, mod, row(g_pre_ffn2[l]), w_gate2[l].astype(BF16), w_up2[l].astype(BF16),
                      w_down2[l].astype(BF16), row(g_post_ffn2[l]), 6)
    return x
```

```python
import functools

import numpy as np
import jax
import jax.numpy as jnp
from jax import lax
from jax.experimental import pallas as pl
from jax.experimental.pallas import tpu as pltpu

F32 = jnp.float32
BF16 = jnp.bfloat16

D_FF = 2816
ROPE_THETA = 10000.0
NORM_EPS = 1e-6
H_A = 8
Q_LORA = 384
KV_LORA = 256
D_NOPE = 64
D_ROPE_A = 32
D_V_A = 64
H_IDX = 8
D_IDX = 32
TOPK_MAX = 256
DIL_PAIRS = ((128, 1), (512, 4), (2048, 16))
N_GROUPS_B = 3
H_B = 4
D_HEAD_B = 64
N_MOD = 9
WIDTH_A = H_A * D_V_A
WIDTH_B = H_B * D_HEAD_B
N_QKV_B = 3 * N_GROUPS_B * WIDTH_B

LANES = 128
HEAD_PAD = 128
VMEM_LIMIT = 56 * 1024 * 1024

TM_FFN = 512
CH_FFN = 256
TP_PROJ = 256
TQ_DSA = 256
V_ROWS = D_V_A + 16
TB_DIL = 128
TM_MERGE = 256
BISECT_MAX = 34
BISECT_CHECK = 2
FOLD_ROWS = 32
NEG_BIG = -1e30
LOG2_E = 1.4426950408889634

C_CKV = 0
C_KR = 256
C_KR_SW = 288
C_KI = 320
C_KI_SW = 352
C_QKVB = 384
C_GATES = C_QKVB + N_QKV_B
N_STD = C_GATES + 2048
R_CQ = 0
R_CKV = Q_LORA
R_WI = Q_LORA + KV_LORA
N_TR = 768


def _params(n_axes):
    return pltpu.CompilerParams(dimension_semantics=("arbitrary",) * n_axes,
                                vmem_limit_bytes=VMEM_LIMIT)


def _rms_rows(x):
    return x * lax.rsqrt(jnp.mean(x * x, axis=-1, keepdims=True) + NORM_EPS)


def _mod_kernel(c_ref, w_ref, b_ref, o_ref):
    c = c_ref[...]
    o_ref[...] = jnp.dot(c * jax.nn.sigmoid(c), w_ref[...], preferred_element_type=F32) + b_ref[...]


def _mod_call(c8, w_mod, b_mod):
    d = c8.shape[1]
    return pl.pallas_call(
        _mod_kernel,
        out_shape=jax.ShapeDtypeStruct((8, N_MOD * d), F32),
        grid=(N_MOD,),
        in_specs=[pl.BlockSpec((8, d), lambda j: (0, 0)),
                  pl.BlockSpec((d, d), lambda j: (0, j)),
                  pl.BlockSpec((1, d), lambda j: (0, j))],
        out_specs=pl.BlockSpec((8, d), lambda j: (0, j)),
        compiler_params=_params(1),
        name="mod",
    )(c8, w_mod, b_mod)


def _rope_kernel(pos_ref, inv_ref, tt_ref, ca_ref, sa_ref, cb_ref, sb_ref):
    s = pos_ref.shape[1]
    ang = inv_ref[...] * pos_ref[...].astype(F32)
    cos = jnp.cos(ang)
    sin = jnp.sin(ang)
    cos_b, sin_b = cos[32:48], sin[32:48]
    tt_ref[0] = cos_b
    tt_ref[1] = sin_b
    cos_a = jnp.concatenate([cos[0:32]] * 4, axis=0)
    sin_a = jnp.concatenate([sin[0:32]] * 4, axis=0)
    cos_bt = jnp.concatenate([cos_b] * 8, axis=0)
    sin_bt = jnp.concatenate([-sin_b, sin_b] * 4, axis=0)
    for j in range(s // LANES):
        cols = slice(j * LANES, (j + 1) * LANES)
        ca_ref[cols, :] = cos_a[:, cols].T
        sa_ref[cols, :] = sin_a[:, cols].T
        cb_ref[cols, :] = cos_bt[:, cols].T
        sb_ref[cols, :] = sin_bt[:, cols].T


def _rope_call(pos3, inv):
    b, _, s = pos3.shape
    tab = jax.ShapeDtypeStruct((b, s, LANES), F32)
    tab_spec = pl.BlockSpec((None, s, LANES), lambda i: (i, 0, 0))
    return pl.pallas_call(
        _rope_kernel,
        out_shape=(jax.ShapeDtypeStruct((b, 2, 16, s), F32), tab, tab, tab, tab),
        grid=(b,),
        in_specs=[pl.BlockSpec((None, 1, s), lambda i: (i, 0, 0)),
                  pl.BlockSpec((48, 1), lambda i: (0, 0))],
        out_specs=(pl.BlockSpec((None, 2, 16, s), lambda i: (i, 0, 0, 0)),
                   tab_spec, tab_spec, tab_spec, tab_spec),
        compiler_params=_params(1),
        name="rope",
    )(pos3, inv)


def _ffn_kernel(x_ref, mod_ref, gpre_ref, wg_ref, wu_ref, wd_ref, gpost_ref, o_ref, acc_ref, *, j0):
    x = x_ref[...]
    sh, sc, gt = mod_ref[j0:j0 + 1, :], mod_ref[j0 + 1:j0 + 2, :], mod_ref[j0 + 2:j0 + 3, :]
    h = (_rms_rows(x) * gpre_ref[...] * (1.0 + sc) + sh).astype(BF16)
    n_ch = wg_ref.shape[1] // CH_FFN
    for ch in range(n_ch):
        cols = slice(ch * CH_FFN, (ch + 1) * CH_FFN)
        g = jnp.dot(h, wg_ref[:, cols], preferred_element_type=F32)
        u = jnp.dot(h, wu_ref[:, cols], preferred_element_type=F32)
        a = (g * jax.nn.sigmoid(g) * u).astype(BF16)
        part = jnp.dot(a, wd_ref[cols, :], preferred_element_type=F32)
        if ch == 0:
            acc_ref[...] = part
        else:
            acc_ref[...] += part
    o_ref[...] = x + 0.5 * gt * (_rms_rows(acc_ref[...]) * gpost_ref[...])


def _ffn_call(x, mod, g_pre, w_gate, w_up, w_down, g_post, j0):
    b, s, d = x.shape
    f = w_gate.shape[1]
    const = lambda shape: pl.BlockSpec(shape, lambda i, j: (0,) * len(shape))
    tile = pl.BlockSpec((None, TM_FFN, d), lambda i, j: (i, j, 0))
    return pl.pallas_call(
        functools.partial(_ffn_kernel, j0=j0),
        out_shape=jax.ShapeDtypeStruct((b, s, d), F32),
        grid=(b, s // TM_FFN),
        in_specs=[tile,
                  pl.BlockSpec((None, N_MOD, d), lambda i, j: (i, 0, 0)),
                  const((1, d)), const((d, f)), const((d, f)), const((f, d)), const((1, d))],
        out_specs=tile,
        scratch_shapes=[pltpu.VMEM((TM_FFN, d), F32)],
        compiler_params=_params(2),
        name="ffn",
    )(x, mod, g_pre, w_gate, w_up, w_down, g_post)


def _proj_kernel(x_ref, mod_ref, gpre_ref, wstd_ref, wtr_ref, gcq_ref, gckv_col_ref, gckv_row_ref,
                 wuq_ref, wiq_ref, wuk_ref, wuv_ref, ekr_ref, eki_ref,
                 tt_ref, ca_ref, sa_ref, cb_ref, sb_ref,
                 qt_ref, qit_ref, wit_ref, k_ref, ki_ref, vt_ref, dq0_ref, dq1_ref, dq2_ref, gates_ref,
                 dil_scr):
    x = x_ref[...]
    sh, sc = mod_ref[3:4, :], mod_ref[4:5, :]
    u = (_rms_rows(x) * gpre_ref[...] * (1.0 + sc) + sh).astype(BF16)
    std = jnp.dot(u, wstd_ref[...], preferred_element_type=F32)
    tr = lax.dot_general(wtr_ref[...], u, (((1,), (1,)), ((), ())),
                         preferred_element_type=F32)

    cq = tr[R_CQ:R_CQ + Q_LORA]
    cq = cq * lax.rsqrt(jnp.mean(cq * cq, axis=0, keepdims=True) + NORM_EPS) * gcq_ref[...]
    cq = cq.astype(BF16)
    ckv_t = tr[R_CKV:R_CKV + KV_LORA]
    ckv_t = ckv_t * lax.rsqrt(jnp.mean(ckv_t * ckv_t, axis=0, keepdims=True) + NORM_EPS) * gckv_col_ref[...]
    cos_t, sin_t = tt_ref[0], tt_ref[1]

    attn_scale = (D_NOPE + D_ROPE_A) ** -0.5 * LOG2_E
    q_t = jnp.dot(wuq_ref[...], cq, preferred_element_type=F32) * attn_scale
    idx_scale = D_IDX ** -0.5
    qi_t = jnp.dot(wiq_ref[...], cq, preferred_element_type=F32) * idx_scale
    for h in range(H_A):
        r0 = h * HEAD_PAD
        x1, x2 = q_t[r0 + 64:r0 + 80], q_t[r0 + 80:r0 + 96]
        qt_ref[r0:r0 + 64, :] = q_t[r0:r0 + 64].astype(BF16)
        qt_ref[r0 + 64:r0 + 80, :] = (x1 * cos_t - x2 * sin_t).astype(BF16)
        qt_ref[r0 + 80:r0 + 96, :] = (x2 * cos_t + x1 * sin_t).astype(BF16)
        qt_ref[r0 + 96:r0 + 128, :] = q_t[r0 + 96:r0 + 128].astype(BF16)
        y1, y2 = qi_t[r0:r0 + 16], qi_t[r0 + 16:r0 + 32]
        qit_ref[r0:r0 + 16, :] = (y1 * cos_t - y2 * sin_t).astype(BF16)
        qit_ref[r0 + 16:r0 + 32, :] = (y2 * cos_t + y1 * sin_t).astype(BF16)
        qit_ref[r0 + 32:r0 + 128, :] = qi_t[r0 + 32:r0 + 128].astype(BF16)
    wit_ref[...] = tr[R_WI:R_WI + H_IDX] * (H_IDX ** -0.5)
    v_t = jnp.dot(wuv_ref[...], ckv_t.astype(BF16), preferred_element_type=F32).astype(BF16)
    for h in range(H_A):
        vt_ref[h * V_ROWS:h * V_ROWS + D_V_A, :] = v_t[h * D_V_A:(h + 1) * D_V_A]
        vt_ref[h * V_ROWS + D_V_A:(h + 1) * V_ROWS, :] = jnp.ones((V_ROWS - D_V_A, v_t.shape[1]), BF16)

    ckv = std[:, C_CKV:C_CKV + KV_LORA]
    ckv = (_rms_rows(ckv) * gckv_row_ref[...]).astype(BF16)
    k_nope = jnp.dot(ckv, wuk_ref[...], preferred_element_type=F32)
    cos_k, sin_k = cb_ref[:, 0:32], sb_ref[:, 0:32]
    k_rope = std[:, C_KR:C_KR + 32] * cos_k + std[:, C_KR_SW:C_KR_SW + 32] * sin_k
    k_ref[...] = (k_nope + jnp.dot(k_rope.astype(BF16), ekr_ref[...],
                                   preferred_element_type=F32)).astype(BF16)
    k_idx = std[:, C_KI:C_KI + 32] * cos_k + std[:, C_KI_SW:C_KI_SW + 32] * sin_k
    ki_ref[...] = jnp.dot(k_idx.astype(BF16), eki_ref[...], preferred_element_type=F32).astype(BF16)

    cos_a, sin_a = ca_ref[...], sa_ref[...]
    tp = x.shape[0]
    for g, (out_ref, (_, dilation)) in enumerate(zip((dq0_ref, dq1_ref, dq2_ref), DIL_PAIRS)):
        for which in range(2):
            scale = (D_HEAD_B ** -0.5) if which == 0 else 1.0
            c0 = C_QKVB + (which * N_GROUPS_B + g) * WIDTH_B
            x1, x2 = std[:, c0:c0 + 128], std[:, c0 + 128:c0 + 256]
            dil_scr[2 * which] = (x1 * cos_a - x2 * sin_a) * scale
            dil_scr[2 * which + 1] = (x2 * cos_a + x1 * sin_a) * scale
        c0 = C_QKVB + (2 * N_GROUPS_B + g) * WIDTH_B
        dil_scr[4] = std[:, c0:c0 + 128]
        dil_scr[5] = std[:, c0 + 128:c0 + 256]
        for r in range(dilation):
            for c in range(3 * WIDTH_B // LANES):
                out_ref[r, :, c * LANES:(c + 1) * LANES] = (
                    dil_scr[c, pl.ds(r, tp // dilation, stride=dilation), :].astype(BF16))
    gates_ref[...] = jax.nn.sigmoid(std[:, C_GATES:]).astype(BF16)


def _proj_call(x, mod, g_pre, w, tabs):
    b, s, d = x.shape
    tt, ca, sa, cb, sb = tabs
    tp = TP_PROJ
    const = lambda a: pl.BlockSpec(a.shape, lambda i, j: (0,) * a.ndim)
    rows = lambda width: pl.BlockSpec((None, tp, width), lambda i, j: (i, j, 0))
    colsT = lambda height: pl.BlockSpec((None, height, tp), lambda i, j: (i, 0, j))
    consts = [g_pre, w["w_std"], w["w_tr"], w["g_cq_col"], w["g_ckv_col"], w["g_ckv_row"],
              w["w_uq_t"], w["w_iq_t"], w["w_uk_p"], w["w_uv_t"], w["e_kr"], w["e_ki"]]
    hq = H_A * HEAD_PAD
    dil_shapes = [jax.ShapeDtypeStruct((b, dil, s // dil, 3 * WIDTH_B), BF16) for _, dil in DIL_PAIRS]
    dil_specs = [pl.BlockSpec((None, dil, tp // dil, 3 * WIDTH_B), lambda i, j: (i, 0, j, 0))
                 for _, dil in DIL_PAIRS]
    return pl.pallas_call(
        _proj_kernel,
        out_shape=(jax.ShapeDtypeStruct((b, hq, s), BF16),
                   jax.ShapeDtypeStruct((b, hq, s), BF16),
                   jax.ShapeDtypeStruct((b, H_IDX, s), F32),
                   jax.ShapeDtypeStruct((b, s, hq), BF16),
                   jax.ShapeDtypeStruct((b, s, LANES), BF16),
                   jax.ShapeDtypeStruct((b, H_A * V_ROWS, s), BF16),
                   *dil_shapes,
                   jax.ShapeDtypeStruct((b, s, 2048), BF16)),
        grid=(b, s // tp),
        in_specs=[rows(d), pl.BlockSpec((None, N_MOD, d), lambda i, j: (i, 0, 0))]
                 + [const(a) for a in consts]
                 + [pl.BlockSpec((None, 2, 16, tp), lambda i, j: (i, 0, 0, j)),
                    rows(LANES), rows(LANES), rows(LANES), rows(LANES)],
        out_specs=(colsT(hq), colsT(hq), colsT(H_IDX), rows(hq), rows(LANES), colsT(H_A * V_ROWS),
                   *dil_specs, rows(2048)),
        scratch_shapes=[pltpu.VMEM((3 * WIDTH_B // LANES, tp, LANES), F32)],
        compiler_params=_params(2),
        name="proj",
    )(x, mod, *consts, tt, ca, sa, cb, sb)


def _dsa_kernel(qit_ref, wit_ref, qt_ref, ki_ref, k_ref, vt_ref, o_ref,
                sc_ref, m_ref, acc_ref, s_ref, *, topk):
    tq = qt_ref.shape[1]
    tk = tq
    s_len = k_ref.shape[0]
    i = pl.program_id(1)
    nkb = i + 1
    npair = (nkb + 1) // 2
    qpos = i * tq + lax.broadcasted_iota(jnp.int32, (1, tq), 1)
    row_iota = lax.broadcasted_iota(jnp.int32, (tk, tq), 0)
    row_iota2 = lax.broadcasted_iota(jnp.int32, (2 * tk, tq), 0)

    def fold(v):
        return jnp.sum(v.reshape(v.shape[0] // FOLD_ROWS, FOLD_ROWS, tq), axis=0)

    def idx_body(kb, carry):
        smax, smin = carry
        k0 = pl.multiple_of(kb * tk, tk)
        ki = ki_ref[pl.ds(k0, tk), :]
        acc = jnp.zeros((tk, tq), F32)
        for h in range(H_IDX):
            lg = jnp.dot(ki, qit_ref[h * HEAD_PAD:(h + 1) * HEAD_PAD, :], preferred_element_type=F32)
            acc = acc + wit_ref[h:h + 1, :] * jnp.maximum(lg, 0.0)
        causal = (k0 + row_iota) <= qpos
        sc_ref[pl.ds(k0, tk), :] = jnp.where(causal, acc, -jnp.inf)
        smax = jnp.maximum(smax, jnp.max(jnp.where(causal, acc, -jnp.inf), axis=0, keepdims=True))
        smin = jnp.minimum(smin, jnp.min(jnp.where(causal, acc, jnp.inf), axis=0, keepdims=True))
        return smax, smin

    smax, smin = lax.fori_loop(0, nkb, idx_body,
                               (jnp.full((1, tq), -jnp.inf, F32), jnp.full((1, tq), jnp.inf, F32)))

    @pl.when(nkb % 2 == 1)
    def _pad_block():
        sc_ref[pl.ds(pl.multiple_of(nkb * tk, tk), tk), :] = jnp.full((tk, tq), -jnp.inf, F32)

    n_causal = (qpos + 1).astype(F32)
    kf = jnp.minimum(qpos + 1, topk).astype(F32)

    def count_pairs(indicator):
        def body(kp, c):
            k0 = pl.multiple_of(kp * (2 * tk), 2 * tk)
            return c + fold(indicator(sc_ref[pl.ds(k0, 2 * tk), :], k0))
        part = lax.fori_loop(0, npair, body, jnp.zeros((FOLD_ROWS, tq), F32))
        return jnp.sum(part, axis=0, keepdims=True)

    def count_gt(thr):
        return count_pairs(lambda sv, k0: jnp.where(sv > thr, 1.0, 0.0))

    def bisect_cond(carry):
        return jnp.logical_and(carry[0] < BISECT_MAX, carry[1] > 0)

    def bisect_body(carry):
        it, _, lo, hi, c_lo, c_hi = carry
        for _ in range(BISECT_CHECK):
            mid = 0.5 * (lo + hi)
            c_mid = count_gt(mid)
            ge = c_mid >= kf
            lo, c_lo = jnp.where(ge, mid, lo), jnp.where(ge, c_mid, c_lo)
            hi, c_hi = jnp.where(ge, hi, mid), jnp.where(ge, c_hi, c_mid)
        open_queries = (jnp.max(c_lo - kf) > 0.0).astype(jnp.int32)
        return it + BISECT_CHECK, open_queries, lo, hi, c_lo, c_hi

    first_open = (jnp.max(n_causal - kf) > 0.0).astype(jnp.int32)
    _, still_open, lo, hi, _, c_hi = lax.while_loop(
        bisect_cond, bisect_body,
        (jnp.int32(0), first_open, smin - (1.0 + jnp.abs(smin)), smax, n_causal, jnp.zeros((1, tq), F32)))

    @pl.when(still_open > 0)
    def _break_ties():
        need = kf - c_hi

        def count_le(jm):
            def ind(sv, k0):
                kpos = (k0 + row_iota2).astype(F32)
                return jnp.where(sv > lo, jnp.where(sv > hi, 0.0, jnp.where(kpos <= jm, 1.0, 0.0)), 0.0)
            return count_pairs(ind)

        def step(_, carry):
            jl, jh = carry
            jm = jnp.floor(0.5 * (jl + jh))
            ge = count_le(jm) >= need
            return jnp.where(ge, jl, jm), jnp.where(ge, jm, jh)

        n_steps = int(np.ceil(np.log2(s_len))) + 1
        _, j_cut = lax.fori_loop(0, n_steps, step,
                                 (jnp.full((1, tq), -1.0, F32), jnp.full((1, tq), s_len - 1.0, F32)))

        def drop(kp, carry):
            k0 = pl.multiple_of(kp * (2 * tk), 2 * tk)
            sv = sc_ref[pl.ds(k0, 2 * tk), :]
            kpos = (k0 + row_iota2).astype(F32)
            sc_ref[pl.ds(k0, 2 * tk), :] = jnp.where(sv > hi, sv, jnp.where(kpos <= j_cut, sv, -jnp.inf))
            return carry

        lax.fori_loop(0, npair, drop, 0)

    m_ref[...] = jnp.full(m_ref.shape, NEG_BIG, F32)
    acc_ref[...] = jnp.zeros(acc_ref.shape, F32)

    def att_body(kb, carry):
        k0 = pl.multiple_of(kb * tk, tk)
        bias = jnp.where(sc_ref[pl.ds(k0, tk), :] > lo, 0.0, NEG_BIG)
        blk_max = []
        for h in range(H_A):
            kh = k_ref[pl.ds(k0, tk), h * HEAD_PAD:(h + 1) * HEAD_PAD]
            s = jnp.dot(kh, qt_ref[h * HEAD_PAD:(h + 1) * HEAD_PAD, :], preferred_element_type=F32) + bias
            s_ref[h] = s
            blk_max.append(jnp.max(s, axis=0, keepdims=True))
        for h in range(H_A):
            m_old = m_ref[h:h + 1, :]
            m_new = jnp.maximum(m_old, blk_max[h])
            alpha = jnp.exp2(m_old - m_new)
            p = jnp.exp2((s_ref[h] - m_new).astype(BF16))
            rows = slice(h * V_ROWS, (h + 1) * V_ROWS)
            pv = jnp.dot(vt_ref[rows, pl.ds(k0, tk)], p, preferred_element_type=F32)
            acc_ref[rows, :] = alpha * acc_ref[rows, :] + pv
            m_ref[h:h + 1, :] = m_new
        return carry

    lax.fori_loop(0, nkb, att_body, 0)
    out = [acc_ref[h * V_ROWS:h * V_ROWS + D_V_A, :] / acc_ref[h * V_ROWS + D_V_A:h * V_ROWS + D_V_A + 1, :]
           for h in range(H_A)]
    o_ref[...] = jnp.concatenate(out, axis=0).T.astype(BF16)


def _dsa_call(qt, qit, wit, k, ki, vt):
    b, hq, s = qt.shape
    tq = TQ_DSA
    topk = min(TOPK_MAX, s // 4)
    colsT = lambda height: pl.BlockSpec((None, height, tq), lambda i, j: (i, 0, j))
    whole = lambda a: pl.BlockSpec((None,) + a.shape[1:], lambda i, j: (i, 0, 0))
    return pl.pallas_call(
        functools.partial(_dsa_kernel, topk=topk),
        out_shape=jax.ShapeDtypeStruct((b, s, WIDTH_A), BF16),
        grid=(b, s // tq),
        in_specs=[colsT(hq), colsT(H_IDX), colsT(hq), whole(ki), whole(k), whole(vt)],
        out_specs=pl.BlockSpec((None, tq, WIDTH_A), lambda i, j: (i, j, 0)),
        scratch_shapes=[pltpu.VMEM((s, tq), F32),
                        pltpu.VMEM((H_A, tq), F32),
                        pltpu.VMEM((H_A * V_ROWS, tq), F32),
                        pltpu.VMEM((H_A, tq, tq), F32)],
        compiler_params=_params(2),
        name="dsa",
    )(qit, wit, qt, ki, k, vt)


def _dil_kernel(q_ref, kc_ref, kp_ref, vc_ref, vp_ref, o_ref, lse_ref):
    tb = q_ref.shape[0]
    has_prev = pl.program_id(2) > 0
    q = q_ref[...]
    kc, kp, vc, vp = kc_ref[...], kp_ref[...], vc_ref[...], vp_ref[...]
    r = lax.broadcasted_iota(jnp.int32, (tb, tb), 0)
    c = lax.broadcasted_iota(jnp.int32, (tb, tb), 1)
    mask_c = c <= r
    mask_p = jnp.logical_and(c >= r, has_prev)
    lane = lax.broadcasted_iota(jnp.int32, (1, WIDTH_B), 1)
    head_qk = (lane % 128) // (D_HEAD_B // 2)
    head_v = lane // D_HEAD_B
    nt = (((1,), (1,)), ((), ()))
    o = jnp.zeros((tb, WIDTH_B), F32)
    lse = jnp.zeros((tb, WIDTH_B), F32)
    for h in range(H_B):
        qh = jnp.where(head_qk == h, q, jnp.zeros_like(q))
        sc = jnp.where(mask_c, lax.dot_general(qh, kc, nt, preferred_element_type=F32), -jnp.inf)
        sp = jnp.where(mask_p, lax.dot_general(qh, kp, nt, preferred_element_type=F32), -jnp.inf)
        m = jnp.maximum(jnp.max(sc, axis=1, keepdims=True), jnp.max(sp, axis=1, keepdims=True))
        pc = jnp.exp(sc - m)
        pp = jnp.exp(sp - m)
        l = jnp.sum(pc, axis=1, keepdims=True) + jnp.sum(pp, axis=1, keepdims=True)
        oh = (jnp.dot(pc.astype(BF16), vc, preferred_element_type=F32)
              + jnp.dot(pp.astype(BF16), vp, preferred_element_type=F32))
        o = jnp.where(head_v == h, oh / l, o)
        lse = jnp.where(head_v == h, m + jnp.log(l), lse)
    o_ref[...] = o
    lse_ref[...] = lse


def _dil_call(qkv):
    b, dilation, n, _ = qkv.shape
    tb = TB_DIL

    def spec(which, prev):
        def index(i, r, j):
            return (i, r, jnp.maximum(j - 1, 0) if prev else j, which)
        return pl.BlockSpec((None, None, tb, WIDTH_B), index)

    out_spec = pl.BlockSpec((None, None, tb, WIDTH_B), lambda i, r, j: (i, r, j, 0))
    out = jax.ShapeDtypeStruct((b, dilation, n, WIDTH_B), F32)
    return pl.pallas_call(
        _dil_kernel,
        out_shape=(out, out),
        grid=(b, dilation, n // tb),
        in_specs=[spec(0, False), spec(1, False), spec(1, True), spec(2, False), spec(2, True)],
        out_specs=(out_spec, out_spec),
        compiler_params=_params(3),
        name="dil",
    )(qkv, qkv, qkv, qkv, qkv)


def _merge_kernel(x_ref, mod_ref, oa_ref, o0_ref, o1_ref, o2_ref, l0_ref, l1_ref, l2_ref, gates_ref,
                  wua_ref, wub_ref, wo_ref, gpost_ref, out_ref, *scr):
    tm = x_ref.shape[0]

    def token_major(ref, buf):
        dilation = ref.shape[0]
        if dilation == 1:
            return ref[0]
        n_slab = ref.shape[2] // LANES
        for r in range(dilation):
            for c in range(n_slab):
                buf[c, pl.ds(r, tm // dilation, stride=dilation), :] = ref[r, :, c * LANES:(c + 1) * LANES]
        return jnp.concatenate([buf[c] for c in range(n_slab)], axis=1)

    o0, l0 = token_major(o0_ref, None), token_major(l0_ref, None)
    o1, l1 = token_major(o1_ref, scr[0]), token_major(l1_ref, scr[1])
    o2, l2 = token_major(o2_ref, scr[2]), token_major(l2_ref, scr[3])
    m = jnp.maximum(jnp.maximum(l0, l1), l2)
    e0, e1, e2 = jnp.exp(l0 - m), jnp.exp(l1 - m), jnp.exp(l2 - m)
    ob = (e0 * o0 + e1 * o1 + e2 * o2) / (e0 + e1 + e2)
    za = jnp.dot(oa_ref[...], wua_ref[...], preferred_element_type=F32)
    zb = jnp.dot(ob.astype(BF16), wub_ref[...], preferred_element_type=F32)
    d = za.shape[1]
    z = gates_ref[:, 0:d].astype(F32) * za + gates_ref[:, d:2 * d].astype(F32) * zb
    y = jnp.dot(z.astype(BF16), wo_ref[...], preferred_element_type=F32)
    out_ref[...] = x_ref[...] + mod_ref[5:6, :] * (_rms_rows(y) * gpost_ref[...])


def _merge_call(x, mod, oa, dil, gates, w_up_a, w_up_b, w_o, g_post):
    b, s, d = x.shape
    tm = TM_MERGE
    rows = lambda width: pl.BlockSpec((None, tm, width), lambda i, j: (i, j, 0))
    const = lambda a: pl.BlockSpec(a.shape, lambda i, j: (0,) * a.ndim)
    (o0, l0), (o1, l1), (o2, l2) = dil
    res = lambda a: pl.BlockSpec((None, a.shape[1], tm // a.shape[1], WIDTH_B), lambda i, j: (i, 0, j, 0))
    return pl.pallas_call(
        _merge_kernel,
        out_shape=jax.ShapeDtypeStruct((b, s, d), F32),
        grid=(b, s // tm),
        in_specs=[rows(d), pl.BlockSpec((None, N_MOD, d), lambda i, j: (i, 0, 0)), rows(WIDTH_A)]
                 + [res(a) for a in (o0, o1, o2, l0, l1, l2)]
                 + [rows(2 * d), const(w_up_a), const(w_up_b), const(w_o), const(g_post)],
        out_specs=rows(d),
        scratch_shapes=[pltpu.VMEM((WIDTH_B // LANES, tm, LANES), F32)] * 4,
        compiler_params=_params(2),
        name="merge",
    )(x, mod, oa, o0, o1, o2, l0, l1, l2, gates, w_up_a, w_up_b, w_o, g_post)


def _pad_heads(w_t):
    h, r, n = w_t.shape
    return jnp.pad(w_t, ((0, 0), (0, HEAD_PAD - r), (0, 0))).reshape(h * HEAD_PAD, n)


def _mixer_weights(w_in, g_cq, g_ckv, w_uq, w_uk, w_uv, w_iq):
    d = w_in.shape[0]
    o_cq, o_ckv, o_kr, o_ki, o_wi = 0, Q_LORA, Q_LORA + KV_LORA, Q_LORA + KV_LORA + D_ROPE_A, \
        Q_LORA + KV_LORA + D_ROPE_A + D_IDX
    o_qkv = o_wi + H_IDX
    o_gates = o_qkv + N_QKV_B

    def swap_halves(w):
        half = w.shape[1] // 2
        return jnp.concatenate([w[:, half:], w[:, :half]], axis=1)

    w_kr, w_ki = w_in[:, o_kr:o_kr + D_ROPE_A], w_in[:, o_ki:o_ki + D_IDX]
    qkv = w_in[:, o_qkv:o_gates].reshape(d, 3, N_GROUPS_B, H_B, 2, D_HEAD_B // 2)
    qk_split = qkv[:, 0:2].transpose(0, 1, 2, 4, 3, 5).reshape(d, 2 * N_GROUPS_B * WIDTH_B)
    v_cols = qkv[:, 2].reshape(d, N_GROUPS_B * WIDTH_B)
    w_std = jnp.concatenate([w_in[:, o_ckv:o_ckv + KV_LORA], w_kr, swap_halves(w_kr), w_ki, swap_halves(w_ki),
                             qk_split, v_cols, w_in[:, o_gates:]], axis=1)
    w_tr = jnp.concatenate([w_in[:, o_cq:o_cq + Q_LORA], w_in[:, o_ckv:o_ckv + KV_LORA],
                            w_in[:, o_wi:o_wi + H_IDX]], axis=1).T
    w_tr = jnp.pad(w_tr, ((0, N_TR - w_tr.shape[0]), (0, 0)))

    e_kr = np.zeros((D_ROPE_A, H_A * HEAD_PAD), np.float32)
    for h in range(H_A):
        e_kr[np.arange(D_ROPE_A), h * HEAD_PAD + D_NOPE + np.arange(D_ROPE_A)] = 1.0
    e_ki = np.zeros((D_IDX, LANES), np.float32)
    e_ki[np.arange(D_IDX), np.arange(D_IDX)] = 1.0
    return {
        "w_std": w_std.astype(BF16),
        "w_tr": w_tr.astype(BF16),
        "g_cq_col": g_cq.reshape(Q_LORA, 1),
        "g_ckv_col": g_ckv.reshape(KV_LORA, 1),
        "g_ckv_row": g_ckv.reshape(1, KV_LORA),
        "w_uq_t": _pad_heads(w_uq.transpose(1, 2, 0)).astype(BF16),
        "w_iq_t": _pad_heads(w_iq.transpose(1, 2, 0)).astype(BF16),
        "w_uk_p": jnp.pad(w_uk, ((0, 0), (0, 0), (0, HEAD_PAD - D_NOPE))).reshape(KV_LORA, H_A * HEAD_PAD).astype(BF16),
        "w_uv_t": w_uv.transpose(1, 2, 0).reshape(WIDTH_A, KV_LORA).astype(BF16),
        "e_kr": jnp.asarray(e_kr, BF16),
        "e_ki": jnp.asarray(e_ki, BF16),
    }


def kernel(x, c, positions, w_mod, b_mod, g_pre_ffn1, w_gate1, w_up1, w_down1, g_post_ffn1, g_pre_mix, w_in, g_cq, g_ckv, w_uq, w_uk, w_uv, w_iq, w_up_a, w_up_b, w_o, g_post_mix, g_pre_ffn2, w_gate2, w_up2, w_down2, g_post_ffn2):
    b, s, d = x.shape
    assert s % (16 * TB_DIL) == 0 and s % TM_FFN == 0 and b <= 8
    c8 = jnp.pad(c, ((0, 8 - b), (0, 0)))
    half_b, half_a = D_HEAD_B // 2, D_ROPE_A // 2
    inv = jnp.concatenate([ROPE_THETA ** (-jnp.arange(half_b, dtype=F32) / half_b),
                           ROPE_THETA ** (-jnp.arange(half_a, dtype=F32) / half_a)]).reshape(48, 1)
    tabs = _rope_call(positions.reshape(b, 1, s), inv)
    row = lambda g: g.reshape(1, -1)
    for l in range(w_mod.shape[0]):
        mod = _mod_call(c8, w_mod[l], b_mod[l].reshape(1, -1))[:b].reshape(b, N_MOD, d)
        x = _ffn_call(x, mod, row(g_pre_ffn1[l]), w_gate1[l].astype(BF16), w_up1[l].astype(BF16),
                      w_down1[l].astype(BF16), row(g_post_ffn1[l]), 0)
        w = _mixer_weights(w_in[l], g_cq[l], g_ckv[l], w_uq[l], w_uk[l], w_uv[l], w_iq[l])
        qt, qit, wit, k, ki, vt, dq0, dq1, dq2, gates = _proj_call(x, mod, row(g_pre_mix[l]), w, tabs)
        oa = _dsa_call(qt, qit, wit, k, ki, vt)
        dil = [_dil_call(dq) for dq in (dq0, dq1, dq2)]
        x = _merge_call(x, mod, oa, dil, gates, w_up_a[l].astype(BF16), w_up_b[l].astype(BF16),
                        w_o[l].astype(BF16), row(g_post_mix[l]))
        x = _ffn_call(x, mod, row(g_pre_ffn2[l]), w_gate2[l].astype(BF16), w_up2[l].astype(BF16),
                      w_down2[l].astype(BF16), row(g_post_ffn2[l]), 6)
    return x
```

```python
import functools

import numpy as np
import jax
import jax.numpy as jnp
from jax import lax
from jax.experimental import pallas as pl
from jax.experimental.pallas import tpu as pltpu

F32 = jnp.float32
BF16 = jnp.bfloat16

D_FF = 2816
ROPE_THETA = 10000.0
NORM_EPS = 1e-6
H_A = 8
Q_LORA = 384
KV_LORA = 256
D_NOPE = 64
D_ROPE_A = 32
D_V_A = 64
H_IDX = 8
D_IDX = 32
TOPK_MAX = 256
DIL_PAIRS = ((128, 1), (512, 4), (2048, 16))
N_GROUPS_B = 3
H_B = 4
D_HEAD_B = 64
N_MOD = 9
WIDTH_A = H_A * D_V_A
WIDTH_B = H_B * D_HEAD_B
N_QKV_B = 3 * N_GROUPS_B * WIDTH_B

LANES = 128
HEAD_PAD = 128
VMEM_LIMIT = 56 * 1024 * 1024

TM_FFN = 512
CH_FFN = 256
TP_PROJ = 256
TQ_DSA = 256
V_ROWS = D_V_A + 16
TB_DIL = 128
TQ_DIL = 256
TM_MERGE = 256
BISECT_MAX = 34
BISECT_CHECK = 2
FOLD_ROWS = 32
NEG_BIG = -1e30
LOG2_E = 1.4426950408889634

C_CKV = 0
C_KR = 256
C_KR_SW = 288
C_KI = 320
C_KI_SW = 352
C_QKVB = 384
C_GATES = C_QKVB + N_QKV_B
N_STD = C_GATES + 2048
R_CQ = 0
R_CKV = Q_LORA
R_WI = Q_LORA + KV_LORA
N_TR = 768


def _params(n_axes):
    return pltpu.CompilerParams(dimension_semantics=("arbitrary",) * n_axes,
                                vmem_limit_bytes=VMEM_LIMIT)


def _rms_rows(x):
    return x * lax.rsqrt(jnp.mean(x * x, axis=-1, keepdims=True) + NORM_EPS)


def _mod_kernel(c_ref, w_ref, b_ref, o_ref):
    c = c_ref[...]
    o_ref[...] = jnp.dot(c * jax.nn.sigmoid(c), w_ref[...], preferred_element_type=F32) + b_ref[...]


def _mod_call(c8, w_mod, b_mod):
    d = c8.shape[1]
    return pl.pallas_call(
        _mod_kernel,
        out_shape=jax.ShapeDtypeStruct((8, N_MOD * d), F32),
        grid=(N_MOD,),
        in_specs=[pl.BlockSpec((8, d), lambda j: (0, 0)),
                  pl.BlockSpec((d, d), lambda j: (0, j)),
                  pl.BlockSpec((1, d), lambda j: (0, j))],
        out_specs=pl.BlockSpec((8, d), lambda j: (0, j)),
        compiler_params=_params(1),
        name="mod",
    )(c8, w_mod, b_mod)


def _rope_kernel(pos_ref, inv_ref, tt_ref, ca_ref, sa_ref, cb_ref, sb_ref):
    s = pos_ref.shape[1]
    ang = inv_ref[...] * pos_ref[...].astype(F32)
    cos = jnp.cos(ang)
    sin = jnp.sin(ang)
    cos_b, sin_b = cos[32:48], sin[32:48]
    tt_ref[0] = cos_b
    tt_ref[1] = sin_b
    cos_a = jnp.concatenate([cos[0:32]] * 4, axis=0)
    sin_a = jnp.concatenate([sin[0:32]] * 4, axis=0)
    cos_bt = jnp.concatenate([cos_b] * 8, axis=0)
    sin_bt = jnp.concatenate([-sin_b, sin_b] * 4, axis=0)
    for j in range(s // LANES):
        cols = slice(j * LANES, (j + 1) * LANES)
        ca_ref[cols, :] = cos_a[:, cols].T
        sa_ref[cols, :] = sin_a[:, cols].T
        cb_ref[cols, :] = cos_bt[:, cols].T
        sb_ref[cols, :] = sin_bt[:, cols].T


def _rope_call(pos3, inv):
    b, _, s = pos3.shape
    tab = jax.ShapeDtypeStruct((b, s, LANES), F32)
    tab_spec = pl.BlockSpec((None, s, LANES), lambda i: (i, 0, 0))
    return pl.pallas_call(
        _rope_kernel,
        out_shape=(jax.ShapeDtypeStruct((b, 2, 16, s), F32), tab, tab, tab, tab),
        grid=(b,),
        in_specs=[pl.BlockSpec((None, 1, s), lambda i: (i, 0, 0)),
                  pl.BlockSpec((48, 1), lambda i: (0, 0))],
        out_specs=(pl.BlockSpec((None, 2, 16, s), lambda i: (i, 0, 0, 0)),
                   tab_spec, tab_spec, tab_spec, tab_spec),
        compiler_params=_params(1),
        name="rope",
    )(pos3, inv)


def _ffn_kernel(x_ref, mod_ref, gpre_ref, wg_ref, wu_ref, wd_ref, gpost_ref, o_ref, acc_ref, *, j0):
    x = x_ref[...]
    sh, sc, gt = mod_ref[j0:j0 + 1, :], mod_ref[j0 + 1:j0 + 2, :], mod_ref[j0 + 2:j0 + 3, :]
    h = (_rms_rows(x) * gpre_ref[...] * (1.0 + sc) + sh).astype(BF16)
    n_ch = wg_ref.shape[1] // CH_FFN
    for ch in range(n_ch):
        cols = slice(ch * CH_FFN, (ch + 1) * CH_FFN)
        g = jnp.dot(h, wg_ref[:, cols], preferred_element_type=F32)
        u = jnp.dot(h, wu_ref[:, cols], preferred_element_type=F32)
        a = (g * jax.nn.sigmoid(g) * u).astype(BF16)
        part = jnp.dot(a, wd_ref[cols, :], preferred_element_type=F32)
        if ch == 0:
            acc_ref[...] = part
        else:
            acc_ref[...] += part
    o_ref[...] = x + 0.5 * gt * (_rms_rows(acc_ref[...]) * gpost_ref[...])


def _ffn_call(x, mod, g_pre, w_gate, w_up, w_down, g_post, j0):
    b, s, d = x.shape
    f = w_gate.shape[1]
    const = lambda shape: pl.BlockSpec(shape, lambda i, j: (0,) * len(shape))
    tile = pl.BlockSpec((None, TM_FFN, d), lambda i, j: (i, j, 0))
    return pl.pallas_call(
        functools.partial(_ffn_kernel, j0=j0),
        out_shape=jax.ShapeDtypeStruct((b, s, d), F32),
        grid=(b, s // TM_FFN),
        in_specs=[tile,
                  pl.BlockSpec((None, N_MOD, d), lambda i, j: (i, 0, 0)),
                  const((1, d)), const((d, f)), const((d, f)), const((f, d)), const((1, d))],
        out_specs=tile,
        scratch_shapes=[pltpu.VMEM((TM_FFN, d), F32)],
        compiler_params=_params(2),
        name="ffn",
    )(x, mod, g_pre, w_gate, w_up, w_down, g_post)


def _proj_kernel(x_ref, mod_ref, gpre_ref, wstd_ref, wtr_ref, gcq_ref, gckv_col_ref, gckv_row_ref,
                 wuq_ref, wiq_ref, wuk_ref, wuv_ref, ekr_ref, eki_ref,
                 tt_ref, ca_ref, sa_ref, cb_ref, sb_ref,
                 qt_ref, qit_ref, wit_ref, k_ref, ki_ref, vt_ref, dq0_ref, dq1_ref, dq2_ref, gates_ref,
                 dil_scr):
    x = x_ref[...]
    sh, sc = mod_ref[3:4, :], mod_ref[4:5, :]
    u = (_rms_rows(x) * gpre_ref[...] * (1.0 + sc) + sh).astype(BF16)
    std = jnp.dot(u, wstd_ref[...], preferred_element_type=F32)
    tr = lax.dot_general(wtr_ref[...], u, (((1,), (1,)), ((), ())),
                         preferred_element_type=F32)

    cq = tr[R_CQ:R_CQ + Q_LORA]
    cq = cq * lax.rsqrt(jnp.mean(cq * cq, axis=0, keepdims=True) + NORM_EPS) * gcq_ref[...]
    cq = cq.astype(BF16)
    ckv_t = tr[R_CKV:R_CKV + KV_LORA]
    ckv_t = ckv_t * lax.rsqrt(jnp.mean(ckv_t * ckv_t, axis=0, keepdims=True) + NORM_EPS) * gckv_col_ref[...]
    cos_t, sin_t = tt_ref[0], tt_ref[1]

    attn_scale = (D_NOPE + D_ROPE_A) ** -0.5 * LOG2_E
    q_t = jnp.dot(wuq_ref[...], cq, preferred_element_type=F32) * attn_scale
    idx_scale = D_IDX ** -0.5
    qi_t = jnp.dot(wiq_ref[...], cq, preferred_element_type=F32) * idx_scale
    for h in range(H_A):
        r0 = h * HEAD_PAD
        x1, x2 = q_t[r0 + 64:r0 + 80], q_t[r0 + 80:r0 + 96]
        qt_ref[r0:r0 + 64, :] = q_t[r0:r0 + 64].astype(BF16)
        qt_ref[r0 + 64:r0 + 80, :] = (x1 * cos_t - x2 * sin_t).astype(BF16)
        qt_ref[r0 + 80:r0 + 96, :] = (x2 * cos_t + x1 * sin_t).astype(BF16)
        qt_ref[r0 + 96:r0 + 128, :] = q_t[r0 + 96:r0 + 128].astype(BF16)
        y1, y2 = qi_t[r0:r0 + 16], qi_t[r0 + 16:r0 + 32]
        qit_ref[r0:r0 + 16, :] = (y1 * cos_t - y2 * sin_t).astype(BF16)
        qit_ref[r0 + 16:r0 + 32, :] = (y2 * cos_t + y1 * sin_t).astype(BF16)
        qit_ref[r0 + 32:r0 + 128, :] = qi_t[r0 + 32:r0 + 128].astype(BF16)
    wit_ref[...] = tr[R_WI:R_WI + H_IDX] * (H_IDX ** -0.5)
    v_t = jnp.dot(wuv_ref[...], ckv_t.astype(BF16), preferred_element_type=F32).astype(BF16)
    for h in range(H_A):
        vt_ref[h * V_ROWS:h * V_ROWS + D_V_A, :] = v_t[h * D_V_A:(h + 1) * D_V_A]
        vt_ref[h * V_ROWS + D_V_A:(h + 1) * V_ROWS, :] = jnp.ones((V_ROWS - D_V_A, v_t.shape[1]), BF16)

    ckv = std[:, C_CKV:C_CKV + KV_LORA]
    ckv = (_rms_rows(ckv) * gckv_row_ref[...]).astype(BF16)
    k_nope = jnp.dot(ckv, wuk_ref[...], preferred_element_type=F32)
    cos_k, sin_k = cb_ref[:, 0:32], sb_ref[:, 0:32]
    k_rope = std[:, C_KR:C_KR + 32] * cos_k + std[:, C_KR_SW:C_KR_SW + 32] * sin_k
    k_ref[...] = (k_nope + jnp.dot(k_rope.astype(BF16), ekr_ref[...],
                                   preferred_element_type=F32)).astype(BF16)
    k_idx = std[:, C_KI:C_KI + 32] * cos_k + std[:, C_KI_SW:C_KI_SW + 32] * sin_k
    ki_ref[...] = jnp.dot(k_idx.astype(BF16), eki_ref[...], preferred_element_type=F32).astype(BF16)

    cos_a, sin_a = ca_ref[...], sa_ref[...]
    tp = x.shape[0]
    for g, (out_ref, (_, dilation)) in enumerate(zip((dq0_ref, dq1_ref, dq2_ref), DIL_PAIRS)):
        for which in range(2):
            scale = (D_HEAD_B ** -0.5) if which == 0 else 1.0
            c0 = C_QKVB + (which * N_GROUPS_B + g) * WIDTH_B
            x1, x2 = std[:, c0:c0 + 128], std[:, c0 + 128:c0 + 256]
            dil_scr[2 * which] = (x1 * cos_a - x2 * sin_a) * scale
            dil_scr[2 * which + 1] = (x2 * cos_a + x1 * sin_a) * scale
        c0 = C_QKVB + (2 * N_GROUPS_B + g) * WIDTH_B
        dil_scr[4] = std[:, c0:c0 + 128]
        dil_scr[5] = std[:, c0 + 128:c0 + 256]
        for r in range(dilation):
            for c in range(3 * WIDTH_B // LANES):
                out_ref[r, :, c * LANES:(c + 1) * LANES] = (
                    dil_scr[c, pl.ds(r, tp // dilation, stride=dilation), :].astype(BF16))
    gates_ref[...] = jax.nn.sigmoid(std[:, C_GATES:]).astype(BF16)


def _proj_call(x, mod, g_pre, w, tabs):
    b, s, d = x.shape
    tt, ca, sa, cb, sb = tabs
    tp = TP_PROJ
    const = lambda a: pl.BlockSpec(a.shape, lambda i, j: (0,) * a.ndim)
    rows = lambda width: pl.BlockSpec((None, tp, width), lambda i, j: (i, j, 0))
    colsT = lambda height: pl.BlockSpec((None, height, tp), lambda i, j: (i, 0, j))
    consts = [g_pre, w["w_std"], w["w_tr"], w["g_cq_col"], w["g_ckv_col"], w["g_ckv_row"],
              w["w_uq_t"], w["w_iq_t"], w["w_uk_p"], w["w_uv_t"], w["e_kr"], w["e_ki"]]
    hq = H_A * HEAD_PAD
    dil_shapes = [jax.ShapeDtypeStruct((b, dil, s // dil, 3 * WIDTH_B), BF16) for _, dil in DIL_PAIRS]
    dil_specs = [pl.BlockSpec((None, dil, tp // dil, 3 * WIDTH_B), lambda i, j: (i, 0, j, 0))
                 for _, dil in DIL_PAIRS]
    return pl.pallas_call(
        _proj_kernel,
        out_shape=(jax.ShapeDtypeStruct((b, hq, s), BF16),
                   jax.ShapeDtypeStruct((b, hq, s), BF16),
                   jax.ShapeDtypeStruct((b, H_IDX, s), F32),
                   jax.ShapeDtypeStruct((b, s, hq), BF16),
                   jax.ShapeDtypeStruct((b, s, LANES), BF16),
                   jax.ShapeDtypeStruct((b, H_A * V_ROWS, s), BF16),
                   *dil_shapes,
                   jax.ShapeDtypeStruct((b, s, 2048), BF16)),
        grid=(b, s // tp),
        in_specs=[rows(d), pl.BlockSpec((None, N_MOD, d), lambda i, j: (i, 0, 0))]
                 + [const(a) for a in consts]
                 + [pl.BlockSpec((None, 2, 16, tp), lambda i, j: (i, 0, 0, j)),
                    rows(LANES), rows(LANES), rows(LANES), rows(LANES)],
        out_specs=(colsT(hq), colsT(hq), colsT(H_IDX), rows(hq), rows(LANES), colsT(H_A * V_ROWS),
                   *dil_specs, rows(2048)),
        scratch_shapes=[pltpu.VMEM((3 * WIDTH_B // LANES, tp, LANES), F32)],
        compiler_params=_params(2),
        name="proj",
    )(x, mod, *consts, tt, ca, sa, cb, sb)


def _dsa_kernel(qit_ref, wit_ref, qt_ref, ki_ref, k_ref, vt_ref, o_ref,
                sc_ref, m_ref, acc_ref, s_ref, s2_ref, smax_ref, smax2_ref, *, topk):
    tq = qt_ref.shape[1]
    tk = tq
    i = pl.program_id(1)
    nkb = i + 1
    npair = (nkb + 1) // 2
    qpos = i * tq + lax.broadcasted_iota(jnp.int32, (1, tq), 1)
    row_iota = lax.broadcasted_iota(jnp.int32, (tk, tq), 0)

    def fold(v):
        return jnp.sum(v.reshape(v.shape[0] // FOLD_ROWS, FOLD_ROWS, tq), axis=0)

    def idx_body(kb, carry):
        smax, smin, neg_max, c_pos, c_nn = carry
        k0 = pl.multiple_of(kb * tk, tk)
        ki = ki_ref[pl.ds(k0, tk), :]
        acc = jnp.zeros((tk, tq), F32)
        for h in range(H_IDX):
            lg = jnp.dot(ki, qit_ref[h * HEAD_PAD:(h + 1) * HEAD_PAD, :], preferred_element_type=F32)
            acc = acc + wit_ref[h:h + 1, :] * jnp.maximum(lg, 0.0)
        causal = (k0 + row_iota) <= qpos
        sv = jnp.where(causal, acc, -jnp.inf)
        sc_ref[pl.ds(k0, tk), :] = sv
        smax = jnp.maximum(smax, jnp.max(sv, axis=0, keepdims=True))
        smin = jnp.minimum(smin, jnp.min(jnp.where(causal, acc, jnp.inf), axis=0, keepdims=True))
        neg_max = jnp.maximum(neg_max, jnp.max(jnp.where(sv < 0.0, sv, -jnp.inf), axis=0, keepdims=True))
        c_pos = c_pos + fold(jnp.where(sv > 0.0, 1.0, 0.0))
        c_nn = c_nn + fold(jnp.where(sv >= 0.0, 1.0, 0.0))
        return smax, smin, neg_max, c_pos, c_nn

    neg_inf_row = jnp.full((1, tq), -jnp.inf, F32)
    zero_part = jnp.zeros((FOLD_ROWS, tq), F32)
    smax, smin, neg_max, c_pos, c_nn = lax.fori_loop(
        0, nkb, idx_body, (neg_inf_row, jnp.full((1, tq), jnp.inf, F32), neg_inf_row, zero_part, zero_part))
    c_pos = jnp.sum(c_pos, axis=0, keepdims=True)
    c_nn = jnp.sum(c_nn, axis=0, keepdims=True)

    @pl.when(nkb % 2 == 1)
    def _pad_block():
        sc_ref[pl.ds(pl.multiple_of(nkb * tk, tk), tk), :] = jnp.full((tk, tq), -jnp.inf, F32)

    n_causal = (qpos + 1).astype(F32)
    kf = jnp.minimum(qpos + 1, topk).astype(F32)

    def count_pairs(indicator):
        def body(kp, c):
            k0 = pl.multiple_of(kp * (2 * tk), 2 * tk)
            return c + fold(indicator(sc_ref[pl.ds(k0, 2 * tk), :], k0))
        part = lax.fori_loop(0, npair, body, jnp.zeros((FOLD_ROWS, tq), F32))
        return jnp.sum(part, axis=0, keepdims=True)

    def count_gt(thr):
        return count_pairs(lambda sv, k0: jnp.where(sv > thr, 1.0, 0.0))

    below = smin - (1.0 + jnp.abs(smin))
    zero_tie = jnp.logical_and(c_pos < kf, kf <= c_nn)
    searching = jnp.where(zero_tie, 0.0, 1.0)

    def bisect_cond(carry):
        return jnp.logical_and(carry[0] < BISECT_MAX, carry[1] > 0)

    def bisect_body(carry):
        it, _, lo, hi, c_lo, c_hi = carry
        for _ in range(BISECT_CHECK):
            mid = 0.5 * (lo + hi)
            c_mid = count_gt(mid)
            ge = c_mid >= kf
            up = jnp.logical_and(ge, searching > 0.0)
            down = jnp.logical_and(jnp.logical_not(ge), searching > 0.0)
            lo, c_lo = jnp.where(up, mid, lo), jnp.where(up, c_mid, c_lo)
            hi, c_hi = jnp.where(down, mid, hi), jnp.where(down, c_mid, c_hi)
        open_queries = (jnp.max((c_lo - kf) * searching) > 0.0).astype(jnp.int32)
        return it + BISECT_CHECK, open_queries, lo, hi, c_lo, c_hi

    c_lo0 = jnp.where(zero_tie, c_nn, n_causal)
    first_open = (jnp.max((c_lo0 - kf) * searching) > 0.0).astype(jnp.int32)
    _, _, lo, hi, c_lo, c_hi = lax.while_loop(
        bisect_cond, bisect_body,
        (jnp.int32(0), first_open,
         jnp.where(zero_tie, jnp.maximum(neg_max, below), below), jnp.where(zero_tie, 0.0, smax),
         c_lo0, jnp.where(zero_tie, c_pos, 0.0)))

    @pl.when(jnp.max(c_lo - kf) > 0.0)
    def _break_ties():
        need = kf - c_hi
        prefix = jnp.where(lax.broadcasted_iota(jnp.int32, (tk, tk), 0)
                           >= lax.broadcasted_iota(jnp.int32, (tk, tk), 1), 1.0, 0.0).astype(BF16)

        def body(kb, seen):
            k0 = pl.multiple_of(kb * tk, tk)
            sv = sc_ref[pl.ds(k0, tk), :]
            member = jnp.where(sv > lo, jnp.where(sv > hi, 0.0, 1.0), 0.0)
            rank = jnp.dot(prefix, member.astype(BF16), preferred_element_type=F32) + seen
            sc_ref[pl.ds(k0, tk), :] = jnp.where(member * rank > need, -jnp.inf, sv)
            return rank[tk - 1:tk, :]

        lax.fori_loop(0, nkb, body, jnp.zeros((1, tq), F32))

    m_ref[...] = jnp.full(m_ref.shape, NEG_BIG, F32)
    acc_ref[...] = jnp.zeros(acc_ref.shape, F32)

    def scores_into(kb, park_ref, max_ref):
        k0 = pl.multiple_of(kb * tk, tk)
        bias = jnp.where(sc_ref[pl.ds(k0, tk), :] > lo, 0.0, NEG_BIG)
        for h in range(H_A):
            kh = k_ref[pl.ds(k0, tk), h * HEAD_PAD:(h + 1) * HEAD_PAD]
            s = jnp.dot(kh, qt_ref[h * HEAD_PAD:(h + 1) * HEAD_PAD, :], preferred_element_type=F32) + bias
            park_ref[h] = s
            max_ref[h:h + 1, :] = jnp.max(s, axis=0, keepdims=True)

    def softmax_from(kb, park_ref, max_ref):
        k0 = pl.multiple_of(kb * tk, tk)
        for h in range(H_A):
            m_old = m_ref[h:h + 1, :]
            m_new = jnp.maximum(m_old, max_ref[h:h + 1, :])
            alpha = jnp.exp2(m_old - m_new)
            p = jnp.exp2((park_ref[h] - m_new).astype(BF16))
            rows = slice(h * V_ROWS, (h + 1) * V_ROWS)
            pv = jnp.dot(vt_ref[rows, pl.ds(k0, tk)], p, preferred_element_type=F32)
            acc_ref[rows, :] = alpha * acc_ref[rows, :] + pv
            m_ref[h:h + 1, :] = m_new

    last_block = 2 * npair - 1
    scores_into(0, s_ref, smax_ref)

    def att_body(kp, carry):
        scores_into(2 * kp + 1, s2_ref, smax2_ref)
        softmax_from(2 * kp, s_ref, smax_ref)
        scores_into(jnp.minimum(2 * kp + 2, last_block), s_ref, smax_ref)
        softmax_from(2 * kp + 1, s2_ref, smax2_ref)
        return carry

    lax.fori_loop(0, npair, att_body, 0)
    out = [acc_ref[h * V_ROWS:h * V_ROWS + D_V_A, :] / acc_ref[h * V_ROWS + D_V_A:h * V_ROWS + D_V_A + 1, :]
           for h in range(H_A)]
    o_ref[...] = jnp.concatenate(out, axis=0).T.astype(BF16)


def _dsa_call(qt, qit, wit, k, ki, vt):
    b, hq, s = qt.shape
    tq = TQ_DSA
    topk = min(TOPK_MAX, s // 4)
    colsT = lambda height: pl.BlockSpec((None, height, tq), lambda i, j: (i, 0, j))
    whole = lambda a: pl.BlockSpec((None,) + a.shape[1:], lambda i, j: (i, 0, 0))
    return pl.pallas_call(
        functools.partial(_dsa_kernel, topk=topk),
        out_shape=jax.ShapeDtypeStruct((b, s, WIDTH_A), BF16),
        grid=(b, s // tq),
        in_specs=[colsT(hq), colsT(H_IDX), colsT(hq), whole(ki), whole(k), whole(vt)],
        out_specs=pl.BlockSpec((None, tq, WIDTH_A), lambda i, j: (i, j, 0)),
        scratch_shapes=[pltpu.VMEM((s, tq), F32),
                        pltpu.VMEM((H_A, tq), F32),
                        pltpu.VMEM((H_A * V_ROWS, tq), F32),
                        pltpu.VMEM((H_A, tq, tq), F32),
                        pltpu.VMEM((H_A, tq, tq), F32),
                        pltpu.VMEM((H_A, tq), F32),
                        pltpu.VMEM((H_A, tq), F32)],
        compiler_params=_params(2),
        name="dsa",
    )(qit, wit, qt, ki, k, vt)


def _dil_kernel(q_ref, kc_ref, kp_ref, vc_ref, vp_ref, o_ref, lse_ref, s_ref):
    tb = kp_ref.shape[0]
    n_sub = q_ref.shape[0] // tb
    first_step = pl.program_id(2) == 0
    k = jnp.concatenate([kp_ref[...], kc_ref[...]], axis=0)
    v = jnp.concatenate([vp_ref[...], vc_ref[...]], axis=0)
    r = lax.broadcasted_iota(jnp.int32, (tb, 2 * tb), 0)
    c = lax.broadcasted_iota(jnp.int32, (tb, 2 * tb), 1)
    behind = (tb + r - c).astype(jnp.uint32)
    bias = jnp.where(behind <= tb, 0.0, -jnp.inf)
    no_prev = jnp.where(first_step, r, tb).astype(jnp.uint32)
    bias0 = jnp.where(behind <= no_prev, 0.0, -jnp.inf)
    lane = lax.broadcasted_iota(jnp.int32, (1, WIDTH_B), 1)
    head_qk = (lane % 128) // (D_HEAD_B // 2)
    head_v = lane // D_HEAD_B
    nt = (((1,), (1,)), ((), ()))
    for j in range(n_sub):
        q = q_ref[j * tb:(j + 1) * tb, :]
        for h in range(H_B):
            qh = jnp.where(head_qk == h, q, jnp.zeros_like(q))
            s_ref[j * H_B + h] = (lax.dot_general(qh, k[j * tb:(j + 2) * tb], nt, preferred_element_type=F32)
                                  + (bias0 if j == 0 else bias))
    for j in range(n_sub):
        vj = v[j * tb:(j + 2) * tb]
        o = jnp.zeros((tb, WIDTH_B), F32)
        lse = jnp.zeros((tb, WIDTH_B), F32)
        for h in range(H_B):
            s = s_ref[j * H_B + h]
            m = jnp.max(s, axis=1, keepdims=True)
            p = jnp.exp(s - m).astype(BF16)
            pv = jnp.dot(p, jnp.where(head_v == h, vj, jnp.ones_like(vj)), preferred_element_type=F32)
            l = pltpu.roll(pv, WIDTH_B // 2, axis=1)
            o = jnp.where(head_v == h, pv / l, o)
            lse = jnp.where(head_v == h, m + jnp.log(l), lse)
        o_ref[j * tb:(j + 1) * tb, :] = o
        lse_ref[j * tb:(j + 1) * tb, :] = lse


def _dil_call(qkv):
    b, dilation, n, _ = qkv.shape
    tb, tq = TB_DIL, min(TQ_DIL, n)
    per = tq // tb
    cur = lambda which: pl.BlockSpec((None, None, tq, WIDTH_B), lambda i, r, j: (i, r, j, which))
    prev = lambda which: pl.BlockSpec((None, None, tb, WIDTH_B),
                                      lambda i, r, j: (i, r, jnp.maximum(j * per - 1, 0), which))
    out_spec = pl.BlockSpec((None, None, tq, WIDTH_B), lambda i, r, j: (i, r, j, 0))
    out = jax.ShapeDtypeStruct((b, dilation, n, WIDTH_B), F32)
    return pl.pallas_call(
        _dil_kernel,
        out_shape=(out, out),
        grid=(b, dilation, n // tq),
        in_specs=[cur(0), cur(1), prev(1), cur(2), prev(2)],
        out_specs=(out_spec, out_spec),
        scratch_shapes=[pltpu.VMEM((per * H_B, tb, 2 * tb), F32)],
        compiler_params=_params(3),
        name="dil",
    )(qkv, qkv, qkv, qkv, qkv)


def _merge_kernel(x_ref, mod_ref, oa_ref, o0_ref, o1_ref, o2_ref, l0_ref, l1_ref, l2_ref, gates_ref,
                  wua_ref, wub_ref, wo_ref, gpost_ref, out_ref, *scr):
    tm = x_ref.shape[0]

    def token_major(ref, buf):
        dilation = ref.shape[0]
        if dilation == 1:
            return ref[0]
        n_slab = ref.shape[2] // LANES
        for r in range(dilation):
            for c in range(n_slab):
                buf[c, pl.ds(r, tm // dilation, stride=dilation), :] = ref[r, :, c * LANES:(c + 1) * LANES]
        return jnp.concatenate([buf[c] for c in range(n_slab)], axis=1)

    o0, l0 = token_major(o0_ref, None), token_major(l0_ref, None)
    o1, l1 = token_major(o1_ref, scr[0]), token_major(l1_ref, scr[1])
    o2, l2 = token_major(o2_ref, scr[2]), token_major(l2_ref, scr[3])
    m = jnp.maximum(jnp.maximum(l0, l1), l2)
    e0, e1, e2 = jnp.exp(l0 - m), jnp.exp(l1 - m), jnp.exp(l2 - m)
    ob = (e0 * o0 + e1 * o1 + e2 * o2) / (e0 + e1 + e2)
    za = jnp.dot(oa_ref[...], wua_ref[...], preferred_element_type=F32)
    zb = jnp.dot(ob.astype(BF16), wub_ref[...], preferred_element_type=F32)
    d = za.shape[1]
    z = gates_ref[:, 0:d].astype(F32) * za + gates_ref[:, d:2 * d].astype(F32) * zb
    y = jnp.dot(z.astype(BF16), wo_ref[...], preferred_element_type=F32)
    out_ref[...] = x_ref[...] + mod_ref[5:6, :] * (_rms_rows(y) * gpost_ref[...])


def _merge_call(x, mod, oa, dil, gates, w_up_a, w_up_b, w_o, g_post):
    b, s, d = x.shape
    tm = TM_MERGE
    rows = lambda width: pl.BlockSpec((None, tm, width), lambda i, j: (i, j, 0))
    const = lambda a: pl.BlockSpec(a.shape, lambda i, j: (0,) * a.ndim)
    (o0, l0), (o1, l1), (o2, l2) = dil
    res = lambda a: pl.BlockSpec((None, a.shape[1], tm // a.shape[1], WIDTH_B), lambda i, j: (i, 0, j, 0))
    return pl.pallas_call(
        _merge_kernel,
        out_shape=jax.ShapeDtypeStruct((b, s, d), F32),
        grid=(b, s // tm),
        in_specs=[rows(d), pl.BlockSpec((None, N_MOD, d), lambda i, j: (i, 0, 0)), rows(WIDTH_A)]
                 + [res(a) for a in (o0, o1, o2, l0, l1, l2)]
                 + [rows(2 * d), const(w_up_a), const(w_up_b), const(w_o), const(g_post)],
        out_specs=rows(d),
        scratch_shapes=[pltpu.VMEM((WIDTH_B // LANES, tm, LANES), F32)] * 4,
        compiler_params=_params(2),
        name="merge",
    )(x, mod, oa, o0, o1, o2, l0, l1, l2, gates, w_up_a, w_up_b, w_o, g_post)


def _pad_heads(w_t):
    h, r, n = w_t.shape
    return jnp.pad(w_t, ((0, 0), (0, HEAD_PAD - r), (0, 0))).reshape(h * HEAD_PAD, n)


def _mixer_weights(w_in, g_cq, g_ckv, w_uq, w_uk, w_uv, w_iq):
    d = w_in.shape[0]
    o_cq, o_ckv, o_kr, o_ki, o_wi = 0, Q_LORA, Q_LORA + KV_LORA, Q_LORA + KV_LORA + D_ROPE_A, \
        Q_LORA + KV_LORA + D_ROPE_A + D_IDX
    o_qkv = o_wi + H_IDX
    o_gates = o_qkv + N_QKV_B

    def swap_halves(w):
        half = w.shape[1] // 2
        return jnp.concatenate([w[:, half:], w[:, :half]], axis=1)

    w_kr, w_ki = w_in[:, o_kr:o_kr + D_ROPE_A], w_in[:, o_ki:o_ki + D_IDX]
    qkv = w_in[:, o_qkv:o_gates].reshape(d, 3, N_GROUPS_B, H_B, 2, D_HEAD_B // 2)
    qk_split = qkv[:, 0:2].transpose(0, 1, 2, 4, 3, 5).reshape(d, 2 * N_GROUPS_B * WIDTH_B)
    v_cols = qkv[:, 2].reshape(d, N_GROUPS_B * WIDTH_B)
    w_std = jnp.concatenate([w_in[:, o_ckv:o_ckv + KV_LORA], w_kr, swap_halves(w_kr), w_ki, swap_halves(w_ki),
                             qk_split, v_cols, w_in[:, o_gates:]], axis=1)
    w_tr = jnp.concatenate([w_in[:, o_cq:o_cq + Q_LORA], w_in[:, o_ckv:o_ckv + KV_LORA],
                            w_in[:, o_wi:o_wi + H_IDX]], axis=1).T
    w_tr = jnp.pad(w_tr, ((0, N_TR - w_tr.shape[0]), (0, 0)))

    e_kr = np.zeros((D_ROPE_A, H_A * HEAD_PAD), np.float32)
    for h in range(H_A):
        e_kr[np.arange(D_ROPE_A), h * HEAD_PAD + D_NOPE + np.arange(D_ROPE_A)] = 1.0
    e_ki = np.zeros((D_IDX, LANES), np.float32)
    e_ki[np.arange(D_IDX), np.arange(D_IDX)] = 1.0
    return {
        "w_std": w_std.astype(BF16),
        "w_tr": w_tr.astype(BF16),
        "g_cq_col": g_cq.reshape(Q_LORA, 1),
        "g_ckv_col": g_ckv.reshape(KV_LORA, 1),
        "g_ckv_row": g_ckv.reshape(1, KV_LORA),
        "w_uq_t": _pad_heads(w_uq.transpose(1, 2, 0)).astype(BF16),
        "w_iq_t": _pad_heads(w_iq.transpose(1, 2, 0)).astype(BF16),
        "w_uk_p": jnp.pad(w_uk, ((0, 0), (0, 0), (0, HEAD_PAD - D_NOPE))).reshape(KV_LORA, H_A * HEAD_PAD).astype(BF16),
        "w_uv_t": w_uv.transpose(1, 2, 0).reshape(WIDTH_A, KV_LORA).astype(BF16),
        "e_kr": jnp.asarray(e_kr, BF16),
        "e_ki": jnp.asarray(e_ki, BF16),
    }


def kernel(x, c, positions, w_mod, b_mod, g_pre_ffn1, w_gate1, w_up1, w_down1, g_post_ffn1, g_pre_mix, w_in, g_cq, g_ckv, w_uq, w_uk, w_uv, w_iq, w_up_a, w_up_b, w_o, g_post_mix, g_pre_ffn2, w_gate2, w_up2, w_down2, g_post_ffn2):
    b, s, d = x.shape
    assert s % (16 * TB_DIL) == 0 and s % TM_FFN == 0 and b <= 8
    c8 = jnp.pad(c, ((0, 8 - b), (0, 0)))
    half_b, half_a = D_HEAD_B // 2, D_ROPE_A // 2
    inv = jnp.concatenate([ROPE_THETA ** (-jnp.arange(half_b, dtype=F32) / half_b),
                           ROPE_THETA ** (-jnp.arange(half_a, dtype=F32) / half_a)]).reshape(48, 1)
    tabs = _rope_call(positions.reshape(b, 1, s), inv)
    row = lambda g: g.reshape(1, -1)
    for l in range(w_mod.shape[0]):
        mod = _mod_call(c8, w_mod[l], b_mod[l].reshape(1, -1))[:b].reshape(b, N_MOD, d)
        x = _ffn_call(x, mod, row(g_pre_ffn1[l]), w_gate1[l].astype(BF16), w_up1[l].astype(BF16),
                      w_down1[l].astype(BF16), row(g_post_ffn1[l]), 0)
        w = _mixer_weights(w_in[l], g_cq[l], g_ckv[l], w_uq[l], w_uk[l], w_uv[l], w_iq[l])
        qt, qit, wit, k, ki, vt, dq0, dq1, dq2, gates = _proj_call(x, mod, row(g_pre_mix[l]), w, tabs)
        oa = _dsa_call(qt, qit, wit, k, ki, vt)
        dil = [_dil_call(dq) for dq in (dq0, dq1, dq2)]
        x = _merge_call(x, mod, oa, dil, gates, w_up_a[l].astype(BF16), w_up_b[l].astype(BF16),
                        w_o[l].astype(BF16), row(g_post_mix[l]))
        x = _ffn_call(x, mod, row(g_pre_ffn2[l]), w_gate2[l].astype(BF16), w_up2[l].astype(BF16),
                      w_down2[l].astype(BF16), row(g_post_ffn2[l]), 6)
    return x
```

```python
import functools

import numpy as np
import jax
import jax.numpy as jnp
from jax import lax
from jax.experimental import pallas as pl
from jax.experimental.pallas import tpu as pltpu

F32 = jnp.float32
BF16 = jnp.bfloat16

D_FF = 2816
ROPE_THETA = 10000.0
NORM_EPS = 1e-6
H_A = 8
Q_LORA = 384
KV_LORA = 256
D_NOPE = 64
D_ROPE_A = 32
D_V_A = 64
H_IDX = 8
D_IDX = 32
TOPK_MAX = 256
DIL_PAIRS = ((128, 1), (512, 4), (2048, 16))
N_GROUPS_B = 3
H_B = 4
D_HEAD_B = 64
N_MOD = 9
WIDTH_A = H_A * D_V_A
WIDTH_B = H_B * D_HEAD_B
N_QKV_B = 3 * N_GROUPS_B * WIDTH_B

LANES = 128
HEAD_PAD = 128
VMEM_LIMIT = 56 * 1024 * 1024

TM_FFN = 512
CH_FFN = 256
TP_PROJ = 512
TQ_DSA = 512
TK_DSA = 256
V_ROWS = D_V_A + 16
TB_DIL = 128
TQ_DIL = 512
TM_MERGE = 256
BISECT_MAX = 34
BISECT_CHECK = 2
FOLD_ROWS = 32
NEG_BIG = -1e30
LOG2_E = 1.4426950408889634

C_CKV = 0
C_KR = 256
C_KR_SW = 288
C_KI = 320
C_KI_SW = 352
C_QKVB = 384
C_GATES = C_QKVB + N_QKV_B
N_STD = C_GATES + 2048
R_CQ = 0
R_CKV = Q_LORA
R_WI = Q_LORA + KV_LORA
N_TR = 768


def _params(n_axes):
    return pltpu.CompilerParams(dimension_semantics=("arbitrary",) * n_axes,
                                vmem_limit_bytes=VMEM_LIMIT)


def _rms_rows(x):
    return x * lax.rsqrt(jnp.mean(x * x, axis=-1, keepdims=True) + NORM_EPS)


def _mod_kernel(c_ref, w_ref, b_ref, o_ref):
    c = c_ref[...]
    o_ref[...] = jnp.dot(c * jax.nn.sigmoid(c), w_ref[...], preferred_element_type=F32) + b_ref[...]


def _mod_call(c8, w_mod, b_mod):
    d = c8.shape[1]
    return pl.pallas_call(
        _mod_kernel,
        out_shape=jax.ShapeDtypeStruct((8, N_MOD * d), F32),
        grid=(N_MOD,),
        in_specs=[pl.BlockSpec((8, d), lambda j: (0, 0)),
                  pl.BlockSpec((d, d), lambda j: (0, j)),
                  pl.BlockSpec((1, d), lambda j: (0, j))],
        out_specs=pl.BlockSpec((8, d), lambda j: (0, j)),
        compiler_params=_params(1),
        name="mod",
    )(c8, w_mod, b_mod)


def _rope_kernel(pos_ref, inv_ref, tt_ref, ca_ref, sa_ref, cb_ref, sb_ref):
    s = pos_ref.shape[1]
    ang = inv_ref[...] * pos_ref[...].astype(F32)
    cos = jnp.cos(ang)
    sin = jnp.sin(ang)
    cos_b, sin_b = cos[32:48], sin[32:48]
    tt_ref[0] = cos_b
    tt_ref[1] = sin_b
    cos_a = jnp.concatenate([cos[0:32]] * 4, axis=0)
    sin_a = jnp.concatenate([sin[0:32]] * 4, axis=0)
    cos_bt = jnp.concatenate([cos_b] * 8, axis=0)
    sin_bt = jnp.concatenate([-sin_b, sin_b] * 4, axis=0)
    for j in range(s // LANES):
        cols = slice(j * LANES, (j + 1) * LANES)
        ca_ref[cols, :] = cos_a[:, cols].T
        sa_ref[cols, :] = sin_a[:, cols].T
        cb_ref[cols, :] = cos_bt[:, cols].T
        sb_ref[cols, :] = sin_bt[:, cols].T


def _rope_call(pos3, inv):
    b, _, s = pos3.shape
    tab = jax.ShapeDtypeStruct((b, s, LANES), F32)
    tab_spec = pl.BlockSpec((None, s, LANES), lambda i: (i, 0, 0))
    return pl.pallas_call(
        _rope_kernel,
        out_shape=(jax.ShapeDtypeStruct((b, 2, 16, s), F32), tab, tab, tab, tab),
        grid=(b,),
        in_specs=[pl.BlockSpec((None, 1, s), lambda i: (i, 0, 0)),
                  pl.BlockSpec((48, 1), lambda i: (0, 0))],
        out_specs=(pl.BlockSpec((None, 2, 16, s), lambda i: (i, 0, 0, 0)),
                   tab_spec, tab_spec, tab_spec, tab_spec),
        compiler_params=_params(1),
        name="rope",
    )(pos3, inv)


def _ffn_kernel(x_ref, mod_ref, gpre_ref, wg_ref, wu_ref, wd_ref, gpost_ref, o_ref, acc_ref, *, j0):
    x = x_ref[...]
    sh, sc, gt = mod_ref[j0:j0 + 1, :], mod_ref[j0 + 1:j0 + 2, :], mod_ref[j0 + 2:j0 + 3, :]
    h = (_rms_rows(x) * gpre_ref[...] * (1.0 + sc) + sh).astype(BF16)
    n_ch = wg_ref.shape[1] // CH_FFN
    for ch in range(n_ch):
        cols = slice(ch * CH_FFN, (ch + 1) * CH_FFN)
        g = jnp.dot(h, wg_ref[:, cols], preferred_element_type=F32)
        u = jnp.dot(h, wu_ref[:, cols], preferred_element_type=F32)
        a = (g * jax.nn.sigmoid(g) * u).astype(BF16)
        part = jnp.dot(a, wd_ref[cols, :], preferred_element_type=F32)
        if ch == 0:
            acc_ref[...] = part
        else:
            acc_ref[...] += part
    o_ref[...] = x + 0.5 * gt * (_rms_rows(acc_ref[...]) * gpost_ref[...])


def _ffn_call(x, mod, g_pre, w_gate, w_up, w_down, g_post, j0):
    b, s, d = x.shape
    f = w_gate.shape[1]
    const = lambda shape: pl.BlockSpec(shape, lambda i, j: (0,) * len(shape))
    tile = pl.BlockSpec((None, TM_FFN, d), lambda i, j: (i, j, 0))
    return pl.pallas_call(
        functools.partial(_ffn_kernel, j0=j0),
        out_shape=jax.ShapeDtypeStruct((b, s, d), F32),
        grid=(b, s // TM_FFN),
        in_specs=[tile,
                  pl.BlockSpec((None, N_MOD, d), lambda i, j: (i, 0, 0)),
                  const((1, d)), const((d, f)), const((d, f)), const((f, d)), const((1, d))],
        out_specs=tile,
        scratch_shapes=[pltpu.VMEM((TM_FFN, d), F32)],
        compiler_params=_params(2),
        name="ffn",
    )(x, mod, g_pre, w_gate, w_up, w_down, g_post)


def _proj_kernel(x_ref, mod_ref, gpre_ref, wstd_ref, wtr_ref, gcq_ref, gckv_col_ref, gckv_row_ref,
                 wuq_ref, wiq_ref, wuk_ref, wuv_ref, ekr_ref, eki_ref,
                 tt_ref, ca_ref, sa_ref, cb_ref, sb_ref,
                 qt_ref, qit_ref, wit_ref, k_ref, ki_ref, vt_ref, dq0_ref, dq1_ref, dq2_ref, gates_ref,
                 dil_scr, dil_tmp):
    x = x_ref[...]
    sh, sc = mod_ref[3:4, :], mod_ref[4:5, :]
    u = (_rms_rows(x) * gpre_ref[...] * (1.0 + sc) + sh).astype(BF16)
    std = jnp.dot(u, wstd_ref[...], preferred_element_type=F32)
    tr = lax.dot_general(wtr_ref[...], u, (((1,), (1,)), ((), ())),
                         preferred_element_type=F32)

    cq = tr[R_CQ:R_CQ + Q_LORA]
    cq = cq * lax.rsqrt(jnp.mean(cq * cq, axis=0, keepdims=True) + NORM_EPS) * gcq_ref[...]
    cq = cq.astype(BF16)
    ckv_t = tr[R_CKV:R_CKV + KV_LORA]
    ckv_t = ckv_t * lax.rsqrt(jnp.mean(ckv_t * ckv_t, axis=0, keepdims=True) + NORM_EPS) * gckv_col_ref[...]
    cos_t, sin_t = tt_ref[0], tt_ref[1]

    attn_scale = (D_NOPE + D_ROPE_A) ** -0.5 * LOG2_E
    q_t = jnp.dot(wuq_ref[...], cq, preferred_element_type=F32) * attn_scale
    idx_scale = D_IDX ** -0.5
    qi_t = jnp.dot(wiq_ref[...], cq, preferred_element_type=F32) * idx_scale
    for h in range(H_A):
        r0 = h * HEAD_PAD
        x1, x2 = q_t[r0 + 64:r0 + 80], q_t[r0 + 80:r0 + 96]
        qt_ref[r0:r0 + 64, :] = q_t[r0:r0 + 64].astype(BF16)
        qt_ref[r0 + 64:r0 + 80, :] = (x1 * cos_t - x2 * sin_t).astype(BF16)
        qt_ref[r0 + 80:r0 + 96, :] = (x2 * cos_t + x1 * sin_t).astype(BF16)
        qt_ref[r0 + 96:r0 + 128, :] = q_t[r0 + 96:r0 + 128].astype(BF16)
        y1, y2 = qi_t[r0:r0 + 16], qi_t[r0 + 16:r0 + 32]
        qit_ref[r0:r0 + 16, :] = (y1 * cos_t - y2 * sin_t).astype(BF16)
        qit_ref[r0 + 16:r0 + 32, :] = (y2 * cos_t + y1 * sin_t).astype(BF16)
        qit_ref[r0 + 32:r0 + 128, :] = qi_t[r0 + 32:r0 + 128].astype(BF16)
    wit_ref[...] = tr[R_WI:R_WI + H_IDX] * (H_IDX ** -0.5)
    v_t = jnp.dot(wuv_ref[...], ckv_t.astype(BF16), preferred_element_type=F32).astype(BF16)
    for h in range(H_A):
        vt_ref[h * V_ROWS:h * V_ROWS + D_V_A, :] = v_t[h * D_V_A:(h + 1) * D_V_A]
        vt_ref[h * V_ROWS + D_V_A:(h + 1) * V_ROWS, :] = jnp.ones((V_ROWS - D_V_A, v_t.shape[1]), BF16)

    ckv = std[:, C_CKV:C_CKV + KV_LORA]
    ckv = (_rms_rows(ckv) * gckv_row_ref[...]).astype(BF16)
    k_nope = jnp.dot(ckv, wuk_ref[...], preferred_element_type=F32)
    cos_k, sin_k = cb_ref[:, 0:32], sb_ref[:, 0:32]
    k_rope = std[:, C_KR:C_KR + 32] * cos_k + std[:, C_KR_SW:C_KR_SW + 32] * sin_k
    k_ref[...] = (k_nope + jnp.dot(k_rope.astype(BF16), ekr_ref[...],
                                   preferred_element_type=F32)).astype(BF16)
    k_idx = std[:, C_KI:C_KI + 32] * cos_k + std[:, C_KI_SW:C_KI_SW + 32] * sin_k
    ki_ref[...] = jnp.dot(k_idx.astype(BF16), eki_ref[...], preferred_element_type=F32).astype(BF16)

    cos_a, sin_a = ca_ref[...], sa_ref[...]
    tp = x.shape[0]
    for g, (out_ref, (_, dilation)) in enumerate(zip((dq0_ref, dq1_ref, dq2_ref), DIL_PAIRS)):
        for which in range(2):
            scale = (D_HEAD_B ** -0.5) if which == 0 else 1.0
            c0 = C_QKVB + (which * N_GROUPS_B + g) * WIDTH_B
            x1, x2 = std[:, c0:c0 + 128], std[:, c0 + 128:c0 + 256]
            dil_scr[2 * which] = (x1 * cos_a - x2 * sin_a) * scale
            dil_scr[2 * which + 1] = (x2 * cos_a + x1 * sin_a) * scale
        c0 = C_QKVB + (2 * N_GROUPS_B + g) * WIDTH_B
        dil_scr[4] = std[:, c0:c0 + 128]
        dil_scr[5] = std[:, c0 + 128:c0 + 256]
        for c in range(3 * WIDTH_B // LANES):
            if dilation <= 4:
                for r in range(dilation):
                    out_ref[r, :, c * LANES:(c + 1) * LANES] = (
                        dil_scr[c, pl.ds(r, tp // dilation, stride=dilation), :].astype(BF16))
            else:
                quarter = tp // 4
                for r4 in range(4):
                    dil_tmp[c, r4 * quarter:(r4 + 1) * quarter, :] = dil_scr[c, pl.ds(r4, quarter, stride=4), :]
                for r in range(dilation):
                    out_ref[r, :, c * LANES:(c + 1) * LANES] = dil_tmp[
                        c, pl.ds((r % 4) * quarter + r // 4, tp // dilation, stride=dilation // 4), :].astype(BF16)
    gates_ref[...] = jax.nn.sigmoid(std[:, C_GATES:]).astype(BF16)


def _proj_call(x, mod, g_pre, w, tabs):
    b, s, d = x.shape
    tt, ca, sa, cb, sb = tabs
    tp = TP_PROJ
    const = lambda a: pl.BlockSpec(a.shape, lambda i, j: (0,) * a.ndim)
    rows = lambda width: pl.BlockSpec((None, tp, width), lambda i, j: (i, j, 0))
    colsT = lambda height: pl.BlockSpec((None, height, tp), lambda i, j: (i, 0, j))
    consts = [g_pre, w["w_std"], w["w_tr"], w["g_cq_col"], w["g_ckv_col"], w["g_ckv_row"],
              w["w_uq_t"], w["w_iq_t"], w["w_uk_p"], w["w_uv_t"], w["e_kr"], w["e_ki"]]
    hq = H_A * HEAD_PAD
    dil_shapes = [jax.ShapeDtypeStruct((b, dil, s // dil, 3 * WIDTH_B), BF16) for _, dil in DIL_PAIRS]
    dil_specs = [pl.BlockSpec((None, dil, tp // dil, 3 * WIDTH_B), lambda i, j: (i, 0, j, 0))
                 for _, dil in DIL_PAIRS]
    return pl.pallas_call(
        _proj_kernel,
        out_shape=(jax.ShapeDtypeStruct((b, hq, s), BF16),
                   jax.ShapeDtypeStruct((b, hq, s), BF16),
                   jax.ShapeDtypeStruct((b, H_IDX, s), F32),
                   jax.ShapeDtypeStruct((b, s, hq), BF16),
                   jax.ShapeDtypeStruct((b, s, LANES), BF16),
                   jax.ShapeDtypeStruct((b, H_A * V_ROWS, s), BF16),
                   *dil_shapes,
                   jax.ShapeDtypeStruct((b, s, 2048), BF16)),
        grid=(b, s // tp),
        in_specs=[rows(d), pl.BlockSpec((None, N_MOD, d), lambda i, j: (i, 0, 0))]
                 + [const(a) for a in consts]
                 + [pl.BlockSpec((None, 2, 16, tp), lambda i, j: (i, 0, 0, j)),
                    rows(LANES), rows(LANES), rows(LANES), rows(LANES)],
        out_specs=(colsT(hq), colsT(hq), colsT(H_IDX), rows(hq), rows(LANES), colsT(H_A * V_ROWS),
                   *dil_specs, rows(2048)),
        scratch_shapes=[pltpu.VMEM((3 * WIDTH_B // LANES, tp, LANES), F32)] * 2,
        compiler_params=_params(2),
        name="proj",
    )(x, mod, *consts, tt, ca, sa, cb, sb)


def _dsa_kernel(qit_ref, wit_ref, qt_ref, ki_ref, k_ref, vt_ref, o_ref,
                sc_ref, m_ref, acc_ref, s_ref, s2_ref, smax_ref, smax2_ref, *, topk):
    tq = qt_ref.shape[1]
    tk = s_ref.shape[1]
    i = pl.program_id(1)
    nkb = (i + 1) * (tq // tk)
    npair = (nkb + 1) // 2
    qpos = i * tq + lax.broadcasted_iota(jnp.int32, (1, tq), 1)
    row_iota = lax.broadcasted_iota(jnp.int32, (tk, tq), 0)

    def fold(v):
        return jnp.sum(v.reshape(v.shape[0] // FOLD_ROWS, FOLD_ROWS, tq), axis=0)

    def idx_body(kb, carry, on_diagonal):
        smax, smin, neg_max, c_pos, c_nn = carry
        k0 = pl.multiple_of(kb * tk, tk)
        ki = ki_ref[pl.ds(k0, tk), :]
        acc = jnp.zeros((tk, tq), F32)
        for h in range(H_IDX):
            lg = jnp.dot(ki, qit_ref[h * HEAD_PAD:(h + 1) * HEAD_PAD, :], preferred_element_type=F32)
            acc = acc + wit_ref[h:h + 1, :] * jnp.maximum(lg, 0.0)
        if on_diagonal:
            causal = (k0 + row_iota) <= qpos
            sv = jnp.where(causal, acc, -jnp.inf)
            smin = jnp.minimum(smin, jnp.min(jnp.where(causal, acc, jnp.inf), axis=0, keepdims=True))
        else:
            sv = acc
            smin = jnp.minimum(smin, jnp.min(acc, axis=0, keepdims=True))
        sc_ref[pl.ds(k0, tk), :] = sv
        smax = jnp.maximum(smax, jnp.max(sv, axis=0, keepdims=True))
        neg_max = jnp.maximum(neg_max, jnp.max(jnp.where(sv < 0.0, sv, -jnp.inf), axis=0, keepdims=True))
        c_pos = c_pos + fold(jnp.where(sv > 0.0, 1.0, 0.0))
        c_nn = c_nn + fold(jnp.where(sv >= 0.0, 1.0, 0.0))
        return smax, smin, neg_max, c_pos, c_nn

    neg_inf_row = jnp.full((1, tq), -jnp.inf, F32)
    zero_part = jnp.zeros((FOLD_ROWS, tq), F32)
    n_diag = tq // tk
    stats = lax.fori_loop(0, nkb - n_diag, functools.partial(idx_body, on_diagonal=False),
                          (neg_inf_row, jnp.full((1, tq), jnp.inf, F32), neg_inf_row, zero_part, zero_part))
    for j in range(n_diag):
        stats = idx_body(nkb - n_diag + j, stats, on_diagonal=True)
    smax, smin, neg_max, c_pos, c_nn = stats
    c_pos = jnp.sum(c_pos, axis=0, keepdims=True)
    c_nn = jnp.sum(c_nn, axis=0, keepdims=True)

    @pl.when(nkb % 2 == 1)
    def _pad_block():
        sc_ref[pl.ds(pl.multiple_of(nkb * tk, tk), tk), :] = jnp.full((tk, tq), -jnp.inf, F32)

    n_causal = (qpos + 1).astype(F32)
    kf = jnp.minimum(qpos + 1, topk).astype(F32)

    def count_pairs(indicator):
        def body(kp, c):
            k0 = pl.multiple_of(kp * (2 * tk), 2 * tk)
            return c + fold(indicator(sc_ref[pl.ds(k0, 2 * tk), :], k0))
        part = lax.fori_loop(0, npair, body, jnp.zeros((FOLD_ROWS, tq), F32))
        return jnp.sum(part, axis=0, keepdims=True)

    def count_gt(thr):
        return count_pairs(lambda sv, k0: jnp.where(sv > thr, 1.0, 0.0))

    below = smin - (1.0 + jnp.abs(smin))
    zero_tie = jnp.logical_and(c_pos < kf, kf <= c_nn)
    searching = jnp.where(zero_tie, 0.0, 1.0)

    def bisect_cond(carry):
        return jnp.logical_and(carry[0] < BISECT_MAX, carry[1] > 0)

    def bisect_body(carry):
        it, _, lo, hi, c_lo, c_hi = carry
        for _ in range(BISECT_CHECK):
            mid = 0.5 * (lo + hi)
            c_mid = count_gt(mid)
            ge = c_mid >= kf
            up = jnp.logical_and(ge, searching > 0.0)
            down = jnp.logical_and(jnp.logical_not(ge), searching > 0.0)
            lo, c_lo = jnp.where(up, mid, lo), jnp.where(up, c_mid, c_lo)
            hi, c_hi = jnp.where(down, mid, hi), jnp.where(down, c_mid, c_hi)
        open_queries = (jnp.max((c_lo - kf) * searching) > 0.0).astype(jnp.int32)
        return it + BISECT_CHECK, open_queries, lo, hi, c_lo, c_hi

    c_lo0 = jnp.where(zero_tie, c_nn, n_causal)
    first_open = (jnp.max((c_lo0 - kf) * searching) > 0.0).astype(jnp.int32)
    _, _, lo, hi, c_lo, c_hi = lax.while_loop(
        bisect_cond, bisect_body,
        (jnp.int32(0), first_open,
         jnp.where(zero_tie, jnp.maximum(neg_max, below), below), jnp.where(zero_tie, 0.0, smax),
         c_lo0, jnp.where(zero_tie, c_pos, 0.0)))

    @pl.when(jnp.max(c_lo - kf) > 0.0)
    def _break_ties():
        need = kf - c_hi
        prefix = jnp.where(lax.broadcasted_iota(jnp.int32, (tk, tk), 0)
                           >= lax.broadcasted_iota(jnp.int32, (tk, tk), 1), 1.0, 0.0).astype(BF16)

        def body(kb, seen):
            k0 = pl.multiple_of(kb * tk, tk)
            sv = sc_ref[pl.ds(k0, tk), :]
            member = jnp.where(sv > lo, jnp.where(sv > hi, 0.0, 1.0), 0.0)
            rank = jnp.dot(prefix, member.astype(BF16), preferred_element_type=F32) + seen
            sc_ref[pl.ds(k0, tk), :] = jnp.where(member * rank > need, -jnp.inf, sv)
            return rank[tk - 1:tk, :]

        lax.fori_loop(0, nkb, body, jnp.zeros((1, tq), F32))

    m_ref[...] = jnp.full(m_ref.shape, NEG_BIG, F32)
    acc_ref[...] = jnp.zeros(acc_ref.shape, F32)

    def scores_into(kb, park_ref, max_ref):
        k0 = pl.multiple_of(kb * tk, tk)
        bias = jnp.where(sc_ref[pl.ds(k0, tk), :] > lo, 0.0, NEG_BIG)
        for h in range(H_A):
            kh = k_ref[pl.ds(k0, tk), h * HEAD_PAD:(h + 1) * HEAD_PAD]
            s = jnp.dot(kh, qt_ref[h * HEAD_PAD:(h + 1) * HEAD_PAD, :], preferred_element_type=F32) + bias
            park_ref[h] = s
            max_ref[h:h + 1, :] = jnp.max(s, axis=0, keepdims=True)

    def softmax_from(kb, park_ref, max_ref):
        k0 = pl.multiple_of(kb * tk, tk)
        for h in range(H_A):
            m_old = m_ref[h:h + 1, :]
            m_new = jnp.maximum(m_old, max_ref[h:h + 1, :])
            alpha = jnp.exp2(m_old - m_new)
            p = jnp.exp2((park_ref[h] - m_new).astype(BF16))
            rows = slice(h * V_ROWS, (h + 1) * V_ROWS)
            pv = jnp.dot(vt_ref[rows, pl.ds(k0, tk)], p, preferred_element_type=F32)
            acc_ref[rows, :] = alpha * acc_ref[rows, :] + pv
            m_ref[h:h + 1, :] = m_new

    last_block = 2 * npair - 1
    scores_into(0, s_ref, smax_ref)

    def att_body(kp, carry):
        scores_into(2 * kp + 1, s2_ref, smax2_ref)
        softmax_from(2 * kp, s_ref, smax_ref)
        scores_into(jnp.minimum(2 * kp + 2, last_block), s_ref, smax_ref)
        softmax_from(2 * kp + 1, s2_ref, smax2_ref)
        return carry

    lax.fori_loop(0, npair, att_body, 0)
    out = [acc_ref[h * V_ROWS:h * V_ROWS + D_V_A, :] / acc_ref[h * V_ROWS + D_V_A:h * V_ROWS + D_V_A + 1, :]
           for h in range(H_A)]
    o_ref[...] = jnp.concatenate(out, axis=0).T.astype(BF16)


def _dsa_call(qt, qit, wit, k, ki, vt):
    b, hq, s = qt.shape
    tq = TQ_DSA
    topk = min(TOPK_MAX, s // 4)
    colsT = lambda height: pl.BlockSpec((None, height, tq), lambda i, j: (i, 0, j))
    whole = lambda a: pl.BlockSpec((None,) + a.shape[1:], lambda i, j: (i, 0, 0))
    return pl.pallas_call(
        functools.partial(_dsa_kernel, topk=topk),
        out_shape=jax.ShapeDtypeStruct((b, s, WIDTH_A), BF16),
        grid=(b, s // tq),
        in_specs=[colsT(hq), colsT(H_IDX), colsT(hq), whole(ki), whole(k), whole(vt)],
        out_specs=pl.BlockSpec((None, tq, WIDTH_A), lambda i, j: (i, j, 0)),
        scratch_shapes=[pltpu.VMEM((s, tq), F32),
                        pltpu.VMEM((H_A, tq), F32),
                        pltpu.VMEM((H_A * V_ROWS, tq), F32),
                        pltpu.VMEM((H_A, TK_DSA, tq), F32),
                        pltpu.VMEM((H_A, TK_DSA, tq), F32),
                        pltpu.VMEM((H_A, tq), F32),
                        pltpu.VMEM((H_A, tq), F32)],
        compiler_params=_params(2),
        name="dsa",
    )(qit, wit, qt, ki, k, vt)


def _dil_kernel(q_ref, kc_ref, kp_ref, vc_ref, vp_ref, o_ref, lse_ref, s_ref):
    tb = kp_ref.shape[0]
    n_sub = q_ref.shape[0] // tb
    first_step = pl.program_id(2) == 0
    k = jnp.concatenate([kp_ref[...], kc_ref[...]], axis=0)
    v = jnp.concatenate([vp_ref[...], vc_ref[...]], axis=0)
    r = lax.broadcasted_iota(jnp.int32, (tb, 2 * tb), 0)
    c = lax.broadcasted_iota(jnp.int32, (tb, 2 * tb), 1)
    behind = (tb + r - c).astype(jnp.uint32)
    bias = jnp.where(behind <= tb, 0.0, -jnp.inf)
    no_prev = jnp.where(first_step, r, tb).astype(jnp.uint32)
    bias0 = jnp.where(behind <= no_prev, 0.0, -jnp.inf)
    lane = lax.broadcasted_iota(jnp.int32, (1, WIDTH_B), 1)
    head_qk = (lane % 128) // (D_HEAD_B // 2)
    head_v = lane // D_HEAD_B
    nt = (((1,), (1,)), ((), ()))
    for j in range(n_sub):
        q = q_ref[j * tb:(j + 1) * tb, :]
        for h in range(H_B):
            qh = jnp.where(head_qk == h, q, jnp.zeros_like(q))
            s_ref[j * H_B + h] = (lax.dot_general(qh, k[j * tb:(j + 2) * tb], nt, preferred_element_type=F32)
                                  + (bias0 if j == 0 else bias))
    for j in range(n_sub):
        vj = v[j * tb:(j + 2) * tb]
        o = jnp.zeros((tb, WIDTH_B), F32)
        lse = jnp.zeros((tb, WIDTH_B), F32)
        for h in range(H_B):
            s = s_ref[j * H_B + h]
            m = jnp.max(s, axis=1, keepdims=True)
            p = jnp.exp(s - m).astype(BF16)
            pv = jnp.dot(p, jnp.where(head_v == h, vj, jnp.ones_like(vj)), preferred_element_type=F32)
            l = pltpu.roll(pv, WIDTH_B // 2, axis=1)
            o = jnp.where(head_v == h, pv / l, o)
            lse = jnp.where(head_v == h, m + jnp.log(l), lse)
        o_ref[j * tb:(j + 1) * tb, :] = o
        lse_ref[j * tb:(j + 1) * tb, :] = lse


def _dil_call(qkv):
    b, dilation, n, _ = qkv.shape
    tb, tq = TB_DIL, min(TQ_DIL, n)
    per = tq // tb
    cur = lambda which: pl.BlockSpec((None, None, tq, WIDTH_B), lambda i, r, j: (i, r, j, which))
    prev = lambda which: pl.BlockSpec((None, None, tb, WIDTH_B),
                                      lambda i, r, j: (i, r, jnp.maximum(j * per - 1, 0), which))
    out_spec = pl.BlockSpec((None, None, tq, WIDTH_B), lambda i, r, j: (i, r, j, 0))
    out = jax.ShapeDtypeStruct((b, dilation, n, WIDTH_B), F32)
    return pl.pallas_call(
        _dil_kernel,
        out_shape=(out, out),
        grid=(b, dilation, n // tq),
        in_specs=[cur(0), cur(1), prev(1), cur(2), prev(2)],
        out_specs=(out_spec, out_spec),
        scratch_shapes=[pltpu.VMEM((per * H_B, tb, 2 * tb), F32)],
        compiler_params=_params(3),
        name="dil",
    )(qkv, qkv, qkv, qkv, qkv)


def _merge_kernel(x_ref, mod_ref, oa_ref, o0_ref, o1_ref, o2_ref, l0_ref, l1_ref, l2_ref, gates_ref,
                  wua_ref, wub_ref, wo_ref, gpost_ref, out_ref, *scr):
    tm = x_ref.shape[0]
    tmp = scr[4]

    def token_major(ref, buf):
        dilation = ref.shape[0]
        if dilation == 1:
            return ref[0]
        n_slab = ref.shape[2] // LANES
        for c in range(n_slab):
            if dilation <= 4:
                for r in range(dilation):
                    buf[c, pl.ds(r, tm // dilation, stride=dilation), :] = ref[r, :, c * LANES:(c + 1) * LANES]
            else:
                quarter = tm // 4
                for r in range(dilation):
                    tmp[c, pl.ds((r % 4) * quarter + r // 4, tm // dilation, stride=dilation // 4), :] = (
                        ref[r, :, c * LANES:(c + 1) * LANES])
                for r4 in range(4):
                    buf[c, pl.ds(r4, quarter, stride=4), :] = tmp[c, r4 * quarter:(r4 + 1) * quarter, :]
        return jnp.concatenate([buf[c] for c in range(n_slab)], axis=1)

    o0, l0 = token_major(o0_ref, None), token_major(l0_ref, None)
    o1, l1 = token_major(o1_ref, scr[0]), token_major(l1_ref, scr[1])
    o2, l2 = token_major(o2_ref, scr[2]), token_major(l2_ref, scr[3])
    m = jnp.maximum(jnp.maximum(l0, l1), l2)
    e0, e1, e2 = jnp.exp(l0 - m), jnp.exp(l1 - m), jnp.exp(l2 - m)
    ob = (e0 * o0 + e1 * o1 + e2 * o2) / (e0 + e1 + e2)
    za = jnp.dot(oa_ref[...], wua_ref[...], preferred_element_type=F32)
    zb = jnp.dot(ob.astype(BF16), wub_ref[...], preferred_element_type=F32)
    d = za.shape[1]
    z = gates_ref[:, 0:d].astype(F32) * za + gates_ref[:, d:2 * d].astype(F32) * zb
    y = jnp.dot(z.astype(BF16), wo_ref[...], preferred_element_type=F32)
    out_ref[...] = x_ref[...] + mod_ref[5:6, :] * (_rms_rows(y) * gpost_ref[...])


def _merge_call(x, mod, oa, dil, gates, w_up_a, w_up_b, w_o, g_post):
    b, s, d = x.shape
    tm = TM_MERGE
    rows = lambda width: pl.BlockSpec((None, tm, width), lambda i, j: (i, j, 0))
    const = lambda a: pl.BlockSpec(a.shape, lambda i, j: (0,) * a.ndim)
    (o0, l0), (o1, l1), (o2, l2) = dil
    res = lambda a: pl.BlockSpec((None, a.shape[1], tm // a.shape[1], WIDTH_B), lambda i, j: (i, 0, j, 0))
    return pl.pallas_call(
        _merge_kernel,
        out_shape=jax.ShapeDtypeStruct((b, s, d), F32),
        grid=(b, s // tm),
        in_specs=[rows(d), pl.BlockSpec((None, N_MOD, d), lambda i, j: (i, 0, 0)), rows(WIDTH_A)]
                 + [res(a) for a in (o0, o1, o2, l0, l1, l2)]
                 + [rows(2 * d), const(w_up_a), const(w_up_b), const(w_o), const(g_post)],
        out_specs=rows(d),
        scratch_shapes=[pltpu.VMEM((WIDTH_B // LANES, tm, LANES), F32)] * 5,
        compiler_params=_params(2),
        name="merge",
    )(x, mod, oa, o0, o1, o2, l0, l1, l2, gates, w_up_a, w_up_b, w_o, g_post)


def _pad_heads(w_t):
    h, r, n = w_t.shape
    return jnp.pad(w_t, ((0, 0), (0, HEAD_PAD - r), (0, 0))).reshape(h * HEAD_PAD, n)


def _mixer_weights(w_in, g_cq, g_ckv, w_uq, w_uk, w_uv, w_iq):
    d = w_in.shape[0]
    o_cq, o_ckv, o_kr, o_ki, o_wi = 0, Q_LORA, Q_LORA + KV_LORA, Q_LORA + KV_LORA + D_ROPE_A, \
        Q_LORA + KV_LORA + D_ROPE_A + D_IDX
    o_qkv = o_wi + H_IDX
    o_gates = o_qkv + N_QKV_B

    def swap_halves(w):
        half = w.shape[1] // 2
        return jnp.concatenate([w[:, half:], w[:, :half]], axis=1)

    w_kr, w_ki = w_in[:, o_kr:o_kr + D_ROPE_A], w_in[:, o_ki:o_ki + D_IDX]
    qkv = w_in[:, o_qkv:o_gates].reshape(d, 3, N_GROUPS_B, H_B, 2, D_HEAD_B // 2)
    qk_split = qkv[:, 0:2].transpose(0, 1, 2, 4, 3, 5).reshape(d, 2 * N_GROUPS_B * WIDTH_B)
    v_cols = qkv[:, 2].reshape(d, N_GROUPS_B * WIDTH_B)
    w_std = jnp.concatenate([w_in[:, o_ckv:o_ckv + KV_LORA], w_kr, swap_halves(w_kr), w_ki, swap_halves(w_ki),
                             qk_split, v_cols, w_in[:, o_gates:]], axis=1)
    w_tr = jnp.concatenate([w_in[:, o_cq:o_cq + Q_LORA], w_in[:, o_ckv:o_ckv + KV_LORA],
                            w_in[:, o_wi:o_wi + H_IDX]], axis=1).T
    w_tr = jnp.pad(w_tr, ((0, N_TR - w_tr.shape[0]), (0, 0)))

    e_kr = np.zeros((D_ROPE_A, H_A * HEAD_PAD), np.float32)
    for h in range(H_A):
        e_kr[np.arange(D_ROPE_A), h * HEAD_PAD + D_NOPE + np.arange(D_ROPE_A)] = 1.0
    e_ki = np.zeros((D_IDX, LANES), np.float32)
    e_ki[np.arange(D_IDX), np.arange(D_IDX)] = 1.0
    return {
        "w_std": w_std.astype(BF16),
        "w_tr": w_tr.astype(BF16),
        "g_cq_col": g_cq.reshape(Q_LORA, 1),
        "g_ckv_col": g_ckv.reshape(KV_LORA, 1),
        "g_ckv_row": g_ckv.reshape(1, KV_LORA),
        "w_uq_t": _pad_heads(w_uq.transpose(1, 2, 0)).astype(BF16),
        "w_iq_t": _pad_heads(w_iq.transpose(1, 2, 0)).astype(BF16),
        "w_uk_p": jnp.pad(w_uk, ((0, 0), (0, 0), (0, HEAD_PAD - D_NOPE))).reshape(KV_LORA, H_A * HEAD_PAD).astype(BF16),
        "w_uv_t": w_uv.transpose(1, 2, 0).reshape(WIDTH_A, KV_LORA).astype(BF16),
        "e_kr": jnp.asarray(e_kr, BF16),
        "e_ki": jnp.asarray(e_ki, BF16),
    }


def kernel(x, c, positions, w_mod, b_mod, g_pre_ffn1, w_gate1, w_up1, w_down1, g_post_ffn1, g_pre_mix, w_in, g_cq, g_ckv, w_uq, w_uk, w_uv, w_iq, w_up_a, w_up_b, w_o, g_post_mix, g_pre_ffn2, w_gate2, w_up2, w_down2, g_post_ffn2):
    b, s, d = x.shape
    assert s % (16 * TB_DIL) == 0 and s % TM_FFN == 0 and b <= 8
    c8 = jnp.pad(c, ((0, 8 - b), (0, 0)))
    half_b, half_a = D_HEAD_B // 2, D_ROPE_A // 2
    inv = jnp.concatenate([ROPE_THETA ** (-jnp.arange(half_b, dtype=F32) / half_b),
                           ROPE_THETA ** (-jnp.arange(half_a, dtype=F32) / half_a)]).reshape(48, 1)
    tabs = _rope_call(positions.reshape(b, 1, s), inv)
    row = lambda g: g.reshape(1, -1)
    for l in range(w_mod.shape[0]):
        mod = _mod_call(c8, w_mod[l], b_mod[l].reshape(1, -1))[:b].reshape(b, N_MOD, d)
        x = _ffn_call(x, mod, row(g_pre_ffn1[l]), w_gate1[l].astype(BF16), w_up1[l].astype(BF16),
                      w_down1[l].astype(BF16), row(g_post_ffn1[l]), 0)
        w = _mixer_weights(w_in[l], g_cq[l], g_ckv[l], w_uq[l], w_uk[l], w_uv[l], w_iq[l])
        qt, qit, wit, k, ki, vt, dq0, dq1, dq2, gates = _proj_call(x, mod, row(g_pre_mix[l]), w, tabs)
        oa = _dsa_call(qt, qit, wit, k, ki, vt)
        dil = [_dil_call(dq) for dq in (dq0, dq1, dq2)]
        x = _merge_call(x, mod, oa, dil, gates, w_up_a[l].astype(BF16), w_up_b[l].astype(BF16),
                        w_o[l].astype(BF16), row(g_post_mix[l]))
        x = _ffn_call(x, mod, row(g_pre_ffn2[l]), w_gate2[l].astype(BF16), w_up2[l].astype(BF16),
                      w_down2[l].astype(BF16), row(g_post_ffn2[l]), 6)
    return x
```

```python
import functools

import numpy as np
import jax
import jax.numpy as jnp
from jax import lax
from jax.experimental import pallas as pl
from jax.experimental.pallas import tpu as pltpu

F32 = jnp.float32
BF16 = jnp.bfloat16

D_FF = 2816
ROPE_THETA = 10000.0
NORM_EPS = 1e-6
H_A = 8
Q_LORA = 384
KV_LORA = 256
D_NOPE = 64
D_ROPE_A = 32
D_V_A = 64
H_IDX = 8
D_IDX = 32
TOPK_MAX = 256
DIL_PAIRS = ((128, 1), (512, 4), (2048, 16))
N_GROUPS_B = 3
H_B = 4
D_HEAD_B = 64
N_MOD = 9
WIDTH_A = H_A * D_V_A
WIDTH_B = H_B * D_HEAD_B
N_QKV_B = 3 * N_GROUPS_B * WIDTH_B

LANES = 128
HEAD_PAD = 128
VMEM_LIMIT = 56 * 1024 * 1024

TM_FFN = 512
CH_FFN = 256
TP_PROJ = 512
TQ_DSA = 512
TK_DSA = 256
V_ROWS = D_V_A + 16
TB_DIL = 128
TQ_DIL = 512
BISECT_MAX = 34
BISECT_CHECK = 2
FOLD_ROWS = 32
NEG_BIG = -1e30
LOG2_E = 1.4426950408889634

C_CKV = 0
C_KR = 256
C_KR_SW = 288
C_KI = 320
C_KI_SW = 352
C_QKVB = 384
C_GATES = C_QKVB + N_QKV_B
N_STD = C_GATES + 2048
R_CQ = 0
R_CKV = Q_LORA
R_WI = Q_LORA + KV_LORA
N_TR = 768


def _params(n_axes):
    return pltpu.CompilerParams(dimension_semantics=("arbitrary",) * n_axes,
                                vmem_limit_bytes=VMEM_LIMIT)


def _rms_rows(x):
    return x * lax.rsqrt(jnp.mean(x * x, axis=-1, keepdims=True) + NORM_EPS)


def _mod_kernel(c_ref, w_ref, b_ref, o_ref):
    c = c_ref[...]
    o_ref[...] = jnp.dot(c * jax.nn.sigmoid(c), w_ref[...], preferred_element_type=F32) + b_ref[...]


def _mod_call(c8, w_mod, b_mod):
    d = c8.shape[1]
    return pl.pallas_call(
        _mod_kernel,
        out_shape=jax.ShapeDtypeStruct((8, N_MOD * d), F32),
        grid=(N_MOD,),
        in_specs=[pl.BlockSpec((8, d), lambda j: (0, 0)),
                  pl.BlockSpec((d, d), lambda j: (0, j)),
                  pl.BlockSpec((1, d), lambda j: (0, j))],
        out_specs=pl.BlockSpec((8, d), lambda j: (0, j)),
        compiler_params=_params(1),
        name="mod",
    )(c8, w_mod, b_mod)


def _rope_kernel(pos_ref, inv_ref, tt_ref, ca_ref, sa_ref, cb_ref, sb_ref):
    s = pos_ref.shape[1]
    ang = inv_ref[...] * pos_ref[...].astype(F32)
    cos = jnp.cos(ang)
    sin = jnp.sin(ang)
    cos_b, sin_b = cos[32:48], sin[32:48]
    tt_ref[0] = cos_b
    tt_ref[1] = sin_b
    cos_a = jnp.concatenate([cos[0:32]] * 4, axis=0)
    sin_a = jnp.concatenate([sin[0:32]] * 4, axis=0)
    cos_bt = jnp.concatenate([cos_b] * 8, axis=0)
    sin_bt = jnp.concatenate([-sin_b, sin_b] * 4, axis=0)
    for j in range(s // LANES):
        cols = slice(j * LANES, (j + 1) * LANES)
        ca_ref[cols, :] = cos_a[:, cols].T
        sa_ref[cols, :] = sin_a[:, cols].T
        cb_ref[cols, :] = cos_bt[:, cols].T
        sb_ref[cols, :] = sin_bt[:, cols].T


def _rope_call(pos3, inv):
    b, _, s = pos3.shape
    tab = jax.ShapeDtypeStruct((b, s, LANES), F32)
    tab_spec = pl.BlockSpec((None, s, LANES), lambda i: (i, 0, 0))
    return pl.pallas_call(
        _rope_kernel,
        out_shape=(jax.ShapeDtypeStruct((b, 2, 16, s), F32), tab, tab, tab, tab),
        grid=(b,),
        in_specs=[pl.BlockSpec((None, 1, s), lambda i: (i, 0, 0)),
                  pl.BlockSpec((48, 1), lambda i: (0, 0))],
        out_specs=(pl.BlockSpec((None, 2, 16, s), lambda i: (i, 0, 0, 0)),
                   tab_spec, tab_spec, tab_spec, tab_spec),
        compiler_params=_params(1),
        name="rope",
    )(pos3, inv)


def _ffn_kernel(x_ref, mod_ref, gpre_ref, wg_ref, wu_ref, wd_ref, gpost_ref, o_ref, acc_ref, *, j0):
    _ffn_tile(x_ref[...], mod_ref, gpre_ref, wg_ref, wu_ref, wd_ref, gpost_ref, o_ref, acc_ref, j0)


def _ffn_tile(x, mod_ref, gpre_ref, wg_ref, wu_ref, wd_ref, gpost_ref, o_ref, acc_ref, j0):
    sh, sc, gt = mod_ref[j0:j0 + 1, :], mod_ref[j0 + 1:j0 + 2, :], mod_ref[j0 + 2:j0 + 3, :]
    h = (_rms_rows(x) * gpre_ref[...] * (1.0 + sc) + sh).astype(BF16)
    n_ch = wg_ref.shape[1] // CH_FFN
    for ch in range(n_ch):
        cols = slice(ch * CH_FFN, (ch + 1) * CH_FFN)
        g = jnp.dot(h, wg_ref[:, cols], preferred_element_type=F32)
        u = jnp.dot(h, wu_ref[:, cols], preferred_element_type=F32)
        a = (g * jax.nn.sigmoid(g) * u).astype(BF16)
        part = jnp.dot(a, wd_ref[cols, :], preferred_element_type=F32)
        if ch == 0:
            acc_ref[...] = part
        else:
            acc_ref[...] += part
    o_ref[...] = x + 0.5 * gt * (_rms_rows(acc_ref[...]) * gpost_ref[...])


def _ffn_call(x, mod, g_pre, w_gate, w_up, w_down, g_post, j0):
    b, s, d = x.shape
    f = w_gate.shape[1]
    const = lambda shape: pl.BlockSpec(shape, lambda i, j: (0,) * len(shape), pipeline_mode=pl.Buffered(1))
    tile = pl.BlockSpec((None, TM_FFN, d), lambda i, j: (i, j, 0))
    return pl.pallas_call(
        functools.partial(_ffn_kernel, j0=j0),
        out_shape=jax.ShapeDtypeStruct((b, s, d), F32),
        grid=(b, s // TM_FFN),
        in_specs=[tile,
                  pl.BlockSpec((None, N_MOD, d), lambda i, j: (i, 0, 0)),
                  const((1, d)), const((d, f)), const((d, f)), const((f, d)), const((1, d))],
        out_specs=tile,
        scratch_shapes=[pltpu.VMEM((TM_FFN, d), F32)],
        compiler_params=_params(2),
        name="ffn",
    )(x, mod, g_pre, w_gate, w_up, w_down, g_post)


def _proj_kernel(x_ref, mod_ref, gpre_ref, wstd_ref, wtr_ref, gcq_ref, gckv_col_ref, gckv_row_ref,
                 wuq_ref, wiq_ref, wuk_ref, wuv_ref, ekr_ref, eki_ref,
                 tt_ref, ca_ref, sa_ref, cb_ref, sb_ref,
                 qt_ref, qit_ref, wit_ref, k_ref, ki_ref, vt_ref, dq0_ref, dq1_ref, dq2_ref, gates_ref,
                 dil_scr, dil_tmp):
    x = x_ref[...]
    sh, sc = mod_ref[3:4, :], mod_ref[4:5, :]
    u = (_rms_rows(x) * gpre_ref[...] * (1.0 + sc) + sh).astype(BF16)
    std = jnp.dot(u, wstd_ref[...], preferred_element_type=F32)
    tr = lax.dot_general(wtr_ref[...], u, (((1,), (1,)), ((), ())),
                         preferred_element_type=F32)

    cq = tr[R_CQ:R_CQ + Q_LORA]
    cq = cq * lax.rsqrt(jnp.mean(cq * cq, axis=0, keepdims=True) + NORM_EPS) * gcq_ref[...]
    cq = cq.astype(BF16)
    ckv_t = tr[R_CKV:R_CKV + KV_LORA]
    ckv_t = ckv_t * lax.rsqrt(jnp.mean(ckv_t * ckv_t, axis=0, keepdims=True) + NORM_EPS) * gckv_col_ref[...]
    cos_t, sin_t = tt_ref[0], tt_ref[1]

    attn_scale = (D_NOPE + D_ROPE_A) ** -0.5 * LOG2_E
    q_t = jnp.dot(wuq_ref[...], cq, preferred_element_type=F32) * attn_scale
    idx_scale = D_IDX ** -0.5
    qi_t = jnp.dot(wiq_ref[...], cq, preferred_element_type=F32) * idx_scale
    for h in range(H_A):
        r0 = h * HEAD_PAD
        x1, x2 = q_t[r0 + 64:r0 + 80], q_t[r0 + 80:r0 + 96]
        qt_ref[r0:r0 + 64, :] = q_t[r0:r0 + 64].astype(BF16)
        qt_ref[r0 + 64:r0 + 80, :] = (x1 * cos_t - x2 * sin_t).astype(BF16)
        qt_ref[r0 + 80:r0 + 96, :] = (x2 * cos_t + x1 * sin_t).astype(BF16)
        qt_ref[r0 + 96:r0 + 128, :] = q_t[r0 + 96:r0 + 128].astype(BF16)
        y1, y2 = qi_t[r0:r0 + 16], qi_t[r0 + 16:r0 + 32]
        qit_ref[r0:r0 + 16, :] = (y1 * cos_t - y2 * sin_t).astype(BF16)
        qit_ref[r0 + 16:r0 + 32, :] = (y2 * cos_t + y1 * sin_t).astype(BF16)
        qit_ref[r0 + 32:r0 + 128, :] = qi_t[r0 + 32:r0 + 128].astype(BF16)
    wit_ref[...] = tr[R_WI:R_WI + H_IDX] * (H_IDX ** -0.5)
    v_t = jnp.dot(wuv_ref[...], ckv_t.astype(BF16), preferred_element_type=F32).astype(BF16)
    for h in range(H_A):
        vt_ref[h * V_ROWS:h * V_ROWS + D_V_A, :] = v_t[h * D_V_A:(h + 1) * D_V_A]
        vt_ref[h * V_ROWS + D_V_A:(h + 1) * V_ROWS, :] = jnp.ones((V_ROWS - D_V_A, v_t.shape[1]), BF16)

    ckv = std[:, C_CKV:C_CKV + KV_LORA]
    ckv = (_rms_rows(ckv) * gckv_row_ref[...]).astype(BF16)
    k_nope = jnp.dot(ckv, wuk_ref[...], preferred_element_type=F32)
    cos_k, sin_k = cb_ref[:, 0:32], sb_ref[:, 0:32]
    k_rope = std[:, C_KR:C_KR + 32] * cos_k + std[:, C_KR_SW:C_KR_SW + 32] * sin_k
    k_ref[...] = (k_nope + jnp.dot(k_rope.astype(BF16), ekr_ref[...],
                                   preferred_element_type=F32)).astype(BF16)
    k_idx = std[:, C_KI:C_KI + 32] * cos_k + std[:, C_KI_SW:C_KI_SW + 32] * sin_k
    ki_ref[...] = jnp.dot(k_idx.astype(BF16), eki_ref[...], preferred_element_type=F32).astype(BF16)

    cos_a, sin_a = ca_ref[...], sa_ref[...]
    tp = x.shape[0]
    for g, (out_ref, (_, dilation)) in enumerate(zip((dq0_ref, dq1_ref, dq2_ref), DIL_PAIRS)):
        for which in range(2):
            scale = (D_HEAD_B ** -0.5) if which == 0 else 1.0
            c0 = C_QKVB + (which * N_GROUPS_B + g) * WIDTH_B
            x1, x2 = std[:, c0:c0 + 128], std[:, c0 + 128:c0 + 256]
            dil_scr[2 * which] = (x1 * cos_a - x2 * sin_a) * scale
            dil_scr[2 * which + 1] = (x2 * cos_a + x1 * sin_a) * scale
        c0 = C_QKVB + (2 * N_GROUPS_B + g) * WIDTH_B
        dil_scr[4] = std[:, c0:c0 + 128]
        dil_scr[5] = std[:, c0 + 128:c0 + 256]
        for c in range(3 * WIDTH_B // LANES):
            if dilation <= 4:
                for r in range(dilation):
                    out_ref[r, :, c * LANES:(c + 1) * LANES] = (
                        dil_scr[c, pl.ds(r, tp // dilation, stride=dilation), :].astype(BF16))
            else:
                quarter = tp // 4
                for r4 in range(4):
                    dil_tmp[c, r4 * quarter:(r4 + 1) * quarter, :] = dil_scr[c, pl.ds(r4, quarter, stride=4), :]
                for r in range(dilation):
                    out_ref[r, :, c * LANES:(c + 1) * LANES] = dil_tmp[
                        c, pl.ds((r % 4) * quarter + r // 4, tp // dilation, stride=dilation // 4), :].astype(BF16)
    gates_ref[...] = jax.nn.sigmoid(std[:, C_GATES:]).astype(BF16)


def _proj_call(x, mod, g_pre, w, tabs):
    b, s, d = x.shape
    tt, ca, sa, cb, sb = tabs
    tp = TP_PROJ
    const = lambda a: pl.BlockSpec(a.shape, lambda i, j: (0,) * a.ndim)
    rows = lambda width: pl.BlockSpec((None, tp, width), lambda i, j: (i, j, 0))
    colsT = lambda height: pl.BlockSpec((None, height, tp), lambda i, j: (i, 0, j))
    consts = [g_pre, w["w_std"], w["w_tr"], w["g_cq_col"], w["g_ckv_col"], w["g_ckv_row"],
              w["w_uq_t"], w["w_iq_t"], w["w_uk_p"], w["w_uv_t"], w["e_kr"], w["e_ki"]]
    hq = H_A * HEAD_PAD
    dil_shapes = [jax.ShapeDtypeStruct((b, dil, s // dil, 3 * WIDTH_B), BF16) for _, dil in DIL_PAIRS]
    dil_specs = [pl.BlockSpec((None, dil, tp // dil, 3 * WIDTH_B), lambda i, j: (i, 0, j, 0))
                 for _, dil in DIL_PAIRS]
    return pl.pallas_call(
        _proj_kernel,
        out_shape=(jax.ShapeDtypeStruct((b, hq, s), BF16),
                   jax.ShapeDtypeStruct((b, hq, s), BF16),
                   jax.ShapeDtypeStruct((b, H_IDX, s), F32),
                   jax.ShapeDtypeStruct((b, s, hq), BF16),
                   jax.ShapeDtypeStruct((b, s, LANES), BF16),
                   jax.ShapeDtypeStruct((b, H_A * V_ROWS, s), BF16),
                   *dil_shapes,
                   jax.ShapeDtypeStruct((b, s, 2048), BF16)),
        grid=(b, s // tp),
        in_specs=[rows(d), pl.BlockSpec((None, N_MOD, d), lambda i, j: (i, 0, 0))]
                 + [const(a) for a in consts]
                 + [pl.BlockSpec((None, 2, 16, tp), lambda i, j: (i, 0, 0, j)),
                    rows(LANES), rows(LANES), rows(LANES), rows(LANES)],
        out_specs=(colsT(hq), colsT(hq), colsT(H_IDX), rows(hq), rows(LANES), colsT(H_A * V_ROWS),
                   *dil_specs, rows(2048)),
        scratch_shapes=[pltpu.VMEM((3 * WIDTH_B // LANES, tp, LANES), F32)] * 2,
        compiler_params=_params(2),
        name="proj",
    )(x, mod, *consts, tt, ca, sa, cb, sb)


def _dsa_kernel(qit_ref, wit_ref, qt_ref, ki_ref, k_ref, vt_ref, o_ref,
                sc_ref, m_ref, acc_ref, s_ref, s2_ref, smax_ref, smax2_ref, *, topk):
    tq = qt_ref.shape[1]
    tk = s_ref.shape[1]
    i = pl.program_id(1)
    nkb = (i + 1) * (tq // tk)
    npair = (nkb + 1) // 2
    qpos = i * tq + lax.broadcasted_iota(jnp.int32, (1, tq), 1)
    row_iota = lax.broadcasted_iota(jnp.int32, (tk, tq), 0)

    def fold(v):
        return jnp.sum(v.reshape(v.shape[0] // FOLD_ROWS, FOLD_ROWS, tq), axis=0)

    def idx_body(kb, carry, on_diagonal):
        smax, smin, neg_max, c_pos, c_nn = carry
        k0 = pl.multiple_of(kb * tk, tk)
        ki = ki_ref[pl.ds(k0, tk), :]
        acc = jnp.zeros((tk, tq), F32)
        for h in range(H_IDX):
            lg = jnp.dot(ki, qit_ref[h * HEAD_PAD:(h + 1) * HEAD_PAD, :], preferred_element_type=F32)
            acc = acc + wit_ref[h:h + 1, :] * jnp.maximum(lg, 0.0)
        if on_diagonal:
            causal = (k0 + row_iota) <= qpos
            sv = jnp.where(causal, acc, -jnp.inf)
            smin = jnp.minimum(smin, jnp.min(jnp.where(causal, acc, jnp.inf), axis=0, keepdims=True))
        else:
            sv = acc
            smin = jnp.minimum(smin, jnp.min(acc, axis=0, keepdims=True))
        sc_ref[pl.ds(k0, tk), :] = sv
        smax = jnp.maximum(smax, jnp.max(sv, axis=0, keepdims=True))
        neg_max = jnp.maximum(neg_max, jnp.max(jnp.where(sv < 0.0, sv, -jnp.inf), axis=0, keepdims=True))
        c_pos = c_pos + fold(jnp.where(sv > 0.0, 1.0, 0.0))
        c_nn = c_nn + fold(jnp.where(sv >= 0.0, 1.0, 0.0))
        return smax, smin, neg_max, c_pos, c_nn

    neg_inf_row = jnp.full((1, tq), -jnp.inf, F32)
    zero_part = jnp.zeros((FOLD_ROWS, tq), F32)
    n_diag = tq // tk
    stats = lax.fori_loop(0, nkb - n_diag, functools.partial(idx_body, on_diagonal=False),
                          (neg_inf_row, jnp.full((1, tq), jnp.inf, F32), neg_inf_row, zero_part, zero_part))
    for j in range(n_diag):
        stats = idx_body(nkb - n_diag + j, stats, on_diagonal=True)
    smax, smin, neg_max, c_pos, c_nn = stats
    c_pos = jnp.sum(c_pos, axis=0, keepdims=True)
    c_nn = jnp.sum(c_nn, axis=0, keepdims=True)

    @pl.when(nkb % 2 == 1)
    def _pad_block():
        sc_ref[pl.ds(pl.multiple_of(nkb * tk, tk), tk), :] = jnp.full((tk, tq), -jnp.inf, F32)

    n_causal = (qpos + 1).astype(F32)
    kf = jnp.minimum(qpos + 1, topk).astype(F32)

    def count_pairs(indicator):
        def body(kp, c):
            k0 = pl.multiple_of(kp * (2 * tk), 2 * tk)
            return c + fold(indicator(sc_ref[pl.ds(k0, 2 * tk), :], k0))
        part = lax.fori_loop(0, npair, body, jnp.zeros((FOLD_ROWS, tq), F32))
        return jnp.sum(part, axis=0, keepdims=True)

    def count_gt(thr):
        return count_pairs(lambda sv, k0: jnp.where(sv > thr, 1.0, 0.0))

    below = smin - (1.0 + jnp.abs(smin))
    zero_tie = jnp.logical_and(c_pos < kf, kf <= c_nn)
    searching = jnp.where(zero_tie, 0.0, 1.0)

    def bisect_cond(carry):
        return jnp.logical_and(carry[0] < BISECT_MAX, carry[1] > 0)

    def bisect_body(carry):
        it, _, lo, hi, c_lo, c_hi = carry
        for _ in range(BISECT_CHECK):
            mid = 0.5 * (lo + hi)
            c_mid = count_gt(mid)
            ge = c_mid >= kf
            up = jnp.logical_and(ge, searching > 0.0)
            down = jnp.logical_and(jnp.logical_not(ge), searching > 0.0)
            lo, c_lo = jnp.where(up, mid, lo), jnp.where(up, c_mid, c_lo)
            hi, c_hi = jnp.where(down, mid, hi), jnp.where(down, c_mid, c_hi)
        open_queries = (jnp.max((c_lo - kf) * searching) > 0.0).astype(jnp.int32)
        return it + BISECT_CHECK, open_queries, lo, hi, c_lo, c_hi

    c_lo0 = jnp.where(zero_tie, c_nn, n_causal)
    first_open = (jnp.max((c_lo0 - kf) * searching) > 0.0).astype(jnp.int32)
    _, _, lo, hi, c_lo, c_hi = lax.while_loop(
        bisect_cond, bisect_body,
        (jnp.int32(0), first_open,
         jnp.where(zero_tie, jnp.maximum(neg_max, below), below), jnp.where(zero_tie, 0.0, smax),
         c_lo0, jnp.where(zero_tie, c_pos, 0.0)))

    @pl.when(jnp.max(c_lo - kf) > 0.0)
    def _break_ties():
        need = kf - c_hi
        prefix = jnp.where(lax.broadcasted_iota(jnp.int32, (tk, tk), 0)
                           >= lax.broadcasted_iota(jnp.int32, (tk, tk), 1), 1.0, 0.0).astype(BF16)

        def body(kb, seen):
            k0 = pl.multiple_of(kb * tk, tk)
            sv = sc_ref[pl.ds(k0, tk), :]
            member = jnp.where(sv > lo, jnp.where(sv > hi, 0.0, 1.0), 0.0)
            rank = jnp.dot(prefix, member.astype(BF16), preferred_element_type=F32) + seen
            sc_ref[pl.ds(k0, tk), :] = jnp.where(member * rank > need, -jnp.inf, sv)
            return rank[tk - 1:tk, :]

        lax.fori_loop(0, nkb, body, jnp.zeros((1, tq), F32))

    m_ref[...] = jnp.full(m_ref.shape, NEG_BIG, F32)
    acc_ref[...] = jnp.zeros(acc_ref.shape, F32)

    def scores_into(kb, park_ref, max_ref):
        k0 = pl.multiple_of(kb * tk, tk)
        bias = jnp.where(sc_ref[pl.ds(k0, tk), :] > lo, 0.0, NEG_BIG)
        for h in range(H_A):
            kh = k_ref[pl.ds(k0, tk), h * HEAD_PAD:(h + 1) * HEAD_PAD]
            s = jnp.dot(kh, qt_ref[h * HEAD_PAD:(h + 1) * HEAD_PAD, :], preferred_element_type=F32) + bias
            park_ref[h] = s
            max_ref[h:h + 1, :] = jnp.max(s, axis=0, keepdims=True)

    def softmax_from(kb, park_ref, max_ref):
        k0 = pl.multiple_of(kb * tk, tk)
        for h in range(H_A):
            m_old = m_ref[h:h + 1, :]
            m_new = jnp.maximum(m_old, max_ref[h:h + 1, :])
            alpha = jnp.exp2(m_old - m_new)
            p = jnp.exp2((park_ref[h] - m_new).astype(BF16))
            rows = slice(h * V_ROWS, (h + 1) * V_ROWS)
            pv = jnp.dot(vt_ref[rows, pl.ds(k0, tk)], p, preferred_element_type=F32)
            acc_ref[rows, :] = alpha * acc_ref[rows, :] + pv
            m_ref[h:h + 1, :] = m_new

    last_block = 2 * npair - 1
    scores_into(0, s_ref, smax_ref)

    def att_body(kp, carry):
        scores_into(2 * kp + 1, s2_ref, smax2_ref)
        softmax_from(2 * kp, s_ref, smax_ref)
        scores_into(jnp.minimum(2 * kp + 2, last_block), s_ref, smax_ref)
        softmax_from(2 * kp + 1, s2_ref, smax2_ref)
        return carry

    lax.fori_loop(0, npair, att_body, 0)
    out = [acc_ref[h * V_ROWS:h * V_ROWS + D_V_A, :] / acc_ref[h * V_ROWS + D_V_A:h * V_ROWS + D_V_A + 1, :]
           for h in range(H_A)]
    o_ref[...] = jnp.concatenate(out, axis=0).T.astype(BF16)


def _dsa_call(qt, qit, wit, k, ki, vt):
    b, hq, s = qt.shape
    tq = TQ_DSA
    topk = min(TOPK_MAX, s // 4)
    colsT = lambda height: pl.BlockSpec((None, height, tq), lambda i, j: (i, 0, j))
    whole = lambda a: pl.BlockSpec((None,) + a.shape[1:], lambda i, j: (i, 0, 0))
    return pl.pallas_call(
        functools.partial(_dsa_kernel, topk=topk),
        out_shape=jax.ShapeDtypeStruct((b, s, WIDTH_A), BF16),
        grid=(b, s // tq),
        in_specs=[colsT(hq), colsT(H_IDX), colsT(hq), whole(ki), whole(k), whole(vt)],
        out_specs=pl.BlockSpec((None, tq, WIDTH_A), lambda i, j: (i, j, 0)),
        scratch_shapes=[pltpu.VMEM((s, tq), F32),
                        pltpu.VMEM((H_A, tq), F32),
                        pltpu.VMEM((H_A * V_ROWS, tq), F32),
                        pltpu.VMEM((H_A, TK_DSA, tq), F32),
                        pltpu.VMEM((H_A, TK_DSA, tq), F32),
                        pltpu.VMEM((H_A, tq), F32),
                        pltpu.VMEM((H_A, tq), F32)],
        compiler_params=_params(2),
        name="dsa",
    )(qit, wit, qt, ki, k, vt)


def _dil_kernel(q_ref, kc_ref, kp_ref, vc_ref, vp_ref, o_ref, lse_ref, s_ref):
    tb = kp_ref.shape[0]
    n_sub = q_ref.shape[0] // tb
    first_step = pl.program_id(2) == 0
    k = jnp.concatenate([kp_ref[...], kc_ref[...]], axis=0)
    v = jnp.concatenate([vp_ref[...], vc_ref[...]], axis=0)
    r = lax.broadcasted_iota(jnp.int32, (tb, 2 * tb), 0)
    c = lax.broadcasted_iota(jnp.int32, (tb, 2 * tb), 1)
    behind = (tb + r - c).astype(jnp.uint32)
    bias = jnp.where(behind <= tb, 0.0, -jnp.inf)
    no_prev = jnp.where(first_step, r, tb).astype(jnp.uint32)
    bias0 = jnp.where(behind <= no_prev, 0.0, -jnp.inf)
    lane = lax.broadcasted_iota(jnp.int32, (1, WIDTH_B), 1)
    head_qk = (lane % 128) // (D_HEAD_B // 2)
    head_v = lane // D_HEAD_B
    nt = (((1,), (1,)), ((), ()))
    for j in range(n_sub):
        q = q_ref[j * tb:(j + 1) * tb, :]
        for h in range(H_B):
            qh = jnp.where(head_qk == h, q, jnp.zeros_like(q))
            s_ref[j * H_B + h] = (lax.dot_general(qh, k[j * tb:(j + 2) * tb], nt, preferred_element_type=F32)
                                  + (bias0 if j == 0 else bias))
    for j in range(n_sub):
        vj = v[j * tb:(j + 2) * tb]
        o = jnp.zeros((tb, WIDTH_B), F32)
        lse = jnp.zeros((tb, WIDTH_B), F32)
        for h in range(H_B):
            s = s_ref[j * H_B + h]
            m = jnp.max(s, axis=1, keepdims=True)
            p = jnp.exp(s - m).astype(BF16)
            pv = jnp.dot(p, jnp.where(head_v == h, vj, jnp.ones_like(vj)), preferred_element_type=F32)
            l = pltpu.roll(pv, WIDTH_B // 2, axis=1)
            o = jnp.where(head_v == h, pv / l, o)
            lse = jnp.where(head_v == h, m + jnp.log(l), lse)
        o_ref[j * tb:(j + 1) * tb, :] = o.astype(o_ref.dtype)
        lse_ref[j * tb:(j + 1) * tb, :] = lse


def _dil_call(qkv):
    b, dilation, n, _ = qkv.shape
    tb, tq = TB_DIL, min(TQ_DIL, n)
    per = tq // tb
    cur = lambda which: pl.BlockSpec((None, None, tq, WIDTH_B), lambda i, r, j: (i, r, j, which))
    prev = lambda which: pl.BlockSpec((None, None, tb, WIDTH_B),
                                      lambda i, r, j: (i, r, jnp.maximum(j * per - 1, 0), which))
    out_spec = pl.BlockSpec((None, None, tq, WIDTH_B), lambda i, r, j: (i, r, j, 0))
    out = lambda dtype: jax.ShapeDtypeStruct((b, dilation, n, WIDTH_B), dtype)
    return pl.pallas_call(
        _dil_kernel,
        out_shape=(out(BF16), out(F32)),
        grid=(b, dilation, n // tq),
        in_specs=[cur(0), cur(1), prev(1), cur(2), prev(2)],
        out_specs=(out_spec, out_spec),
        scratch_shapes=[pltpu.VMEM((per * H_B, tb, 2 * tb), F32)],
        compiler_params=_params(3),
        name="dil",
    )(qkv, qkv, qkv, qkv, qkv)


def _merge_ffn_kernel(x_ref, mod_ref, oa_ref, o0_ref, o1_ref, o2_ref, l0_ref, l1_ref, l2_ref, gates_ref,
                      wua_ref, wub_ref, wo_ref, gpost_ref, gpre2_ref, wg_ref, wu_ref, wd_ref, gpost2_ref,
                      out_ref, acc_ref, *scr):
    tm = x_ref.shape[0]
    tmp = scr[4]

    def token_major(ref, buf):
        dilation = ref.shape[0]
        if dilation == 1:
            return ref[0]
        n_slab = ref.shape[2] // LANES
        for c in range(n_slab):
            if dilation <= 4:
                for r in range(dilation):
                    buf[c, pl.ds(r, tm // dilation, stride=dilation), :] = (
                        ref[r, :, c * LANES:(c + 1) * LANES].astype(F32))
            else:
                quarter = tm // 4
                for r in range(dilation):
                    tmp[c, pl.ds((r % 4) * quarter + r // 4, tm // dilation, stride=dilation // 4), :] = (
                        ref[r, :, c * LANES:(c + 1) * LANES].astype(F32))
                for r4 in range(4):
                    buf[c, pl.ds(r4, quarter, stride=4), :] = tmp[c, r4 * quarter:(r4 + 1) * quarter, :]
        return jnp.concatenate([buf[c] for c in range(n_slab)], axis=1)

    o0, l0 = token_major(o0_ref, None).astype(F32), token_major(l0_ref, None)
    o1, l1 = token_major(o1_ref, scr[0]), token_major(l1_ref, scr[1])
    o2, l2 = token_major(o2_ref, scr[2]), token_major(l2_ref, scr[3])
    m = jnp.maximum(jnp.maximum(l0, l1), l2)
    e0, e1, e2 = jnp.exp(l0 - m), jnp.exp(l1 - m), jnp.exp(l2 - m)
    ob = (e0 * o0 + e1 * o1 + e2 * o2) / (e0 + e1 + e2)
    za = jnp.dot(oa_ref[...], wua_ref[...], preferred_element_type=F32)
    zb = jnp.dot(ob.astype(BF16), wub_ref[...], preferred_element_type=F32)
    d = za.shape[1]
    z = gates_ref[:, 0:d].astype(F32) * za + gates_ref[:, d:2 * d].astype(F32) * zb
    y = jnp.dot(z.astype(BF16), wo_ref[...], preferred_element_type=F32)
    x_mixed = x_ref[...] + mod_ref[5:6, :] * (_rms_rows(y) * gpost_ref[...])
    _ffn_tile(x_mixed, mod_ref, gpre2_ref, wg_ref, wu_ref, wd_ref, gpost2_ref, out_ref, acc_ref, 6)


def _merge_ffn_call(x, mod, oa, dil, gates, w_up_a, w_up_b, w_o, g_post, g_pre2, w_gate, w_up, w_down, g_post2):
    b, s, d = x.shape
    tm = TM_FFN
    rows = lambda width: pl.BlockSpec((None, tm, width), lambda i, j: (i, j, 0))
    const = lambda a: pl.BlockSpec(a.shape, lambda i, j: (0,) * a.ndim, pipeline_mode=pl.Buffered(1))
    (o0, l0), (o1, l1), (o2, l2) = dil
    res = lambda a: pl.BlockSpec((None, a.shape[1], tm // a.shape[1], WIDTH_B), lambda i, j: (i, 0, j, 0))
    consts = [w_up_a, w_up_b, w_o, g_post, g_pre2, w_gate, w_up, w_down, g_post2]
    return pl.pallas_call(
        _merge_ffn_kernel,
        out_shape=jax.ShapeDtypeStruct((b, s, d), F32),
        grid=(b, s // tm),
        in_specs=[rows(d), pl.BlockSpec((None, N_MOD, d), lambda i, j: (i, 0, 0)), rows(WIDTH_A)]
                 + [res(a) for a in (o0, o1, o2, l0, l1, l2)]
                 + [rows(2 * d)] + [const(a) for a in consts],
        out_specs=rows(d),
        scratch_shapes=[pltpu.VMEM((tm, d), F32)] + [pltpu.VMEM((WIDTH_B // LANES, tm, LANES), F32)] * 5,
        compiler_params=_params(2),
        name="merge_ffn",
    )(x, mod, oa, o0, o1, o2, l0, l1, l2, gates, *consts)


def _pad_heads(w_t):
    h, r, n = w_t.shape
    return jnp.pad(w_t, ((0, 0), (0, HEAD_PAD - r), (0, 0))).reshape(h * HEAD_PAD, n)


def _mixer_weights(w_in, g_cq, g_ckv, w_uq, w_uk, w_uv, w_iq):
    d = w_in.shape[0]
    o_cq, o_ckv, o_kr, o_ki, o_wi = 0, Q_LORA, Q_LORA + KV_LORA, Q_LORA + KV_LORA + D_ROPE_A, \
        Q_LORA + KV_LORA + D_ROPE_A + D_IDX
    o_qkv = o_wi + H_IDX
    o_gates = o_qkv + N_QKV_B

    def swap_halves(w):
        half = w.shape[1] // 2
        return jnp.concatenate([w[:, half:], w[:, :half]], axis=1)

    w_kr, w_ki = w_in[:, o_kr:o_kr + D_ROPE_A], w_in[:, o_ki:o_ki + D_IDX]
    qkv = w_in[:, o_qkv:o_gates].reshape(d, 3, N_GROUPS_B, H_B, 2, D_HEAD_B // 2)
    qk_split = qkv[:, 0:2].transpose(0, 1, 2, 4, 3, 5).reshape(d, 2 * N_GROUPS_B * WIDTH_B)
    v_cols = qkv[:, 2].reshape(d, N_GROUPS_B * WIDTH_B)
    w_std = jnp.concatenate([w_in[:, o_ckv:o_ckv + KV_LORA], w_kr, swap_halves(w_kr), w_ki, swap_halves(w_ki),
                             qk_split, v_cols, w_in[:, o_gates:]], axis=1)
    w_tr = jnp.concatenate([w_in[:, o_cq:o_cq + Q_LORA], w_in[:, o_ckv:o_ckv + KV_LORA],
                            w_in[:, o_wi:o_wi + H_IDX]], axis=1).T
    w_tr = jnp.pad(w_tr, ((0, N_TR - w_tr.shape[0]), (0, 0)))

    e_kr = np.zeros((D_ROPE_A, H_A * HEAD_PAD), np.float32)
    for h in range(H_A):
        e_kr[np.arange(D_ROPE_A), h * HEAD_PAD + D_NOPE + np.arange(D_ROPE_A)] = 1.0
    e_ki = np.zeros((D_IDX, LANES), np.float32)
    e_ki[np.arange(D_IDX), np.arange(D_IDX)] = 1.0
    return {
        "w_std": w_std.astype(BF16),
        "w_tr": w_tr.astype(BF16),
        "g_cq_col": g_cq.reshape(Q_LORA, 1),
        "g_ckv_col": g_ckv.reshape(KV_LORA, 1),
        "g_ckv_row": g_ckv.reshape(1, KV_LORA),
        "w_uq_t": _pad_heads(w_uq.transpose(1, 2, 0)).astype(BF16),
        "w_iq_t": _pad_heads(w_iq.transpose(1, 2, 0)).astype(BF16),
        "w_uk_p": jnp.pad(w_uk, ((0, 0), (0, 0), (0, HEAD_PAD - D_NOPE))).reshape(KV_LORA, H_A * HEAD_PAD).astype(BF16),
        "w_uv_t": w_uv.transpose(1, 2, 0).reshape(WIDTH_A, KV_LORA).astype(BF16),
        "e_kr": jnp.asarray(e_kr, BF16),
        "e_ki": jnp.asarray(e_ki, BF16),
    }


def kernel(x, c, positions, w_mod, b_mod, g_pre_ffn1, w_gate1, w_up1, w_down1, g_post_ffn1, g_pre_mix, w_in, g_cq, g_ckv, w_uq, w_uk, w_uv, w_iq, w_up_a, w_up_b, w_o, g_post_mix, g_pre_ffn2, w_gate2, w_up2, w_down2, g_post_ffn2):
    b, s, d = x.shape
    assert s % (16 * TB_DIL) == 0 and s % TM_FFN == 0 and b <= 8
    c8 = jnp.pad(c, ((0, 8 - b), (0, 0)))
    half_b, half_a = D_HEAD_B // 2, D_ROPE_A // 2
    inv = jnp.concatenate([ROPE_THETA ** (-jnp.arange(half_b, dtype=F32) / half_b),
                           ROPE_THETA ** (-jnp.arange(half_a, dtype=F32) / half_a)]).reshape(48, 1)
    tabs = _rope_call(positions.reshape(b, 1, s), inv)
    row = lambda g: g.reshape(1, -1)
    for l in range(w_mod.shape[0]):
        mod = _mod_call(c8, w_mod[l], b_mod[l].reshape(1, -1))[:b].reshape(b, N_MOD, d)
        x = _ffn_call(x, mod, row(g_pre_ffn1[l]), w_gate1[l].astype(BF16), w_up1[l].astype(BF16),
                      w_down1[l].astype(BF16), row(g_post_ffn1[l]), 0)
        w = _mixer_weights(w_in[l], g_cq[l], g_ckv[l], w_uq[l], w_uk[l], w_uv[l], w_iq[l])
        qt, qit, wit, k, ki, vt, dq0, dq1, dq2, gates = _proj_call(x, mod, row(g_pre_mix[l]), w, tabs)
        oa = _dsa_call(qt, qit, wit, k, ki, vt)
        dil = [_dil_call(dq) for dq in (dq0, dq1, dq2)]
        x = _merge_ffn_call(x, mod, oa, dil, gates, w_up_a[l].astype(BF16), w_up_b[l].astype(BF16),
                            w_o[l].astype(BF16), row(g_post_mix[l]), row(g_pre_ffn2[l]), w_gate2[l].astype(BF16),
                            w_up2[l].astype(BF16), w_down2[l].astype(BF16), row(g_post_ffn2[l]))
    return x
```

```python
import functools

import numpy as np
import jax
import jax.numpy as jnp
from jax import lax
from jax.experimental import pallas as pl
from jax.experimental.pallas import tpu as pltpu

F32 = jnp.float32
BF16 = jnp.bfloat16

D_FF = 2816
ROPE_THETA = 10000.0
NORM_EPS = 1e-6
H_A = 8
Q_LORA = 384
KV_LORA = 256
D_NOPE = 64
D_ROPE_A = 32
D_V_A = 64
H_IDX = 8
D_IDX = 32
TOPK_MAX = 256
DIL_PAIRS = ((128, 1), (512, 4), (2048, 16))
N_GROUPS_B = 3
H_B = 4
D_HEAD_B = 64
N_MOD = 9
WIDTH_A = H_A * D_V_A
WIDTH_B = H_B * D_HEAD_B
N_QKV_B = 3 * N_GROUPS_B * WIDTH_B

LANES = 128
HEAD_PAD = 128
VMEM_LIMIT = 56 * 1024 * 1024

TM_FFN = 512
CH_FFN = 256
TP_PROJ = 512
TQ_DSA = 512
TK_DSA = 256
V_ROWS = D_V_A + 16
TB_DIL = 128
TQ_DIL = 512
BISECT_MAX = 34
BISECT_CHECK = 2
FOLD_ROWS = 16
COUNT_ROWS = 64
IDX_ROWS = 64
NEG_BIG = -1e30
LOG2_E = 1.4426950408889634

C_CKV = 0
C_KR = 256
C_KR_SW = 288
C_KI = 320
C_KI_SW = 352
C_QKVB = 384
C_GATES = C_QKVB + N_QKV_B
N_STD = C_GATES + 2048
R_CQ = 0
R_CKV = Q_LORA
R_WI = Q_LORA + KV_LORA
N_TR = 768


def _params(n_axes):
    return pltpu.CompilerParams(dimension_semantics=("arbitrary",) * n_axes,
                                vmem_limit_bytes=VMEM_LIMIT)


def _rms_rows(x):
    return x * lax.rsqrt(jnp.mean(x * x, axis=-1, keepdims=True) + NORM_EPS)


def _mod_kernel(c_ref, w_ref, b_ref, o_ref):
    c = c_ref[...]
    o_ref[...] = jnp.dot(c * jax.nn.sigmoid(c), w_ref[...], preferred_element_type=F32) + b_ref[...]


def _mod_call(c8, w_mod, b_mod):
    d = c8.shape[1]
    return pl.pallas_call(
        _mod_kernel,
        out_shape=jax.ShapeDtypeStruct((8, N_MOD * d), F32),
        grid=(N_MOD,),
        in_specs=[pl.BlockSpec((8, d), lambda j: (0, 0)),
                  pl.BlockSpec((d, d), lambda j: (0, j)),
                  pl.BlockSpec((1, d), lambda j: (0, j))],
        out_specs=pl.BlockSpec((8, d), lambda j: (0, j)),
        compiler_params=_params(1),
        name="mod",
    )(c8, w_mod, b_mod)


def _rope_kernel(pos_ref, inv_ref, tt_ref, ca_ref, sa_ref, cb_ref, sb_ref):
    s = pos_ref.shape[1]
    ang = inv_ref[...] * pos_ref[...].astype(F32)
    cos = jnp.cos(ang)
    sin = jnp.sin(ang)
    cos_b, sin_b = cos[32:48], sin[32:48]
    tt_ref[0] = cos_b
    tt_ref[1] = sin_b
    cos_a = jnp.concatenate([cos[0:32]] * 4, axis=0)
    sin_a = jnp.concatenate([sin[0:32]] * 4, axis=0)
    cos_bt = jnp.concatenate([cos_b] * 8, axis=0)
    sin_bt = jnp.concatenate([-sin_b, sin_b] * 4, axis=0)
    for j in range(s // LANES):
        cols = slice(j * LANES, (j + 1) * LANES)
        ca_ref[cols, :] = cos_a[:, cols].T
        sa_ref[cols, :] = sin_a[:, cols].T
        cb_ref[cols, :] = cos_bt[:, cols].T
        sb_ref[cols, :] = sin_bt[:, cols].T


def _rope_call(pos3, inv):
    b, _, s = pos3.shape
    tab = jax.ShapeDtypeStruct((b, s, LANES), F32)
    tab_spec = pl.BlockSpec((None, s, LANES), lambda i: (i, 0, 0))
    return pl.pallas_call(
        _rope_kernel,
        out_shape=(jax.ShapeDtypeStruct((b, 2, 16, s), F32), tab, tab, tab, tab),
        grid=(b,),
        in_specs=[pl.BlockSpec((None, 1, s), lambda i: (i, 0, 0)),
                  pl.BlockSpec((48, 1), lambda i: (0, 0))],
        out_specs=(pl.BlockSpec((None, 2, 16, s), lambda i: (i, 0, 0, 0)),
                   tab_spec, tab_spec, tab_spec, tab_spec),
        compiler_params=_params(1),
        name="rope",
    )(pos3, inv)


def _ffn_kernel(x_ref, mod_ref, gpre_ref, wg_ref, wu_ref, wd_ref, gpost_ref, o_ref, acc_ref, *, j0):
    _ffn_tile(x_ref[...], mod_ref, gpre_ref, wg_ref, wu_ref, wd_ref, gpost_ref, o_ref, acc_ref, j0)


def _ffn_tile(x, mod_ref, gpre_ref, wg_ref, wu_ref, wd_ref, gpost_ref, o_ref, acc_ref, j0):
    sh, sc, gt = mod_ref[j0:j0 + 1, :], mod_ref[j0 + 1:j0 + 2, :], mod_ref[j0 + 2:j0 + 3, :]
    h = (_rms_rows(x) * gpre_ref[...] * (1.0 + sc) + sh).astype(BF16)
    n_ch = wg_ref.shape[1] // CH_FFN
    for ch in range(n_ch):
        cols = slice(ch * CH_FFN, (ch + 1) * CH_FFN)
        g = jnp.dot(h, wg_ref[:, cols], preferred_element_type=F32)
        u = jnp.dot(h, wu_ref[:, cols], preferred_element_type=F32)
        a = (g * jax.nn.sigmoid(g) * u).astype(BF16)
        part = jnp.dot(a, wd_ref[cols, :], preferred_element_type=F32)
        if ch == 0:
            acc_ref[...] = part
        else:
            acc_ref[...] += part
    o_ref[...] = x + 0.5 * gt * (_rms_rows(acc_ref[...]) * gpost_ref[...])


def _ffn_call(x, mod, g_pre, w_gate, w_up, w_down, g_post, j0):
    b, s, d = x.shape
    f = w_gate.shape[1]
    const = lambda shape: pl.BlockSpec(shape, lambda i, j: (0,) * len(shape), pipeline_mode=pl.Buffered(1))
    tile = pl.BlockSpec((None, TM_FFN, d), lambda i, j: (i, j, 0))
    return pl.pallas_call(
        functools.partial(_ffn_kernel, j0=j0),
        out_shape=jax.ShapeDtypeStruct((b, s, d), F32),
        grid=(b, s // TM_FFN),
        in_specs=[tile,
                  pl.BlockSpec((None, N_MOD, d), lambda i, j: (i, 0, 0)),
                  const((1, d)), const((d, f)), const((d, f)), const((f, d)), const((1, d))],
        out_specs=tile,
        scratch_shapes=[pltpu.VMEM((TM_FFN, d), F32)],
        compiler_params=_params(2),
        name="ffn",
    )(x, mod, g_pre, w_gate, w_up, w_down, g_post)


def _proj_kernel(x_ref, mod_ref, gpre_ref, wstd_ref, wtr_ref, gcq_ref, gckv_col_ref, gckv_row_ref,
                 wuq_ref, wiq_ref, wuk_ref, wuv_ref, ekr_ref, eki_ref,
                 tt_ref, ca_ref, sa_ref, cb_ref, sb_ref,
                 qt_ref, qit_ref, wit_ref, k_ref, ki_ref, vt_ref, dq0_ref, dq1_ref, dq2_ref, gates_ref,
                 dil_scr, dil_tmp):
    x = x_ref[...]
    sh, sc = mod_ref[3:4, :], mod_ref[4:5, :]
    u = (_rms_rows(x) * gpre_ref[...] * (1.0 + sc) + sh).astype(BF16)
    std = jnp.dot(u, wstd_ref[...], preferred_element_type=F32)
    tr = lax.dot_general(wtr_ref[...], u, (((1,), (1,)), ((), ())),
                         preferred_element_type=F32)

    cq = tr[R_CQ:R_CQ + Q_LORA]
    cq = cq * lax.rsqrt(jnp.mean(cq * cq, axis=0, keepdims=True) + NORM_EPS) * gcq_ref[...]
    cq = cq.astype(BF16)
    ckv_t = tr[R_CKV:R_CKV + KV_LORA]
    ckv_t = ckv_t * lax.rsqrt(jnp.mean(ckv_t * ckv_t, axis=0, keepdims=True) + NORM_EPS) * gckv_col_ref[...]
    cos_t, sin_t = tt_ref[0], tt_ref[1]

    attn_scale = (D_NOPE + D_ROPE_A) ** -0.5 * LOG2_E
    q_t = jnp.dot(wuq_ref[...], cq, preferred_element_type=F32) * attn_scale
    idx_scale = D_IDX ** -0.5
    qi_t = jnp.dot(wiq_ref[...], cq, preferred_element_type=F32) * idx_scale
    for h in range(H_A):
        r0 = h * HEAD_PAD
        x1, x2 = q_t[r0 + 64:r0 + 80], q_t[r0 + 80:r0 + 96]
        qt_ref[r0:r0 + 64, :] = q_t[r0:r0 + 64].astype(BF16)
        qt_ref[r0 + 64:r0 + 80, :] = (x1 * cos_t - x2 * sin_t).astype(BF16)
        qt_ref[r0 + 80:r0 + 96, :] = (x2 * cos_t + x1 * sin_t).astype(BF16)
        qt_ref[r0 + 96:r0 + 128, :] = q_t[r0 + 96:r0 + 128].astype(BF16)
        y1, y2 = qi_t[r0:r0 + 16], qi_t[r0 + 16:r0 + 32]
        qit_ref[r0:r0 + 16, :] = (y1 * cos_t - y2 * sin_t).astype(BF16)
        qit_ref[r0 + 16:r0 + 32, :] = (y2 * cos_t + y1 * sin_t).astype(BF16)
        qit_ref[r0 + 32:r0 + 128, :] = qi_t[r0 + 32:r0 + 128].astype(BF16)
    wit_ref[...] = tr[R_WI:R_WI + H_IDX] * (H_IDX ** -0.5)
    v_t = jnp.dot(wuv_ref[...], ckv_t.astype(BF16), preferred_element_type=F32).astype(BF16)
    for h in range(H_A):
        vt_ref[h * V_ROWS:h * V_ROWS + D_V_A, :] = v_t[h * D_V_A:(h + 1) * D_V_A]
        vt_ref[h * V_ROWS + D_V_A:(h + 1) * V_ROWS, :] = jnp.ones((V_ROWS - D_V_A, v_t.shape[1]), BF16)

    ckv = std[:, C_CKV:C_CKV + KV_LORA]
    ckv = (_rms_rows(ckv) * gckv_row_ref[...]).astype(BF16)
    k_nope = jnp.dot(ckv, wuk_ref[...], preferred_element_type=F32)
    cos_k, sin_k = cb_ref[:, 0:32], sb_ref[:, 0:32]
    k_rope = std[:, C_KR:C_KR + 32] * cos_k + std[:, C_KR_SW:C_KR_SW + 32] * sin_k
    k_ref[...] = (k_nope + jnp.dot(k_rope.astype(BF16), ekr_ref[...],
                                   preferred_element_type=F32)).astype(BF16)
    k_idx = std[:, C_KI:C_KI + 32] * cos_k + std[:, C_KI_SW:C_KI_SW + 32] * sin_k
    ki_ref[...] = jnp.dot(k_idx.astype(BF16), eki_ref[...], preferred_element_type=F32).astype(BF16)

    cos_a, sin_a = ca_ref[...], sa_ref[...]
    tp = x.shape[0]
    for g, (out_ref, (_, dilation)) in enumerate(zip((dq0_ref, dq1_ref, dq2_ref), DIL_PAIRS)):
        for which in range(2):
            scale = (D_HEAD_B ** -0.5) if which == 0 else 1.0
            c0 = C_QKVB + (which * N_GROUPS_B + g) * WIDTH_B
            x1, x2 = std[:, c0:c0 + 128], std[:, c0 + 128:c0 + 256]
            dil_scr[2 * which] = (x1 * cos_a - x2 * sin_a) * scale
            dil_scr[2 * which + 1] = (x2 * cos_a + x1 * sin_a) * scale
        c0 = C_QKVB + (2 * N_GROUPS_B + g) * WIDTH_B
        dil_scr[4] = std[:, c0:c0 + 128]
        dil_scr[5] = std[:, c0 + 128:c0 + 256]
        for c in range(3 * WIDTH_B // LANES):
            if dilation <= 4:
                for r in range(dilation):
                    out_ref[r, :, c * LANES:(c + 1) * LANES] = (
                        dil_scr[c, pl.ds(r, tp // dilation, stride=dilation), :].astype(BF16))
            else:
                quarter = tp // 4
                for r4 in range(4):
                    dil_tmp[c, r4 * quarter:(r4 + 1) * quarter, :] = dil_scr[c, pl.ds(r4, quarter, stride=4), :]
                for r in range(dilation):
                    out_ref[r, :, c * LANES:(c + 1) * LANES] = dil_tmp[
                        c, pl.ds((r % 4) * quarter + r // 4, tp // dilation, stride=dilation // 4), :].astype(BF16)
    gates_ref[...] = jax.nn.sigmoid(std[:, C_GATES:]).astype(BF16)


def _proj_call(x, mod, g_pre, w, tabs):
    b, s, d = x.shape
    tt, ca, sa, cb, sb = tabs
    tp = TP_PROJ
    const = lambda a: pl.BlockSpec(a.shape, lambda i, j: (0,) * a.ndim)
    rows = lambda width: pl.BlockSpec((None, tp, width), lambda i, j: (i, j, 0))
    colsT = lambda height: pl.BlockSpec((None, height, tp), lambda i, j: (i, 0, j))
    consts = [g_pre, w["w_std"], w["w_tr"], w["g_cq_col"], w["g_ckv_col"], w["g_ckv_row"],
              w["w_uq_t"], w["w_iq_t"], w["w_uk_p"], w["w_uv_t"], w["e_kr"], w["e_ki"]]
    hq = H_A * HEAD_PAD
    dil_shapes = [jax.ShapeDtypeStruct((b, dil, s // dil, 3 * WIDTH_B), BF16) for _, dil in DIL_PAIRS]
    dil_specs = [pl.BlockSpec((None, dil, tp // dil, 3 * WIDTH_B), lambda i, j: (i, 0, j, 0))
                 for _, dil in DIL_PAIRS]
    return pl.pallas_call(
        _proj_kernel,
        out_shape=(jax.ShapeDtypeStruct((b, hq, s), BF16),
                   jax.ShapeDtypeStruct((b, hq, s), BF16),
                   jax.ShapeDtypeStruct((b, H_IDX, s), F32),
                   jax.ShapeDtypeStruct((b, s, hq), BF16),
                   jax.ShapeDtypeStruct((b, s, LANES), BF16),
                   jax.ShapeDtypeStruct((b, H_A * V_ROWS, s), BF16),
                   *dil_shapes,
                   jax.ShapeDtypeStruct((b, s, 2048), BF16)),
        grid=(b, s // tp),
        in_specs=[rows(d), pl.BlockSpec((None, N_MOD, d), lambda i, j: (i, 0, 0))]
                 + [const(a) for a in consts]
                 + [pl.BlockSpec((None, 2, 16, tp), lambda i, j: (i, 0, 0, j)),
                    rows(LANES), rows(LANES), rows(LANES), rows(LANES)],
        out_specs=(colsT(hq), colsT(hq), colsT(H_IDX), rows(hq), rows(LANES), colsT(H_A * V_ROWS),
                   *dil_specs, rows(2048)),
        scratch_shapes=[pltpu.VMEM((3 * WIDTH_B // LANES, tp, LANES), F32)] * 2,
        compiler_params=_params(2),
        name="proj",
    )(x, mod, *consts, tt, ca, sa, cb, sb)


def _dsa_kernel(qit_ref, wit_ref, qt_ref, ki_ref, k_ref, vt_ref, o_ref,
                sc_ref, m_ref, acc_ref, s_ref, s2_ref, smax_ref, smax2_ref, *, topk):
    tq = qt_ref.shape[1]
    tk = s_ref.shape[1]
    i = pl.program_id(1)
    nkb = (i + 1) * (tq // tk)
    npair = (nkb + 1) // 2
    qpos = i * tq + lax.broadcasted_iota(jnp.int32, (1, tq), 1)
    row_iota = lax.broadcasted_iota(jnp.int32, (tk, tq), 0)

    def fold(v):
        return jnp.sum(v.reshape(v.shape[0] // FOLD_ROWS, FOLD_ROWS, tq), axis=0)

    def park_logits(kb, park_ref):
        ki = ki_ref[pl.ds(pl.multiple_of(kb * tk, tk), tk), :]
        for h in range(H_IDX):
            park_ref[h] = jnp.dot(ki, qit_ref[h * HEAD_PAD:(h + 1) * HEAD_PAD, :], preferred_element_type=F32)

    def rows8(v, op):
        return op(v.reshape(IDX_ROWS // 8, 8, LANES), axis=0)

    piece_iota = lax.broadcasted_iota(jnp.int32, (IDX_ROWS, LANES), 0)

    def reduce_block(kb, park_ref, stats, on_diagonal):
        k0 = pl.multiple_of(kb * tk, tk)
        columns = []
        for c in range(tq // LANES):
            lanes = slice(c * LANES, (c + 1) * LANES)
            smax, smin, neg_max, c_pos, c_nn = [v[:, lanes] for v in stats]
            w = [wit_ref[h:h + 1, lanes] for h in range(H_IDX)]
            for j in range(tk // IDX_ROWS):
                r0 = j * IDX_ROWS
                acc = jnp.zeros((IDX_ROWS, LANES), F32)
                for h in range(H_IDX):
                    acc = acc + w[h] * jnp.maximum(park_ref[h, r0:r0 + IDX_ROWS, lanes], 0.0)
                if on_diagonal:
                    causal = (k0 + r0 + piece_iota) <= qpos[:, lanes]
                    sv = jnp.where(causal, acc, -jnp.inf)
                    smin = jnp.minimum(smin, rows8(jnp.where(causal, acc, jnp.inf), jnp.min))
                else:
                    sv = acc
                    smin = jnp.minimum(smin, rows8(acc, jnp.min))
                sc_ref[pl.ds(k0 + r0, IDX_ROWS), lanes] = sv
                smax = jnp.maximum(smax, rows8(sv, jnp.max))
                neg_max = jnp.maximum(neg_max, rows8(jnp.where(sv < 0.0, sv, -jnp.inf), jnp.max))
                c_pos = c_pos + rows8(jnp.where(sv > 0.0, 1.0, 0.0), jnp.sum)
                c_nn = c_nn + rows8(jnp.where(sv >= 0.0, 1.0, 0.0), jnp.sum)
            columns.append((smax, smin, neg_max, c_pos, c_nn))
        return tuple(jnp.concatenate(parts, axis=1) for parts in zip(*columns))

    def idx_pair(kp, stats):
        park_logits(2 * kp + 1, s2_ref)
        stats = reduce_block(2 * kp, s_ref, stats, False)
        park_logits(2 * kp + 2, s_ref)
        return reduce_block(2 * kp + 1, s2_ref, stats, False)

    neg_inf8 = jnp.full((8, tq), -jnp.inf, F32)
    zero8 = jnp.zeros((8, tq), F32)
    park_logits(0, s_ref)
    stats = lax.fori_loop(0, i, idx_pair, (neg_inf8, jnp.full((8, tq), jnp.inf, F32), neg_inf8, zero8, zero8))
    park_logits(nkb - 1, s2_ref)
    stats = reduce_block(nkb - 2, s_ref, stats, True)
    stats = reduce_block(nkb - 1, s2_ref, stats, True)
    smax, neg_max = [jnp.max(v, axis=0, keepdims=True) for v in (stats[0], stats[2])]
    smin = jnp.min(stats[1], axis=0, keepdims=True)
    c_pos, c_nn = [jnp.sum(v, axis=0, keepdims=True) for v in stats[3:]]

    @pl.when(nkb % 2 == 1)
    def _pad_block():
        sc_ref[pl.ds(pl.multiple_of(nkb * tk, tk), tk), :] = jnp.full((tk, tq), -jnp.inf, F32)

    n_causal = (qpos + 1).astype(F32)
    kf = jnp.minimum(qpos + 1, topk).astype(F32)

    def count_gt(thr):
        def body(kp, c):
            k0 = pl.multiple_of(kp * (2 * tk), 2 * tk)
            for j in range(2 * tk // COUNT_ROWS):
                rows = pl.ds(k0 + j * COUNT_ROWS, COUNT_ROWS)
                c = c + fold(jnp.where(sc_ref[rows, :] > thr, 1.0, 0.0))
            return c
        part = lax.fori_loop(0, npair, body, jnp.zeros((FOLD_ROWS, tq), F32))
        return jnp.sum(part, axis=0, keepdims=True)

    below = smin - (1.0 + jnp.abs(smin))
    zero_tie = jnp.logical_and(c_pos < kf, kf <= c_nn)
    searching = jnp.where(zero_tie, 0.0, 1.0)

    def bisect_cond(carry):
        return jnp.logical_and(carry[0] < BISECT_MAX, carry[1] > 0)

    def bisect_body(carry):
        it, _, lo, hi, c_lo, c_hi = carry
        for _ in range(BISECT_CHECK):
            mid = 0.5 * (lo + hi)
            c_mid = count_gt(mid)
            ge = c_mid >= kf
            up = jnp.logical_and(ge, searching > 0.0)
            down = jnp.logical_and(jnp.logical_not(ge), searching > 0.0)
            lo, c_lo = jnp.where(up, mid, lo), jnp.where(up, c_mid, c_lo)
            hi, c_hi = jnp.where(down, mid, hi), jnp.where(down, c_mid, c_hi)
        open_queries = (jnp.max((c_lo - kf) * searching) > 0.0).astype(jnp.int32)
        return it + BISECT_CHECK, open_queries, lo, hi, c_lo, c_hi

    c_lo0 = jnp.where(zero_tie, c_nn, n_causal)
    first_open = (jnp.max((c_lo0 - kf) * searching) > 0.0).astype(jnp.int32)
    _, _, lo, hi, c_lo, c_hi = lax.while_loop(
        bisect_cond, bisect_body,
        (jnp.int32(0), first_open,
         jnp.where(zero_tie, jnp.maximum(neg_max, below), below), jnp.where(zero_tie, 0.0, smax),
         c_lo0, jnp.where(zero_tie, c_pos, 0.0)))

    @pl.when(jnp.max(c_lo - kf) > 0.0)
    def _break_ties():
        need = kf - c_hi
        prefix = jnp.where(lax.broadcasted_iota(jnp.int32, (tk, tk), 0)
                           >= lax.broadcasted_iota(jnp.int32, (tk, tk), 1), 1.0, 0.0).astype(BF16)

        def body(kb, seen):
            k0 = pl.multiple_of(kb * tk, tk)
            sv = sc_ref[pl.ds(k0, tk), :]
            member = jnp.where(sv > lo, jnp.where(sv > hi, 0.0, 1.0), 0.0)
            rank = jnp.dot(prefix, member.astype(BF16), preferred_element_type=F32) + seen
            sc_ref[pl.ds(k0, tk), :] = jnp.where(member * rank > need, -jnp.inf, sv)
            return rank[tk - 1:tk, :]

        lax.fori_loop(0, nkb, body, jnp.zeros((1, tq), F32))

    m_ref[...] = jnp.full(m_ref.shape, NEG_BIG, F32)
    acc_ref[...] = jnp.zeros(acc_ref.shape, F32)

    def scores_into(kb, park_ref, max_ref):
        k0 = pl.multiple_of(kb * tk, tk)
        bias = jnp.where(sc_ref[pl.ds(k0, tk), :] > lo, 0.0, NEG_BIG)
        for h in range(H_A):
            kh = k_ref[pl.ds(k0, tk), h * HEAD_PAD:(h + 1) * HEAD_PAD]
            s = jnp.dot(kh, qt_ref[h * HEAD_PAD:(h + 1) * HEAD_PAD, :], preferred_element_type=F32) + bias
            park_ref[h] = s
            max_ref[h:h + 1, :] = jnp.max(s, axis=0, keepdims=True)

    def softmax_from(kb, park_ref, max_ref):
        k0 = pl.multiple_of(kb * tk, tk)
        for h in range(H_A):
            m_old = m_ref[h:h + 1, :]
            m_new = jnp.maximum(m_old, max_ref[h:h + 1, :])
            alpha = jnp.exp2(m_old - m_new)
            p = jnp.exp2((park_ref[h] - m_new).astype(BF16))
            rows = slice(h * V_ROWS, (h + 1) * V_ROWS)
            pv = jnp.dot(vt_ref[rows, pl.ds(k0, tk)], p, preferred_element_type=F32)
            acc_ref[rows, :] = alpha * acc_ref[rows, :] + pv
            m_ref[h:h + 1, :] = m_new

    last_block = 2 * npair - 1
    scores_into(0, s_ref, smax_ref)

    def att_body(kp, carry):
        scores_into(2 * kp + 1, s2_ref, smax2_ref)
        softmax_from(2 * kp, s_ref, smax_ref)
        scores_into(jnp.minimum(2 * kp + 2, last_block), s_ref, smax_ref)
        softmax_from(2 * kp + 1, s2_ref, smax2_ref)
        return carry

    lax.fori_loop(0, npair, att_body, 0)
    out = [acc_ref[h * V_ROWS:h * V_ROWS + D_V_A, :] / acc_ref[h * V_ROWS + D_V_A:h * V_ROWS + D_V_A + 1, :]
           for h in range(H_A)]
    o_ref[...] = jnp.concatenate(out, axis=0).T.astype(BF16)


def _dsa_call(qt, qit, wit, k, ki, vt):
    b, hq, s = qt.shape
    tq = TQ_DSA
    topk = min(TOPK_MAX, s // 4)
    colsT = lambda height: pl.BlockSpec((None, height, tq), lambda i, j: (i, 0, j))
    whole = lambda a: pl.BlockSpec((None,) + a.shape[1:], lambda i, j: (i, 0, 0))
    return pl.pallas_call(
        functools.partial(_dsa_kernel, topk=topk),
        out_shape=jax.ShapeDtypeStruct((b, s, WIDTH_A), BF16),
        grid=(b, s // tq),
        in_specs=[colsT(hq), colsT(H_IDX), colsT(hq), whole(ki), whole(k), whole(vt)],
        out_specs=pl.BlockSpec((None, tq, WIDTH_A), lambda i, j: (i, j, 0)),
        scratch_shapes=[pltpu.VMEM((s, tq), F32),
                        pltpu.VMEM((H_A, tq), F32),
                        pltpu.VMEM((H_A * V_ROWS, tq), F32),
                        pltpu.VMEM((H_A, TK_DSA, tq), F32),
                        pltpu.VMEM((H_A, TK_DSA, tq), F32),
                        pltpu.VMEM((H_A, tq), F32),
                        pltpu.VMEM((H_A, tq), F32)],
        compiler_params=_params(2),
        name="dsa",
    )(qit, wit, qt, ki, k, vt)


def _dil_kernel(q_ref, kc_ref, kp_ref, vc_ref, vp_ref, o_ref, lse_ref, s_ref):
    tb = kp_ref.shape[0]
    n_sub = q_ref.shape[0] // tb
    first_step = pl.program_id(2) == 0
    k = jnp.concatenate([kp_ref[...], kc_ref[...]], axis=0)
    v = jnp.concatenate([vp_ref[...], vc_ref[...]], axis=0)
    r = lax.broadcasted_iota(jnp.int32, (tb, 2 * tb), 0)
    c = lax.broadcasted_iota(jnp.int32, (tb, 2 * tb), 1)
    behind = (tb + r - c).astype(jnp.uint32)
    bias = jnp.where(behind <= tb, 0.0, -jnp.inf)
    no_prev = jnp.where(first_step, r, tb).astype(jnp.uint32)
    bias0 = jnp.where(behind <= no_prev, 0.0, -jnp.inf)
    lane = lax.broadcasted_iota(jnp.int32, (1, WIDTH_B), 1)
    head_qk = (lane % 128) // (D_HEAD_B // 2)
    head_v = lane // D_HEAD_B
    nt = (((1,), (1,)), ((), ()))
    for j in range(n_sub):
        q = q_ref[j * tb:(j + 1) * tb, :]
        for h in range(H_B):
            qh = jnp.where(head_qk == h, q, jnp.zeros_like(q))
            s_ref[j * H_B + h] = (lax.dot_general(qh, k[j * tb:(j + 2) * tb], nt, preferred_element_type=F32)
                                  + (bias0 if j == 0 else bias))
    for j in range(n_sub):
        vj = v[j * tb:(j + 2) * tb]
        o = jnp.zeros((tb, WIDTH_B), F32)
        lse = jnp.zeros((tb, WIDTH_B), F32)
        for h in range(H_B):
            s = s_ref[j * H_B + h]
            m = jnp.max(s, axis=1, keepdims=True)
            p = jnp.exp(s - m).astype(BF16)
            pv = jnp.dot(p, jnp.where(head_v == h, vj, jnp.ones_like(vj)), preferred_element_type=F32)
            l = pltpu.roll(pv, WIDTH_B // 2, axis=1)
            o = jnp.where(head_v == h, pv / l, o)
            lse = jnp.where(head_v == h, m + jnp.log(l), lse)
        o_ref[j * tb:(j + 1) * tb, :] = o.astype(o_ref.dtype)
        lse_ref[j * tb:(j + 1) * tb, :] = lse


def _dil_call(qkv):
    b, dilation, n, _ = qkv.shape
    tb, tq = TB_DIL, min(TQ_DIL, n)
    per = tq // tb
    cur = lambda which: pl.BlockSpec((None, None, tq, WIDTH_B), lambda i, r, j: (i, r, j, which))
    prev = lambda which: pl.BlockSpec((None, None, tb, WIDTH_B),
                                      lambda i, r, j: (i, r, jnp.maximum(j * per - 1, 0), which))
    out_spec = pl.BlockSpec((None, None, tq, WIDTH_B), lambda i, r, j: (i, r, j, 0))
    out = lambda dtype: jax.ShapeDtypeStruct((b, dilation, n, WIDTH_B), dtype)
    return pl.pallas_call(
        _dil_kernel,
        out_shape=(out(BF16), out(F32)),
        grid=(b, dilation, n // tq),
        in_specs=[cur(0), cur(1), prev(1), cur(2), prev(2)],
        out_specs=(out_spec, out_spec),
        scratch_shapes=[pltpu.VMEM((per * H_B, tb, 2 * tb), F32)],
        compiler_params=_params(3),
        name="dil",
    )(qkv, qkv, qkv, qkv, qkv)


def _merge_ffn_kernel(x_ref, mod_ref, oa_ref, o0_ref, o1_ref, o2_ref, l0_ref, l1_ref, l2_ref, gates_ref,
                      wua_ref, wub_ref, wo_ref, gpost_ref, gpre2_ref, wg_ref, wu_ref, wd_ref, gpost2_ref,
                      out_ref, acc_ref, *scr):
    tm = x_ref.shape[0]
    tmp = scr[4]

    def token_major(ref, buf):
        dilation = ref.shape[0]
        if dilation == 1:
            return ref[0]
        n_slab = ref.shape[2] // LANES
        for c in range(n_slab):
            if dilation <= 4:
                for r in range(dilation):
                    buf[c, pl.ds(r, tm // dilation, stride=dilation), :] = (
                        ref[r, :, c * LANES:(c + 1) * LANES].astype(F32))
            else:
                quarter = tm // 4
                for r in range(dilation):
                    tmp[c, pl.ds((r % 4) * quarter + r // 4, tm // dilation, stride=dilation // 4), :] = (
                        ref[r, :, c * LANES:(c + 1) * LANES].astype(F32))
                for r4 in range(4):
                    buf[c, pl.ds(r4, quarter, stride=4), :] = tmp[c, r4 * quarter:(r4 + 1) * quarter, :]
        return jnp.concatenate([buf[c] for c in range(n_slab)], axis=1)

    o0, l0 = token_major(o0_ref, None).astype(F32), token_major(l0_ref, None)
    o1, l1 = token_major(o1_ref, scr[0]), token_major(l1_ref, scr[1])
    o2, l2 = token_major(o2_ref, scr[2]), token_major(l2_ref, scr[3])
    m = jnp.maximum(jnp.maximum(l0, l1), l2)
    e0, e1, e2 = jnp.exp(l0 - m), jnp.exp(l1 - m), jnp.exp(l2 - m)
    ob = (e0 * o0 + e1 * o1 + e2 * o2) / (e0 + e1 + e2)
    za = jnp.dot(oa_ref[...], wua_ref[...], preferred_element_type=F32)
    zb = jnp.dot(ob.astype(BF16), wub_ref[...], preferred_element_type=F32)
    d = za.shape[1]
    z = gates_ref[:, 0:d].astype(F32) * za + gates_ref[:, d:2 * d].astype(F32) * zb
    y = jnp.dot(z.astype(BF16), wo_ref[...], preferred_element_type=F32)
    x_mixed = x_ref[...] + mod_ref[5:6, :] * (_rms_rows(y) * gpost_ref[...])
    _ffn_tile(x_mixed, mod_ref, gpre2_ref, wg_ref, wu_ref, wd_ref, gpost2_ref, out_ref, acc_ref, 6)


def _merge_ffn_call(x, mod, oa, dil, gates, w_up_a, w_up_b, w_o, g_post, g_pre2, w_gate, w_up, w_down, g_post2):
    b, s, d = x.shape
    tm = TM_FFN
    rows = lambda width: pl.BlockSpec((None, tm, width), lambda i, j: (i, j, 0))
    const = lambda a: pl.BlockSpec(a.shape, lambda i, j: (0,) * a.ndim, pipeline_mode=pl.Buffered(1))
    (o0, l0), (o1, l1), (o2, l2) = dil
    res = lambda a: pl.BlockSpec((None, a.shape[1], tm // a.shape[1], WIDTH_B), lambda i, j: (i, 0, j, 0))
    consts = [w_up_a, w_up_b, w_o, g_post, g_pre2, w_gate, w_up, w_down, g_post2]
    return pl.pallas_call(
        _merge_ffn_kernel,
        out_shape=jax.ShapeDtypeStruct((b, s, d), F32),
        grid=(b, s // tm),
        in_specs=[rows(d), pl.BlockSpec((None, N_MOD, d), lambda i, j: (i, 0, 0)), rows(WIDTH_A)]
                 + [res(a) for a in (o0, o1, o2, l0, l1, l2)]
                 + [rows(2 * d)] + [const(a) for a in consts],
        out_specs=rows(d),
        scratch_shapes=[pltpu.VMEM((tm, d), F32)] + [pltpu.VMEM((WIDTH_B // LANES, tm, LANES), F32)] * 5,
        compiler_params=_params(2),
        name="merge_ffn",
    )(x, mod, oa, o0, o1, o2, l0, l1, l2, gates, *consts)


def _pad_heads(w_t):
    h, r, n = w_t.shape
    return jnp.pad(w_t, ((0, 0), (0, HEAD_PAD - r), (0, 0))).reshape(h * HEAD_PAD, n)


def _mixer_weights(w_in, g_cq, g_ckv, w_uq, w_uk, w_uv, w_iq):
    d = w_in.shape[0]
    o_cq, o_ckv, o_kr, o_ki, o_wi = 0, Q_LORA, Q_LORA + KV_LORA, Q_LORA + KV_LORA + D_ROPE_A, \
        Q_LORA + KV_LORA + D_ROPE_A + D_IDX
    o_qkv = o_wi + H_IDX
    o_gates = o_qkv + N_QKV_B

    def swap_halves(w):
        half = w.shape[1] // 2
        return jnp.concatenate([w[:, half:], w[:, :half]], axis=1)

    w_kr, w_ki = w_in[:, o_kr:o_kr + D_ROPE_A], w_in[:, o_ki:o_ki + D_IDX]
    qkv = w_in[:, o_qkv:o_gates].reshape(d, 3, N_GROUPS_B, H_B, 2, D_HEAD_B // 2)
    qk_split = qkv[:, 0:2].transpose(0, 1, 2, 4, 3, 5).reshape(d, 2 * N_GROUPS_B * WIDTH_B)
    v_cols = qkv[:, 2].reshape(d, N_GROUPS_B * WIDTH_B)
    w_std = jnp.concatenate([w_in[:, o_ckv:o_ckv + KV_LORA], w_kr, swap_halves(w_kr), w_ki, swap_halves(w_ki),
                             qk_split, v_cols, w_in[:, o_gates:]], axis=1)
    w_tr = jnp.concatenate([w_in[:, o_cq:o_cq + Q_LORA], w_in[:, o_ckv:o_ckv + KV_LORA],
                            w_in[:, o_wi:o_wi + H_IDX]], axis=1).T
    w_tr = jnp.pad(w_tr, ((0, N_TR - w_tr.shape[0]), (0, 0)))

    e_kr = np.zeros((D_ROPE_A, H_A * HEAD_PAD), np.float32)
    for h in range(H_A):
        e_kr[np.arange(D_ROPE_A), h * HEAD_PAD + D_NOPE + np.arange(D_ROPE_A)] = 1.0
    e_ki = np.zeros((D_IDX, LANES), np.float32)
    e_ki[np.arange(D_IDX), np.arange(D_IDX)] = 1.0
    return {
        "w_std": w_std.astype(BF16),
        "w_tr": w_tr.astype(BF16),
        "g_cq_col": g_cq.reshape(Q_LORA, 1),
        "g_ckv_col": g_ckv.reshape(KV_LORA, 1),
        "g_ckv_row": g_ckv.reshape(1, KV_LORA),
        "w_uq_t": _pad_heads(w_uq.transpose(1, 2, 0)).astype(BF16),
        "w_iq_t": _pad_heads(w_iq.transpose(1, 2, 0)).astype(BF16),
        "w_uk_p": jnp.pad(w_uk, ((0, 0), (0, 0), (0, HEAD_PAD - D_NOPE))).reshape(KV_LORA, H_A * HEAD_PAD).astype(BF16),
        "w_uv_t": w_uv.transpose(1, 2, 0).reshape(WIDTH_A, KV_LORA).astype(BF16),
        "e_kr": jnp.asarray(e_kr, BF16),
        "e_ki": jnp.asarray(e_ki, BF16),
    }


def kernel(x, c, positions, w_mod, b_mod, g_pre_ffn1, w_gate1, w_up1, w_down1, g_post_ffn1, g_pre_mix, w_in, g_cq, g_ckv, w_uq, w_uk, w_uv, w_iq, w_up_a, w_up_b, w_o, g_post_mix, g_pre_ffn2, w_gate2, w_up2, w_down2, g_post_ffn2):
    b, s, d = x.shape
    assert s % (16 * TB_DIL) == 0 and s % TM_FFN == 0 and b <= 8
    c8 = jnp.pad(c, ((0, 8 - b), (0, 0)))
    half_b, half_a = D_HEAD_B // 2, D_ROPE_A // 2
    inv = jnp.concatenate([ROPE_THETA ** (-jnp.arange(half_b, dtype=F32) / half_b),
                           ROPE_THETA ** (-jnp.arange(half_a, dtype=F32) / half_a)]).reshape(48, 1)
    tabs = _rope_call(positions.reshape(b, 1, s), inv)
    row = lambda g: g.reshape(1, -1)
    for l in range(w_mod.shape[0]):
        mod = _mod_call(c8, w_mod[l], b_mod[l].reshape(1, -1))[:b].reshape(b, N_MOD, d)
        x = _ffn_call(x, mod, row(g_pre_ffn1[l]), w_gate1[l].astype(BF16), w_up1[l].astype(BF16),
                      w_down1[l].astype(BF16), row(g_post_ffn1[l]), 0)
        w = _mixer_weights(w_in[l], g_cq[l], g_ckv[l], w_uq[l], w_uk[l], w_uv[l], w_iq[l])
        qt, qit, wit, k, ki, vt, dq0, dq1, dq2, gates = _proj_call(x, mod, row(g_pre_mix[l]), w, tabs)
        oa = _dsa_call(qt, qit, wit, k, ki, vt)
        dil = [_dil_call(dq) for dq in (dq0, dq1, dq2)]
        x = _merge_ffn_call(x, mod, oa, dil, gates, w_up_a[l].astype(BF16), w_up_b[l].astype(BF16),
                            w_o[l].astype(BF16), row(g_post_mix[l]), row(g_pre_ffn2[l]), w_gate2[l].astype(BF16),
                            w_up2[l].astype(BF16), w_down2[l].astype(BF16), row(g_post_ffn2[l]))
    return x
```

```python
import functools

import numpy as np
import jax
import jax.numpy as jnp
from jax import lax
from jax.experimental import pallas as pl
from jax.experimental.pallas import tpu as pltpu

F32 = jnp.float32
BF16 = jnp.bfloat16

D_FF = 2816
ROPE_THETA = 10000.0
NORM_EPS = 1e-6
H_A = 8
Q_LORA = 384
KV_LORA = 256
D_NOPE = 64
D_ROPE_A = 32
D_V_A = 64
H_IDX = 8
D_IDX = 32
TOPK_MAX = 256
DIL_PAIRS = ((128, 1), (512, 4), (2048, 16))
N_GROUPS_B = 3
H_B = 4
D_HEAD_B = 64
N_MOD = 9
WIDTH_A = H_A * D_V_A
WIDTH_B = H_B * D_HEAD_B
N_QKV_B = 3 * N_GROUPS_B * WIDTH_B

LANES = 128
HEAD_PAD = 128
VMEM_LIMIT = 56 * 1024 * 1024

TM_FFN = 512
CH_FFN = 256
TP_PROJ = 512
TQ_DSA = 512
TK_DSA = 256
V_ROWS = D_V_A + 16
TB_DIL = 128
TQ_DIL = 512
BISECT_MAX = 34
BISECT_CHECK = 2
FOLD_ROWS = 16
COUNT_ROWS = 64
IDX_ROWS = 64
NEG_BIG = -1e30
LOG2_E = 1.4426950408889634

C_CKV = 0
C_KR = 256
C_KR_SW = 288
C_KI = 320
C_KI_SW = 352
C_QKVB = 384
C_GATES = C_QKVB + N_QKV_B
N_STD = C_GATES + 2048
R_CQ = 0
R_CKV = Q_LORA
R_WI = Q_LORA + KV_LORA
N_TR = 768


def _params(n_axes):
    return pltpu.CompilerParams(dimension_semantics=("arbitrary",) * n_axes,
                                vmem_limit_bytes=VMEM_LIMIT)


def _rms_rows(x):
    return x * lax.rsqrt(jnp.mean(x * x, axis=-1, keepdims=True) + NORM_EPS)


def _mod_kernel(c_ref, w_ref, b_ref, o_ref):
    c = c_ref[...]
    o_ref[...] = jnp.dot(c * jax.nn.sigmoid(c), w_ref[...], preferred_element_type=F32) + b_ref[...]


def _mod_call(c8, w_mod, b_mod):
    d = c8.shape[1]
    return pl.pallas_call(
        _mod_kernel,
        out_shape=jax.ShapeDtypeStruct((8, N_MOD * d), F32),
        grid=(N_MOD,),
        in_specs=[pl.BlockSpec((8, d), lambda j: (0, 0)),
                  pl.BlockSpec((d, d), lambda j: (0, j)),
                  pl.BlockSpec((1, d), lambda j: (0, j))],
        out_specs=pl.BlockSpec((8, d), lambda j: (0, j)),
        compiler_params=_params(1),
        name="mod",
    )(c8, w_mod, b_mod)


def _rope_kernel(pos_ref, inv_ref, tt_ref, ca_ref, sa_ref, cb_ref, sb_ref):
    s = pos_ref.shape[1]
    ang = inv_ref[...] * pos_ref[...].astype(F32)
    cos = jnp.cos(ang)
    sin = jnp.sin(ang)
    cos_b, sin_b = cos[32:48], sin[32:48]
    tt_ref[0] = cos_b
    tt_ref[1] = sin_b
    cos_a = jnp.concatenate([cos[0:32]] * 4, axis=0)
    sin_a = jnp.concatenate([sin[0:32]] * 4, axis=0)
    cos_bt = jnp.concatenate([cos_b] * 8, axis=0)
    sin_bt = jnp.concatenate([-sin_b, sin_b] * 4, axis=0)
    for j in range(s // LANES):
        cols = slice(j * LANES, (j + 1) * LANES)
        ca_ref[cols, :] = cos_a[:, cols].T
        sa_ref[cols, :] = sin_a[:, cols].T
        cb_ref[cols, :] = cos_bt[:, cols].T
        sb_ref[cols, :] = sin_bt[:, cols].T


def _rope_call(pos3, inv):
    b, _, s = pos3.shape
    tab = jax.ShapeDtypeStruct((b, s, LANES), F32)
    tab_spec = pl.BlockSpec((None, s, LANES), lambda i: (i, 0, 0))
    return pl.pallas_call(
        _rope_kernel,
        out_shape=(jax.ShapeDtypeStruct((b, 2, 16, s), F32), tab, tab, tab, tab),
        grid=(b,),
        in_specs=[pl.BlockSpec((None, 1, s), lambda i: (i, 0, 0)),
                  pl.BlockSpec((48, 1), lambda i: (0, 0))],
        out_specs=(pl.BlockSpec((None, 2, 16, s), lambda i: (i, 0, 0, 0)),
                   tab_spec, tab_spec, tab_spec, tab_spec),
        compiler_params=_params(1),
        name="rope",
    )(pos3, inv)


def _ffn_kernel(x_ref, mod_ref, gpre_ref, wg_ref, wu_ref, wd_ref, gpost_ref, o_ref, acc_ref, *, j0):
    _ffn_tile(x_ref[...], mod_ref, gpre_ref, wg_ref, wu_ref, wd_ref, gpost_ref, o_ref, acc_ref, j0)


def _ffn_tile(x, mod_ref, gpre_ref, wg_ref, wu_ref, wd_ref, gpost_ref, o_ref, acc_ref, j0):
    sh, sc, gt = mod_ref[j0:j0 + 1, :], mod_ref[j0 + 1:j0 + 2, :], mod_ref[j0 + 2:j0 + 3, :]
    h = (_rms_rows(x) * gpre_ref[...] * (1.0 + sc) + sh).astype(BF16)
    n_ch = wg_ref.shape[1] // CH_FFN
    for ch in range(n_ch):
        cols = slice(ch * CH_FFN, (ch + 1) * CH_FFN)
        g = jnp.dot(h, wg_ref[:, cols], preferred_element_type=F32)
        u = jnp.dot(h, wu_ref[:, cols], preferred_element_type=F32)
        a = (g * jax.nn.sigmoid(g) * u).astype(BF16)
        part = jnp.dot(a, wd_ref[cols, :], preferred_element_type=F32)
        if ch == 0:
            acc_ref[...] = part
        else:
            acc_ref[...] += part
    o_ref[...] = x + 0.5 * gt * (_rms_rows(acc_ref[...]) * gpost_ref[...])


def _ffn_call(x, mod, g_pre, w_gate, w_up, w_down, g_post, j0):
    b, s, d = x.shape
    f = w_gate.shape[1]
    const = lambda shape: pl.BlockSpec(shape, lambda i, j: (0,) * len(shape), pipeline_mode=pl.Buffered(1))
    tile = pl.BlockSpec((None, TM_FFN, d), lambda i, j: (i, j, 0))
    return pl.pallas_call(
        functools.partial(_ffn_kernel, j0=j0),
        out_shape=jax.ShapeDtypeStruct((b, s, d), F32),
        grid=(b, s // TM_FFN),
        in_specs=[tile,
                  pl.BlockSpec((None, N_MOD, d), lambda i, j: (i, 0, 0)),
                  const((1, d)), const((d, f)), const((d, f)), const((f, d)), const((1, d))],
        out_specs=tile,
        scratch_shapes=[pltpu.VMEM((TM_FFN, d), F32)],
        compiler_params=_params(2),
        name="ffn",
    )(x, mod, g_pre, w_gate, w_up, w_down, g_post)


def _proj_kernel(x_ref, mod_ref, gpre_ref, wstd_ref, wtr_ref, gcq_ref, gckv_col_ref, gckv_row_ref,
                 wuq_ref, wiq_ref, wuk_ref, wuv_ref, ekr_ref, eki_ref,
                 tt_ref, ca_ref, sa_ref, cb_ref, sb_ref,
                 qt_ref, qit_ref, wit_ref, k_ref, ki_ref, vt_ref, dq0_ref, dq1_ref, dq2_ref, gates_ref,
                 dil_scr, dil_tmp):
    x = x_ref[...]
    sh, sc = mod_ref[3:4, :], mod_ref[4:5, :]
    u = (_rms_rows(x) * gpre_ref[...] * (1.0 + sc) + sh).astype(BF16)
    std = jnp.dot(u, wstd_ref[...], preferred_element_type=F32)
    tr = lax.dot_general(wtr_ref[...], u, (((1,), (1,)), ((), ())),
                         preferred_element_type=F32)

    cq = tr[R_CQ:R_CQ + Q_LORA]
    cq = cq * lax.rsqrt(jnp.mean(cq * cq, axis=0, keepdims=True) + NORM_EPS) * gcq_ref[...]
    cq = cq.astype(BF16)
    ckv_t = tr[R_CKV:R_CKV + KV_LORA]
    ckv_t = ckv_t * lax.rsqrt(jnp.mean(ckv_t * ckv_t, axis=0, keepdims=True) + NORM_EPS) * gckv_col_ref[...]
    cos_t, sin_t = tt_ref[0], tt_ref[1]

    attn_scale = (D_NOPE + D_ROPE_A) ** -0.5 * LOG2_E
    q_t = jnp.dot(wuq_ref[...], cq, preferred_element_type=F32) * attn_scale
    idx_scale = D_IDX ** -0.5
    qi_t = jnp.dot(wiq_ref[...], cq, preferred_element_type=F32) * idx_scale
    for h in range(H_A):
        r0 = h * HEAD_PAD
        x1, x2 = q_t[r0 + 64:r0 + 80], q_t[r0 + 80:r0 + 96]
        qt_ref[r0:r0 + 64, :] = q_t[r0:r0 + 64].astype(BF16)
        qt_ref[r0 + 64:r0 + 80, :] = (x1 * cos_t - x2 * sin_t).astype(BF16)
        qt_ref[r0 + 80:r0 + 96, :] = (x2 * cos_t + x1 * sin_t).astype(BF16)
        qt_ref[r0 + 96:r0 + 128, :] = q_t[r0 + 96:r0 + 128].astype(BF16)
        y1, y2 = qi_t[r0:r0 + 16], qi_t[r0 + 16:r0 + 32]
        qit_ref[r0:r0 + 16, :] = (y1 * cos_t - y2 * sin_t).astype(BF16)
        qit_ref[r0 + 16:r0 + 32, :] = (y2 * cos_t + y1 * sin_t).astype(BF16)
        qit_ref[r0 + 32:r0 + 128, :] = qi_t[r0 + 32:r0 + 128].astype(BF16)
    wit_ref[...] = tr[R_WI:R_WI + H_IDX] * (H_IDX ** -0.5)
    v_t = jnp.dot(wuv_ref[...], ckv_t.astype(BF16), preferred_element_type=F32).astype(BF16)
    for h in range(H_A):
        vt_ref[h * V_ROWS:h * V_ROWS + D_V_A, :] = v_t[h * D_V_A:(h + 1) * D_V_A]
        vt_ref[h * V_ROWS + D_V_A:(h + 1) * V_ROWS, :] = jnp.ones((V_ROWS - D_V_A, v_t.shape[1]), BF16)

    ckv = std[:, C_CKV:C_CKV + KV_LORA]
    ckv = (_rms_rows(ckv) * gckv_row_ref[...]).astype(BF16)
    k_nope = jnp.dot(ckv, wuk_ref[...], preferred_element_type=F32)
    cos_k, sin_k = cb_ref[:, 0:32], sb_ref[:, 0:32]
    k_rope = std[:, C_KR:C_KR + 32] * cos_k + std[:, C_KR_SW:C_KR_SW + 32] * sin_k
    k_ref[...] = (k_nope + jnp.dot(k_rope.astype(BF16), ekr_ref[...],
                                   preferred_element_type=F32)).astype(BF16)
    k_idx = std[:, C_KI:C_KI + 32] * cos_k + std[:, C_KI_SW:C_KI_SW + 32] * sin_k
    ki_ref[...] = jnp.dot(k_idx.astype(BF16), eki_ref[...], preferred_element_type=F32).astype(BF16)

    cos_a, sin_a = ca_ref[...], sa_ref[...]
    tp = x.shape[0]
    for g, (out_ref, (_, dilation)) in enumerate(zip((dq0_ref, dq1_ref, dq2_ref), DIL_PAIRS)):
        for which in range(2):
            scale = (D_HEAD_B ** -0.5) if which == 0 else 1.0
            c0 = C_QKVB + (which * N_GROUPS_B + g) * WIDTH_B
            x1, x2 = std[:, c0:c0 + 128], std[:, c0 + 128:c0 + 256]
            dil_scr[2 * which] = (x1 * cos_a - x2 * sin_a) * scale
            dil_scr[2 * which + 1] = (x2 * cos_a + x1 * sin_a) * scale
        c0 = C_QKVB + (2 * N_GROUPS_B + g) * WIDTH_B
        dil_scr[4] = std[:, c0:c0 + 128]
        dil_scr[5] = std[:, c0 + 128:c0 + 256]
        for c in range(3 * WIDTH_B // LANES):
            if dilation <= 4:
                for r in range(dilation):
                    out_ref[r, :, c * LANES:(c + 1) * LANES] = (
                        dil_scr[c, pl.ds(r, tp // dilation, stride=dilation), :].astype(BF16))
            else:
                quarter = tp // 4
                for r4 in range(4):
                    dil_tmp[c, r4 * quarter:(r4 + 1) * quarter, :] = dil_scr[c, pl.ds(r4, quarter, stride=4), :]
                for r in range(dilation):
                    out_ref[r, :, c * LANES:(c + 1) * LANES] = dil_tmp[
                        c, pl.ds((r % 4) * quarter + r // 4, tp // dilation, stride=dilation // 4), :].astype(BF16)
    gates_ref[...] = jax.nn.sigmoid(std[:, C_GATES:]).astype(BF16)


def _proj_call(x, mod, g_pre, w, tabs):
    b, s, d = x.shape
    tt, ca, sa, cb, sb = tabs
    tp = TP_PROJ
    const = lambda a: pl.BlockSpec(a.shape, lambda i, j: (0,) * a.ndim)
    rows = lambda width: pl.BlockSpec((None, tp, width), lambda i, j: (i, j, 0))
    colsT = lambda height: pl.BlockSpec((None, height, tp), lambda i, j: (i, 0, j))
    consts = [g_pre, w["w_std"], w["w_tr"], w["g_cq_col"], w["g_ckv_col"], w["g_ckv_row"],
              w["w_uq_t"], w["w_iq_t"], w["w_uk_p"], w["w_uv_t"], w["e_kr"], w["e_ki"]]
    hq = H_A * HEAD_PAD
    dil_shapes = [jax.ShapeDtypeStruct((b, dil, s // dil, 3 * WIDTH_B), BF16) for _, dil in DIL_PAIRS]
    dil_specs = [pl.BlockSpec((None, dil, tp // dil, 3 * WIDTH_B), lambda i, j: (i, 0, j, 0))
                 for _, dil in DIL_PAIRS]
    return pl.pallas_call(
        _proj_kernel,
        out_shape=(jax.ShapeDtypeStruct((b, hq, s), BF16),
                   jax.ShapeDtypeStruct((b, hq, s), BF16),
                   jax.ShapeDtypeStruct((b, H_IDX, s), F32),
                   jax.ShapeDtypeStruct((b, s, hq), BF16),
                   jax.ShapeDtypeStruct((b, s, LANES), BF16),
                   jax.ShapeDtypeStruct((b, H_A * V_ROWS, s), BF16),
                   *dil_shapes,
                   jax.ShapeDtypeStruct((b, s, 2048), BF16)),
        grid=(b, s // tp),
        in_specs=[rows(d), pl.BlockSpec((None, N_MOD, d), lambda i, j: (i, 0, 0))]
                 + [const(a) for a in consts]
                 + [pl.BlockSpec((None, 2, 16, tp), lambda i, j: (i, 0, 0, j)),
                    rows(LANES), rows(LANES), rows(LANES), rows(LANES)],
        out_specs=(colsT(hq), colsT(hq), colsT(H_IDX), rows(hq), rows(LANES), colsT(H_A * V_ROWS),
                   *dil_specs, rows(2048)),
        scratch_shapes=[pltpu.VMEM((3 * WIDTH_B // LANES, tp, LANES), F32)] * 2,
        compiler_params=_params(2),
        name="proj",
    )(x, mod, *consts, tt, ca, sa, cb, sb)


def _dsa_kernel(qit_ref, wit_ref, qt_ref, ki_ref, k_ref, vt_ref, o_ref,
                sc_ref, m_ref, acc_ref, s_ref, s2_ref, smax_ref, smax2_ref, *, topk):
    tq = qt_ref.shape[1]
    tk = s_ref.shape[1]
    i = pl.program_id(1)
    nkb = (i + 1) * (tq // tk)
    npair = (nkb + 1) // 2
    qpos = i * tq + lax.broadcasted_iota(jnp.int32, (1, tq), 1)

    def fold(v):
        return jnp.sum(v.reshape(v.shape[0] // FOLD_ROWS, FOLD_ROWS, tq), axis=0)

    def park_logits(kb, park_ref):
        ki = ki_ref[pl.ds(pl.multiple_of(kb * tk, tk), tk), :]
        for h in range(H_IDX):
            park_ref[h] = jnp.dot(ki, qit_ref[h * HEAD_PAD:(h + 1) * HEAD_PAD, :], preferred_element_type=F32)

    def rows8(v, op):
        return op(v.reshape(IDX_ROWS // 8, 8, LANES), axis=0)

    piece_iota = lax.broadcasted_iota(jnp.int32, (IDX_ROWS, LANES), 0)

    def reduce_block(kb, park_ref, stats, on_diagonal):
        k0 = pl.multiple_of(kb * tk, tk)
        columns = []
        for c in range(tq // LANES):
            lanes = slice(c * LANES, (c + 1) * LANES)
            smax, smin = [v[:, lanes] for v in stats]
            w = [wit_ref[h:h + 1, lanes] for h in range(H_IDX)]
            for j in range(tk // IDX_ROWS):
                r0 = j * IDX_ROWS
                acc = jnp.zeros((IDX_ROWS, LANES), F32)
                for h in range(H_IDX):
                    acc = acc + w[h] * jnp.maximum(park_ref[h, r0:r0 + IDX_ROWS, lanes], 0.0)
                if on_diagonal:
                    causal = (k0 + r0 + piece_iota) <= qpos[:, lanes]
                    sv = jnp.where(causal, acc, -jnp.inf)
                    smin = jnp.minimum(smin, rows8(jnp.where(causal, acc, jnp.inf), jnp.min))
                else:
                    sv = acc
                    smin = jnp.minimum(smin, rows8(acc, jnp.min))
                sc_ref[pl.ds(k0 + r0, IDX_ROWS), lanes] = sv
                smax = jnp.maximum(smax, rows8(sv, jnp.max))
            columns.append((smax, smin))
        return tuple(jnp.concatenate(parts, axis=1) for parts in zip(*columns))

    def idx_pair(kp, stats):
        park_logits(2 * kp + 1, s2_ref)
        stats = reduce_block(2 * kp, s_ref, stats, False)
        park_logits(2 * kp + 2, s_ref)
        return reduce_block(2 * kp + 1, s2_ref, stats, False)

    park_logits(0, s_ref)
    stats = lax.fori_loop(0, i, idx_pair, (jnp.full((8, tq), -jnp.inf, F32), jnp.full((8, tq), jnp.inf, F32)))
    park_logits(nkb - 1, s2_ref)
    stats = reduce_block(nkb - 2, s_ref, stats, True)
    stats = reduce_block(nkb - 1, s2_ref, stats, True)
    smax = jnp.max(stats[0], axis=0, keepdims=True)
    smin = jnp.min(stats[1], axis=0, keepdims=True)

    @pl.when(nkb % 2 == 1)
    def _pad_block():
        sc_ref[pl.ds(pl.multiple_of(nkb * tk, tk), tk), :] = jnp.full((tk, tq), -jnp.inf, F32)

    n_causal = (qpos + 1).astype(F32)
    kf = jnp.minimum(qpos + 1, topk).astype(F32)

    def count_gt(thr, or_equal=False):
        def body(kp, c):
            k0 = pl.multiple_of(kp * (2 * tk), 2 * tk)
            for j in range(2 * tk // COUNT_ROWS):
                sv = sc_ref[pl.ds(k0 + j * COUNT_ROWS, COUNT_ROWS), :]
                c = c + fold(jnp.where(sv >= thr if or_equal else sv > thr, 1.0, 0.0))
            return c
        part = lax.fori_loop(0, npair, body, jnp.zeros((FOLD_ROWS, tq), F32))
        return jnp.sum(part, axis=0, keepdims=True)

    zero_row = jnp.zeros((1, tq), F32)
    c_pos = count_gt(zero_row)
    c_nn = count_gt(zero_row, or_equal=True)
    below = smin - (1.0 + jnp.abs(smin))
    zero_tie = jnp.logical_and(c_pos < kf, kf <= c_nn)
    positive = c_pos >= kf
    searching = jnp.where(zero_tie, 0.0, 1.0)

    def bisect_cond(carry):
        return jnp.logical_and(carry[0] < BISECT_MAX, carry[1] > 0)

    def bisect_body(carry):
        it, _, lo, hi, c_lo, c_hi = carry
        for _ in range(BISECT_CHECK):
            mid = 0.5 * (lo + hi)
            c_mid = count_gt(mid)
            ge = c_mid >= kf
            up = jnp.logical_and(ge, searching > 0.0)
            down = jnp.logical_and(jnp.logical_not(ge), searching > 0.0)
            lo, c_lo = jnp.where(up, mid, lo), jnp.where(up, c_mid, c_lo)
            hi, c_hi = jnp.where(down, mid, hi), jnp.where(down, c_mid, c_hi)
        open_queries = (jnp.max((c_lo - kf) * searching) > 0.0).astype(jnp.int32)
        return it + BISECT_CHECK, open_queries, lo, hi, c_lo, c_hi

    lo0 = jnp.where(positive, 0.0, jnp.where(zero_tie, 0.0, below))
    hi0 = jnp.where(positive, smax, 0.0)
    c_lo0 = jnp.where(positive, c_pos, jnp.where(zero_tie, c_nn, n_causal))
    c_hi0 = jnp.where(positive, 0.0, c_pos)
    first_open = (jnp.max((c_lo0 - kf) * searching) > 0.0).astype(jnp.int32)
    _, _, lo, hi, c_lo, c_hi = lax.while_loop(
        bisect_cond, bisect_body, (jnp.int32(0), first_open, lo0, hi0, c_lo0, c_hi0))

    @pl.when(jnp.max(jnp.maximum(c_lo - kf, 1.0 - searching)) > 0.0)
    def _break_ties():
        need = kf - c_hi
        closed = 1.0 - searching
        prefix = jnp.where(lax.broadcasted_iota(jnp.int32, (tk, tk), 0)
                           >= lax.broadcasted_iota(jnp.int32, (tk, tk), 1), 1.0, 0.0).astype(BF16)

        def body(kb, seen):
            k0 = pl.multiple_of(kb * tk, tk)
            sv = sc_ref[pl.ds(k0, tk), :]
            above_lo = jnp.where(sv > lo, 1.0, jnp.where(sv >= lo, closed, 0.0))
            member = jnp.where(sv > hi, 0.0, above_lo)
            rank = jnp.dot(prefix, member.astype(BF16), preferred_element_type=F32) + seen
            sc_ref[pl.ds(k0, tk), :] = jnp.where(member > 0.0, jnp.where(rank > need, -jnp.inf, jnp.inf), sv)
            return rank[tk - 1:tk, :]

        lax.fori_loop(0, nkb, body, jnp.zeros((1, tq), F32))

    m_ref[...] = jnp.full(m_ref.shape, NEG_BIG, F32)
    acc_ref[...] = jnp.zeros(acc_ref.shape, F32)

    def scores_into(kb, park_ref, max_ref):
        k0 = pl.multiple_of(kb * tk, tk)
        bias = jnp.where(sc_ref[pl.ds(k0, tk), :] > lo, 0.0, NEG_BIG)
        for h in range(H_A):
            kh = k_ref[pl.ds(k0, tk), h * HEAD_PAD:(h + 1) * HEAD_PAD]
            s = jnp.dot(kh, qt_ref[h * HEAD_PAD:(h + 1) * HEAD_PAD, :], preferred_element_type=F32) + bias
            park_ref[h] = s
            max_ref[h:h + 1, :] = jnp.max(s, axis=0, keepdims=True)

    def softmax_from(kb, park_ref, max_ref):
        k0 = pl.multiple_of(kb * tk, tk)
        for h in range(H_A):
            m_old = m_ref[h:h + 1, :]
            m_new = jnp.maximum(m_old, max_ref[h:h + 1, :])
            alpha = jnp.exp2(m_old - m_new)
            p = jnp.exp2((park_ref[h] - m_new).astype(BF16))
            rows = slice(h * V_ROWS, (h + 1) * V_ROWS)
            pv = jnp.dot(vt_ref[rows, pl.ds(k0, tk)], p, preferred_element_type=F32)
            acc_ref[rows, :] = alpha * acc_ref[rows, :] + pv
            m_ref[h:h + 1, :] = m_new

    last_block = 2 * npair - 1
    scores_into(0, s_ref, smax_ref)

    def att_body(kp, carry):
        scores_into(2 * kp + 1, s2_ref, smax2_ref)
        softmax_from(2 * kp, s_ref, smax_ref)
        scores_into(jnp.minimum(2 * kp + 2, last_block), s_ref, smax_ref)
        softmax_from(2 * kp + 1, s2_ref, smax2_ref)
        return carry

    lax.fori_loop(0, npair, att_body, 0)
    out = [acc_ref[h * V_ROWS:h * V_ROWS + D_V_A, :] / acc_ref[h * V_ROWS + D_V_A:h * V_ROWS + D_V_A + 1, :]
           for h in range(H_A)]
    o_ref[...] = jnp.concatenate(out, axis=0).T.astype(BF16)


def _dsa_call(qt, qit, wit, k, ki, vt):
    b, hq, s = qt.shape
    tq = TQ_DSA
    topk = min(TOPK_MAX, s // 4)
    colsT = lambda height: pl.BlockSpec((None, height, tq), lambda i, j: (i, 0, j))
    whole = lambda a: pl.BlockSpec((None,) + a.shape[1:], lambda i, j: (i, 0, 0))
    return pl.pallas_call(
        functools.partial(_dsa_kernel, topk=topk),
        out_shape=jax.ShapeDtypeStruct((b, s, WIDTH_A), BF16),
        grid=(b, s // tq),
        in_specs=[colsT(hq), colsT(H_IDX), colsT(hq), whole(ki), whole(k), whole(vt)],
        out_specs=pl.BlockSpec((None, tq, WIDTH_A), lambda i, j: (i, j, 0)),
        scratch_shapes=[pltpu.VMEM((s, tq), F32),
                        pltpu.VMEM((H_A, tq), F32),
                        pltpu.VMEM((H_A * V_ROWS, tq), F32),
                        pltpu.VMEM((H_A, TK_DSA, tq), F32),
                        pltpu.VMEM((H_A, TK_DSA, tq), F32),
                        pltpu.VMEM((H_A, tq), F32),
                        pltpu.VMEM((H_A, tq), F32)],
        compiler_params=_params(2),
        name="dsa",
    )(qit, wit, qt, ki, k, vt)


def _dil_kernel(q_ref, kc_ref, kp_ref, vc_ref, vp_ref, o_ref, lse_ref, s_ref):
    tb = kp_ref.shape[0]
    n_sub = q_ref.shape[0] // tb
    first_step = pl.program_id(2) == 0
    k = jnp.concatenate([kp_ref[...], kc_ref[...]], axis=0)
    v = jnp.concatenate([vp_ref[...], vc_ref[...]], axis=0)
    r = lax.broadcasted_iota(jnp.int32, (tb, 2 * tb), 0)
    c = lax.broadcasted_iota(jnp.int32, (tb, 2 * tb), 1)
    behind = (tb + r - c).astype(jnp.uint32)
    bias = jnp.where(behind <= tb, 0.0, -jnp.inf)
    no_prev = jnp.where(first_step, r, tb).astype(jnp.uint32)
    bias0 = jnp.where(behind <= no_prev, 0.0, -jnp.inf)
    lane = lax.broadcasted_iota(jnp.int32, (1, WIDTH_B), 1)
    head_qk = (lane % 128) // (D_HEAD_B // 2)
    head_v = lane // D_HEAD_B
    nt = (((1,), (1,)), ((), ()))
    for j in range(n_sub):
        q = q_ref[j * tb:(j + 1) * tb, :]
        for h in range(H_B):
            qh = jnp.where(head_qk == h, q, jnp.zeros_like(q))
            s_ref[j * H_B + h] = (lax.dot_general(qh, k[j * tb:(j + 2) * tb], nt, preferred_element_type=F32)
                                  + (bias0 if j == 0 else bias))
    for j in range(n_sub):
        vj = v[j * tb:(j + 2) * tb]
        o = jnp.zeros((tb, WIDTH_B), F32)
        lse = jnp.zeros((tb, WIDTH_B), F32)
        for h in range(H_B):
            s = s_ref[j * H_B + h]
            m = jnp.max(s, axis=1, keepdims=True)
            p = jnp.exp(s - m).astype(BF16)
            pv = jnp.dot(p, jnp.where(head_v == h, vj, jnp.ones_like(vj)), preferred_element_type=F32)
            l = pltpu.roll(pv, WIDTH_B // 2, axis=1)
            o = jnp.where(head_v == h, pv / l, o)
            lse = jnp.where(head_v == h, m + jnp.log(l), lse)
        o_ref[j * tb:(j + 1) * tb, :] = o.astype(o_ref.dtype)
        lse_ref[j * tb:(j + 1) * tb, :] = lse


def _dil_call(qkv):
    b, dilation, n, _ = qkv.shape
    tb, tq = TB_DIL, min(TQ_DIL, n)
    per = tq // tb
    cur = lambda which: pl.BlockSpec((None, None, tq, WIDTH_B), lambda i, r, j: (i, r, j, which))
    prev = lambda which: pl.BlockSpec((None, None, tb, WIDTH_B),
                                      lambda i, r, j: (i, r, jnp.maximum(j * per - 1, 0), which))
    out_spec = pl.BlockSpec((None, None, tq, WIDTH_B), lambda i, r, j: (i, r, j, 0))
    out = lambda dtype: jax.ShapeDtypeStruct((b, dilation, n, WIDTH_B), dtype)
    return pl.pallas_call(
        _dil_kernel,
        out_shape=(out(BF16), out(F32)),
        grid=(b, dilation, n // tq),
        in_specs=[cur(0), cur(1), prev(1), cur(2), prev(2)],
        out_specs=(out_spec, out_spec),
        scratch_shapes=[pltpu.VMEM((per * H_B, tb, 2 * tb), F32)],
        compiler_params=_params(3),
        name="dil",
    )(qkv, qkv, qkv, qkv, qkv)


def _merge_ffn_kernel(x_ref, mod_ref, oa_ref, o0_ref, o1_ref, o2_ref, l0_ref, l1_ref, l2_ref, gates_ref,
                      wua_ref, wub_ref, wo_ref, gpost_ref, gpre2_ref, wg_ref, wu_ref, wd_ref, gpost2_ref,
                      out_ref, acc_ref, *scr):
    tm = x_ref.shape[0]
    tmp = scr[4]

    def token_major(ref, buf):
        dilation = ref.shape[0]
        if dilation == 1:
            return ref[0]
        n_slab = ref.shape[2] // LANES
        for c in range(n_slab):
            if dilation <= 4:
                for r in range(dilation):
                    buf[c, pl.ds(r, tm // dilation, stride=dilation), :] = (
                        ref[r, :, c * LANES:(c + 1) * LANES].astype(F32))
            else:
                quarter = tm // 4
                for r in range(dilation):
                    tmp[c, pl.ds((r % 4) * quarter + r // 4, tm // dilation, stride=dilation // 4), :] = (
                        ref[r, :, c * LANES:(c + 1) * LANES].astype(F32))
                for r4 in range(4):
                    buf[c, pl.ds(r4, quarter, stride=4), :] = tmp[c, r4 * quarter:(r4 + 1) * quarter, :]
        return jnp.concatenate([buf[c] for c in range(n_slab)], axis=1)

    o0, l0 = token_major(o0_ref, None).astype(F32), token_major(l0_ref, None)
    o1, l1 = token_major(o1_ref, scr[0]), token_major(l1_ref, scr[1])
    o2, l2 = token_major(o2_ref, scr[2]), token_major(l2_ref, scr[3])
    m = jnp.maximum(jnp.maximum(l0, l1), l2)
    e0, e1, e2 = jnp.exp(l0 - m), jnp.exp(l1 - m), jnp.exp(l2 - m)
    ob = (e0 * o0 + e1 * o1 + e2 * o2) / (e0 + e1 + e2)
    za = jnp.dot(oa_ref[...], wua_ref[...], preferred_element_type=F32)
    zb = jnp.dot(ob.astype(BF16), wub_ref[...], preferred_element_type=F32)
    d = za.shape[1]
    z = gates_ref[:, 0:d].astype(F32) * za + gates_ref[:, d:2 * d].astype(F32) * zb
    y = jnp.dot(z.astype(BF16), wo_ref[...], preferred_element_type=F32)
    x_mixed = x_ref[...] + mod_ref[5:6, :] * (_rms_rows(y) * gpost_ref[...])
    _ffn_tile(x_mixed, mod_ref, gpre2_ref, wg_ref, wu_ref, wd_ref, gpost2_ref, out_ref, acc_ref, 6)


def _merge_ffn_call(x, mod, oa, dil, gates, w_up_a, w_up_b, w_o, g_post, g_pre2, w_gate, w_up, w_down, g_post2):
    b, s, d = x.shape
    tm = TM_FFN
    rows = lambda width: pl.BlockSpec((None, tm, width), lambda i, j: (i, j, 0))
    const = lambda a: pl.BlockSpec(a.shape, lambda i, j: (0,) * a.ndim, pipeline_mode=pl.Buffered(1))
    (o0, l0), (o1, l1), (o2, l2) = dil
    res = lambda a: pl.BlockSpec((None, a.shape[1], tm // a.shape[1], WIDTH_B), lambda i, j: (i, 0, j, 0))
    consts = [w_up_a, w_up_b, w_o, g_post, g_pre2, w_gate, w_up, w_down, g_post2]
    return pl.pallas_call(
        _merge_ffn_kernel,
        out_shape=jax.ShapeDtypeStruct((b, s, d), F32),
        grid=(b, s // tm),
        in_specs=[rows(d), pl.BlockSpec((None, N_MOD, d), lambda i, j: (i, 0, 0)), rows(WIDTH_A)]
                 + [res(a) for a in (o0, o1, o2, l0, l1, l2)]
                 + [rows(2 * d)] + [const(a) for a in consts],
        out_specs=rows(d),
        scratch_shapes=[pltpu.VMEM((tm, d), F32)] + [pltpu.VMEM((WIDTH_B // LANES, tm, LANES), F32)] * 5,
        compiler_params=_params(2),
        name="merge_ffn",
    )(x, mod, oa, o0, o1, o2, l0, l1, l2, gates, *consts)


def _pad_heads(w_t):
    h, r, n = w_t.shape
    return jnp.pad(w_t, ((0, 0), (0, HEAD_PAD - r), (0, 0))).reshape(h * HEAD_PAD, n)


def _mixer_weights(w_in, g_cq, g_ckv, w_uq, w_uk, w_uv, w_iq):
    d = w_in.shape[0]
    o_cq, o_ckv, o_kr, o_ki, o_wi = 0, Q_LORA, Q_LORA + KV_LORA, Q_LORA + KV_LORA + D_ROPE_A, \
        Q_LORA + KV_LORA + D_ROPE_A + D_IDX
    o_qkv = o_wi + H_IDX
    o_gates = o_qkv + N_QKV_B

    def swap_halves(w):
        half = w.shape[1] // 2
        return jnp.concatenate([w[:, half:], w[:, :half]], axis=1)

    w_kr, w_ki = w_in[:, o_kr:o_kr + D_ROPE_A], w_in[:, o_ki:o_ki + D_IDX]
    qkv = w_in[:, o_qkv:o_gates].reshape(d, 3, N_GROUPS_B, H_B, 2, D_HEAD_B // 2)
    qk_split = qkv[:, 0:2].transpose(0, 1, 2, 4, 3, 5).reshape(d, 2 * N_GROUPS_B * WIDTH_B)
    v_cols = qkv[:, 2].reshape(d, N_GROUPS_B * WIDTH_B)
    w_std = jnp.concatenate([w_in[:, o_ckv:o_ckv + KV_LORA], w_kr, swap_halves(w_kr), w_ki, swap_halves(w_ki),
                             qk_split, v_cols, w_in[:, o_gates:]], axis=1)
    w_tr = jnp.concatenate([w_in[:, o_cq:o_cq + Q_LORA], w_in[:, o_ckv:o_ckv + KV_LORA],
                            w_in[:, o_wi:o_wi + H_IDX]], axis=1).T
    w_tr = jnp.pad(w_tr, ((0, N_TR - w_tr.shape[0]), (0, 0)))

    e_kr = np.zeros((D_ROPE_A, H_A * HEAD_PAD), np.float32)
    for h in range(H_A):
        e_kr[np.arange(D_ROPE_A), h * HEAD_PAD + D_NOPE + np.arange(D_ROPE_A)] = 1.0
    e_ki = np.zeros((D_IDX, LANES), np.float32)
    e_ki[np.arange(D_IDX), np.arange(D_IDX)] = 1.0
    return {
        "w_std": w_std.astype(BF16),
        "w_tr": w_tr.astype(BF16),
        "g_cq_col": g_cq.reshape(Q_LORA, 1),
        "g_ckv_col": g_ckv.reshape(KV_LORA, 1),
        "g_ckv_row": g_ckv.reshape(1, KV_LORA),
        "w_uq_t": _pad_heads(w_uq.transpose(1, 2, 0)).astype(BF16),
        "w_iq_t": _pad_heads(w_iq.transpose(1, 2, 0)).astype(BF16),
        "w_uk_p": jnp.pad(w_uk, ((0, 0), (0, 0), (0, HEAD_PAD - D_NOPE))).reshape(KV_LORA, H_A * HEAD_PAD).astype(BF16),
        "w_uv_t": w_uv.transpose(1, 2, 0).reshape(WIDTH_A, KV_LORA).astype(BF16),
        "e_kr": jnp.asarray(e_kr, BF16),
        "e_ki": jnp.asarray(e_ki, BF16),
    }


def kernel(x, c, positions, w_mod, b_mod, g_pre_ffn1, w_gate1, w_up1, w_down1, g_post_ffn1, g_pre_mix, w_in, g_cq, g_ckv, w_uq, w_uk, w_uv, w_iq, w_up_a, w_up_b, w_o, g_post_mix, g_pre_ffn2, w_gate2, w_up2, w_down2, g_post_ffn2):
    b, s, d = x.shape
    assert s % (16 * TB_DIL) == 0 and s % TM_FFN == 0 and b <= 8
    c8 = jnp.pad(c, ((0, 8 - b), (0, 0)))
    half_b, half_a = D_HEAD_B // 2, D_ROPE_A // 2
    inv = jnp.concatenate([ROPE_THETA ** (-jnp.arange(half_b, dtype=F32) / half_b),
                           ROPE_THETA ** (-jnp.arange(half_a, dtype=F32) / half_a)]).reshape(48, 1)
    tabs = _rope_call(positions.reshape(b, 1, s), inv)
    row = lambda g: g.reshape(1, -1)
    for l in range(w_mod.shape[0]):
        mod = _mod_call(c8, w_mod[l], b_mod[l].reshape(1, -1))[:b].reshape(b, N_MOD, d)
        x = _ffn_call(x, mod, row(g_pre_ffn1[l]), w_gate1[l].astype(BF16), w_up1[l].astype(BF16),
                      w_down1[l].astype(BF16), row(g_post_ffn1[l]), 0)
        w = _mixer_weights(w_in[l], g_cq[l], g_ckv[l], w_uq[l], w_uk[l], w_uv[l], w_iq[l])
        qt, qit, wit, k, ki, vt, dq0, dq1, dq2, gates = _proj_call(x, mod, row(g_pre_mix[l]), w, tabs)
        oa = _dsa_call(qt, qit, wit, k, ki, vt)
        dil = [_dil_call(dq) for dq in (dq0, dq1, dq2)]
        x = _merge_ffn_call(x, mod, oa, dil, gates, w_up_a[l].astype(BF16), w_up_b[l].astype(BF16),
                            w_o[l].astype(BF16), row(g_post_mix[l]), row(g_pre_ffn2[l]), w_gate2[l].astype(BF16),
                            w_up2[l].astype(BF16), w_down2[l].astype(BF16), row(g_post_ffn2[l]))
    return x
```

```python
import functools

import numpy as np
import jax
import jax.numpy as jnp
from jax import lax
from jax.experimental import pallas as pl
from jax.experimental.pallas import tpu as pltpu

F32 = jnp.float32
BF16 = jnp.bfloat16

D_FF = 2816
ROPE_THETA = 10000.0
NORM_EPS = 1e-6
H_A = 8
Q_LORA = 384
KV_LORA = 256
D_NOPE = 64
D_ROPE_A = 32
D_V_A = 64
H_IDX = 8
D_IDX = 32
TOPK_MAX = 256
DIL_PAIRS = ((128, 1), (512, 4), (2048, 16))
N_GROUPS_B = 3
H_B = 4
D_HEAD_B = 64
N_MOD = 9
WIDTH_A = H_A * D_V_A
WIDTH_B = H_B * D_HEAD_B
N_QKV_B = 3 * N_GROUPS_B * WIDTH_B

LANES = 128
HEAD_PAD = 128
VMEM_LIMIT = 56 * 1024 * 1024

TM_FFN = 512
CH_FFN = 256
TP_PROJ = 512
TQ_DSA = 512
TK_DSA = 256
V_ROWS = D_V_A + 16
TB_DIL = 128
TQ_DIL = 512
BISECT_MAX = 34
BISECT_CHECK = 2
FOLD_ROWS = 16
COUNT_ROWS = 64
IDX_ROWS = 64
NEG_BIG = -1e30
LOG2_E = 1.4426950408889634

C_CKV = 0
C_KR = 256
C_KR_SW = 288
C_KI = 320
C_KI_SW = 352
C_QKVB = 384
C_GATES = C_QKVB + N_QKV_B
N_STD = C_GATES + 2048
R_CQ = 0
R_CKV = Q_LORA
R_WI = Q_LORA + KV_LORA
N_TR = 768


def _params(n_axes):
    return pltpu.CompilerParams(dimension_semantics=("arbitrary",) * n_axes,
                                vmem_limit_bytes=VMEM_LIMIT)


def _rms_rows(x):
    return x * lax.rsqrt(jnp.mean(x * x, axis=-1, keepdims=True) + NORM_EPS)


def _mod_kernel(c_ref, w_ref, b_ref, o_ref):
    c = c_ref[...]
    o_ref[...] = jnp.dot(c * jax.nn.sigmoid(c), w_ref[...], preferred_element_type=F32) + b_ref[...]


def _mod_call(c8, w_mod, b_mod):
    d = c8.shape[1]
    return pl.pallas_call(
        _mod_kernel,
        out_shape=jax.ShapeDtypeStruct((8, N_MOD * d), F32),
        grid=(N_MOD,),
        in_specs=[pl.BlockSpec((8, d), lambda j: (0, 0)),
                  pl.BlockSpec((d, d), lambda j: (0, j)),
                  pl.BlockSpec((1, d), lambda j: (0, j))],
        out_specs=pl.BlockSpec((8, d), lambda j: (0, j)),
        compiler_params=_params(1),
        name="mod",
    )(c8, w_mod, b_mod)


def _rope_kernel(pos_ref, inv_ref, tt_ref, ca_ref, sa_ref, cb_ref, sb_ref):
    s = pos_ref.shape[1]
    ang = inv_ref[...] * pos_ref[...].astype(F32)
    cos = jnp.cos(ang)
    sin = jnp.sin(ang)
    cos_b, sin_b = cos[32:48], sin[32:48]
    tt_ref[0] = cos_b
    tt_ref[1] = sin_b
    cos_a = jnp.concatenate([cos[0:32]] * 4, axis=0)
    sin_a = jnp.concatenate([sin[0:32]] * 4, axis=0)
    cos_bt = jnp.concatenate([cos_b] * 8, axis=0)
    sin_bt = jnp.concatenate([-sin_b, sin_b] * 4, axis=0)
    for j in range(s // LANES):
        cols = slice(j * LANES, (j + 1) * LANES)
        ca_ref[cols, :] = cos_a[:, cols].T
        sa_ref[cols, :] = sin_a[:, cols].T
        cb_ref[cols, :] = cos_bt[:, cols].T
        sb_ref[cols, :] = sin_bt[:, cols].T


def _rope_call(pos3, inv):
    b, _, s = pos3.shape
    tab = jax.ShapeDtypeStruct((b, s, LANES), F32)
    tab_spec = pl.BlockSpec((None, s, LANES), lambda i: (i, 0, 0))
    return pl.pallas_call(
        _rope_kernel,
        out_shape=(jax.ShapeDtypeStruct((b, 2, 16, s), F32), tab, tab, tab, tab),
        grid=(b,),
        in_specs=[pl.BlockSpec((None, 1, s), lambda i: (i, 0, 0)),
                  pl.BlockSpec((48, 1), lambda i: (0, 0))],
        out_specs=(pl.BlockSpec((None, 2, 16, s), lambda i: (i, 0, 0, 0)),
                   tab_spec, tab_spec, tab_spec, tab_spec),
        compiler_params=_params(1),
        name="rope",
    )(pos3, inv)


def _ffn_kernel(x_ref, mod_ref, gpre_ref, wg_ref, wu_ref, wd_ref, gpost_ref, o_ref, acc_ref, *, j0):
    _ffn_tile(x_ref[...], mod_ref, gpre_ref, wg_ref, wu_ref, wd_ref, gpost_ref, o_ref, acc_ref, j0)


def _ffn_tile(x, mod_ref, gpre_ref, wg_ref, wu_ref, wd_ref, gpost_ref, o_ref, acc_ref, j0):
    sh, sc, gt = mod_ref[j0:j0 + 1, :], mod_ref[j0 + 1:j0 + 2, :], mod_ref[j0 + 2:j0 + 3, :]
    h = (_rms_rows(x) * gpre_ref[...] * (1.0 + sc) + sh).astype(BF16)
    n_ch = wg_ref.shape[1] // CH_FFN
    for ch in range(n_ch):
        cols = slice(ch * CH_FFN, (ch + 1) * CH_FFN)
        g = jnp.dot(h, wg_ref[:, cols], preferred_element_type=F32)
        u = jnp.dot(h, wu_ref[:, cols], preferred_element_type=F32)
        a = (g * jax.nn.sigmoid(g) * u).astype(BF16)
        part = jnp.dot(a, wd_ref[cols, :], preferred_element_type=F32)
        if ch == 0:
            acc_ref[...] = part
        else:
            acc_ref[...] += part
    o_ref[...] = x + 0.5 * gt * (_rms_rows(acc_ref[...]) * gpost_ref[...])


def _ffn_call(x, mod, g_pre, w_gate, w_up, w_down, g_post, j0):
    b, s, d = x.shape
    f = w_gate.shape[1]
    const = lambda shape: pl.BlockSpec(shape, lambda i, j: (0,) * len(shape), pipeline_mode=pl.Buffered(1))
    tile = pl.BlockSpec((None, TM_FFN, d), lambda i, j: (i, j, 0))
    return pl.pallas_call(
        functools.partial(_ffn_kernel, j0=j0),
        out_shape=jax.ShapeDtypeStruct((b, s, d), F32),
        grid=(b, s // TM_FFN),
        in_specs=[tile,
                  pl.BlockSpec((None, N_MOD, d), lambda i, j: (i, 0, 0)),
                  const((1, d)), const((d, f)), const((d, f)), const((f, d)), const((1, d))],
        out_specs=tile,
        scratch_shapes=[pltpu.VMEM((TM_FFN, d), F32)],
        compiler_params=_params(2),
        name="ffn",
    )(x, mod, g_pre, w_gate, w_up, w_down, g_post)


def _proj_kernel(x_ref, mod_ref, gpre_ref, wstd_ref, wtr_ref, gcq_ref, gckv_col_ref, gckv_row_ref,
                 wuq_ref, wiq_ref, wuk_ref, wuv_ref, ekr_ref, eki_ref,
                 tt_ref, ca_ref, sa_ref, cb_ref, sb_ref,
                 qt_ref, qit_ref, wit_ref, k_ref, ki_ref, vt_ref, dq0_ref, dq1_ref, dq2_ref, gates_ref,
                 dil_scr, dil_tmp):
    x = x_ref[...]
    sh, sc = mod_ref[3:4, :], mod_ref[4:5, :]
    u = (_rms_rows(x) * gpre_ref[...] * (1.0 + sc) + sh).astype(BF16)
    std = jnp.dot(u, wstd_ref[...], preferred_element_type=F32)
    tr = lax.dot_general(wtr_ref[...], u, (((1,), (1,)), ((), ())),
                         preferred_element_type=F32)

    cq = tr[R_CQ:R_CQ + Q_LORA]
    cq = cq * lax.rsqrt(jnp.mean(cq * cq, axis=0, keepdims=True) + NORM_EPS) * gcq_ref[...]
    cq = cq.astype(BF16)
    ckv_t = tr[R_CKV:R_CKV + KV_LORA]
    ckv_t = ckv_t * lax.rsqrt(jnp.mean(ckv_t * ckv_t, axis=0, keepdims=True) + NORM_EPS) * gckv_col_ref[...]
    cos_t, sin_t = tt_ref[0], tt_ref[1]

    attn_scale = (D_NOPE + D_ROPE_A) ** -0.5 * LOG2_E
    q_t = jnp.dot(wuq_ref[...], cq, preferred_element_type=F32) * attn_scale
    idx_scale = D_IDX ** -0.5
    qi_t = jnp.dot(wiq_ref[...], cq, preferred_element_type=F32) * idx_scale
    for h in range(H_A):
        r0 = h * HEAD_PAD
        x1, x2 = q_t[r0 + 64:r0 + 80], q_t[r0 + 80:r0 + 96]
        qt_ref[r0:r0 + 64, :] = q_t[r0:r0 + 64].astype(BF16)
        qt_ref[r0 + 64:r0 + 80, :] = (x1 * cos_t - x2 * sin_t).astype(BF16)
        qt_ref[r0 + 80:r0 + 96, :] = (x2 * cos_t + x1 * sin_t).astype(BF16)
        qt_ref[r0 + 96:r0 + 128, :] = q_t[r0 + 96:r0 + 128].astype(BF16)
        y1, y2 = qi_t[r0:r0 + 16], qi_t[r0 + 16:r0 + 32]
        qit_ref[r0:r0 + 16, :] = (y1 * cos_t - y2 * sin_t).astype(BF16)
        qit_ref[r0 + 16:r0 + 32, :] = (y2 * cos_t + y1 * sin_t).astype(BF16)
        qit_ref[r0 + 32:r0 + 128, :] = qi_t[r0 + 32:r0 + 128].astype(BF16)
    wit_ref[...] = tr[R_WI:R_WI + H_IDX] * (H_IDX ** -0.5)
    v_t = jnp.dot(wuv_ref[...], ckv_t.astype(BF16), preferred_element_type=F32).astype(BF16)
    for h in range(H_A):
        vt_ref[h * V_ROWS:h * V_ROWS + D_V_A, :] = v_t[h * D_V_A:(h + 1) * D_V_A]
        vt_ref[h * V_ROWS + D_V_A:(h + 1) * V_ROWS, :] = jnp.ones((V_ROWS - D_V_A, v_t.shape[1]), BF16)

    ckv = std[:, C_CKV:C_CKV + KV_LORA]
    ckv = (_rms_rows(ckv) * gckv_row_ref[...]).astype(BF16)
    k_nope = jnp.dot(ckv, wuk_ref[...], preferred_element_type=F32)
    cos_k, sin_k = cb_ref[:, 0:32], sb_ref[:, 0:32]
    k_rope = std[:, C_KR:C_KR + 32] * cos_k + std[:, C_KR_SW:C_KR_SW + 32] * sin_k
    k_ref[...] = (k_nope + jnp.dot(k_rope.astype(BF16), ekr_ref[...],
                                   preferred_element_type=F32)).astype(BF16)
    k_idx = std[:, C_KI:C_KI + 32] * cos_k + std[:, C_KI_SW:C_KI_SW + 32] * sin_k
    ki_ref[...] = jnp.dot(k_idx.astype(BF16), eki_ref[...], preferred_element_type=F32).astype(BF16)

    cos_a, sin_a = ca_ref[...], sa_ref[...]
    tp = x.shape[0]
    for g, (out_ref, (_, dilation)) in enumerate(zip((dq0_ref, dq1_ref, dq2_ref), DIL_PAIRS)):
        for which in range(2):
            scale = (D_HEAD_B ** -0.5 * LOG2_E) if which == 0 else 1.0
            c0 = C_QKVB + (which * N_GROUPS_B + g) * WIDTH_B
            x1, x2 = std[:, c0:c0 + 128], std[:, c0 + 128:c0 + 256]
            dil_scr[2 * which] = (x1 * cos_a - x2 * sin_a) * scale
            dil_scr[2 * which + 1] = (x2 * cos_a + x1 * sin_a) * scale
        c0 = C_QKVB + (2 * N_GROUPS_B + g) * WIDTH_B
        dil_scr[4] = std[:, c0:c0 + 128]
        dil_scr[5] = std[:, c0 + 128:c0 + 256]
        for c in range(3 * WIDTH_B // LANES):
            if dilation <= 4:
                for r in range(dilation):
                    out_ref[r, :, c * LANES:(c + 1) * LANES] = (
                        dil_scr[c, pl.ds(r, tp // dilation, stride=dilation), :].astype(BF16))
            else:
                quarter = tp // 4
                for r4 in range(4):
                    dil_tmp[c, r4 * quarter:(r4 + 1) * quarter, :] = dil_scr[c, pl.ds(r4, quarter, stride=4), :]
                for r in range(dilation):
                    out_ref[r, :, c * LANES:(c + 1) * LANES] = dil_tmp[
                        c, pl.ds((r % 4) * quarter + r // 4, tp // dilation, stride=dilation // 4), :].astype(BF16)
    gates_ref[...] = jax.nn.sigmoid(std[:, C_GATES:]).astype(BF16)


def _proj_call(x, mod, g_pre, w, tabs):
    b, s, d = x.shape
    tt, ca, sa, cb, sb = tabs
    tp = TP_PROJ
    const = lambda a: pl.BlockSpec(a.shape, lambda i, j: (0,) * a.ndim)
    rows = lambda width: pl.BlockSpec((None, tp, width), lambda i, j: (i, j, 0))
    colsT = lambda height: pl.BlockSpec((None, height, tp), lambda i, j: (i, 0, j))
    consts = [g_pre, w["w_std"], w["w_tr"], w["g_cq_col"], w["g_ckv_col"], w["g_ckv_row"],
              w["w_uq_t"], w["w_iq_t"], w["w_uk_p"], w["w_uv_t"], w["e_kr"], w["e_ki"]]
    hq = H_A * HEAD_PAD
    dil_shapes = [jax.ShapeDtypeStruct((b, dil, s // dil, 3 * WIDTH_B), BF16) for _, dil in DIL_PAIRS]
    dil_specs = [pl.BlockSpec((None, dil, tp // dil, 3 * WIDTH_B), lambda i, j: (i, 0, j, 0))
                 for _, dil in DIL_PAIRS]
    return pl.pallas_call(
        _proj_kernel,
        out_shape=(jax.ShapeDtypeStruct((b, hq, s), BF16),
                   jax.ShapeDtypeStruct((b, hq, s), BF16),
                   jax.ShapeDtypeStruct((b, H_IDX, s), F32),
                   jax.ShapeDtypeStruct((b, s, hq), BF16),
                   jax.ShapeDtypeStruct((b, s, LANES), BF16),
                   jax.ShapeDtypeStruct((b, H_A * V_ROWS, s), BF16),
                   *dil_shapes,
                   jax.ShapeDtypeStruct((b, s, 2048), BF16)),
        grid=(b, s // tp),
        in_specs=[rows(d), pl.BlockSpec((None, N_MOD, d), lambda i, j: (i, 0, 0))]
                 + [const(a) for a in consts]
                 + [pl.BlockSpec((None, 2, 16, tp), lambda i, j: (i, 0, 0, j)),
                    rows(LANES), rows(LANES), rows(LANES), rows(LANES)],
        out_specs=(colsT(hq), colsT(hq), colsT(H_IDX), rows(hq), rows(LANES), colsT(H_A * V_ROWS),
                   *dil_specs, rows(2048)),
        scratch_shapes=[pltpu.VMEM((3 * WIDTH_B // LANES, tp, LANES), F32)] * 2,
        compiler_params=_params(2),
        name="proj",
    )(x, mod, *consts, tt, ca, sa, cb, sb)


def _dsa_kernel(qit_ref, wit_ref, qt_ref, ki_ref, k_ref, vt_ref, o_ref,
                sc_ref, m_ref, acc_ref, s_ref, s2_ref, smax_ref, smax2_ref, *, topk):
    tq = qt_ref.shape[1]
    tk = s_ref.shape[1]
    i = pl.program_id(1)
    nkb = (i + 1) * (tq // tk)
    npair = (nkb + 1) // 2
    qpos = i * tq + lax.broadcasted_iota(jnp.int32, (1, tq), 1)

    def fold(v):
        return jnp.sum(v.reshape(v.shape[0] // FOLD_ROWS, FOLD_ROWS, tq), axis=0)

    def park_logits(kb, park_ref):
        ki = ki_ref[pl.ds(pl.multiple_of(kb * tk, tk), tk), :]
        for h in range(H_IDX):
            park_ref[h] = jnp.dot(ki, qit_ref[h * HEAD_PAD:(h + 1) * HEAD_PAD, :], preferred_element_type=F32)

    def rows8(v, op):
        return op(v.reshape(IDX_ROWS // 8, 8, LANES), axis=0)

    piece_iota = lax.broadcasted_iota(jnp.int32, (IDX_ROWS, LANES), 0)

    def reduce_block(kb, park_ref, stats, on_diagonal):
        k0 = pl.multiple_of(kb * tk, tk)
        columns = []
        for c in range(tq // LANES):
            lanes = slice(c * LANES, (c + 1) * LANES)
            smax, smin, c_pos, c_nn = [v[:, lanes] for v in stats]
            w = [wit_ref[h:h + 1, lanes] for h in range(H_IDX)]
            for j in range(tk // IDX_ROWS):
                r0 = j * IDX_ROWS
                acc = jnp.zeros((IDX_ROWS, LANES), F32)
                for h in range(H_IDX):
                    acc = acc + w[h] * jnp.maximum(park_ref[h, r0:r0 + IDX_ROWS, lanes], 0.0)
                if on_diagonal:
                    causal = (k0 + r0 + piece_iota) <= qpos[:, lanes]
                    sv = jnp.where(causal, acc, -jnp.inf)
                    smin = jnp.minimum(smin, rows8(jnp.where(causal, acc, jnp.inf), jnp.min))
                else:
                    sv = acc
                    smin = jnp.minimum(smin, rows8(acc, jnp.min))
                sc_ref[pl.ds(k0 + r0, IDX_ROWS), lanes] = sv
                smax = jnp.maximum(smax, rows8(sv, jnp.max))
                c_pos = c_pos + rows8(jnp.where(sv > 0.0, 1.0, 0.0), jnp.sum)
                c_nn = c_nn + rows8(jnp.where(sv >= 0.0, 1.0, 0.0), jnp.sum)
            columns.append((smax, smin, c_pos, c_nn))
        return tuple(jnp.concatenate(parts, axis=1) for parts in zip(*columns))

    def idx_pair(kp, stats):
        park_logits(2 * kp + 1, s2_ref)
        stats = reduce_block(2 * kp, s_ref, stats, False)
        park_logits(2 * kp + 2, s_ref)
        return reduce_block(2 * kp + 1, s2_ref, stats, False)

    park_logits(0, s_ref)
    zero8 = jnp.zeros((8, tq), F32)
    stats = lax.fori_loop(0, i, idx_pair,
                          (jnp.full((8, tq), -jnp.inf, F32), jnp.full((8, tq), jnp.inf, F32), zero8, zero8))
    park_logits(nkb - 1, s2_ref)
    stats = reduce_block(nkb - 2, s_ref, stats, True)
    stats = reduce_block(nkb - 1, s2_ref, stats, True)
    smax = jnp.max(stats[0], axis=0, keepdims=True)
    smin = jnp.min(stats[1], axis=0, keepdims=True)
    c_pos, c_nn = [jnp.sum(v, axis=0, keepdims=True) for v in stats[2:]]

    @pl.when(nkb % 2 == 1)
    def _pad_block():
        sc_ref[pl.ds(pl.multiple_of(nkb * tk, tk), tk), :] = jnp.full((tk, tq), -jnp.inf, F32)

    n_causal = (qpos + 1).astype(F32)
    kf = jnp.minimum(qpos + 1, topk).astype(F32)

    def count_gt(thr):
        def body(kp, c):
            k0 = pl.multiple_of(kp * (2 * tk), 2 * tk)
            for j in range(2 * tk // COUNT_ROWS):
                rows = pl.ds(k0 + j * COUNT_ROWS, COUNT_ROWS)
                c = c + fold(jnp.where(sc_ref[rows, :] > thr, 1.0, 0.0))
            return c
        part = lax.fori_loop(0, npair, body, jnp.zeros((FOLD_ROWS, tq), F32))
        return jnp.sum(part, axis=0, keepdims=True)

    below = smin - (1.0 + jnp.abs(smin))
    zero_tie = jnp.logical_and(c_pos < kf, kf <= c_nn)
    positive = c_pos >= kf
    searching = jnp.where(zero_tie, 0.0, 1.0)

    def bisect_cond(carry):
        return jnp.logical_and(carry[0] < BISECT_MAX, carry[1] > 0)

    def bisect_body(carry):
        it, _, lo, hi, c_lo, c_hi = carry
        for _ in range(BISECT_CHECK):
            mid = 0.5 * (lo + hi)
            c_mid = count_gt(mid)
            ge = c_mid >= kf
            up = jnp.logical_and(ge, searching > 0.0)
            down = jnp.logical_and(jnp.logical_not(ge), searching > 0.0)
            lo, c_lo = jnp.where(up, mid, lo), jnp.where(up, c_mid, c_lo)
            hi, c_hi = jnp.where(down, mid, hi), jnp.where(down, c_mid, c_hi)
        open_queries = (jnp.max((c_lo - kf) * searching) > 0.0).astype(jnp.int32)
        return it + BISECT_CHECK, open_queries, lo, hi, c_lo, c_hi

    lo0 = jnp.where(positive, 0.0, jnp.where(zero_tie, 0.0, below))
    hi0 = jnp.where(positive, smax, 0.0)
    c_lo0 = jnp.where(positive, c_pos, jnp.where(zero_tie, c_nn, n_causal))
    c_hi0 = jnp.where(positive, 0.0, c_pos)
    first_open = (jnp.max((c_lo0 - kf) * searching) > 0.0).astype(jnp.int32)
    _, _, lo, hi, c_lo, c_hi = lax.while_loop(
        bisect_cond, bisect_body, (jnp.int32(0), first_open, lo0, hi0, c_lo0, c_hi0))

    @pl.when(jnp.max(jnp.maximum(c_lo - kf, 1.0 - searching)) > 0.0)
    def _break_ties():
        need = kf - c_hi
        closed = 1.0 - searching
        prefix = jnp.where(lax.broadcasted_iota(jnp.int32, (tk, tk), 0)
                           >= lax.broadcasted_iota(jnp.int32, (tk, tk), 1), 1.0, 0.0).astype(BF16)

        def body(kp, seen):
            blocks = []
            for j in range(2):
                k0 = pl.multiple_of((2 * kp + j) * tk, tk)
                sv = sc_ref[pl.ds(k0, tk), :]
                above_lo = jnp.where(sv > lo, 1.0, jnp.where(sv >= lo, closed, 0.0))
                member = jnp.where(sv > hi, 0.0, above_lo)
                blocks.append((k0, member, jnp.dot(prefix, member.astype(BF16), preferred_element_type=F32)))
            for k0, member, within in blocks:
                rank = within + seen
                sv = sc_ref[pl.ds(k0, tk), :]
                sc_ref[pl.ds(k0, tk), :] = jnp.where(member > 0.0, jnp.where(rank > need, -jnp.inf, jnp.inf), sv)
                seen = rank[tk - 1:tk, :]
            return seen

        lax.fori_loop(0, npair, body, jnp.zeros((1, tq), F32))

    m_ref[...] = jnp.full(m_ref.shape, NEG_BIG, F32)
    acc_ref[...] = jnp.zeros(acc_ref.shape, F32)

    def scores_into(kb, park_ref, max_ref):
        k0 = pl.multiple_of(kb * tk, tk)
        bias = jnp.where(sc_ref[pl.ds(k0, tk), :] > lo, 0.0, NEG_BIG)
        for h in range(H_A):
            kh = k_ref[pl.ds(k0, tk), h * HEAD_PAD:(h + 1) * HEAD_PAD]
            s = jnp.dot(kh, qt_ref[h * HEAD_PAD:(h + 1) * HEAD_PAD, :], preferred_element_type=F32) + bias
            park_ref[h] = s
            max_ref[h:h + 1, :] = jnp.max(s, axis=0, keepdims=True)

    def softmax_from(kb, park_ref, max_ref):
        k0 = pl.multiple_of(kb * tk, tk)
        for h in range(H_A):
            m_old = m_ref[h:h + 1, :]
            m_new = jnp.maximum(m_old, max_ref[h:h + 1, :])
            alpha = jnp.exp2(m_old - m_new)
            p = jnp.exp2((park_ref[h] - m_new).astype(BF16))
            rows = slice(h * V_ROWS, (h + 1) * V_ROWS)
            pv = jnp.dot(vt_ref[rows, pl.ds(k0, tk)], p, preferred_element_type=F32)
            acc_ref[rows, :] = alpha * acc_ref[rows, :] + pv
            m_ref[h:h + 1, :] = m_new

    last_block = 2 * npair - 1
    scores_into(0, s_ref, smax_ref)

    def att_body(kp, carry):
        scores_into(2 * kp + 1, s2_ref, smax2_ref)
        softmax_from(2 * kp, s_ref, smax_ref)
        scores_into(jnp.minimum(2 * kp + 2, last_block), s_ref, smax_ref)
        softmax_from(2 * kp + 1, s2_ref, smax2_ref)
        return carry

    lax.fori_loop(0, npair, att_body, 0)
    out = [acc_ref[h * V_ROWS:h * V_ROWS + D_V_A, :] / acc_ref[h * V_ROWS + D_V_A:h * V_ROWS + D_V_A + 1, :]
           for h in range(H_A)]
    o_ref[...] = jnp.concatenate(out, axis=0).T.astype(BF16)


def _dsa_call(qt, qit, wit, k, ki, vt):
    b, hq, s = qt.shape
    tq = TQ_DSA
    topk = min(TOPK_MAX, s // 4)
    colsT = lambda height: pl.BlockSpec((None, height, tq), lambda i, j: (i, 0, j))
    whole = lambda a: pl.BlockSpec((None,) + a.shape[1:], lambda i, j: (i, 0, 0))
    return pl.pallas_call(
        functools.partial(_dsa_kernel, topk=topk),
        out_shape=jax.ShapeDtypeStruct((b, s, WIDTH_A), BF16),
        grid=(b, s // tq),
        in_specs=[colsT(hq), colsT(H_IDX), colsT(hq), whole(ki), whole(k), whole(vt)],
        out_specs=pl.BlockSpec((None, tq, WIDTH_A), lambda i, j: (i, j, 0)),
        scratch_shapes=[pltpu.VMEM((s, tq), F32),
                        pltpu.VMEM((H_A, tq), F32),
                        pltpu.VMEM((H_A * V_ROWS, tq), F32),
                        pltpu.VMEM((H_A, TK_DSA, tq), F32),
                        pltpu.VMEM((H_A, TK_DSA, tq), F32),
                        pltpu.VMEM((H_A, tq), F32),
                        pltpu.VMEM((H_A, tq), F32)],
        compiler_params=_params(2),
        name="dsa",
    )(qit, wit, qt, ki, k, vt)


def _dil_kernel(q_ref, kc_ref, kp_ref, vc_ref, vp_ref, o_ref, lse_ref, s_ref):
    tb = kp_ref.shape[0]
    n_sub = q_ref.shape[0] // tb
    first_step = pl.program_id(2) == 0
    k = jnp.concatenate([kp_ref[...], kc_ref[...]], axis=0)
    v = jnp.concatenate([vp_ref[...], vc_ref[...]], axis=0)
    r = lax.broadcasted_iota(jnp.int32, (tb, 2 * tb), 0)
    c = lax.broadcasted_iota(jnp.int32, (tb, 2 * tb), 1)
    behind = (tb + r - c).astype(jnp.uint32)
    bias = jnp.where(behind <= tb, 0.0, -jnp.inf)
    no_prev = jnp.where(first_step, r, tb).astype(jnp.uint32)
    bias0 = jnp.where(behind <= no_prev, 0.0, -jnp.inf)
    lane = lax.broadcasted_iota(jnp.int32, (1, WIDTH_B), 1)
    head_qk = (lane % 128) // (D_HEAD_B // 2)
    head_v = lane // D_HEAD_B
    nt = (((1,), (1,)), ((), ()))
    for j in range(n_sub):
        q = q_ref[j * tb:(j + 1) * tb, :]
        for h in range(H_B):
            qh = jnp.where(head_qk == h, q, jnp.zeros_like(q))
            s_ref[j * H_B + h] = (lax.dot_general(qh, k[j * tb:(j + 2) * tb], nt, preferred_element_type=F32)
                                  + (bias0 if j == 0 else bias))
    for j in range(n_sub):
        vj = v[j * tb:(j + 2) * tb]
        num = jnp.zeros((tb, WIDTH_B), F32)
        den = jnp.zeros((tb, WIDTH_B), F32)
        top = jnp.zeros((tb, WIDTH_B), F32)
        for h in range(H_B):
            s = s_ref[j * H_B + h]
            m = jnp.max(s, axis=1, keepdims=True)
            p = jnp.exp2(s - m).astype(BF16)
            pv = jnp.dot(p, jnp.where(head_v == h, vj, jnp.ones_like(vj)), preferred_element_type=F32)
            mine = head_v == h
            num = jnp.where(mine, pv, num)
            den = jnp.where(mine, pltpu.roll(pv, WIDTH_B // 2, axis=1), den)
            top = jnp.where(mine, m, top)
        o_ref[j * tb:(j + 1) * tb, :] = (num / den).astype(o_ref.dtype)
        lse_ref[j * tb:(j + 1) * tb, :] = top + jnp.log2(den)


def _dil_call(qkv):
    b, dilation, n, _ = qkv.shape
    tb, tq = TB_DIL, min(TQ_DIL, n)
    per = tq // tb
    cur = lambda which: pl.BlockSpec((None, None, tq, WIDTH_B), lambda i, r, j: (i, r, j, which))
    prev = lambda which: pl.BlockSpec((None, None, tb, WIDTH_B),
                                      lambda i, r, j: (i, r, jnp.maximum(j * per - 1, 0), which))
    out_spec = pl.BlockSpec((None, None, tq, WIDTH_B), lambda i, r, j: (i, r, j, 0))
    out = lambda dtype: jax.ShapeDtypeStruct((b, dilation, n, WIDTH_B), dtype)
    return pl.pallas_call(
        _dil_kernel,
        out_shape=(out(BF16), out(F32)),
        grid=(b, dilation, n // tq),
        in_specs=[cur(0), cur(1), prev(1), cur(2), prev(2)],
        out_specs=(out_spec, out_spec),
        scratch_shapes=[pltpu.VMEM((per * H_B, tb, 2 * tb), F32)],
        compiler_params=_params(3),
        name="dil",
    )(qkv, qkv, qkv, qkv, qkv)


def _merge_ffn_kernel(x_ref, mod_ref, oa_ref, o0_ref, o1_ref, o2_ref, l0_ref, l1_ref, l2_ref, gates_ref,
                      wua_ref, wub_ref, wo_ref, gpost_ref, gpre2_ref, wg_ref, wu_ref, wd_ref, gpost2_ref,
                      out_ref, acc_ref, *scr):
    tm = x_ref.shape[0]
    tmp = scr[4]

    def token_major(ref, buf):
        dilation = ref.shape[0]
        if dilation == 1:
            return ref[0]
        n_slab = ref.shape[2] // LANES
        for c in range(n_slab):
            if dilation <= 4:
                for r in range(dilation):
                    buf[c, pl.ds(r, tm // dilation, stride=dilation), :] = (
                        ref[r, :, c * LANES:(c + 1) * LANES].astype(F32))
            else:
                quarter = tm // 4
                for r in range(dilation):
                    tmp[c, pl.ds((r % 4) * quarter + r // 4, tm // dilation, stride=dilation // 4), :] = (
                        ref[r, :, c * LANES:(c + 1) * LANES].astype(F32))
                for r4 in range(4):
                    buf[c, pl.ds(r4, quarter, stride=4), :] = tmp[c, r4 * quarter:(r4 + 1) * quarter, :]
        return jnp.concatenate([buf[c] for c in range(n_slab)], axis=1)

    o0, l0 = token_major(o0_ref, None).astype(F32), token_major(l0_ref, None)
    o1, l1 = token_major(o1_ref, scr[0]), token_major(l1_ref, scr[1])
    o2, l2 = token_major(o2_ref, scr[2]), token_major(l2_ref, scr[3])
    m = jnp.maximum(jnp.maximum(l0, l1), l2)
    e0, e1, e2 = jnp.exp2(l0 - m), jnp.exp2(l1 - m), jnp.exp2(l2 - m)
    ob = (e0 * o0 + e1 * o1 + e2 * o2) / (e0 + e1 + e2)
    za = jnp.dot(oa_ref[...], wua_ref[...], preferred_element_type=F32)
    zb = jnp.dot(ob.astype(BF16), wub_ref[...], preferred_element_type=F32)
    d = za.shape[1]
    z = gates_ref[:, 0:d].astype(F32) * za + gates_ref[:, d:2 * d].astype(F32) * zb
    y = jnp.dot(z.astype(BF16), wo_ref[...], preferred_element_type=F32)
    x_mixed = x_ref[...] + mod_ref[5:6, :] * (_rms_rows(y) * gpost_ref[...])
    _ffn_tile(x_mixed, mod_ref, gpre2_ref, wg_ref, wu_ref, wd_ref, gpost2_ref, out_ref, acc_ref, 6)


def _merge_ffn_call(x, mod, oa, dil, gates, w_up_a, w_up_b, w_o, g_post, g_pre2, w_gate, w_up, w_down, g_post2):
    b, s, d = x.shape
    tm = TM_FFN
    rows = lambda width: pl.BlockSpec((None, tm, width), lambda i, j: (i, j, 0))
    const = lambda a: pl.BlockSpec(a.shape, lambda i, j: (0,) * a.ndim, pipeline_mode=pl.Buffered(1))
    (o0, l0), (o1, l1), (o2, l2) = dil
    res = lambda a: pl.BlockSpec((None, a.shape[1], tm // a.shape[1], WIDTH_B), lambda i, j: (i, 0, j, 0))
    consts = [w_up_a, w_up_b, w_o, g_post, g_pre2, w_gate, w_up, w_down, g_post2]
    return pl.pallas_call(
        _merge_ffn_kernel,
        out_shape=jax.ShapeDtypeStruct((b, s, d), F32),
        grid=(b, s // tm),
        in_specs=[rows(d), pl.BlockSpec((None, N_MOD, d), lambda i, j: (i, 0, 0)), rows(WIDTH_A)]
                 + [res(a) for a in (o0, o1, o2, l0, l1, l2)]
                 + [rows(2 * d)] + [const(a) for a in consts],
        out_specs=rows(d),
        scratch_shapes=[pltpu.VMEM((tm, d), F32)] + [pltpu.VMEM((WIDTH_B // LANES, tm, LANES), F32)] * 5,
        compiler_params=_params(2),
        name="merge_ffn",
    )(x, mod, oa, o0, o1, o2, l0, l1, l2, gates, *consts)


def _pad_heads(w_t):
    h, r, n = w_t.shape
    return jnp.pad(w_t, ((0, 0), (0, HEAD_PAD - r), (0, 0))).reshape(h * HEAD_PAD, n)


def _mixer_weights(w_in, g_cq, g_ckv, w_uq, w_uk, w_uv, w_iq):
    d = w_in.shape[0]
    o_cq, o_ckv, o_kr, o_ki, o_wi = 0, Q_LORA, Q_LORA + KV_LORA, Q_LORA + KV_LORA + D_ROPE_A, \
        Q_LORA + KV_LORA + D_ROPE_A + D_IDX
    o_qkv = o_wi + H_IDX
    o_gates = o_qkv + N_QKV_B

    def swap_halves(w):
        half = w.shape[1] // 2
        return jnp.concatenate([w[:, half:], w[:, :half]], axis=1)

    w_kr, w_ki = w_in[:, o_kr:o_kr + D_ROPE_A], w_in[:, o_ki:o_ki + D_IDX]
    qkv = w_in[:, o_qkv:o_gates].reshape(d, 3, N_GROUPS_B, H_B, 2, D_HEAD_B // 2)
    qk_split = qkv[:, 0:2].transpose(0, 1, 2, 4, 3, 5).reshape(d, 2 * N_GROUPS_B * WIDTH_B)
    v_cols = qkv[:, 2].reshape(d, N_GROUPS_B * WIDTH_B)
    w_std = jnp.concatenate([w_in[:, o_ckv:o_ckv + KV_LORA], w_kr, swap_halves(w_kr), w_ki, swap_halves(w_ki),
                             qk_split, v_cols, w_in[:, o_gates:]], axis=1)
    w_tr = jnp.concatenate([w_in[:, o_cq:o_cq + Q_LORA], w_in[:, o_ckv:o_ckv + KV_LORA],
                            w_in[:, o_wi:o_wi + H_IDX]], axis=1).T
    w_tr = jnp.pad(w_tr, ((0, N_TR - w_tr.shape[0]), (0, 0)))

    e_kr = np.zeros((D_ROPE_A, H_A * HEAD_PAD), np.float32)
    for h in range(H_A):
        e_kr[np.arange(D_ROPE_A), h * HEAD_PAD + D_NOPE + np.arange(D_ROPE_A)] = 1.0
    e_ki = np.zeros((D_IDX, LANES), np.float32)
    e_ki[np.arange(D_IDX), np.arange(D_IDX)] = 1.0
    return {
        "w_std": w_std.astype(BF16),
        "w_tr": w_tr.astype(BF16),
        "g_cq_col": g_cq.reshape(Q_LORA, 1),
        "g_ckv_col": g_ckv.reshape(KV_LORA, 1),
        "g_ckv_row": g_ckv.reshape(1, KV_LORA),
        "w_uq_t": _pad_heads(w_uq.transpose(1, 2, 0)).astype(BF16),
        "w_iq_t": _pad_heads(w_iq.transpose(1, 2, 0)).astype(BF16),
        "w_uk_p": jnp.pad(w_uk, ((0, 0), (0, 0), (0, HEAD_PAD - D_NOPE))).reshape(KV_LORA, H_A * HEAD_PAD).astype(BF16),
        "w_uv_t": w_uv.transpose(1, 2, 0).reshape(WIDTH_A, KV_LORA).astype(BF16),
        "e_kr": jnp.asarray(e_kr, BF16),
        "e_ki": jnp.asarray(e_ki, BF16),
    }


def kernel(x, c, positions, w_mod, b_mod, g_pre_ffn1, w_gate1, w_up1, w_down1, g_post_ffn1, g_pre_mix, w_in, g_cq, g_ckv, w_uq, w_uk, w_uv, w_iq, w_up_a, w_up_b, w_o, g_post_mix, g_pre_ffn2, w_gate2, w_up2, w_down2, g_post_ffn2):
    b, s, d = x.shape
    assert s % (16 * TB_DIL) == 0 and s % TM_FFN == 0 and b <= 8
    c8 = jnp.pad(c, ((0, 8 - b), (0, 0)))
    half_b, half_a = D_HEAD_B // 2, D_ROPE_A // 2
    inv = jnp.concatenate([ROPE_THETA ** (-jnp.arange(half_b, dtype=F32) / half_b),
                           ROPE_THETA ** (-jnp.arange(half_a, dtype=F32) / half_a)]).reshape(48, 1)
    tabs = _rope_call(positions.reshape(b, 1, s), inv)
    row = lambda g: g.reshape(1, -1)
    for l in range(w_mod.shape[0]):
        mod = _mod_call(c8, w_mod[l], b_mod[l].reshape(1, -1))[:b].reshape(b, N_MOD, d)
        x = _ffn_call(x, mod, row(g_pre_ffn1[l]), w_gate1[l].astype(BF16), w_up1[l].astype(BF16),
                      w_down1[l].astype(BF16), row(g_post_ffn1[l]), 0)
        w = _mixer_weights(w_in[l], g_cq[l], g_ckv[l], w_uq[l], w_uk[l], w_uv[l], w_iq[l])
        qt, qit, wit, k, ki, vt, dq0, dq1, dq2, gates = _proj_call(x, mod, row(g_pre_mix[l]), w, tabs)
        oa = _dsa_call(qt, qit, wit, k, ki, vt)
        dil = [_dil_call(dq) for dq in (dq0, dq1, dq2)]
        x = _merge_ffn_call(x, mod, oa, dil, gates, w_up_a[l].astype(BF16), w_up_b[l].astype(BF16),
                            w_o[l].astype(BF16), row(g_post_mix[l]), row(g_pre_ffn2[l]), w_gate2[l].astype(BF16),
                            w_up2[l].astype(BF16), w_down2[l].astype(BF16), row(g_post_ffn2[l]))
    return x
```

```python
import functools

import numpy as np
import jax
import jax.numpy as jnp
from jax import lax
from jax.experimental import pallas as pl
from jax.experimental.pallas import tpu as pltpu

F32 = jnp.float32
BF16 = jnp.bfloat16

D_FF = 2816
ROPE_THETA = 10000.0
NORM_EPS = 1e-6
H_A = 8
Q_LORA = 384
KV_LORA = 256
D_NOPE = 64
D_ROPE_A = 32
D_V_A = 64
H_IDX = 8
D_IDX = 32
TOPK_MAX = 256
DIL_PAIRS = ((128, 1), (512, 4), (2048, 16))
N_GROUPS_B = 3
H_B = 4
D_HEAD_B = 64
N_MOD = 9
WIDTH_A = H_A * D_V_A
WIDTH_B = H_B * D_HEAD_B
N_QKV_B = 3 * N_GROUPS_B * WIDTH_B

LANES = 128
HEAD_PAD = 128
VMEM_LIMIT = 56 * 1024 * 1024

TM_FFN = 512
CH_FFN = 256
TP_PROJ = 512
TQ_DSA = 512
TK_DSA = 256
V_ROWS = D_V_A + 16
TB_DIL = 128
TQ_DIL = 512
BISECT_MAX = 34
BISECT_CHECK = 2
FOLD_ROWS = 16
COUNT_ROWS = 64
IDX_ROWS = 64
NEG_BIG = -1e30
LOG2_E = 1.4426950408889634

C_CKV = 0
C_KR = 256
C_KR_SW = 288
C_KI = 320
C_KI_SW = 352
C_QKVB = 384
C_GATES = C_QKVB + N_QKV_B
N_STD = C_GATES + 2048
R_CQ = 0
R_CKV = Q_LORA
R_WI = Q_LORA + KV_LORA
N_TR = 768


def _params(n_axes):
    return pltpu.CompilerParams(dimension_semantics=("arbitrary",) * n_axes,
                                vmem_limit_bytes=VMEM_LIMIT)


def _rms_rows(x):
    return x * lax.rsqrt(jnp.mean(x * x, axis=-1, keepdims=True) + NORM_EPS)


def _mod_kernel(c_ref, w_ref, b_ref, o_ref):
    c = c_ref[...]
    o_ref[...] = jnp.dot(c * jax.nn.sigmoid(c), w_ref[...], preferred_element_type=F32) + b_ref[...]


def _mod_call(c8, w_mod, b_mod):
    d = c8.shape[1]
    return pl.pallas_call(
        _mod_kernel,
        out_shape=jax.ShapeDtypeStruct((8, N_MOD * d), F32),
        grid=(N_MOD,),
        in_specs=[pl.BlockSpec((8, d), lambda j: (0, 0)),
                  pl.BlockSpec((d, d), lambda j: (0, j)),
                  pl.BlockSpec((1, d), lambda j: (0, j))],
        out_specs=pl.BlockSpec((8, d), lambda j: (0, j)),
        compiler_params=_params(1),
        name="mod",
    )(c8, w_mod, b_mod)


def _rope_kernel(pos_ref, inv_ref, tt_ref, ca_ref, sa_ref, cb_ref, sb_ref):
    s = pos_ref.shape[1]
    ang = inv_ref[...] * pos_ref[...].astype(F32)
    cos = jnp.cos(ang)
    sin = jnp.sin(ang)
    cos_b, sin_b = cos[32:48], sin[32:48]
    tt_ref[0] = cos_b
    tt_ref[1] = sin_b
    cos_a = jnp.concatenate([cos[0:32]] * 4, axis=0)
    sin_a = jnp.concatenate([sin[0:32]] * 4, axis=0)
    cos_bt = jnp.concatenate([cos_b] * 8, axis=0)
    sin_bt = jnp.concatenate([-sin_b, sin_b] * 4, axis=0)
    for j in range(s // LANES):
        cols = slice(j * LANES, (j + 1) * LANES)
        ca_ref[cols, :] = cos_a[:, cols].T
        sa_ref[cols, :] = sin_a[:, cols].T
        cb_ref[cols, :] = cos_bt[:, cols].T
        sb_ref[cols, :] = sin_bt[:, cols].T


def _rope_call(pos3, inv):
    b, _, s = pos3.shape
    tab = jax.ShapeDtypeStruct((b, s, LANES), F32)
    tab_spec = pl.BlockSpec((None, s, LANES), lambda i: (i, 0, 0))
    return pl.pallas_call(
        _rope_kernel,
        out_shape=(jax.ShapeDtypeStruct((b, 2, 16, s), F32), tab, tab, tab, tab),
        grid=(b,),
        in_specs=[pl.BlockSpec((None, 1, s), lambda i: (i, 0, 0)),
                  pl.BlockSpec((48, 1), lambda i: (0, 0))],
        out_specs=(pl.BlockSpec((None, 2, 16, s), lambda i: (i, 0, 0, 0)),
                   tab_spec, tab_spec, tab_spec, tab_spec),
        compiler_params=_params(1),
        name="rope",
    )(pos3, inv)


def _ffn_kernel(x_ref, mod_ref, gpre_ref, wg_ref, wu_ref, wd_ref, gpost_ref, o_ref, acc_ref, *, j0):
    _ffn_tile(x_ref[...], mod_ref, gpre_ref, wg_ref, wu_ref, wd_ref, gpost_ref, o_ref, acc_ref, j0)


def _ffn_tile(x, mod_ref, gpre_ref, wg_ref, wu_ref, wd_ref, gpost_ref, o_ref, acc_ref, j0):
    sh, sc, gt = mod_ref[j0:j0 + 1, :], mod_ref[j0 + 1:j0 + 2, :], mod_ref[j0 + 2:j0 + 3, :]
    h = (_rms_rows(x) * gpre_ref[...] * (1.0 + sc) + sh).astype(BF16)
    n_ch = wg_ref.shape[1] // CH_FFN
    for ch in range(n_ch):
        cols = slice(ch * CH_FFN, (ch + 1) * CH_FFN)
        g = jnp.dot(h, wg_ref[:, cols], preferred_element_type=F32)
        u = jnp.dot(h, wu_ref[:, cols], preferred_element_type=F32)
        a = (g * jax.nn.sigmoid(g) * u).astype(BF16)
        part = jnp.dot(a, wd_ref[cols, :], preferred_element_type=F32)
        if ch == 0:
            acc_ref[...] = part
        else:
            acc_ref[...] += part
    o_ref[...] = x + 0.5 * gt * (_rms_rows(acc_ref[...]) * gpost_ref[...])


def _ffn_call(x, mod, g_pre, w_gate, w_up, w_down, g_post, j0):
    b, s, d = x.shape
    f = w_gate.shape[1]
    const = lambda shape: pl.BlockSpec(shape, lambda i, j: (0,) * len(shape), pipeline_mode=pl.Buffered(1))
    tile = pl.BlockSpec((None, TM_FFN, d), lambda i, j: (i, j, 0))
    return pl.pallas_call(
        functools.partial(_ffn_kernel, j0=j0),
        out_shape=jax.ShapeDtypeStruct((b, s, d), F32),
        grid=(b, s // TM_FFN),
        in_specs=[tile,
                  pl.BlockSpec((None, N_MOD, d), lambda i, j: (i, 0, 0)),
                  const((1, d)), const((d, f)), const((d, f)), const((f, d)), const((1, d))],
        out_specs=tile,
        scratch_shapes=[pltpu.VMEM((TM_FFN, d), F32)],
        compiler_params=_params(2),
        name="ffn",
    )(x, mod, g_pre, w_gate, w_up, w_down, g_post)


def _proj_kernel(x_ref, mod_ref, gpre_ref, wstd_ref, wtr_ref, gcq_ref, gckv_col_ref, gckv_row_ref,
                 wuq_ref, wiq_ref, wuk_ref, wuv_ref, ekr_ref, eki_ref,
                 tt_ref, ca_ref, sa_ref, cb_ref, sb_ref,
                 qt_ref, qit_ref, wit_ref, k_ref, ki_ref, vt_ref, dq0_ref, dq1_ref, dq2_ref, gates_ref,
                 dil_scr, dil_tmp):
    x = x_ref[...]
    sh, sc = mod_ref[3:4, :], mod_ref[4:5, :]
    u = (_rms_rows(x) * gpre_ref[...] * (1.0 + sc) + sh).astype(BF16)
    std = jnp.dot(u, wstd_ref[...], preferred_element_type=F32)
    tr = lax.dot_general(wtr_ref[...], u, (((1,), (1,)), ((), ())),
                         preferred_element_type=F32)

    cq = tr[R_CQ:R_CQ + Q_LORA]
    cq = cq * lax.rsqrt(jnp.mean(cq * cq, axis=0, keepdims=True) + NORM_EPS) * gcq_ref[...]
    cq = cq.astype(BF16)
    ckv_t = tr[R_CKV:R_CKV + KV_LORA]
    ckv_t = ckv_t * lax.rsqrt(jnp.mean(ckv_t * ckv_t, axis=0, keepdims=True) + NORM_EPS) * gckv_col_ref[...]
    cos_t, sin_t = tt_ref[0], tt_ref[1]

    attn_scale = (D_NOPE + D_ROPE_A) ** -0.5 * LOG2_E
    q_t = jnp.dot(wuq_ref[...], cq, preferred_element_type=F32) * attn_scale
    idx_scale = D_IDX ** -0.5
    qi_t = jnp.dot(wiq_ref[...], cq, preferred_element_type=F32) * idx_scale
    for h in range(H_A):
        r0 = h * HEAD_PAD
        x1, x2 = q_t[r0 + 64:r0 + 80], q_t[r0 + 80:r0 + 96]
        qt_ref[r0:r0 + 64, :] = q_t[r0:r0 + 64].astype(BF16)
        qt_ref[r0 + 64:r0 + 80, :] = (x1 * cos_t - x2 * sin_t).astype(BF16)
        qt_ref[r0 + 80:r0 + 96, :] = (x2 * cos_t + x1 * sin_t).astype(BF16)
        qt_ref[r0 + 96:r0 + 128, :] = q_t[r0 + 96:r0 + 128].astype(BF16)
        y1, y2 = qi_t[r0:r0 + 16], qi_t[r0 + 16:r0 + 32]
        qit_ref[r0:r0 + 16, :] = (y1 * cos_t - y2 * sin_t).astype(BF16)
        qit_ref[r0 + 16:r0 + 32, :] = (y2 * cos_t + y1 * sin_t).astype(BF16)
        qit_ref[r0 + 32:r0 + 128, :] = qi_t[r0 + 32:r0 + 128].astype(BF16)
    wit_ref[...] = tr[R_WI:R_WI + H_IDX] * (H_IDX ** -0.5)
    v_t = jnp.dot(wuv_ref[...], ckv_t.astype(BF16), preferred_element_type=F32).astype(BF16)
    for h in range(H_A):
        vt_ref[h * V_ROWS:h * V_ROWS + D_V_A, :] = v_t[h * D_V_A:(h + 1) * D_V_A]
        vt_ref[h * V_ROWS + D_V_A:(h + 1) * V_ROWS, :] = jnp.ones((V_ROWS - D_V_A, v_t.shape[1]), BF16)

    ckv = std[:, C_CKV:C_CKV + KV_LORA]
    ckv = (_rms_rows(ckv) * gckv_row_ref[...]).astype(BF16)
    k_nope = jnp.dot(ckv, wuk_ref[...], preferred_element_type=F32)
    cos_k, sin_k = cb_ref[:, 0:32], sb_ref[:, 0:32]
    k_rope = std[:, C_KR:C_KR + 32] * cos_k + std[:, C_KR_SW:C_KR_SW + 32] * sin_k
    k_ref[...] = (k_nope + jnp.dot(k_rope.astype(BF16), ekr_ref[...],
                                   preferred_element_type=F32)).astype(BF16)
    k_idx = std[:, C_KI:C_KI + 32] * cos_k + std[:, C_KI_SW:C_KI_SW + 32] * sin_k
    ki_ref[...] = jnp.dot(k_idx.astype(BF16), eki_ref[...], preferred_element_type=F32).astype(BF16)

    cos_a, sin_a = ca_ref[...], sa_ref[...]
    tp = x.shape[0]
    for g, (out_ref, (_, dilation)) in enumerate(zip((dq0_ref, dq1_ref, dq2_ref), DIL_PAIRS)):
        for which in range(2):
            scale = (D_HEAD_B ** -0.5 * LOG2_E) if which == 0 else 1.0
            c0 = C_QKVB + (which * N_GROUPS_B + g) * WIDTH_B
            x1, x2 = std[:, c0:c0 + 128], std[:, c0 + 128:c0 + 256]
            dil_scr[2 * which] = (x1 * cos_a - x2 * sin_a) * scale
            dil_scr[2 * which + 1] = (x2 * cos_a + x1 * sin_a) * scale
        c0 = C_QKVB + (2 * N_GROUPS_B + g) * WIDTH_B
        dil_scr[4] = std[:, c0:c0 + 128]
        dil_scr[5] = std[:, c0 + 128:c0 + 256]
        for c in range(3 * WIDTH_B // LANES):
            if dilation <= 4:
                for r in range(dilation):
                    out_ref[r, :, c * LANES:(c + 1) * LANES] = (
                        dil_scr[c, pl.ds(r, tp // dilation, stride=dilation), :].astype(BF16))
            else:
                quarter = tp // 4
                for r4 in range(4):
                    dil_tmp[c, r4 * quarter:(r4 + 1) * quarter, :] = dil_scr[c, pl.ds(r4, quarter, stride=4), :]
                for r in range(dilation):
                    out_ref[r, :, c * LANES:(c + 1) * LANES] = dil_tmp[
                        c, pl.ds((r % 4) * quarter + r // 4, tp // dilation, stride=dilation // 4), :].astype(BF16)
    gates_ref[...] = jax.nn.sigmoid(std[:, C_GATES:]).astype(BF16)


def _proj_call(x, mod, g_pre, w, tabs):
    b, s, d = x.shape
    tt, ca, sa, cb, sb = tabs
    tp = TP_PROJ
    const = lambda a: pl.BlockSpec(a.shape, lambda i, j: (0,) * a.ndim)
    rows = lambda width: pl.BlockSpec((None, tp, width), lambda i, j: (i, j, 0))
    colsT = lambda height: pl.BlockSpec((None, height, tp), lambda i, j: (i, 0, j))
    consts = [g_pre, w["w_std"], w["w_tr"], w["g_cq_col"], w["g_ckv_col"], w["g_ckv_row"],
              w["w_uq_t"], w["w_iq_t"], w["w_uk_p"], w["w_uv_t"], w["e_kr"], w["e_ki"]]
    hq = H_A * HEAD_PAD
    dil_shapes = [jax.ShapeDtypeStruct((b, dil, s // dil, 3 * WIDTH_B), BF16) for _, dil in DIL_PAIRS]
    dil_specs = [pl.BlockSpec((None, dil, tp // dil, 3 * WIDTH_B), lambda i, j: (i, 0, j, 0))
                 for _, dil in DIL_PAIRS]
    return pl.pallas_call(
        _proj_kernel,
        out_shape=(jax.ShapeDtypeStruct((b, hq, s), BF16),
                   jax.ShapeDtypeStruct((b, hq, s), BF16),
                   jax.ShapeDtypeStruct((b, H_IDX, s), F32),
                   jax.ShapeDtypeStruct((b, s, hq), BF16),
                   jax.ShapeDtypeStruct((b, s, LANES), BF16),
                   jax.ShapeDtypeStruct((b, H_A * V_ROWS, s), BF16),
                   *dil_shapes,
                   jax.ShapeDtypeStruct((b, s, 2048), BF16)),
        grid=(b, s // tp),
        in_specs=[rows(d), pl.BlockSpec((None, N_MOD, d), lambda i, j: (i, 0, 0))]
                 + [const(a) for a in consts]
                 + [pl.BlockSpec((None, 2, 16, tp), lambda i, j: (i, 0, 0, j)),
                    rows(LANES), rows(LANES), rows(LANES), rows(LANES)],
        out_specs=(colsT(hq), colsT(hq), colsT(H_IDX), rows(hq), rows(LANES), colsT(H_A * V_ROWS),
                   *dil_specs, rows(2048)),
        scratch_shapes=[pltpu.VMEM((3 * WIDTH_B // LANES, tp, LANES), F32)] * 2,
        compiler_params=_params(2),
        name="proj",
    )(x, mod, *consts, tt, ca, sa, cb, sb)


def _dsa_kernel(qit_ref, wit_ref, qt_ref, ki_ref, k_ref, vt_ref, o_ref,
                sc_ref, m_ref, acc_ref, s_ref, s2_ref, smax_ref, smax2_ref, *, topk):
    tq = qt_ref.shape[1]
    tk = s_ref.shape[1]
    i = pl.program_id(1)
    nkb = (i + 1) * (tq // tk)
    npair = (nkb + 1) // 2
    qpos = i * tq + lax.broadcasted_iota(jnp.int32, (1, tq), 1)

    def fold(v):
        return jnp.sum(v.reshape(v.shape[0] // FOLD_ROWS, FOLD_ROWS, tq), axis=0)

    def park_logits(kb, park_ref, heads=range(H_IDX)):
        ki = ki_ref[pl.ds(pl.multiple_of(kb * tk, tk), tk), :]
        for h in heads:
            park_ref[h] = jnp.dot(ki, qit_ref[h * HEAD_PAD:(h + 1) * HEAD_PAD, :], preferred_element_type=F32)

    def rows8(v, op):
        return op(v.reshape(IDX_ROWS // 8, 8, LANES), axis=0)

    piece_iota = lax.broadcasted_iota(jnp.int32, (IDX_ROWS, LANES), 0)
    n_col = tq // LANES
    heads_per_col = H_IDX // n_col

    def reduce_block(kb, park_ref, stats, on_diagonal, ahead=None):
        k0 = pl.multiple_of(kb * tk, tk)
        columns = []
        for c in range(n_col):
            if ahead is not None:
                park_logits(ahead[0], ahead[1], range(c * heads_per_col, (c + 1) * heads_per_col))
            lanes = slice(c * LANES, (c + 1) * LANES)
            smax, smin, c_pos, c_nn = [v[:, lanes] for v in stats]
            w = [wit_ref[h:h + 1, lanes] for h in range(H_IDX)]
            for j in range(tk // IDX_ROWS):
                r0 = j * IDX_ROWS
                acc = jnp.zeros((IDX_ROWS, LANES), F32)
                for h in range(H_IDX):
                    acc = acc + w[h] * jnp.maximum(park_ref[h, r0:r0 + IDX_ROWS, lanes], 0.0)
                if on_diagonal:
                    causal = (k0 + r0 + piece_iota) <= qpos[:, lanes]
                    sv = jnp.where(causal, acc, -jnp.inf)
                    smin = jnp.minimum(smin, rows8(jnp.where(causal, acc, jnp.inf), jnp.min))
                else:
                    sv = acc
                    smin = jnp.minimum(smin, rows8(acc, jnp.min))
                sc_ref[pl.ds(k0 + r0, IDX_ROWS), lanes] = sv
                smax = jnp.maximum(smax, rows8(sv, jnp.max))
                c_pos = c_pos + rows8(jnp.where(sv > 0.0, 1.0, 0.0), jnp.sum)
                c_nn = c_nn + rows8(jnp.where(sv >= 0.0, 1.0, 0.0), jnp.sum)
            columns.append((smax, smin, c_pos, c_nn))
        return tuple(jnp.concatenate(parts, axis=1) for parts in zip(*columns))

    def idx_pair(kp, stats):
        stats = reduce_block(2 * kp, s_ref, stats, False, ahead=(2 * kp + 1, s2_ref))
        return reduce_block(2 * kp + 1, s2_ref, stats, False, ahead=(2 * kp + 2, s_ref))

    park_logits(0, s_ref)
    zero8 = jnp.zeros((8, tq), F32)
    stats = lax.fori_loop(0, i, idx_pair,
                          (jnp.full((8, tq), -jnp.inf, F32), jnp.full((8, tq), jnp.inf, F32), zero8, zero8))
    stats = reduce_block(nkb - 2, s_ref, stats, True, ahead=(nkb - 1, s2_ref))
    stats = reduce_block(nkb - 1, s2_ref, stats, True)
    smax = jnp.max(stats[0], axis=0, keepdims=True)
    smin = jnp.min(stats[1], axis=0, keepdims=True)
    c_pos, c_nn = [jnp.sum(v, axis=0, keepdims=True) for v in stats[2:]]

    @pl.when(nkb % 2 == 1)
    def _pad_block():
        sc_ref[pl.ds(pl.multiple_of(nkb * tk, tk), tk), :] = jnp.full((tk, tq), -jnp.inf, F32)

    n_causal = (qpos + 1).astype(F32)
    kf = jnp.minimum(qpos + 1, topk).astype(F32)

    def count_gt(thr):
        def body(kp, c):
            k0 = pl.multiple_of(kp * (2 * tk), 2 * tk)
            for j in range(2 * tk // COUNT_ROWS):
                rows = pl.ds(k0 + j * COUNT_ROWS, COUNT_ROWS)
                c = c + fold(jnp.where(sc_ref[rows, :] > thr, 1.0, 0.0))
            return c
        part = lax.fori_loop(0, npair, body, jnp.zeros((FOLD_ROWS, tq), F32))
        return jnp.sum(part, axis=0, keepdims=True)

    below = smin - (1.0 + jnp.abs(smin))
    zero_tie = jnp.logical_and(c_pos < kf, kf <= c_nn)
    positive = c_pos >= kf
    searching = jnp.where(zero_tie, 0.0, 1.0)

    def bisect_cond(carry):
        return jnp.logical_and(carry[0] < BISECT_MAX, carry[1] > 0)

    def bisect_body(carry):
        it, _, lo, hi, c_lo, c_hi = carry
        for _ in range(BISECT_CHECK):
            mid = 0.5 * (lo + hi)
            c_mid = count_gt(mid)
            ge = c_mid >= kf
            up = jnp.logical_and(ge, searching > 0.0)
            down = jnp.logical_and(jnp.logical_not(ge), searching > 0.0)
            lo, c_lo = jnp.where(up, mid, lo), jnp.where(up, c_mid, c_lo)
            hi, c_hi = jnp.where(down, mid, hi), jnp.where(down, c_mid, c_hi)
        open_queries = (jnp.max((c_lo - kf) * searching) > 0.0).astype(jnp.int32)
        return it + BISECT_CHECK, open_queries, lo, hi, c_lo, c_hi

    lo0 = jnp.where(positive, 0.0, jnp.where(zero_tie, 0.0, below))
    hi0 = jnp.where(positive, smax, 0.0)
    c_lo0 = jnp.where(positive, c_pos, jnp.where(zero_tie, c_nn, n_causal))
    c_hi0 = jnp.where(positive, 0.0, c_pos)
    first_open = (jnp.max((c_lo0 - kf) * searching) > 0.0).astype(jnp.int32)
    _, _, lo, hi, c_lo, c_hi = lax.while_loop(
        bisect_cond, bisect_body, (jnp.int32(0), first_open, lo0, hi0, c_lo0, c_hi0))

    @pl.when(jnp.max(jnp.maximum(c_lo - kf, 1.0 - searching)) > 0.0)
    def _break_ties():
        need = kf - c_hi
        closed = 1.0 - searching
        prefix = jnp.where(lax.broadcasted_iota(jnp.int32, (tk, tk), 0)
                           >= lax.broadcasted_iota(jnp.int32, (tk, tk), 1), 1.0, 0.0).astype(BF16)

        def body(kp, seen):
            blocks = []
            for j in range(2):
                k0 = pl.multiple_of((2 * kp + j) * tk, tk)
                sv = sc_ref[pl.ds(k0, tk), :]
                above_lo = jnp.where(sv > lo, 1.0, jnp.where(sv >= lo, closed, 0.0))
                member = jnp.where(sv > hi, 0.0, above_lo)
                blocks.append((k0, member, jnp.dot(prefix, member.astype(BF16), preferred_element_type=F32)))
            for k0, member, within in blocks:
                rank = within + seen
                sv = sc_ref[pl.ds(k0, tk), :]
                sc_ref[pl.ds(k0, tk), :] = jnp.where(member > 0.0, jnp.where(rank > need, -jnp.inf, jnp.inf), sv)
                seen = rank[tk - 1:tk, :]
            return seen

        lax.fori_loop(0, npair, body, jnp.zeros((1, tq), F32))

    m_ref[...] = jnp.full(m_ref.shape, NEG_BIG, F32)
    acc_ref[...] = jnp.zeros(acc_ref.shape, F32)

    def selection_bias(kb):
        return jnp.where(sc_ref[pl.ds(pl.multiple_of(kb * tk, tk), tk), :] > lo, 0.0, NEG_BIG)

    def score_head(kb, h, bias, park_ref, max_ref):
        kh = k_ref[pl.ds(pl.multiple_of(kb * tk, tk), tk), h * HEAD_PAD:(h + 1) * HEAD_PAD]
        s = jnp.dot(kh, qt_ref[h * HEAD_PAD:(h + 1) * HEAD_PAD, :], preferred_element_type=F32) + bias
        park_ref[h] = s
        max_ref[h:h + 1, :] = jnp.max(s, axis=0, keepdims=True)

    def softmax_head(kb, h, park_ref, max_ref):
        m_old = m_ref[h:h + 1, :]
        m_new = jnp.maximum(m_old, max_ref[h:h + 1, :])
        alpha = jnp.exp2(m_old - m_new)
        p = jnp.exp2((park_ref[h] - m_new).astype(BF16))
        rows = slice(h * V_ROWS, (h + 1) * V_ROWS)
        pv = jnp.dot(vt_ref[rows, pl.ds(pl.multiple_of(kb * tk, tk), tk)], p, preferred_element_type=F32)
        acc_ref[rows, :] = alpha * acc_ref[rows, :] + pv
        m_ref[h:h + 1, :] = m_new

    last_block = 2 * npair - 1
    bias0 = selection_bias(0)
    for h in range(H_A):
        score_head(0, h, bias0, s_ref, smax_ref)

    def att_body(kp, carry):
        odd, nxt = 2 * kp + 1, jnp.minimum(2 * kp + 2, last_block)
        bias = selection_bias(odd)
        for h in range(H_A):
            score_head(odd, h, bias, s2_ref, smax2_ref)
            softmax_head(2 * kp, h, s_ref, smax_ref)
        bias = selection_bias(nxt)
        for h in range(H_A):
            score_head(nxt, h, bias, s_ref, smax_ref)
            softmax_head(odd, h, s2_ref, smax2_ref)
        return carry

    lax.fori_loop(0, npair, att_body, 0)
    out = [acc_ref[h * V_ROWS:h * V_ROWS + D_V_A, :] / acc_ref[h * V_ROWS + D_V_A:h * V_ROWS + D_V_A + 1, :]
           for h in range(H_A)]
    o_ref[...] = jnp.concatenate(out, axis=0).T.astype(BF16)


def _dsa_call(qt, qit, wit, k, ki, vt):
    b, hq, s = qt.shape
    tq = TQ_DSA
    topk = min(TOPK_MAX, s // 4)
    colsT = lambda height: pl.BlockSpec((None, height, tq), lambda i, j: (i, 0, j))
    whole = lambda a: pl.BlockSpec((None,) + a.shape[1:], lambda i, j: (i, 0, 0))
    return pl.pallas_call(
        functools.partial(_dsa_kernel, topk=topk),
        out_shape=jax.ShapeDtypeStruct((b, s, WIDTH_A), BF16),
        grid=(b, s // tq),
        in_specs=[colsT(hq), colsT(H_IDX), colsT(hq), whole(ki), whole(k), whole(vt)],
        out_specs=pl.BlockSpec((None, tq, WIDTH_A), lambda i, j: (i, j, 0)),
        scratch_shapes=[pltpu.VMEM((s, tq), F32),
                        pltpu.VMEM((H_A, tq), F32),
                        pltpu.VMEM((H_A * V_ROWS, tq), F32),
                        pltpu.VMEM((H_A, TK_DSA, tq), F32),
                        pltpu.VMEM((H_A, TK_DSA, tq), F32),
                        pltpu.VMEM((H_A, tq), F32),
                        pltpu.VMEM((H_A, tq), F32)],
        compiler_params=_params(2),
        name="dsa",
    )(qit, wit, qt, ki, k, vt)


def _dil_kernel(q_ref, kc_ref, kp_ref, vc_ref, vp_ref, o_ref, lse_ref, s_ref):
    tb = kp_ref.shape[0]
    n_sub = q_ref.shape[0] // tb
    first_step = pl.program_id(2) == 0
    k = jnp.concatenate([kp_ref[...], kc_ref[...]], axis=0)
    v = jnp.concatenate([vp_ref[...], vc_ref[...]], axis=0)
    r = lax.broadcasted_iota(jnp.int32, (tb, 2 * tb), 0)
    c = lax.broadcasted_iota(jnp.int32, (tb, 2 * tb), 1)
    behind = (tb + r - c).astype(jnp.uint32)
    bias = jnp.where(behind <= tb, 0.0, -jnp.inf)
    no_prev = jnp.where(first_step, r, tb).astype(jnp.uint32)
    bias0 = jnp.where(behind <= no_prev, 0.0, -jnp.inf)
    lane = lax.broadcasted_iota(jnp.int32, (1, WIDTH_B), 1)
    head_qk = (lane % 128) // (D_HEAD_B // 2)
    head_v = lane // D_HEAD_B
    nt = (((1,), (1,)), ((), ()))
    for j in range(n_sub):
        q = q_ref[j * tb:(j + 1) * tb, :]
        for h in range(H_B):
            qh = jnp.where(head_qk == h, q, jnp.zeros_like(q))
            s_ref[j * H_B + h] = (lax.dot_general(qh, k[j * tb:(j + 2) * tb], nt, preferred_element_type=F32)
                                  + (bias0 if j == 0 else bias))
    for j in range(n_sub):
        vj = v[j * tb:(j + 2) * tb]
        num = jnp.zeros((tb, WIDTH_B), F32)
        den = jnp.zeros((tb, WIDTH_B), F32)
        top = jnp.zeros((tb, WIDTH_B), F32)
        for h in range(H_B):
            s = s_ref[j * H_B + h]
            m = jnp.max(s, axis=1, keepdims=True)
            p = jnp.exp2(s - m).astype(BF16)
            pv = jnp.dot(p, jnp.where(head_v == h, vj, jnp.ones_like(vj)), preferred_element_type=F32)
            mine = head_v == h
            num = jnp.where(mine, pv, num)
            den = jnp.where(mine, pltpu.roll(pv, WIDTH_B // 2, axis=1), den)
            top = jnp.where(mine, m, top)
        o_ref[j * tb:(j + 1) * tb, :] = (num / den).astype(o_ref.dtype)
        lse_ref[j * tb:(j + 1) * tb, :] = top + jnp.log2(den)


def _dil_call(qkv):
    b, dilation, n, _ = qkv.shape
    tb, tq = TB_DIL, min(TQ_DIL, n)
    per = tq // tb
    cur = lambda which: pl.BlockSpec((None, None, tq, WIDTH_B), lambda i, r, j: (i, r, j, which))
    prev = lambda which: pl.BlockSpec((None, None, tb, WIDTH_B),
                                      lambda i, r, j: (i, r, jnp.maximum(j * per - 1, 0), which))
    out_spec = pl.BlockSpec((None, None, tq, WIDTH_B), lambda i, r, j: (i, r, j, 0))
    out = lambda dtype: jax.ShapeDtypeStruct((b, dilation, n, WIDTH_B), dtype)
    return pl.pallas_call(
        _dil_kernel,
        out_shape=(out(BF16), out(F32)),
        grid=(b, dilation, n // tq),
        in_specs=[cur(0), cur(1), prev(1), cur(2), prev(2)],
        out_specs=(out_spec, out_spec),
        scratch_shapes=[pltpu.VMEM((per * H_B, tb, 2 * tb), F32)],
        compiler_params=_params(3),
        name="dil",
    )(qkv, qkv, qkv, qkv, qkv)


def _merge_ffn_kernel(x_ref, mod_ref, oa_ref, o0_ref, o1_ref, o2_ref, l0_ref, l1_ref, l2_ref, gates_ref,
                      wua_ref, wub_ref, wo_ref, gpost_ref, gpre2_ref, wg_ref, wu_ref, wd_ref, gpost2_ref,
                      out_ref, acc_ref, *scr):
    tm = x_ref.shape[0]
    tmp = scr[4]

    def token_major(ref, buf):
        dilation = ref.shape[0]
        if dilation == 1:
            return ref[0]
        n_slab = ref.shape[2] // LANES
        for c in range(n_slab):
            if dilation <= 4:
                for r in range(dilation):
                    buf[c, pl.ds(r, tm // dilation, stride=dilation), :] = (
                        ref[r, :, c * LANES:(c + 1) * LANES].astype(F32))
            else:
                quarter = tm // 4
                for r in range(dilation):
                    tmp[c, pl.ds((r % 4) * quarter + r // 4, tm // dilation, stride=dilation // 4), :] = (
                        ref[r, :, c * LANES:(c + 1) * LANES].astype(F32))
                for r4 in range(4):
                    buf[c, pl.ds(r4, quarter, stride=4), :] = tmp[c, r4 * quarter:(r4 + 1) * quarter, :]
        return jnp.concatenate([buf[c] for c in range(n_slab)], axis=1)

    o0, l0 = token_major(o0_ref, None).astype(F32), token_major(l0_ref, None)
    o1, l1 = token_major(o1_ref, scr[0]), token_major(l1_ref, scr[1])
    o2, l2 = token_major(o2_ref, scr[2]), token_major(l2_ref, scr[3])
    m = jnp.maximum(jnp.maximum(l0, l1), l2)
    e0, e1, e2 = jnp.exp2(l0 - m), jnp.exp2(l1 - m), jnp.exp2(l2 - m)
    ob = (e0 * o0 + e1 * o1 + e2 * o2) / (e0 + e1 + e2)
    za = jnp.dot(oa_ref[...], wua_ref[...], preferred_element_type=F32)
    zb = jnp.dot(ob.astype(BF16), wub_ref[...], preferred_element_type=F32)
    d = za.shape[1]
    z = gates_ref[:, 0:d].astype(F32) * za + gates_ref[:, d:2 * d].astype(F32) * zb
    y = jnp.dot(z.astype(BF16), wo_ref[...], preferred_element_type=F32)
    x_mixed = x_ref[...] + mod_ref[5:6, :] * (_rms_rows(y) * gpost_ref[...])
    _ffn_tile(x_mixed, mod_ref, gpre2_ref, wg_ref, wu_ref, wd_ref, gpost2_ref, out_ref, acc_ref, 6)


def _merge_ffn_call(x, mod, oa, dil, gates, w_up_a, w_up_b, w_o, g_post, g_pre2, w_gate, w_up, w_down, g_post2):
    b, s, d = x.shape
    tm = TM_FFN
    rows = lambda width: pl.BlockSpec((None, tm, width), lambda i, j: (i, j, 0))
    const = lambda a: pl.BlockSpec(a.shape, lambda i, j: (0,) * a.ndim, pipeline_mode=pl.Buffered(1))
    (o0, l0), (o1, l1), (o2, l2) = dil
    res = lambda a: pl.BlockSpec((None, a.shape[1], tm // a.shape[1], WIDTH_B), lambda i, j: (i, 0, j, 0))
    consts = [w_up_a, w_up_b, w_o, g_post, g_pre2, w_gate, w_up, w_down, g_post2]
    return pl.pallas_call(
        _merge_ffn_kernel,
        out_shape=jax.ShapeDtypeStruct((b, s, d), F32),
        grid=(b, s // tm),
        in_specs=[rows(d), pl.BlockSpec((None, N_MOD, d), lambda i, j: (i, 0, 0)), rows(WIDTH_A)]
                 + [res(a) for a in (o0, o1, o2, l0, l1, l2)]
                 + [rows(2 * d)] + [const(a) for a in consts],
        out_specs=rows(d),
        scratch_shapes=[pltpu.VMEM((tm, d), F32)] + [pltpu.VMEM((WIDTH_B // LANES, tm, LANES), F32)] * 5,
        compiler_params=_params(2),
        name="merge_ffn",
    )(x, mod, oa, o0, o1, o2, l0, l1, l2, gates, *consts)


def _pad_heads(w_t):
    h, r, n = w_t.shape
    return jnp.pad(w_t, ((0, 0), (0, HEAD_PAD - r), (0, 0))).reshape(h * HEAD_PAD, n)


def _mixer_weights(w_in, g_cq, g_ckv, w_uq, w_uk, w_uv, w_iq):
    d = w_in.shape[0]
    o_cq, o_ckv, o_kr, o_ki, o_wi = 0, Q_LORA, Q_LORA + KV_LORA, Q_LORA + KV_LORA + D_ROPE_A, \
        Q_LORA + KV_LORA + D_ROPE_A + D_IDX
    o_qkv = o_wi + H_IDX
    o_gates = o_qkv + N_QKV_B

    def swap_halves(w):
        half = w.shape[1] // 2
        return jnp.concatenate([w[:, half:], w[:, :half]], axis=1)

    w_kr, w_ki = w_in[:, o_kr:o_kr + D_ROPE_A], w_in[:, o_ki:o_ki + D_IDX]
    qkv = w_in[:, o_qkv:o_gates].reshape(d, 3, N_GROUPS_B, H_B, 2, D_HEAD_B // 2)
    qk_split = qkv[:, 0:2].transpose(0, 1, 2, 4, 3, 5).reshape(d, 2 * N_GROUPS_B * WIDTH_B)
    v_cols = qkv[:, 2].reshape(d, N_GROUPS_B * WIDTH_B)
    w_std = jnp.concatenate([w_in[:, o_ckv:o_ckv + KV_LORA], w_kr, swap_halves(w_kr), w_ki, swap_halves(w_ki),
                             qk_split, v_cols, w_in[:, o_gates:]], axis=1)
    w_tr = jnp.concatenate([w_in[:, o_cq:o_cq + Q_LORA], w_in[:, o_ckv:o_ckv + KV_LORA],
                            w_in[:, o_wi:o_wi + H_IDX]], axis=1).T
    w_tr = jnp.pad(w_tr, ((0, N_TR - w_tr.shape[0]), (0, 0)))

    e_kr = np.zeros((D_ROPE_A, H_A * HEAD_PAD), np.float32)
    for h in range(H_A):
        e_kr[np.arange(D_ROPE_A), h * HEAD_PAD + D_NOPE + np.arange(D_ROPE_A)] = 1.0
    e_ki = np.zeros((D_IDX, LANES), np.float32)
    e_ki[np.arange(D_IDX), np.arange(D_IDX)] = 1.0
    return {
        "w_std": w_std.astype(BF16),
        "w_tr": w_tr.astype(BF16),
        "g_cq_col": g_cq.reshape(Q_LORA, 1),
        "g_ckv_col": g_ckv.reshape(KV_LORA, 1),
        "g_ckv_row": g_ckv.reshape(1, KV_LORA),
        "w_uq_t": _pad_heads(w_uq.transpose(1, 2, 0)).astype(BF16),
        "w_iq_t": _pad_heads(w_iq.transpose(1, 2, 0)).astype(BF16),
        "w_uk_p": jnp.pad(w_uk, ((0, 0), (0, 0), (0, HEAD_PAD - D_NOPE))).reshape(KV_LORA, H_A * HEAD_PAD).astype(BF16),
        "w_uv_t": w_uv.transpose(1, 2, 0).reshape(WIDTH_A, KV_LORA).astype(BF16),
        "e_kr": jnp.asarray(e_kr, BF16),
        "e_ki": jnp.asarray(e_ki, BF16),
    }


def kernel(x, c, positions, w_mod, b_mod, g_pre_ffn1, w_gate1, w_up1, w_down1, g_post_ffn1, g_pre_mix, w_in, g_cq, g_ckv, w_uq, w_uk, w_uv, w_iq, w_up_a, w_up_b, w_o, g_post_mix, g_pre_ffn2, w_gate2, w_up2, w_down2, g_post_ffn2):
    b, s, d = x.shape
    assert s % (16 * TB_DIL) == 0 and s % TM_FFN == 0 and b <= 8
    c8 = jnp.pad(c, ((0, 8 - b), (0, 0)))
    half_b, half_a = D_HEAD_B // 2, D_ROPE_A // 2
    inv = jnp.concatenate([ROPE_THETA ** (-jnp.arange(half_b, dtype=F32) / half_b),
                           ROPE_THETA ** (-jnp.arange(half_a, dtype=F32) / half_a)]).reshape(48, 1)
    tabs = _rope_call(positions.reshape(b, 1, s), inv)
    row = lambda g: g.reshape(1, -1)
    for l in range(w_mod.shape[0]):
        mod = _mod_call(c8, w_mod[l], b_mod[l].reshape(1, -1))[:b].reshape(b, N_MOD, d)
        x = _ffn_call(x, mod, row(g_pre_ffn1[l]), w_gate1[l].astype(BF16), w_up1[l].astype(BF16),
                      w_down1[l].astype(BF16), row(g_post_ffn1[l]), 0)
        w = _mixer_weights(w_in[l], g_cq[l], g_ckv[l], w_uq[l], w_uk[l], w_uv[l], w_iq[l])
        qt, qit, wit, k, ki, vt, dq0, dq1, dq2, gates = _proj_call(x, mod, row(g_pre_mix[l]), w, tabs)
        oa = _dsa_call(qt, qit, wit, k, ki, vt)
        dil = [_dil_call(dq) for dq in (dq0, dq1, dq2)]
        x = _merge_ffn_call(x, mod, oa, dil, gates, w_up_a[l].astype(BF16), w_up_b[l].astype(BF16),
                            w_o[l].astype(BF16), row(g_post_mix[l]), row(g_pre_ffn2[l]), w_gate2[l].astype(BF16),
                            w_up2[l].astype(BF16), w_down2[l].astype(BF16), row(g_post_ffn2[l]))
    return x
```

```python
import functools

import numpy as np
import jax
import jax.numpy as jnp
from jax import lax
from jax.experimental import pallas as pl
from jax.experimental.pallas import tpu as pltpu

F32 = jnp.float32
BF16 = jnp.bfloat16

D_FF = 2816
ROPE_THETA = 10000.0
NORM_EPS = 1e-6
H_A = 8
Q_LORA = 384
KV_LORA = 256
D_NOPE = 64
D_ROPE_A = 32
D_V_A = 64
H_IDX = 8
D_IDX = 32
TOPK_MAX = 256
DIL_PAIRS = ((128, 1), (512, 4), (2048, 16))
N_GROUPS_B = 3
H_B = 4
D_HEAD_B = 64
N_MOD = 9
WIDTH_A = H_A * D_V_A
WIDTH_B = H_B * D_HEAD_B
N_QKV_B = 3 * N_GROUPS_B * WIDTH_B

LANES = 128
HEAD_PAD = 128
VMEM_LIMIT = 56 * 1024 * 1024

TM_FFN = 512
CH_FFN = 256
TP_PROJ = 512
TQ_DSA = 512
TK_DSA = 256
V_ROWS = D_V_A + 16
TB_DIL = 128
TQ_DIL = 512
BISECT_MAX = 34
BISECT_CHECK = 2
FOLD_ROWS = 16
COUNT_ROWS = 64
IDX_ROWS = 64
NEG_BIG = -1e30
LOG2_E = 1.4426950408889634

C_CKV = 0
C_KR = 256
C_KR_SW = 288
C_KI = 320
C_KI_SW = 352
C_QKVB = 384
C_GATES = C_QKVB + N_QKV_B
N_STD = C_GATES + 2048
R_CQ = 0
R_CKV = Q_LORA
R_WI = Q_LORA + KV_LORA
N_TR = 656


def _params(n_axes):
    return pltpu.CompilerParams(dimension_semantics=("arbitrary",) * n_axes,
                                vmem_limit_bytes=VMEM_LIMIT)


def _rms_rows(x):
    return x * lax.rsqrt(jnp.mean(x * x, axis=-1, keepdims=True) + NORM_EPS)


def _mod_kernel(c_ref, w_ref, b_ref, o_ref):
    c = c_ref[...]
    o_ref[...] = jnp.dot(c * jax.nn.sigmoid(c), w_ref[...], preferred_element_type=F32) + b_ref[...]


def _mod_call(c8, w_mod, b_mod):
    d = c8.shape[1]
    return pl.pallas_call(
        _mod_kernel,
        out_shape=jax.ShapeDtypeStruct((8, N_MOD * d), F32),
        grid=(N_MOD,),
        in_specs=[pl.BlockSpec((8, d), lambda j: (0, 0)),
                  pl.BlockSpec((d, d), lambda j: (0, j)),
                  pl.BlockSpec((1, d), lambda j: (0, j))],
        out_specs=pl.BlockSpec((8, d), lambda j: (0, j)),
        compiler_params=_params(1),
        name="mod",
    )(c8, w_mod, b_mod)


def _rope_kernel(pos_ref, inv_ref, tt_ref, ca_ref, sa_ref, cb_ref, sb_ref):
    s = pos_ref.shape[1]
    ang = inv_ref[...] * pos_ref[...].astype(F32)
    cos = jnp.cos(ang)
    sin = jnp.sin(ang)
    cos_b, sin_b = cos[32:48], sin[32:48]
    tt_ref[0] = cos_b
    tt_ref[1] = sin_b
    cos_a = jnp.concatenate([cos[0:32]] * 4, axis=0)
    sin_a = jnp.concatenate([sin[0:32]] * 4, axis=0)
    cos_bt = jnp.concatenate([cos_b] * 8, axis=0)
    sin_bt = jnp.concatenate([-sin_b, sin_b] * 4, axis=0)
    for j in range(s // LANES):
        cols = slice(j * LANES, (j + 1) * LANES)
        ca_ref[cols, :] = cos_a[:, cols].T
        sa_ref[cols, :] = sin_a[:, cols].T
        cb_ref[cols, :] = cos_bt[:, cols].T
        sb_ref[cols, :] = sin_bt[:, cols].T


def _rope_call(pos3, inv):
    b, _, s = pos3.shape
    tab = jax.ShapeDtypeStruct((b, s, LANES), F32)
    tab_spec = pl.BlockSpec((None, s, LANES), lambda i: (i, 0, 0))
    return pl.pallas_call(
        _rope_kernel,
        out_shape=(jax.ShapeDtypeStruct((b, 2, 16, s), F32), tab, tab, tab, tab),
        grid=(b,),
        in_specs=[pl.BlockSpec((None, 1, s), lambda i: (i, 0, 0)),
                  pl.BlockSpec((48, 1), lambda i: (0, 0))],
        out_specs=(pl.BlockSpec((None, 2, 16, s), lambda i: (i, 0, 0, 0)),
                   tab_spec, tab_spec, tab_spec, tab_spec),
        compiler_params=_params(1),
        name="rope",
    )(pos3, inv)


def _ffn_kernel(x_ref, mod_ref, gpre_ref, wg_ref, wu_ref, wd_ref, gpost_ref, o_ref, acc_ref, *, j0):
    _ffn_tile(x_ref[...], mod_ref, gpre_ref, wg_ref, wu_ref, wd_ref, gpost_ref, o_ref, acc_ref, j0)


def _ffn_tile(x, mod_ref, gpre_ref, wg_ref, wu_ref, wd_ref, gpost_ref, o_ref, acc_ref, j0):
    sh, sc, gt = mod_ref[j0:j0 + 1, :], mod_ref[j0 + 1:j0 + 2, :], mod_ref[j0 + 2:j0 + 3, :]
    h = (_rms_rows(x) * gpre_ref[...] * (1.0 + sc) + sh).astype(BF16)
    n_ch = wg_ref.shape[1] // CH_FFN
    for ch in range(n_ch):
        cols = slice(ch * CH_FFN, (ch + 1) * CH_FFN)
        g = jnp.dot(h, wg_ref[:, cols], preferred_element_type=F32)
        u = jnp.dot(h, wu_ref[:, cols], preferred_element_type=F32)
        a = (g * jax.nn.sigmoid(g) * u).astype(BF16)
        part = jnp.dot(a, wd_ref[cols, :], preferred_element_type=F32)
        if ch == 0:
            acc_ref[...] = part
        else:
            acc_ref[...] += part
    o_ref[...] = x + 0.5 * gt * (_rms_rows(acc_ref[...]) * gpost_ref[...])


def _ffn_call(x, mod, g_pre, w_gate, w_up, w_down, g_post, j0):
    b, s, d = x.shape
    f = w_gate.shape[1]
    const = lambda shape: pl.BlockSpec(shape, lambda i, j: (0,) * len(shape), pipeline_mode=pl.Buffered(1))
    tile = pl.BlockSpec((None, TM_FFN, d), lambda i, j: (i, j, 0))
    return pl.pallas_call(
        functools.partial(_ffn_kernel, j0=j0),
        out_shape=jax.ShapeDtypeStruct((b, s, d), F32),
        grid=(b, s // TM_FFN),
        in_specs=[tile,
                  pl.BlockSpec((None, N_MOD, d), lambda i, j: (i, 0, 0)),
                  const((1, d)), const((d, f)), const((d, f)), const((f, d)), const((1, d))],
        out_specs=tile,
        scratch_shapes=[pltpu.VMEM((TM_FFN, d), F32)],
        compiler_params=_params(2),
        name="ffn",
    )(x, mod, g_pre, w_gate, w_up, w_down, g_post)


def _proj_kernel(x_ref, mod_ref, gpre_ref, wstd_ref, wtr_ref, gcq_ref, gckv_col_ref, gckv_row_ref,
                 wuq_ref, wiq_ref, wuk_ref, wuv_ref,
                 tt_ref, ca_ref, sa_ref, cb_ref, sb_ref,
                 qt_ref, qit_ref, wit_ref, k_ref, ki_ref, vt_ref, dq0_ref, dq1_ref, dq2_ref, gates_ref,
                 dil_scr, dil_tmp):
    x = x_ref[...]
    sh, sc = mod_ref[3:4, :], mod_ref[4:5, :]
    u = (_rms_rows(x) * gpre_ref[...] * (1.0 + sc) + sh).astype(BF16)
    std = jnp.dot(u, wstd_ref[...], preferred_element_type=F32)
    tr = lax.dot_general(wtr_ref[...], u, (((1,), (1,)), ((), ())),
                         preferred_element_type=F32)

    cq = tr[R_CQ:R_CQ + Q_LORA]
    cq = cq * lax.rsqrt(jnp.mean(cq * cq, axis=0, keepdims=True) + NORM_EPS) * gcq_ref[...]
    cq = cq.astype(BF16)
    ckv_t = tr[R_CKV:R_CKV + KV_LORA]
    ckv_t = ckv_t * lax.rsqrt(jnp.mean(ckv_t * ckv_t, axis=0, keepdims=True) + NORM_EPS) * gckv_col_ref[...]
    cos_t, sin_t = tt_ref[0], tt_ref[1]

    attn_scale = (D_NOPE + D_ROPE_A) ** -0.5 * LOG2_E
    q_t = jnp.dot(wuq_ref[...], cq, preferred_element_type=F32) * attn_scale
    idx_scale = D_IDX ** -0.5
    qi_t = jnp.dot(wiq_ref[...], cq, preferred_element_type=F32) * idx_scale
    for h in range(H_A):
        r0 = h * HEAD_PAD
        x1, x2 = q_t[r0 + 64:r0 + 80], q_t[r0 + 80:r0 + 96]
        qt_ref[r0:r0 + 64, :] = q_t[r0:r0 + 64].astype(BF16)
        qt_ref[r0 + 64:r0 + 80, :] = (x1 * cos_t - x2 * sin_t).astype(BF16)
        qt_ref[r0 + 80:r0 + 96, :] = (x2 * cos_t + x1 * sin_t).astype(BF16)
        qt_ref[r0 + 96:r0 + 128, :] = q_t[r0 + 96:r0 + 128].astype(BF16)
        i0 = h * D_IDX
        y1, y2 = qi_t[i0:i0 + 16], qi_t[i0 + 16:i0 + 32]
        qit_ref[i0:i0 + 16, :] = (y1 * cos_t - y2 * sin_t).astype(BF16)
        qit_ref[i0 + 16:i0 + 32, :] = (y2 * cos_t + y1 * sin_t).astype(BF16)
    wit_ref[...] = tr[R_WI:R_WI + H_IDX] * (H_IDX ** -0.5)
    v_t = jnp.dot(wuv_ref[...], ckv_t.astype(BF16), preferred_element_type=F32).astype(BF16)
    for h in range(H_A):
        vt_ref[h * V_ROWS:h * V_ROWS + D_V_A, :] = v_t[h * D_V_A:(h + 1) * D_V_A]
        vt_ref[h * V_ROWS + D_V_A:(h + 1) * V_ROWS, :] = jnp.ones((V_ROWS - D_V_A, v_t.shape[1]), BF16)

    ckv = std[:, C_CKV:C_CKV + KV_LORA]
    ckv = (_rms_rows(ckv) * gckv_row_ref[...]).astype(BF16)
    k_nope = jnp.dot(ckv, wuk_ref[...], preferred_element_type=F32)
    t = std[:, C_KR:C_KR + LANES]
    lane = lax.broadcasted_iota(jnp.int32, (1, LANES), 1)
    unswapped = jnp.where((lane // 32) % 2 == 0, 1.0, 0.0)
    roped = (t * cb_ref[...] + pltpu.roll(t, LANES - 32, axis=1) * sb_ref[...]) * unswapped
    moved = pltpu.roll(roped, 64, axis=1)
    k_rope_tile = jnp.where(lane >= 64, moved, 0.0)
    for h in range(H_A):
        k_ref[:, h * HEAD_PAD:(h + 1) * HEAD_PAD] = (k_nope[:, h * HEAD_PAD:(h + 1) * HEAD_PAD]
                                                      + k_rope_tile).astype(BF16)
    ki_ref[...] = jnp.where(lane < 32, moved, 0.0).astype(BF16)

    cos_a, sin_a = ca_ref[...], sa_ref[...]
    tp = x.shape[0]
    for g, (out_ref, (_, dilation)) in enumerate(zip((dq0_ref, dq1_ref, dq2_ref), DIL_PAIRS)):
        for which in range(2):
            scale = (D_HEAD_B ** -0.5 * LOG2_E) if which == 0 else 1.0
            c0 = C_QKVB + (which * N_GROUPS_B + g) * WIDTH_B
            x1, x2 = std[:, c0:c0 + 128], std[:, c0 + 128:c0 + 256]
            dil_scr[2 * which] = (x1 * cos_a - x2 * sin_a) * scale
            dil_scr[2 * which + 1] = (x2 * cos_a + x1 * sin_a) * scale
        c0 = C_QKVB + (2 * N_GROUPS_B + g) * WIDTH_B
        dil_scr[4] = std[:, c0:c0 + 128]
        dil_scr[5] = std[:, c0 + 128:c0 + 256]
        for c in range(3 * WIDTH_B // LANES):
            if dilation <= 4:
                for r in range(dilation):
                    out_ref[r, :, c * LANES:(c + 1) * LANES] = (
                        dil_scr[c, pl.ds(r, tp // dilation, stride=dilation), :].astype(BF16))
            else:
                quarter = tp // 4
                for r4 in range(4):
                    dil_tmp[c, r4 * quarter:(r4 + 1) * quarter, :] = dil_scr[c, pl.ds(r4, quarter, stride=4), :]
                for r in range(dilation):
                    out_ref[r, :, c * LANES:(c + 1) * LANES] = dil_tmp[
                        c, pl.ds((r % 4) * quarter + r // 4, tp // dilation, stride=dilation // 4), :].astype(BF16)
    gates_ref[...] = jax.nn.sigmoid(std[:, C_GATES:]).astype(BF16)


def _proj_call(x, mod, g_pre, w, tabs):
    b, s, d = x.shape
    tt, ca, sa, cb, sb = tabs
    tp = TP_PROJ
    const = lambda a: pl.BlockSpec(a.shape, lambda i, j: (0,) * a.ndim)
    rows = lambda width: pl.BlockSpec((None, tp, width), lambda i, j: (i, j, 0))
    colsT = lambda height: pl.BlockSpec((None, height, tp), lambda i, j: (i, 0, j))
    consts = [g_pre, w["w_std"], w["w_tr"], w["g_cq_col"], w["g_ckv_col"], w["g_ckv_row"],
              w["w_uq_t"], w["w_iq_t"], w["w_uk_p"], w["w_uv_t"]]
    hq = H_A * HEAD_PAD
    dil_shapes = [jax.ShapeDtypeStruct((b, dil, s // dil, 3 * WIDTH_B), BF16) for _, dil in DIL_PAIRS]
    dil_specs = [pl.BlockSpec((None, dil, tp // dil, 3 * WIDTH_B), lambda i, j: (i, 0, j, 0))
                 for _, dil in DIL_PAIRS]
    return pl.pallas_call(
        _proj_kernel,
        out_shape=(jax.ShapeDtypeStruct((b, hq, s), BF16),
                   jax.ShapeDtypeStruct((b, H_IDX * D_IDX, s), BF16),
                   jax.ShapeDtypeStruct((b, H_IDX, s), F32),
                   jax.ShapeDtypeStruct((b, s, hq), BF16),
                   jax.ShapeDtypeStruct((b, s, LANES), BF16),
                   jax.ShapeDtypeStruct((b, H_A * V_ROWS, s), BF16),
                   *dil_shapes,
                   jax.ShapeDtypeStruct((b, s, 2048), BF16)),
        grid=(b, s // tp),
        in_specs=[rows(d), pl.BlockSpec((None, N_MOD, d), lambda i, j: (i, 0, 0))]
                 + [const(a) for a in consts]
                 + [pl.BlockSpec((None, 2, 16, tp), lambda i, j: (i, 0, 0, j)),
                    rows(LANES), rows(LANES), rows(LANES), rows(LANES)],
        out_specs=(colsT(hq), colsT(H_IDX * D_IDX), colsT(H_IDX), rows(hq), rows(LANES), colsT(H_A * V_ROWS),
                   *dil_specs, rows(2048)),
        scratch_shapes=[pltpu.VMEM((3 * WIDTH_B // LANES, tp, LANES), F32)] * 2,
        compiler_params=_params(2),
        name="proj",
    )(x, mod, *consts, tt, ca, sa, cb, sb)


def _dsa_kernel(qit_ref, wit_ref, qt_ref, ki_ref, k_ref, vt_ref, o_ref,
                sc_ref, m_ref, acc_ref, s_ref, s2_ref, smax_ref, smax2_ref, qpad_ref, *, topk):
    tq = qt_ref.shape[1]
    tk = s_ref.shape[1]
    i = pl.program_id(1)
    nkb = (i + 1) * (tq // tk)
    npair = (nkb + 1) // 2
    qpos = i * tq + lax.broadcasted_iota(jnp.int32, (1, tq), 1)

    def fold(v):
        return jnp.sum(v.reshape(v.shape[0] // FOLD_ROWS, FOLD_ROWS, tq), axis=0)

    qpad_ref[...] = jnp.zeros(qpad_ref.shape, BF16)
    for h in range(H_IDX):
        qpad_ref[h * HEAD_PAD:h * HEAD_PAD + D_IDX, :] = qit_ref[h * D_IDX:(h + 1) * D_IDX, :]

    def park_logits(kb, park_ref, heads=range(H_IDX)):
        ki = ki_ref[pl.ds(pl.multiple_of(kb * tk, tk), tk), :]
        for h in heads:
            park_ref[h] = jnp.dot(ki, qpad_ref[h * HEAD_PAD:(h + 1) * HEAD_PAD, :], preferred_element_type=F32)

    def rows8(v, op):
        return op(v.reshape(IDX_ROWS // 8, 8, LANES), axis=0)

    piece_iota = lax.broadcasted_iota(jnp.int32, (IDX_ROWS, LANES), 0)
    n_col = tq // LANES
    heads_per_col = H_IDX // n_col

    def reduce_block(kb, park_ref, stats, on_diagonal, ahead=None):
        k0 = pl.multiple_of(kb * tk, tk)
        columns = []
        for c in range(n_col):
            if ahead is not None:
                park_logits(ahead[0], ahead[1], range(c * heads_per_col, (c + 1) * heads_per_col))
            lanes = slice(c * LANES, (c + 1) * LANES)
            smax, smin, c_pos, c_nn = [v[:, lanes] for v in stats]
            w = [wit_ref[h:h + 1, lanes] for h in range(H_IDX)]
            for j in range(tk // IDX_ROWS):
                r0 = j * IDX_ROWS
                acc = jnp.zeros((IDX_ROWS, LANES), F32)
                for h in range(H_IDX):
                    acc = acc + w[h] * jnp.maximum(park_ref[h, r0:r0 + IDX_ROWS, lanes], 0.0)
                if on_diagonal:
                    causal = (k0 + r0 + piece_iota) <= qpos[:, lanes]
                    sv = jnp.where(causal, acc, -jnp.inf)
                    smin = jnp.minimum(smin, rows8(jnp.where(causal, acc, jnp.inf), jnp.min))
                else:
                    sv = acc
                    smin = jnp.minimum(smin, rows8(acc, jnp.min))
                sc_ref[pl.ds(k0 + r0, IDX_ROWS), lanes] = sv
                smax = jnp.maximum(smax, rows8(sv, jnp.max))
                c_pos = c_pos + rows8(jnp.where(sv > 0.0, 1.0, 0.0), jnp.sum)
                c_nn = c_nn + rows8(jnp.where(sv >= 0.0, 1.0, 0.0), jnp.sum)
            columns.append((smax, smin, c_pos, c_nn))
        return tuple(jnp.concatenate(parts, axis=1) for parts in zip(*columns))

    def idx_pair(kp, stats):
        stats = reduce_block(2 * kp, s_ref, stats, False, ahead=(2 * kp + 1, s2_ref))
        return reduce_block(2 * kp + 1, s2_ref, stats, False, ahead=(2 * kp + 2, s_ref))

    park_logits(0, s_ref)
    zero8 = jnp.zeros((8, tq), F32)
    stats = lax.fori_loop(0, i, idx_pair,
                          (jnp.full((8, tq), -jnp.inf, F32), jnp.full((8, tq), jnp.inf, F32), zero8, zero8))
    stats = reduce_block(nkb - 2, s_ref, stats, True, ahead=(nkb - 1, s2_ref))
    stats = reduce_block(nkb - 1, s2_ref, stats, True)
    smax = jnp.max(stats[0], axis=0, keepdims=True)
    smin = jnp.min(stats[1], axis=0, keepdims=True)
    c_pos, c_nn = [jnp.sum(v, axis=0, keepdims=True) for v in stats[2:]]

    @pl.when(nkb % 2 == 1)
    def _pad_block():
        sc_ref[pl.ds(pl.multiple_of(nkb * tk, tk), tk), :] = jnp.full((tk, tq), -jnp.inf, F32)

    n_causal = (qpos + 1).astype(F32)
    kf = jnp.minimum(qpos + 1, topk).astype(F32)

    def count_gt(thr):
        def body(kp, c):
            k0 = pl.multiple_of(kp * (2 * tk), 2 * tk)
            for j in range(2 * tk // COUNT_ROWS):
                rows = pl.ds(k0 + j * COUNT_ROWS, COUNT_ROWS)
                c = c + fold(jnp.where(sc_ref[rows, :] > thr, 1.0, 0.0))
            return c
        part = lax.fori_loop(0, npair, body, jnp.zeros((FOLD_ROWS, tq), F32))
        return jnp.sum(part, axis=0, keepdims=True)

    below = smin - (1.0 + jnp.abs(smin))
    zero_tie = jnp.logical_and(c_pos < kf, kf <= c_nn)
    positive = c_pos >= kf
    searching = jnp.where(zero_tie, 0.0, 1.0)

    def bisect_cond(carry):
        return jnp.logical_and(carry[0] < BISECT_MAX, carry[1] > 0)

    def bisect_body(carry):
        it, _, lo, hi, c_lo, c_hi = carry
        for _ in range(BISECT_CHECK):
            mid = 0.5 * (lo + hi)
            c_mid = count_gt(mid)
            ge = c_mid >= kf
            up = jnp.logical_and(ge, searching > 0.0)
            down = jnp.logical_and(jnp.logical_not(ge), searching > 0.0)
            lo, c_lo = jnp.where(up, mid, lo), jnp.where(up, c_mid, c_lo)
            hi, c_hi = jnp.where(down, mid, hi), jnp.where(down, c_mid, c_hi)
        open_queries = (jnp.max((c_lo - kf) * searching) > 0.0).astype(jnp.int32)
        return it + BISECT_CHECK, open_queries, lo, hi, c_lo, c_hi

    lo0 = jnp.where(positive, 0.0, jnp.where(zero_tie, 0.0, below))
    hi0 = jnp.where(positive, smax, 0.0)
    c_lo0 = jnp.where(positive, c_pos, jnp.where(zero_tie, c_nn, n_causal))
    c_hi0 = jnp.where(positive, 0.0, c_pos)
    first_open = (jnp.max((c_lo0 - kf) * searching) > 0.0).astype(jnp.int32)
    _, _, lo, hi, c_lo, c_hi = lax.while_loop(
        bisect_cond, bisect_body, (jnp.int32(0), first_open, lo0, hi0, c_lo0, c_hi0))

    @pl.when(jnp.max(jnp.maximum(c_lo - kf, 1.0 - searching)) > 0.0)
    def _break_ties():
        need = kf - c_hi
        closed = 1.0 - searching
        prefix = jnp.where(lax.broadcasted_iota(jnp.int32, (tk, tk), 0)
                           >= lax.broadcasted_iota(jnp.int32, (tk, tk), 1), 1.0, 0.0).astype(BF16)

        def body(kp, seen):
            blocks = []
            for j in range(2):
                k0 = pl.multiple_of((2 * kp + j) * tk, tk)
                sv = sc_ref[pl.ds(k0, tk), :]
                above_lo = jnp.where(sv > lo, 1.0, jnp.where(sv >= lo, closed, 0.0))
                member = jnp.where(sv > hi, 0.0, above_lo)
                blocks.append((k0, member, jnp.dot(prefix, member.astype(BF16), preferred_element_type=F32)))
            for k0, member, within in blocks:
                rank = within + seen
                sv = sc_ref[pl.ds(k0, tk), :]
                sc_ref[pl.ds(k0, tk), :] = jnp.where(member > 0.0, jnp.where(rank > need, -jnp.inf, jnp.inf), sv)
                seen = rank[tk - 1:tk, :]
            return seen

        lax.fori_loop(0, npair, body, jnp.zeros((1, tq), F32))

    m_ref[...] = jnp.full(m_ref.shape, NEG_BIG, F32)
    acc_ref[...] = jnp.zeros(acc_ref.shape, F32)

    def selection_bias(kb):
        return jnp.where(sc_ref[pl.ds(pl.multiple_of(kb * tk, tk), tk), :] > lo, 0.0, NEG_BIG)

    def score_head(kb, h, bias, park_ref, max_ref):
        kh = k_ref[pl.ds(pl.multiple_of(kb * tk, tk), tk), h * HEAD_PAD:(h + 1) * HEAD_PAD]
        s = jnp.dot(kh, qt_ref[h * HEAD_PAD:(h + 1) * HEAD_PAD, :], preferred_element_type=F32) + bias
        park_ref[h] = s
        max_ref[h:h + 1, :] = jnp.max(s, axis=0, keepdims=True)

    def softmax_head(kb, h, park_ref, max_ref):
        m_old = m_ref[h:h + 1, :]
        m_new = jnp.maximum(m_old, max_ref[h:h + 1, :])
        alpha = jnp.exp2(m_old - m_new)
        p = jnp.exp2((park_ref[h] - m_new).astype(BF16))
        rows = slice(h * V_ROWS, (h + 1) * V_ROWS)
        pv = jnp.dot(vt_ref[rows, pl.ds(pl.multiple_of(kb * tk, tk), tk)], p, preferred_element_type=F32)
        acc_ref[rows, :] = alpha * acc_ref[rows, :] + pv
        m_ref[h:h + 1, :] = m_new

    last_block = 2 * npair - 1
    bias0 = selection_bias(0)
    for h in range(H_A):
        score_head(0, h, bias0, s_ref, smax_ref)

    def att_body(kp, carry):
        odd, nxt = 2 * kp + 1, jnp.minimum(2 * kp + 2, last_block)
        bias = selection_bias(odd)
        for h in range(H_A):
            score_head(odd, h, bias, s2_ref, smax2_ref)
            softmax_head(2 * kp, h, s_ref, smax_ref)
        bias = selection_bias(nxt)
        for h in range(H_A):
            score_head(nxt, h, bias, s_ref, smax_ref)
            softmax_head(odd, h, s2_ref, smax2_ref)
        return carry

    lax.fori_loop(0, npair, att_body, 0)
    out = [acc_ref[h * V_ROWS:h * V_ROWS + D_V_A, :] / acc_ref[h * V_ROWS + D_V_A:h * V_ROWS + D_V_A + 1, :]
           for h in range(H_A)]
    o_ref[...] = jnp.concatenate(out, axis=0).T.astype(BF16)


def _dsa_call(qt, qit, wit, k, ki, vt):
    b, hq, s = qt.shape
    tq = TQ_DSA
    topk = min(TOPK_MAX, s // 4)
    colsT = lambda height: pl.BlockSpec((None, height, tq), lambda i, j: (i, 0, j))
    whole = lambda a: pl.BlockSpec((None,) + a.shape[1:], lambda i, j: (i, 0, 0))
    return pl.pallas_call(
        functools.partial(_dsa_kernel, topk=topk),
        out_shape=jax.ShapeDtypeStruct((b, s, WIDTH_A), BF16),
        grid=(b, s // tq),
        in_specs=[colsT(H_IDX * D_IDX), colsT(H_IDX), colsT(hq), whole(ki), whole(k), whole(vt)],
        out_specs=pl.BlockSpec((None, tq, WIDTH_A), lambda i, j: (i, j, 0)),
        scratch_shapes=[pltpu.VMEM((s, tq), F32),
                        pltpu.VMEM((H_A, tq), F32),
                        pltpu.VMEM((H_A * V_ROWS, tq), F32),
                        pltpu.VMEM((H_A, TK_DSA, tq), F32),
                        pltpu.VMEM((H_A, TK_DSA, tq), F32),
                        pltpu.VMEM((H_A, tq), F32),
                        pltpu.VMEM((H_A, tq), F32),
                        pltpu.VMEM((H_IDX * HEAD_PAD, tq), BF16)],
        compiler_params=_params(2),
        name="dsa",
    )(qit, wit, qt, ki, k, vt)


def _dil_kernel(q_ref, kc_ref, kp_ref, vc_ref, vp_ref, o_ref, lse_ref, s_ref):
    tb = kp_ref.shape[0]
    n_sub = q_ref.shape[0] // tb
    first_step = pl.program_id(2) == 0
    k = jnp.concatenate([kp_ref[...], kc_ref[...]], axis=0)
    v = jnp.concatenate([vp_ref[...], vc_ref[...]], axis=0)
    r = lax.broadcasted_iota(jnp.int32, (tb, 2 * tb), 0)
    c = lax.broadcasted_iota(jnp.int32, (tb, 2 * tb), 1)
    behind = (tb + r - c).astype(jnp.uint32)
    bias = jnp.where(behind <= tb, 0.0, -jnp.inf)
    no_prev = jnp.where(first_step, r, tb).astype(jnp.uint32)
    bias0 = jnp.where(behind <= no_prev, 0.0, -jnp.inf)
    lane = lax.broadcasted_iota(jnp.int32, (1, WIDTH_B), 1)
    head_qk = (lane % 128) // (D_HEAD_B // 2)
    head_v = lane // D_HEAD_B
    nt = (((1,), (1,)), ((), ()))
    for j in range(n_sub):
        q = q_ref[j * tb:(j + 1) * tb, :]
        for h in range(H_B):
            qh = jnp.where(head_qk == h, q, jnp.zeros_like(q))
            s_ref[j * H_B + h] = (lax.dot_general(qh, k[j * tb:(j + 2) * tb], nt, preferred_element_type=F32)
                                  + (bias0 if j == 0 else bias))
    for j in range(n_sub):
        vj = v[j * tb:(j + 2) * tb]
        num = jnp.zeros((tb, WIDTH_B), F32)
        den = jnp.zeros((tb, WIDTH_B), F32)
        top = jnp.zeros((tb, WIDTH_B), F32)
        for h in range(H_B):
            s = s_ref[j * H_B + h]
            m = jnp.max(s, axis=1, keepdims=True)
            p = jnp.exp2(s - m).astype(BF16)
            pv = jnp.dot(p, jnp.where(head_v == h, vj, jnp.ones_like(vj)), preferred_element_type=F32)
            mine = head_v == h
            num = jnp.where(mine, pv, num)
            den = jnp.where(mine, pltpu.roll(pv, WIDTH_B // 2, axis=1), den)
            top = jnp.where(mine, m, top)
        o_ref[j * tb:(j + 1) * tb, :] = (num / den).astype(o_ref.dtype)
        lse_ref[j * tb:(j + 1) * tb, :] = top + jnp.log2(den)


def _dil_call(qkv):
    b, dilation, n, _ = qkv.shape
    tb, tq = TB_DIL, min(TQ_DIL, n)
    per = tq // tb
    cur = lambda which: pl.BlockSpec((None, None, tq, WIDTH_B), lambda i, r, j: (i, r, j, which))
    prev = lambda which: pl.BlockSpec((None, None, tb, WIDTH_B),
                                      lambda i, r, j: (i, r, jnp.maximum(j * per - 1, 0), which))
    out_spec = pl.BlockSpec((None, None, tq, WIDTH_B), lambda i, r, j: (i, r, j, 0))
    out = lambda dtype: jax.ShapeDtypeStruct((b, dilation, n, WIDTH_B), dtype)
    return pl.pallas_call(
        _dil_kernel,
        out_shape=(out(BF16), out(F32)),
        grid=(b, dilation, n // tq),
        in_specs=[cur(0), cur(1), prev(1), cur(2), prev(2)],
        out_specs=(out_spec, out_spec),
        scratch_shapes=[pltpu.VMEM((per * H_B, tb, 2 * tb), F32)],
        compiler_params=_params(3),
        name="dil",
    )(qkv, qkv, qkv, qkv, qkv)


def _merge_ffn_kernel(x_ref, mod_ref, oa_ref, o0_ref, o1_ref, o2_ref, l0_ref, l1_ref, l2_ref, gates_ref,
                      wua_ref, wub_ref, wo_ref, gpost_ref, gpre2_ref, wg_ref, wu_ref, wd_ref, gpost2_ref,
                      out_ref, acc_ref, *scr):
    tm = x_ref.shape[0]
    tmp = scr[4]

    def token_major(ref, buf):
        dilation = ref.shape[0]
        if dilation == 1:
            return ref[0]
        n_slab = ref.shape[2] // LANES
        for c in range(n_slab):
            if dilation <= 4:
                for r in range(dilation):
                    buf[c, pl.ds(r, tm // dilation, stride=dilation), :] = (
                        ref[r, :, c * LANES:(c + 1) * LANES].astype(F32))
            else:
                quarter = tm // 4
                for r in range(dilation):
                    tmp[c, pl.ds((r % 4) * quarter + r // 4, tm // dilation, stride=dilation // 4), :] = (
                        ref[r, :, c * LANES:(c + 1) * LANES].astype(F32))
                for r4 in range(4):
                    buf[c, pl.ds(r4, quarter, stride=4), :] = tmp[c, r4 * quarter:(r4 + 1) * quarter, :]
        return jnp.concatenate([buf[c] for c in range(n_slab)], axis=1)

    o0, l0 = token_major(o0_ref, None).astype(F32), token_major(l0_ref, None)
    o1, l1 = token_major(o1_ref, scr[0]), token_major(l1_ref, scr[1])
    o2, l2 = token_major(o2_ref, scr[2]), token_major(l2_ref, scr[3])
    m = jnp.maximum(jnp.maximum(l0, l1), l2)
    e0, e1, e2 = jnp.exp2(l0 - m), jnp.exp2(l1 - m), jnp.exp2(l2 - m)
    ob = (e0 * o0 + e1 * o1 + e2 * o2) / (e0 + e1 + e2)
    za = jnp.dot(oa_ref[...], wua_ref[...], preferred_element_type=F32)
    zb = jnp.dot(ob.astype(BF16), wub_ref[...], preferred_element_type=F32)
    d = za.shape[1]
    z = gates_ref[:, 0:d].astype(F32) * za + gates_ref[:, d:2 * d].astype(F32) * zb
    y = jnp.dot(z.astype(BF16), wo_ref[...], preferred_element_type=F32)
    x_mixed = x_ref[...] + mod_ref[5:6, :] * (_rms_rows(y) * gpost_ref[...])
    _ffn_tile(x_mixed, mod_ref, gpre2_ref, wg_ref, wu_ref, wd_ref, gpost2_ref, out_ref, acc_ref, 6)


def _merge_ffn_call(x, mod, oa, dil, gates, w_up_a, w_up_b, w_o, g_post, g_pre2, w_gate, w_up, w_down, g_post2):
    b, s, d = x.shape
    tm = TM_FFN
    rows = lambda width: pl.BlockSpec((None, tm, width), lambda i, j: (i, j, 0))
    const = lambda a: pl.BlockSpec(a.shape, lambda i, j: (0,) * a.ndim, pipeline_mode=pl.Buffered(1))
    (o0, l0), (o1, l1), (o2, l2) = dil
    res = lambda a: pl.BlockSpec((None, a.shape[1], tm // a.shape[1], WIDTH_B), lambda i, j: (i, 0, j, 0))
    consts = [w_up_a, w_up_b, w_o, g_post, g_pre2, w_gate, w_up, w_down, g_post2]
    return pl.pallas_call(
        _merge_ffn_kernel,
        out_shape=jax.ShapeDtypeStruct((b, s, d), F32),
        grid=(b, s // tm),
        in_specs=[rows(d), pl.BlockSpec((None, N_MOD, d), lambda i, j: (i, 0, 0)), rows(WIDTH_A)]
                 + [res(a) for a in (o0, o1, o2, l0, l1, l2)]
                 + [rows(2 * d)] + [const(a) for a in consts],
        out_specs=rows(d),
        scratch_shapes=[pltpu.VMEM((tm, d), F32)] + [pltpu.VMEM((WIDTH_B // LANES, tm, LANES), F32)] * 5,
        compiler_params=_params(2),
        name="merge_ffn",
    )(x, mod, oa, o0, o1, o2, l0, l1, l2, gates, *consts)


def _pad_heads(w_t):
    h, r, n = w_t.shape
    return jnp.pad(w_t, ((0, 0), (0, HEAD_PAD - r), (0, 0))).reshape(h * HEAD_PAD, n)


def _mixer_weights(w_in, g_cq, g_ckv, w_uq, w_uk, w_uv, w_iq):
    d = w_in.shape[0]
    o_cq, o_ckv, o_kr, o_ki, o_wi = 0, Q_LORA, Q_LORA + KV_LORA, Q_LORA + KV_LORA + D_ROPE_A, \
        Q_LORA + KV_LORA + D_ROPE_A + D_IDX
    o_qkv = o_wi + H_IDX
    o_gates = o_qkv + N_QKV_B

    def swap_halves(w):
        half = w.shape[1] // 2
        return jnp.concatenate([w[:, half:], w[:, :half]], axis=1)

    w_kr, w_ki = w_in[:, o_kr:o_kr + D_ROPE_A], w_in[:, o_ki:o_ki + D_IDX]
    qkv = w_in[:, o_qkv:o_gates].reshape(d, 3, N_GROUPS_B, H_B, 2, D_HEAD_B // 2)
    qk_split = qkv[:, 0:2].transpose(0, 1, 2, 4, 3, 5).reshape(d, 2 * N_GROUPS_B * WIDTH_B)
    v_cols = qkv[:, 2].reshape(d, N_GROUPS_B * WIDTH_B)
    w_std = jnp.concatenate([w_in[:, o_ckv:o_ckv + KV_LORA], w_kr, swap_halves(w_kr), w_ki, swap_halves(w_ki),
                             qk_split, v_cols, w_in[:, o_gates:]], axis=1)
    w_tr = jnp.concatenate([w_in[:, o_cq:o_cq + Q_LORA], w_in[:, o_ckv:o_ckv + KV_LORA],
                            w_in[:, o_wi:o_wi + H_IDX]], axis=1).T
    w_tr = jnp.pad(w_tr, ((0, N_TR - w_tr.shape[0]), (0, 0)))

    return {
        "w_std": w_std.astype(BF16),
        "w_tr": w_tr.astype(BF16),
        "g_cq_col": g_cq.reshape(Q_LORA, 1),
        "g_ckv_col": g_ckv.reshape(KV_LORA, 1),
        "g_ckv_row": g_ckv.reshape(1, KV_LORA),
        "w_uq_t": _pad_heads(w_uq.transpose(1, 2, 0)).astype(BF16),
        "w_iq_t": w_iq.transpose(1, 2, 0).reshape(H_IDX * D_IDX, Q_LORA).astype(BF16),
        "w_uk_p": jnp.pad(w_uk, ((0, 0), (0, 0), (0, HEAD_PAD - D_NOPE))).reshape(KV_LORA, H_A * HEAD_PAD).astype(BF16),
        "w_uv_t": w_uv.transpose(1, 2, 0).reshape(WIDTH_A, KV_LORA).astype(BF16),
    }


def kernel(x, c, positions, w_mod, b_mod, g_pre_ffn1, w_gate1, w_up1, w_down1, g_post_ffn1, g_pre_mix, w_in, g_cq, g_ckv, w_uq, w_uk, w_uv, w_iq, w_up_a, w_up_b, w_o, g_post_mix, g_pre_ffn2, w_gate2, w_up2, w_down2, g_post_ffn2):
    b, s, d = x.shape
    assert s % (16 * TB_DIL) == 0 and s % TM_FFN == 0 and b <= 8
    c8 = jnp.pad(c, ((0, 8 - b), (0, 0)))
    half_b, half_a = D_HEAD_B // 2, D_ROPE_A // 2
    inv = jnp.concatenate([ROPE_THETA ** (-jnp.arange(half_b, dtype=F32) / half_b),
                           ROPE_THETA ** (-jnp.arange(half_a, dtype=F32) / half_a)]).reshape(48, 1)
    tabs = _rope_call(positions.reshape(b, 1, s), inv)
    row = lambda g: g.reshape(1, -1)
    for l in range(w_mod.shape[0]):
        mod = _mod_call(c8, w_mod[l], b_mod[l].reshape(1, -1))[:b].reshape(b, N_MOD, d)
        x = _ffn_call(x, mod, row(g_pre_ffn1[l]), w_gate1[l].astype(BF16), w_up1[l].astype(BF16),
                      w_down1[l].astype(BF16), row(g_post_ffn1[l]), 0)
        w = _mixer_weights(w_in[l], g_cq[l], g_ckv[l], w_uq[l], w_uk[l], w_uv[l], w_iq[l])
        qt, qit, wit, k, ki, vt, dq0, dq1, dq2, gates = _proj_call(x, mod, row(g_pre_mix[l]), w, tabs)
        oa = _dsa_call(qt, qit, wit, k, ki, vt)
        dil = [_dil_call(dq) for dq in (dq0, dq1, dq2)]
        x = _merge_ffn_call(x, mod, oa, dil, gates, w_up_a[l].astype(BF16), w_up_b[l].astype(BF16),
                            w_o[l].astype(BF16), row(g_post_mix[l]), row(g_pre_ffn2[l]), w_gate2[l].astype(BF16),
                            w_up2[l].astype(BF16), w_down2[l].astype(BF16), row(g_post_ffn2[l]))
    return x
```

```python
import functools

import numpy as np
import jax
import jax.numpy as jnp
from jax import lax
from jax.experimental import pallas as pl
from jax.experimental.pallas import tpu as pltpu

F32 = jnp.float32
BF16 = jnp.bfloat16

D_FF = 2816
ROPE_THETA = 10000.0
NORM_EPS = 1e-6
H_A = 8
Q_LORA = 384
KV_LORA = 256
D_NOPE = 64
D_ROPE_A = 32
D_V_A = 64
H_IDX = 8
D_IDX = 32
TOPK_MAX = 256
DIL_PAIRS = ((128, 1), (512, 4), (2048, 16))
N_GROUPS_B = 3
H_B = 4
D_HEAD_B = 64
N_MOD = 9
WIDTH_A = H_A * D_V_A
WIDTH_B = H_B * D_HEAD_B
N_QKV_B = 3 * N_GROUPS_B * WIDTH_B

LANES = 128
HEAD_PAD = 128
VMEM_LIMIT = 56 * 1024 * 1024

TM_FFN = 512
CH_FFN = 256
TP_PROJ = 512
TQ_DSA = 512
TK_DSA = 256
V_ROWS = D_V_A + 16
TB_DIL = 128
TQ_DIL = 512
BISECT_MAX = 34
BISECT_CHECK = 2
FOLD_ROWS = 16
COUNT_ROWS = 64
IDX_ROWS = 64
NEG_BIG = -1e30
LOG2_E = 1.4426950408889634

C_CKV = 0
C_KR = 256
C_KR_SW = 288
C_KI = 320
C_KI_SW = 352
C_QKVB = 384
C_GATES = C_QKVB + N_QKV_B
N_STD = C_GATES + 2048
R_CQ = 0
R_CKV = Q_LORA
R_WI = Q_LORA + KV_LORA
N_TR = 656


def _params(n_axes):
    return pltpu.CompilerParams(dimension_semantics=("arbitrary",) * n_axes,
                                vmem_limit_bytes=VMEM_LIMIT)


def _rms_rows(x):
    return x * lax.rsqrt(jnp.mean(x * x, axis=-1, keepdims=True) + NORM_EPS)


def _mod_kernel(c_ref, w_ref, b_ref, o_ref):
    c = c_ref[...]
    o_ref[...] = jnp.dot(c * jax.nn.sigmoid(c), w_ref[...], preferred_element_type=F32) + b_ref[...]


def _mod_call(c8, w_mod, b_mod):
    d = c8.shape[1]
    return pl.pallas_call(
        _mod_kernel,
        out_shape=jax.ShapeDtypeStruct((8, N_MOD * d), F32),
        grid=(N_MOD,),
        in_specs=[pl.BlockSpec((8, d), lambda j: (0, 0)),
                  pl.BlockSpec((d, d), lambda j: (0, j)),
                  pl.BlockSpec((1, d), lambda j: (0, j))],
        out_specs=pl.BlockSpec((8, d), lambda j: (0, j)),
        compiler_params=_params(1),
        name="mod",
    )(c8, w_mod, b_mod)


def _rope_kernel(pos_ref, inv_ref, tt_ref, ca_ref, sa_ref, cb_ref, sb_ref):
    s = pos_ref.shape[1]
    ang = inv_ref[...] * pos_ref[...].astype(F32)
    cos = jnp.cos(ang)
    sin = jnp.sin(ang)
    cos_b, sin_b = cos[32:48], sin[32:48]
    tt_ref[0] = cos_b
    tt_ref[1] = sin_b
    cos_a = jnp.concatenate([cos[0:32]] * 4, axis=0)
    sin_a = jnp.concatenate([sin[0:32]] * 4, axis=0)
    cos_bt = jnp.concatenate([cos_b] * 8, axis=0)
    sin_bt = jnp.concatenate([-sin_b, sin_b] * 4, axis=0)
    for j in range(s // LANES):
        cols = slice(j * LANES, (j + 1) * LANES)
        ca_ref[cols, :] = cos_a[:, cols].T
        sa_ref[cols, :] = sin_a[:, cols].T
        cb_ref[cols, :] = cos_bt[:, cols].T
        sb_ref[cols, :] = sin_bt[:, cols].T


def _rope_call(pos3, inv):
    b, _, s = pos3.shape
    tab = jax.ShapeDtypeStruct((b, s, LANES), F32)
    tab_spec = pl.BlockSpec((None, s, LANES), lambda i: (i, 0, 0))
    return pl.pallas_call(
        _rope_kernel,
        out_shape=(jax.ShapeDtypeStruct((b, 2, 16, s), F32), tab, tab, tab, tab),
        grid=(b,),
        in_specs=[pl.BlockSpec((None, 1, s), lambda i: (i, 0, 0)),
                  pl.BlockSpec((48, 1), lambda i: (0, 0))],
        out_specs=(pl.BlockSpec((None, 2, 16, s), lambda i: (i, 0, 0, 0)),
                   tab_spec, tab_spec, tab_spec, tab_spec),
        compiler_params=_params(1),
        name="rope",
    )(pos3, inv)


def _ffn_kernel(x_ref, mod_ref, gpre_ref, wg_ref, wu_ref, wd_ref, gpost_ref, o_ref, acc_ref, *, j0):
    _ffn_tile(x_ref[...], mod_ref, gpre_ref, wg_ref, wu_ref, wd_ref, gpost_ref, o_ref, acc_ref, j0)


def _ffn_tile(x, mod_ref, gpre_ref, wg_ref, wu_ref, wd_ref, gpost_ref, o_ref, acc_ref, j0):
    sh, sc, gt = mod_ref[j0:j0 + 1, :], mod_ref[j0 + 1:j0 + 2, :], mod_ref[j0 + 2:j0 + 3, :]
    h = (_rms_rows(x) * gpre_ref[...] * (1.0 + sc) + sh).astype(BF16)
    n_ch = wg_ref.shape[1] // CH_FFN
    for ch in range(n_ch):
        cols = slice(ch * CH_FFN, (ch + 1) * CH_FFN)
        g = jnp.dot(h, wg_ref[:, cols], preferred_element_type=F32)
        u = jnp.dot(h, wu_ref[:, cols], preferred_element_type=F32)
        a = (g * jax.nn.sigmoid(g) * u).astype(BF16)
        part = jnp.dot(a, wd_ref[cols, :], preferred_element_type=F32)
        if ch == 0:
            acc_ref[...] = part
        else:
            acc_ref[...] += part
    o_ref[...] = x + 0.5 * gt * (_rms_rows(acc_ref[...]) * gpost_ref[...])


def _ffn_call(x, mod, g_pre, w_gate, w_up, w_down, g_post, j0):
    b, s, d = x.shape
    f = w_gate.shape[1]
    const = lambda shape: pl.BlockSpec(shape, lambda i, j: (0,) * len(shape), pipeline_mode=pl.Buffered(1))
    tile = pl.BlockSpec((None, TM_FFN, d), lambda i, j: (i, j, 0))
    return pl.pallas_call(
        functools.partial(_ffn_kernel, j0=j0),
        out_shape=jax.ShapeDtypeStruct((b, s, d), F32),
        grid=(b, s // TM_FFN),
        in_specs=[tile,
                  pl.BlockSpec((None, N_MOD, d), lambda i, j: (i, 0, 0)),
                  const((1, d)), const((d, f)), const((d, f)), const((f, d)), const((1, d))],
        out_specs=tile,
        scratch_shapes=[pltpu.VMEM((TM_FFN, d), F32)],
        compiler_params=_params(2),
        name="ffn",
    )(x, mod, g_pre, w_gate, w_up, w_down, g_post)


def _proj_kernel(x_ref, mod_ref, gpre_ref, wstd_ref, wtr_ref, gcq_ref, gckv_col_ref, gckv_row_ref,
                 wuq_ref, wiq_ref, wuk_ref, wuv_ref,
                 tt_ref, ca_ref, sa_ref, cb_ref, sb_ref,
                 qt_ref, qit_ref, wit_ref, k_ref, ki_ref, vt_ref, dq0_ref, dq1_ref, dq2_ref, gates_ref,
                 dil_scr, dil_tmp):
    x = x_ref[...]
    sh, sc = mod_ref[3:4, :], mod_ref[4:5, :]
    u = (_rms_rows(x) * gpre_ref[...] * (1.0 + sc) + sh).astype(BF16)
    std = jnp.dot(u, wstd_ref[...], preferred_element_type=F32)
    tr = lax.dot_general(wtr_ref[...], u, (((1,), (1,)), ((), ())),
                         preferred_element_type=F32)

    cq = tr[R_CQ:R_CQ + Q_LORA]
    cq = cq * lax.rsqrt(jnp.mean(cq * cq, axis=0, keepdims=True) + NORM_EPS) * gcq_ref[...]
    cq = cq.astype(BF16)
    ckv_t = tr[R_CKV:R_CKV + KV_LORA]
    ckv_t = ckv_t * lax.rsqrt(jnp.mean(ckv_t * ckv_t, axis=0, keepdims=True) + NORM_EPS) * gckv_col_ref[...]
    cos_t, sin_t = tt_ref[0], tt_ref[1]

    attn_scale = (D_NOPE + D_ROPE_A) ** -0.5 * LOG2_E
    q_t = jnp.dot(wuq_ref[...], cq, preferred_element_type=F32) * attn_scale
    idx_scale = D_IDX ** -0.5
    qi_t = jnp.dot(wiq_ref[...], cq, preferred_element_type=F32) * idx_scale
    for h in range(H_A):
        r0 = h * HEAD_PAD
        x1, x2 = q_t[r0 + 64:r0 + 80], q_t[r0 + 80:r0 + 96]
        qt_ref[r0:r0 + 64, :] = q_t[r0:r0 + 64].astype(BF16)
        qt_ref[r0 + 64:r0 + 80, :] = (x1 * cos_t - x2 * sin_t).astype(BF16)
        qt_ref[r0 + 80:r0 + 96, :] = (x2 * cos_t + x1 * sin_t).astype(BF16)
        qt_ref[r0 + 96:r0 + 128, :] = q_t[r0 + 96:r0 + 128].astype(BF16)
        i0 = h * D_IDX
        y1, y2 = qi_t[i0:i0 + 16], qi_t[i0 + 16:i0 + 32]
        qit_ref[i0:i0 + 16, :] = (y1 * cos_t - y2 * sin_t).astype(BF16)
        qit_ref[i0 + 16:i0 + 32, :] = (y2 * cos_t + y1 * sin_t).astype(BF16)
    wit_ref[...] = tr[R_WI:R_WI + H_IDX] * (H_IDX ** -0.5)
    v_t = jnp.dot(wuv_ref[...], ckv_t.astype(BF16), preferred_element_type=F32).astype(BF16)
    for h in range(H_A):
        vt_ref[h * V_ROWS:h * V_ROWS + D_V_A, :] = v_t[h * D_V_A:(h + 1) * D_V_A]
        vt_ref[h * V_ROWS + D_V_A:(h + 1) * V_ROWS, :] = jnp.ones((V_ROWS - D_V_A, v_t.shape[1]), BF16)

    ckv = std[:, C_CKV:C_CKV + KV_LORA]
    ckv = (_rms_rows(ckv) * gckv_row_ref[...]).astype(BF16)
    k_nope = jnp.dot(ckv, wuk_ref[...], preferred_element_type=F32)
    t = std[:, C_KR:C_KR + LANES]
    lane = lax.broadcasted_iota(jnp.int32, (1, LANES), 1)
    unswapped = jnp.where((lane // 32) % 2 == 0, 1.0, 0.0)
    roped = (t * cb_ref[...] + pltpu.roll(t, LANES - 32, axis=1) * sb_ref[...]) * unswapped
    moved = pltpu.roll(roped, 64, axis=1)
    k_rope_tile = jnp.where(lane >= 64, moved, 0.0)
    for h in range(H_A):
        k_ref[:, h * HEAD_PAD:(h + 1) * HEAD_PAD] = (k_nope[:, h * HEAD_PAD:(h + 1) * HEAD_PAD]
                                                      + k_rope_tile).astype(BF16)
    ki_ref[...] = jnp.where(lane < 32, moved, 0.0).astype(BF16)

    cos_a, sin_a = ca_ref[...], sa_ref[...]
    tp = x.shape[0]
    for g, (out_ref, (_, dilation)) in enumerate(zip((dq0_ref, dq1_ref, dq2_ref), DIL_PAIRS)):
        for which in range(2):
            scale = (D_HEAD_B ** -0.5 * LOG2_E) if which == 0 else 1.0
            c0 = C_QKVB + (which * N_GROUPS_B + g) * WIDTH_B
            x1, x2 = std[:, c0:c0 + 128], std[:, c0 + 128:c0 + 256]
            dil_scr[2 * which] = (x1 * cos_a - x2 * sin_a) * scale
            dil_scr[2 * which + 1] = (x2 * cos_a + x1 * sin_a) * scale
        c0 = C_QKVB + (2 * N_GROUPS_B + g) * WIDTH_B
        dil_scr[4] = std[:, c0:c0 + 128]
        dil_scr[5] = std[:, c0 + 128:c0 + 256]
        for c in range(3 * WIDTH_B // LANES):
            if dilation <= 4:
                for r in range(dilation):
                    out_ref[r, :, c * LANES:(c + 1) * LANES] = (
                        dil_scr[c, pl.ds(r, tp // dilation, stride=dilation), :].astype(BF16))
            else:
                quarter = tp // 4
                for r4 in range(4):
                    dil_tmp[c, r4 * quarter:(r4 + 1) * quarter, :] = dil_scr[c, pl.ds(r4, quarter, stride=4), :]
                for r in range(dilation):
                    out_ref[r, :, c * LANES:(c + 1) * LANES] = dil_tmp[
                        c, pl.ds((r % 4) * quarter + r // 4, tp // dilation, stride=dilation // 4), :].astype(BF16)
    gates_ref[...] = jax.nn.sigmoid(std[:, C_GATES:]).astype(BF16)


def _proj_call(x, mod, g_pre, w, tabs):
    b, s, d = x.shape
    tt, ca, sa, cb, sb = tabs
    tp = TP_PROJ
    const = lambda a: pl.BlockSpec(a.shape, lambda i, j: (0,) * a.ndim)
    rows = lambda width: pl.BlockSpec((None, tp, width), lambda i, j: (i, j, 0))
    colsT = lambda height: pl.BlockSpec((None, height, tp), lambda i, j: (i, 0, j))
    consts = [g_pre, w["w_std"], w["w_tr"], w["g_cq_col"], w["g_ckv_col"], w["g_ckv_row"],
              w["w_uq_t"], w["w_iq_t"], w["w_uk_p"], w["w_uv_t"]]
    hq = H_A * HEAD_PAD
    dil_shapes = [jax.ShapeDtypeStruct((b, dil, s // dil, 3 * WIDTH_B), BF16) for _, dil in DIL_PAIRS]
    dil_specs = [pl.BlockSpec((None, dil, tp // dil, 3 * WIDTH_B), lambda i, j: (i, 0, j, 0))
                 for _, dil in DIL_PAIRS]
    return pl.pallas_call(
        _proj_kernel,
        out_shape=(jax.ShapeDtypeStruct((b, hq, s), BF16),
                   jax.ShapeDtypeStruct((b, H_IDX * D_IDX, s), BF16),
                   jax.ShapeDtypeStruct((b, H_IDX, s), F32),
                   jax.ShapeDtypeStruct((b, s, hq), BF16),
                   jax.ShapeDtypeStruct((b, s, LANES), BF16),
                   jax.ShapeDtypeStruct((b, H_A * V_ROWS, s), BF16),
                   *dil_shapes,
                   jax.ShapeDtypeStruct((b, s, 2048), BF16)),
        grid=(b, s // tp),
        in_specs=[rows(d), pl.BlockSpec((None, N_MOD, d), lambda i, j: (i, 0, 0))]
                 + [const(a) for a in consts]
                 + [pl.BlockSpec((None, 2, 16, tp), lambda i, j: (i, 0, 0, j)),
                    rows(LANES), rows(LANES), rows(LANES), rows(LANES)],
        out_specs=(colsT(hq), colsT(H_IDX * D_IDX), colsT(H_IDX), rows(hq), rows(LANES), colsT(H_A * V_ROWS),
                   *dil_specs, rows(2048)),
        scratch_shapes=[pltpu.VMEM((3 * WIDTH_B // LANES, tp, LANES), F32)] * 2,
        compiler_params=_params(2),
        name="proj",
    )(x, mod, *consts, tt, ca, sa, cb, sb)


def _dsa_kernel(qit_ref, wit_ref, qt_ref, ki_ref, k_ref, vt_ref, o_ref,
                sc_ref, m_ref, acc_ref, s_ref, s2_ref, smax_ref, smax2_ref, qpad_ref, *, topk):
    tq = qt_ref.shape[1]
    tk = s_ref.shape[1]
    i = pl.program_id(1)
    nkb = (i + 1) * (tq // tk)
    npair = (nkb + 1) // 2
    qpos = i * tq + lax.broadcasted_iota(jnp.int32, (1, tq), 1)

    def fold(v):
        return jnp.sum(v.reshape(v.shape[0] // FOLD_ROWS, FOLD_ROWS, tq), axis=0)

    qpad_ref[...] = jnp.zeros(qpad_ref.shape, BF16)
    for h in range(H_IDX):
        qpad_ref[h * HEAD_PAD:h * HEAD_PAD + D_IDX, :] = qit_ref[h * D_IDX:(h + 1) * D_IDX, :]

    def park_logits(kb, park_ref, heads=range(H_IDX)):
        ki = ki_ref[pl.ds(pl.multiple_of(kb * tk, tk), tk), :]
        for h in heads:
            park_ref[h] = jnp.dot(ki, qpad_ref[h * HEAD_PAD:(h + 1) * HEAD_PAD, :], preferred_element_type=F32)

    def rows8(v, op):
        return op(v.reshape(IDX_ROWS // 8, 8, LANES), axis=0)

    piece_iota = lax.broadcasted_iota(jnp.int32, (IDX_ROWS, LANES), 0)
    n_col = tq // LANES
    heads_per_col = H_IDX // n_col

    def reduce_block(kb, park_ref, stats, on_diagonal, ahead=None):
        k0 = pl.multiple_of(kb * tk, tk)
        columns = []
        for c in range(n_col):
            if ahead is not None:
                park_logits(ahead[0], ahead[1], range(c * heads_per_col, (c + 1) * heads_per_col))
            lanes = slice(c * LANES, (c + 1) * LANES)
            smax, smin, c_pos, c_nn = [v[:, lanes] for v in stats]
            w = [wit_ref[h:h + 1, lanes] for h in range(H_IDX)]
            for j in range(tk // IDX_ROWS):
                r0 = j * IDX_ROWS
                acc = jnp.zeros((IDX_ROWS, LANES), F32)
                for h in range(H_IDX):
                    acc = acc + w[h] * jnp.maximum(park_ref[h, r0:r0 + IDX_ROWS, lanes], 0.0)
                if on_diagonal:
                    causal = (k0 + r0 + piece_iota) <= qpos[:, lanes]
                    sv = jnp.where(causal, acc, -jnp.inf)
                    smin = jnp.minimum(smin, rows8(jnp.where(causal, acc, jnp.inf), jnp.min))
                else:
                    sv = acc
                    smin = jnp.minimum(smin, rows8(acc, jnp.min))
                sc_ref[pl.ds(k0 + r0, IDX_ROWS), lanes] = sv
                smax = jnp.maximum(smax, rows8(sv, jnp.max))
                c_pos = c_pos + rows8(jnp.where(sv > 0.0, 1.0, 0.0), jnp.sum)
                c_nn = c_nn + rows8(jnp.where(sv >= 0.0, 1.0, 0.0), jnp.sum)
            columns.append((smax, smin, c_pos, c_nn))
        return tuple(jnp.concatenate(parts, axis=1) for parts in zip(*columns))

    def idx_pair(kp, stats):
        stats = reduce_block(2 * kp, s_ref, stats, False, ahead=(2 * kp + 1, s2_ref))
        return reduce_block(2 * kp + 1, s2_ref, stats, False, ahead=(2 * kp + 2, s_ref))

    park_logits(0, s_ref)
    zero8 = jnp.zeros((8, tq), F32)
    stats = lax.fori_loop(0, i, idx_pair,
                          (jnp.full((8, tq), -jnp.inf, F32), jnp.full((8, tq), jnp.inf, F32), zero8, zero8))
    stats = reduce_block(nkb - 2, s_ref, stats, True, ahead=(nkb - 1, s2_ref))
    stats = reduce_block(nkb - 1, s2_ref, stats, True)
    smax = jnp.max(stats[0], axis=0, keepdims=True)
    smin = jnp.min(stats[1], axis=0, keepdims=True)
    c_pos, c_nn = [jnp.sum(v, axis=0, keepdims=True) for v in stats[2:]]

    @pl.when(nkb % 2 == 1)
    def _pad_block():
        sc_ref[pl.ds(pl.multiple_of(nkb * tk, tk), tk), :] = jnp.full((tk, tq), -jnp.inf, F32)

    n_causal = (qpos + 1).astype(F32)
    kf = jnp.minimum(qpos + 1, topk).astype(F32)

    def count_gt(thr):
        def body(kp, c):
            k0 = pl.multiple_of(kp * (2 * tk), 2 * tk)
            for j in range(2 * tk // COUNT_ROWS):
                rows = pl.ds(k0 + j * COUNT_ROWS, COUNT_ROWS)
                c = c + fold(jnp.where(sc_ref[rows, :] > thr, 1.0, 0.0))
            return c
        part = lax.fori_loop(0, npair, body, jnp.zeros((FOLD_ROWS, tq), F32))
        return jnp.sum(part, axis=0, keepdims=True)

    below = smin - (1.0 + jnp.abs(smin))
    zero_tie = jnp.logical_and(c_pos < kf, kf <= c_nn)
    positive = c_pos >= kf
    searching = jnp.where(zero_tie, 0.0, 1.0)

    def bisect_cond(carry):
        return jnp.logical_and(carry[0] < BISECT_MAX, carry[1] > 0)

    def bisect_body(carry):
        it, _, lo, hi, c_lo, c_hi = carry
        for _ in range(BISECT_CHECK):
            mid = 0.5 * (lo + hi)
            c_mid = count_gt(mid)
            ge = c_mid >= kf
            up = jnp.logical_and(ge, searching > 0.0)
            down = jnp.logical_and(jnp.logical_not(ge), searching > 0.0)
            lo, c_lo = jnp.where(up, mid, lo), jnp.where(up, c_mid, c_lo)
            hi, c_hi = jnp.where(down, mid, hi), jnp.where(down, c_mid, c_hi)
        open_queries = (jnp.max((c_lo - kf) * searching) > 0.0).astype(jnp.int32)
        return it + BISECT_CHECK, open_queries, lo, hi, c_lo, c_hi

    lo0 = jnp.where(positive, 0.0, jnp.where(zero_tie, 0.0, below))
    hi0 = jnp.where(positive, smax, 0.0)
    c_lo0 = jnp.where(positive, c_pos, jnp.where(zero_tie, c_nn, n_causal))
    c_hi0 = jnp.where(positive, 0.0, c_pos)
    first_open = (jnp.max((c_lo0 - kf) * searching) > 0.0).astype(jnp.int32)
    _, _, lo, hi, c_lo, c_hi = lax.while_loop(
        bisect_cond, bisect_body, (jnp.int32(0), first_open, lo0, hi0, c_lo0, c_hi0))

    @pl.when(jnp.max(jnp.maximum(c_lo - kf, 1.0 - searching)) > 0.0)
    def _break_ties():
        need = kf - c_hi
        closed = 1.0 - searching
        prefix = jnp.where(lax.broadcasted_iota(jnp.int32, (tk, tk), 0)
                           >= lax.broadcasted_iota(jnp.int32, (tk, tk), 1), 1.0, 0.0).astype(BF16)

        def body(kp, seen):
            blocks = []
            for j in range(2):
                k0 = pl.multiple_of((2 * kp + j) * tk, tk)
                sv = sc_ref[pl.ds(k0, tk), :]
                above_lo = jnp.where(sv > lo, 1.0, jnp.where(sv >= lo, closed, 0.0))
                member = jnp.where(sv > hi, 0.0, above_lo)
                blocks.append((k0, member, jnp.dot(prefix, member.astype(BF16), preferred_element_type=F32)))
            for k0, member, within in blocks:
                rank = within + seen
                sv = sc_ref[pl.ds(k0, tk), :]
                sc_ref[pl.ds(k0, tk), :] = jnp.where(member > 0.0, jnp.where(rank > need, -jnp.inf, jnp.inf), sv)
                seen = rank[tk - 1:tk, :]
            return seen

        lax.fori_loop(0, npair, body, jnp.zeros((1, tq), F32))

    m_ref[...] = jnp.full(m_ref.shape, NEG_BIG, F32)
    acc_ref[...] = jnp.zeros(acc_ref.shape, F32)

    def selection_bias(kb):
        return jnp.where(sc_ref[pl.ds(pl.multiple_of(kb * tk, tk), tk), :] > lo, 0.0, NEG_BIG)

    def score_head(kb, h, bias, park_ref, max_ref):
        kh = k_ref[pl.ds(pl.multiple_of(kb * tk, tk), tk), h * HEAD_PAD:(h + 1) * HEAD_PAD]
        s = jnp.dot(kh, qt_ref[h * HEAD_PAD:(h + 1) * HEAD_PAD, :], preferred_element_type=F32) + bias
        park_ref[h] = s
        max_ref[h:h + 1, :] = jnp.max(s, axis=0, keepdims=True)

    def softmax_head(kb, h, park_ref, max_ref):
        m_old = m_ref[h:h + 1, :]
        m_new = jnp.maximum(m_old, max_ref[h:h + 1, :])
        alpha = jnp.exp2(m_old - m_new)
        p = jnp.exp2((park_ref[h] - m_new).astype(BF16))
        rows = slice(h * V_ROWS, (h + 1) * V_ROWS)
        pv = jnp.dot(vt_ref[rows, pl.ds(pl.multiple_of(kb * tk, tk), tk)], p, preferred_element_type=F32)
        acc_ref[rows, :] = alpha * acc_ref[rows, :] + pv
        m_ref[h:h + 1, :] = m_new

    last_block = 2 * npair - 1
    bias0 = selection_bias(0)
    for h in range(H_A):
        score_head(0, h, bias0, s_ref, smax_ref)

    def att_body(kp, carry):
        odd, nxt = 2 * kp + 1, jnp.minimum(2 * kp + 2, last_block)
        bias = selection_bias(odd)
        for h in range(H_A):
            score_head(odd, h, bias, s2_ref, smax2_ref)
            softmax_head(2 * kp, h, s_ref, smax_ref)
        bias = selection_bias(nxt)
        for h in range(H_A):
            score_head(nxt, h, bias, s_ref, smax_ref)
            softmax_head(odd, h, s2_ref, smax2_ref)
        return carry

    lax.fori_loop(0, npair, att_body, 0)
    out = [acc_ref[h * V_ROWS:h * V_ROWS + D_V_A, :] / acc_ref[h * V_ROWS + D_V_A:h * V_ROWS + D_V_A + 1, :]
           for h in range(H_A)]
    o_ref[...] = jnp.concatenate(out, axis=0).T.astype(BF16)


def _dsa_call(qt, qit, wit, k, ki, vt):
    b, hq, s = qt.shape
    tq = TQ_DSA
    topk = min(TOPK_MAX, s // 4)
    colsT = lambda height: pl.BlockSpec((None, height, tq), lambda i, j: (i, 0, j))
    whole = lambda a: pl.BlockSpec((None,) + a.shape[1:], lambda i, j: (i, 0, 0))
    return pl.pallas_call(
        functools.partial(_dsa_kernel, topk=topk),
        out_shape=jax.ShapeDtypeStruct((b, s, WIDTH_A), BF16),
        grid=(b, s // tq),
        in_specs=[colsT(H_IDX * D_IDX), colsT(H_IDX), colsT(hq), whole(ki), whole(k), whole(vt)],
        out_specs=pl.BlockSpec((None, tq, WIDTH_A), lambda i, j: (i, j, 0)),
        scratch_shapes=[pltpu.VMEM((s, tq), F32),
                        pltpu.VMEM((H_A, tq), F32),
                        pltpu.VMEM((H_A * V_ROWS, tq), F32),
                        pltpu.VMEM((H_A, TK_DSA, tq), F32),
                        pltpu.VMEM((H_A, TK_DSA, tq), F32),
                        pltpu.VMEM((H_A, tq), F32),
                        pltpu.VMEM((H_A, tq), F32),
                        pltpu.VMEM((H_IDX * HEAD_PAD, tq), BF16)],
        compiler_params=_params(2),
        name="dsa",
    )(qit, wit, qt, ki, k, vt)


def _dil_kernel(q_ref, kc_ref, kp_ref, vc_ref, vp_ref, o_ref, lse_ref, s_ref):
    for res in range(q_ref.shape[0]):
        _dil_residue(q_ref.at[res], kc_ref.at[res], kp_ref.at[res], vc_ref.at[res], vp_ref.at[res],
                     o_ref.at[res], lse_ref.at[res], s_ref.at[res])


def _dil_residue(q_ref, kc_ref, kp_ref, vc_ref, vp_ref, o_ref, lse_ref, s_ref):
    tb = kp_ref.shape[0]
    n_sub = q_ref.shape[0] // tb
    first_step = pl.program_id(2) == 0
    k = jnp.concatenate([kp_ref[...], kc_ref[...]], axis=0)
    v = jnp.concatenate([vp_ref[...], vc_ref[...]], axis=0)
    r = lax.broadcasted_iota(jnp.int32, (tb, 2 * tb), 0)
    c = lax.broadcasted_iota(jnp.int32, (tb, 2 * tb), 1)
    behind = (tb + r - c).astype(jnp.uint32)
    bias = jnp.where(behind <= tb, 0.0, -jnp.inf)
    no_prev = jnp.where(first_step, r, tb).astype(jnp.uint32)
    bias0 = jnp.where(behind <= no_prev, 0.0, -jnp.inf)
    lane = lax.broadcasted_iota(jnp.int32, (1, WIDTH_B), 1)
    head_qk = (lane % 128) // (D_HEAD_B // 2)
    head_v = lane // D_HEAD_B
    nt = (((1,), (1,)), ((), ()))
    for j in range(n_sub):
        q = q_ref[j * tb:(j + 1) * tb, :]
        for h in range(H_B):
            qh = jnp.where(head_qk == h, q, jnp.zeros_like(q))
            s_ref[j * H_B + h] = (lax.dot_general(qh, k[j * tb:(j + 2) * tb], nt, preferred_element_type=F32)
                                  + (bias0 if j == 0 else bias))
    for j in range(n_sub):
        vj = v[j * tb:(j + 2) * tb]
        num = jnp.zeros((tb, WIDTH_B), F32)
        den = jnp.zeros((tb, WIDTH_B), F32)
        top = jnp.zeros((tb, WIDTH_B), F32)
        for h in range(H_B):
            s = s_ref[j * H_B + h]
            m = jnp.max(s, axis=1, keepdims=True)
            p = jnp.exp2(s - m).astype(BF16)
            pv = jnp.dot(p, jnp.where(head_v == h, vj, jnp.ones_like(vj)), preferred_element_type=F32)
            mine = head_v == h
            num = jnp.where(mine, pv, num)
            den = jnp.where(mine, pltpu.roll(pv, WIDTH_B // 2, axis=1), den)
            top = jnp.where(mine, m, top)
        o_ref[j * tb:(j + 1) * tb, :] = (num / den).astype(o_ref.dtype)
        lse_ref[j * tb:(j + 1) * tb, :] = top + jnp.log2(den)


def _dil_call(qkv):
    b, dilation, n, _ = qkv.shape
    tb, tq = TB_DIL, min(TQ_DIL, n)
    per = tq // tb
    n_res = max(1, min(dilation, TQ_DIL // tq))
    cur = lambda which: pl.BlockSpec((None, n_res, tq, WIDTH_B), lambda i, r, j: (i, r, j, which))
    prev = lambda which: pl.BlockSpec((None, n_res, tb, WIDTH_B),
                                      lambda i, r, j: (i, r, jnp.maximum(j * per - 1, 0), which))
    out_spec = pl.BlockSpec((None, n_res, tq, WIDTH_B), lambda i, r, j: (i, r, j, 0))
    out = lambda dtype: jax.ShapeDtypeStruct((b, dilation, n, WIDTH_B), dtype)
    return pl.pallas_call(
        _dil_kernel,
        out_shape=(out(BF16), out(F32)),
        grid=(b, dilation // n_res, n // tq),
        in_specs=[cur(0), cur(1), prev(1), cur(2), prev(2)],
        out_specs=(out_spec, out_spec),
        scratch_shapes=[pltpu.VMEM((n_res, per * H_B, tb, 2 * tb), F32)],
        compiler_params=_params(3),
        name="dil",
    )(qkv, qkv, qkv, qkv, qkv)


def _merge_ffn_kernel(x_ref, mod_ref, oa_ref, o0_ref, o1_ref, o2_ref, l0_ref, l1_ref, l2_ref, gates_ref,
                      wua_ref, wub_ref, wo_ref, gpost_ref, gpre2_ref, wg_ref, wu_ref, wd_ref, gpost2_ref,
                      out_ref, acc_ref, *scr):
    tm = x_ref.shape[0]
    tmp = scr[4]

    def token_major(ref, buf):
        dilation = ref.shape[0]
        if dilation == 1:
            return ref[0]
        n_slab = ref.shape[2] // LANES
        for c in range(n_slab):
            if dilation <= 4:
                for r in range(dilation):
                    buf[c, pl.ds(r, tm // dilation, stride=dilation), :] = (
                        ref[r, :, c * LANES:(c + 1) * LANES].astype(F32))
            else:
                quarter = tm // 4
                for r in range(dilation):
                    tmp[c, pl.ds((r % 4) * quarter + r // 4, tm // dilation, stride=dilation // 4), :] = (
                        ref[r, :, c * LANES:(c + 1) * LANES].astype(F32))
                for r4 in range(4):
                    buf[c, pl.ds(r4, quarter, stride=4), :] = tmp[c, r4 * quarter:(r4 + 1) * quarter, :]
        return jnp.concatenate([buf[c] for c in range(n_slab)], axis=1)

    o0, l0 = token_major(o0_ref, None).astype(F32), token_major(l0_ref, None)
    o1, l1 = token_major(o1_ref, scr[0]), token_major(l1_ref, scr[1])
    o2, l2 = token_major(o2_ref, scr[2]), token_major(l2_ref, scr[3])
    m = jnp.maximum(jnp.maximum(l0, l1), l2)
    e0, e1, e2 = jnp.exp2(l0 - m), jnp.exp2(l1 - m), jnp.exp2(l2 - m)
    ob = (e0 * o0 + e1 * o1 + e2 * o2) / (e0 + e1 + e2)
    za = jnp.dot(oa_ref[...], wua_ref[...], preferred_element_type=F32)
    zb = jnp.dot(ob.astype(BF16), wub_ref[...], preferred_element_type=F32)
    d = za.shape[1]
    z = gates_ref[:, 0:d].astype(F32) * za + gates_ref[:, d:2 * d].astype(F32) * zb
    y = jnp.dot(z.astype(BF16), wo_ref[...], preferred_element_type=F32)
    x_mixed = x_ref[...] + mod_ref[5:6, :] * (_rms_rows(y) * gpost_ref[...])
    _ffn_tile(x_mixed, mod_ref, gpre2_ref, wg_ref, wu_ref, wd_ref, gpost2_ref, out_ref, acc_ref, 6)


def _merge_ffn_call(x, mod, oa, dil, gates, w_up_a, w_up_b, w_o, g_post, g_pre2, w_gate, w_up, w_down, g_post2):
    b, s, d = x.shape
    tm = TM_FFN
    rows = lambda width: pl.BlockSpec((None, tm, width), lambda i, j: (i, j, 0))
    const = lambda a: pl.BlockSpec(a.shape, lambda i, j: (0,) * a.ndim, pipeline_mode=pl.Buffered(1))
    (o0, l0), (o1, l1), (o2, l2) = dil
    res = lambda a: pl.BlockSpec((None, a.shape[1], tm // a.shape[1], WIDTH_B), lambda i, j: (i, 0, j, 0))
    consts = [w_up_a, w_up_b, w_o, g_post, g_pre2, w_gate, w_up, w_down, g_post2]
    return pl.pallas_call(
        _merge_ffn_kernel,
        out_shape=jax.ShapeDtypeStruct((b, s, d), F32),
        grid=(b, s // tm),
        in_specs=[rows(d), pl.BlockSpec((None, N_MOD, d), lambda i, j: (i, 0, 0)), rows(WIDTH_A)]
                 + [res(a) for a in (o0, o1, o2, l0, l1, l2)]
                 + [rows(2 * d)] + [const(a) for a in consts],
        out_specs=rows(d),
        scratch_shapes=[pltpu.VMEM((tm, d), F32)] + [pltpu.VMEM((WIDTH_B // LANES, tm, LANES), F32)] * 5,
        compiler_params=_params(2),
        name="merge_ffn",
    )(x, mod, oa, o0, o1, o2, l0, l1, l2, gates, *consts)


def _pad_heads(w_t):
    h, r, n = w_t.shape
    return jnp.pad(w_t, ((0, 0), (0, HEAD_PAD - r), (0, 0))).reshape(h * HEAD_PAD, n)


def _mixer_weights(w_in, g_cq, g_ckv, w_uq, w_uk, w_uv, w_iq):
    d = w_in.shape[0]
    o_cq, o_ckv, o_kr, o_ki, o_wi = 0, Q_LORA, Q_LORA + KV_LORA, Q_LORA + KV_LORA + D_ROPE_A, \
        Q_LORA + KV_LORA + D_ROPE_A + D_IDX
    o_qkv = o_wi + H_IDX
    o_gates = o_qkv + N_QKV_B

    def swap_halves(w):
        half = w.shape[1] // 2
        return jnp.concatenate([w[:, half:], w[:, :half]], axis=1)

    w_kr, w_ki = w_in[:, o_kr:o_kr + D_ROPE_A], w_in[:, o_ki:o_ki + D_IDX]
    qkv = w_in[:, o_qkv:o_gates].reshape(d, 3, N_GROUPS_B, H_B, 2, D_HEAD_B // 2)
    qk_split = qkv[:, 0:2].transpose(0, 1, 2, 4, 3, 5).reshape(d, 2 * N_GROUPS_B * WIDTH_B)
    v_cols = qkv[:, 2].reshape(d, N_GROUPS_B * WIDTH_B)
    w_std = jnp.concatenate([w_in[:, o_ckv:o_ckv + KV_LORA], w_kr, swap_halves(w_kr), w_ki, swap_halves(w_ki),
                             qk_split, v_cols, w_in[:, o_gates:]], axis=1)
    w_tr = jnp.concatenate([w_in[:, o_cq:o_cq + Q_LORA], w_in[:, o_ckv:o_ckv + KV_LORA],
                            w_in[:, o_wi:o_wi + H_IDX]], axis=1).T
    w_tr = jnp.pad(w_tr, ((0, N_TR - w_tr.shape[0]), (0, 0)))

    return {
        "w_std": w_std.astype(BF16),
        "w_tr": w_tr.astype(BF16),
        "g_cq_col": g_cq.reshape(Q_LORA, 1),
        "g_ckv_col": g_ckv.reshape(KV_LORA, 1),
        "g_ckv_row": g_ckv.reshape(1, KV_LORA),
        "w_uq_t": _pad_heads(w_uq.transpose(1, 2, 0)).astype(BF16),
        "w_iq_t": w_iq.transpose(1, 2, 0).reshape(H_IDX * D_IDX, Q_LORA).astype(BF16),
        "w_uk_p": jnp.pad(w_uk, ((0, 0), (0, 0), (0, HEAD_PAD - D_NOPE))).reshape(KV_LORA, H_A * HEAD_PAD).astype(BF16),
        "w_uv_t": w_uv.transpose(1, 2, 0).reshape(WIDTH_A, KV_LORA).astype(BF16),
    }


def kernel(x, c, positions, w_mod, b_mod, g_pre_ffn1, w_gate1, w_up1, w_down1, g_post_ffn1, g_pre_mix, w_in, g_cq, g_ckv, w_uq, w_uk, w_uv, w_iq, w_up_a, w_up_b, w_o, g_post_mix, g_pre_ffn2, w_gate2, w_up2, w_down2, g_post_ffn2):
    b, s, d = x.shape
    assert s % (16 * TB_DIL) == 0 and s % TM_FFN == 0 and b <= 8
    c8 = jnp.pad(c, ((0, 8 - b), (0, 0)))
    half_b, half_a = D_HEAD_B // 2, D_ROPE_A // 2
    inv = jnp.concatenate([ROPE_THETA ** (-jnp.arange(half_b, dtype=F32) / half_b),
                           ROPE_THETA ** (-jnp.arange(half_a, dtype=F32) / half_a)]).reshape(48, 1)
    tabs = _rope_call(positions.reshape(b, 1, s), inv)
    row = lambda g: g.reshape(1, -1)
    for l in range(w_mod.shape[0]):
        mod = _mod_call(c8, w_mod[l], b_mod[l].reshape(1, -1))[:b].reshape(b, N_MOD, d)
        x = _ffn_call(x, mod, row(g_pre_ffn1[l]), w_gate1[l].astype(BF16), w_up1[l].astype(BF16),
                      w_down1[l].astype(BF16), row(g_post_ffn1[l]), 0)
        w = _mixer_weights(w_in[l], g_cq[l], g_ckv[l], w_uq[l], w_uk[l], w_uv[l], w_iq[l])
        qt, qit, wit, k, ki, vt, dq0, dq1, dq2, gates = _proj_call(x, mod, row(g_pre_mix[l]), w, tabs)
        oa = _dsa_call(qt, qit, wit, k, ki, vt)
        dil = [_dil_call(dq) for dq in (dq0, dq1, dq2)]
        x = _merge_ffn_call(x, mod, oa, dil, gates, w_up_a[l].astype(BF16), w_up_b[l].astype(BF16),
                            w_o[l].astype(BF16), row(g_post_mix[l]), row(g_pre_ffn2[l]), w_gate2[l].astype(BF16),
                            w_up2[l].astype(BF16), w_down2[l].astype(BF16), row(g_post_ffn2[l]))
    return x
```

```python
import functools

import numpy as np
import jax
import jax.numpy as jnp
from jax import lax
from jax.experimental import pallas as pl
from jax.experimental.pallas import tpu as pltpu

F32 = jnp.float32
BF16 = jnp.bfloat16

D_FF = 2816
ROPE_THETA = 10000.0
NORM_EPS = 1e-6
H_A = 8
Q_LORA = 384
KV_LORA = 256
D_NOPE = 64
D_ROPE_A = 32
D_V_A = 64
H_IDX = 8
D_IDX = 32
TOPK_MAX = 256
DIL_PAIRS = ((128, 1), (512, 4), (2048, 16))
N_GROUPS_B = 3
H_B = 4
D_HEAD_B = 64
N_MOD = 9
WIDTH_A = H_A * D_V_A
WIDTH_B = H_B * D_HEAD_B
N_QKV_B = 3 * N_GROUPS_B * WIDTH_B

LANES = 128
HEAD_PAD = 128
VMEM_LIMIT = 56 * 1024 * 1024

TM_FFN = 512
CH_FFN = 256
TP_PROJ = 512
TQ_DSA = 512
TK_DSA = 256
V_ROWS = D_V_A + 16
TB_DIL = 128
TQ_DIL = 512
BISECT_MAX = 34
BISECT_CHECK = 2
FOLD_ROWS = 16
COUNT_ROWS = 64
IDX_ROWS = 64
NEG_BIG = -1e30
LOG2_E = 1.4426950408889634

C_KR = 0
C_QKVB = 128
C_GATES = C_QKVB + N_QKV_B
N_STD = C_GATES + 2048
R_CQ = 0
R_CKV = Q_LORA
R_WI = Q_LORA + KV_LORA
N_TR = 656


def _params(n_axes):
    return pltpu.CompilerParams(dimension_semantics=("arbitrary",) * n_axes,
                                vmem_limit_bytes=VMEM_LIMIT)


def _rms_rows(x):
    return x * lax.rsqrt(jnp.mean(x * x, axis=-1, keepdims=True) + NORM_EPS)


def _mod_kernel(c_ref, w_ref, b_ref, o_ref):
    c = c_ref[...]
    o_ref[...] = jnp.dot(c * jax.nn.sigmoid(c), w_ref[...], preferred_element_type=F32) + b_ref[...]


def _mod_call(c8, w_mod, b_mod):
    d = c8.shape[1]
    return pl.pallas_call(
        _mod_kernel,
        out_shape=jax.ShapeDtypeStruct((8, N_MOD * d), F32),
        grid=(N_MOD,),
        in_specs=[pl.BlockSpec((8, d), lambda j: (0, 0)),
                  pl.BlockSpec((d, d), lambda j: (0, j)),
                  pl.BlockSpec((1, d), lambda j: (0, j))],
        out_specs=pl.BlockSpec((8, d), lambda j: (0, j)),
        compiler_params=_params(1),
        name="mod",
    )(c8, w_mod, b_mod)


def _rope_kernel(pos_ref, inv_ref, tt_ref, ca_ref, sa_ref, cb_ref, sb_ref):
    s = pos_ref.shape[1]
    ang = inv_ref[...] * pos_ref[...].astype(F32)
    cos = jnp.cos(ang)
    sin = jnp.sin(ang)
    cos_b, sin_b = cos[32:48], sin[32:48]
    tt_ref[0] = cos_b
    tt_ref[1] = sin_b
    cos_a = jnp.concatenate([cos[0:32]] * 4, axis=0)
    sin_a = jnp.concatenate([sin[0:32]] * 4, axis=0)
    cos_bt = jnp.concatenate([cos_b] * 8, axis=0)
    sin_bt = jnp.concatenate([-sin_b, sin_b] * 4, axis=0)
    for j in range(s // LANES):
        cols = slice(j * LANES, (j + 1) * LANES)
        ca_ref[cols, :] = cos_a[:, cols].T
        sa_ref[cols, :] = sin_a[:, cols].T
        cb_ref[cols, :] = cos_bt[:, cols].T
        sb_ref[cols, :] = sin_bt[:, cols].T


def _rope_call(pos3, inv):
    b, _, s = pos3.shape
    tab = jax.ShapeDtypeStruct((b, s, LANES), F32)
    tab_spec = pl.BlockSpec((None, s, LANES), lambda i: (i, 0, 0))
    return pl.pallas_call(
        _rope_kernel,
        out_shape=(jax.ShapeDtypeStruct((b, 2, 16, s), F32), tab, tab, tab, tab),
        grid=(b,),
        in_specs=[pl.BlockSpec((None, 1, s), lambda i: (i, 0, 0)),
                  pl.BlockSpec((48, 1), lambda i: (0, 0))],
        out_specs=(pl.BlockSpec((None, 2, 16, s), lambda i: (i, 0, 0, 0)),
                   tab_spec, tab_spec, tab_spec, tab_spec),
        compiler_params=_params(1),
        name="rope",
    )(pos3, inv)


def _ffn_kernel(x_ref, mod_ref, gpre_ref, wg_ref, wu_ref, wd_ref, gpost_ref, o_ref, acc_ref, *, j0):
    _ffn_tile(x_ref[...], mod_ref, gpre_ref, wg_ref, wu_ref, wd_ref, gpost_ref, o_ref, acc_ref, j0)


def _ffn_tile(x, mod_ref, gpre_ref, wg_ref, wu_ref, wd_ref, gpost_ref, o_ref, acc_ref, j0):
    sh, sc, gt = mod_ref[j0:j0 + 1, :], mod_ref[j0 + 1:j0 + 2, :], mod_ref[j0 + 2:j0 + 3, :]
    h = (_rms_rows(x) * gpre_ref[...] * (1.0 + sc) + sh).astype(BF16)
    n_ch = wg_ref.shape[1] // CH_FFN
    for ch in range(n_ch):
        cols = slice(ch * CH_FFN, (ch + 1) * CH_FFN)
        g = jnp.dot(h, wg_ref[:, cols], preferred_element_type=F32)
        u = jnp.dot(h, wu_ref[:, cols], preferred_element_type=F32)
        a = (g * jax.nn.sigmoid(g) * u).astype(BF16)
        part = jnp.dot(a, wd_ref[cols, :], preferred_element_type=F32)
        if ch == 0:
            acc_ref[...] = part
        else:
            acc_ref[...] += part
    o_ref[...] = x + 0.5 * gt * (_rms_rows(acc_ref[...]) * gpost_ref[...])


def _ffn_call(x, mod, g_pre, w_gate, w_up, w_down, g_post, j0):
    b, s, d = x.shape
    f = w_gate.shape[1]
    const = lambda shape: pl.BlockSpec(shape, lambda i, j: (0,) * len(shape), pipeline_mode=pl.Buffered(1))
    tile = pl.BlockSpec((None, TM_FFN, d), lambda i, j: (i, j, 0))
    return pl.pallas_call(
        functools.partial(_ffn_kernel, j0=j0),
        out_shape=jax.ShapeDtypeStruct((b, s, d), F32),
        grid=(b, s // TM_FFN),
        in_specs=[tile,
                  pl.BlockSpec((None, N_MOD, d), lambda i, j: (i, 0, 0)),
                  const((1, d)), const((d, f)), const((d, f)), const((f, d)), const((1, d))],
        out_specs=tile,
        scratch_shapes=[pltpu.VMEM((TM_FFN, d), F32)],
        compiler_params=_params(2),
        name="ffn",
    )(x, mod, g_pre, w_gate, w_up, w_down, g_post)


def _proj_kernel(x_ref, mod_ref, gpre_ref, wstd_ref, wtr_ref, gcq_ref, gckv_col_ref,
                 wuq_ref, wiq_ref, wuk_ref, wuv_ref,
                 tt_ref, ca_ref, sa_ref, cb_ref, sb_ref,
                 qt_ref, qit_ref, wit_ref, k_ref, ki_ref, vt_ref, dq0_ref, dq1_ref, dq2_ref, gates_ref,
                 dil_scr, dil_tmp):
    x = x_ref[...]
    sh, sc = mod_ref[3:4, :], mod_ref[4:5, :]
    u = (_rms_rows(x) * gpre_ref[...] * (1.0 + sc) + sh).astype(BF16)
    std = jnp.dot(u, wstd_ref[...], preferred_element_type=F32)
    tr = lax.dot_general(wtr_ref[...], u, (((1,), (1,)), ((), ())),
                         preferred_element_type=F32)

    cq = tr[R_CQ:R_CQ + Q_LORA]
    cq = cq * lax.rsqrt(jnp.mean(cq * cq, axis=0, keepdims=True) + NORM_EPS) * gcq_ref[...]
    cq = cq.astype(BF16)
    ckv_t = tr[R_CKV:R_CKV + KV_LORA]
    ckv_t = ckv_t * lax.rsqrt(jnp.mean(ckv_t * ckv_t, axis=0, keepdims=True) + NORM_EPS) * gckv_col_ref[...]
    cos_t, sin_t = tt_ref[0], tt_ref[1]

    attn_scale = (D_NOPE + D_ROPE_A) ** -0.5 * LOG2_E
    q_t = jnp.dot(wuq_ref[...], cq, preferred_element_type=F32) * attn_scale
    idx_scale = D_IDX ** -0.5
    qi_t = jnp.dot(wiq_ref[...], cq, preferred_element_type=F32) * idx_scale
    d_qk = D_NOPE + D_ROPE_A
    for h in range(H_A):
        r0, s0 = h * HEAD_PAD, h * d_qk
        x1, x2 = q_t[s0 + 64:s0 + 80], q_t[s0 + 80:s0 + 96]
        qt_ref[r0:r0 + 64, :] = q_t[s0:s0 + 64].astype(BF16)
        qt_ref[r0 + 64:r0 + 80, :] = (x1 * cos_t - x2 * sin_t).astype(BF16)
        qt_ref[r0 + 80:r0 + 96, :] = (x2 * cos_t + x1 * sin_t).astype(BF16)
        qt_ref[r0 + 96:r0 + 128, :] = jnp.zeros((HEAD_PAD - d_qk, q_t.shape[1]), BF16)
        i0 = h * D_IDX
        y1, y2 = qi_t[i0:i0 + 16], qi_t[i0 + 16:i0 + 32]
        qit_ref[i0:i0 + 16, :] = (y1 * cos_t - y2 * sin_t).astype(BF16)
        qit_ref[i0 + 16:i0 + 32, :] = (y2 * cos_t + y1 * sin_t).astype(BF16)
    wit_ref[...] = tr[R_WI:R_WI + H_IDX] * (H_IDX ** -0.5)
    v_t = jnp.dot(wuv_ref[...], ckv_t.astype(BF16), preferred_element_type=F32).astype(BF16)
    for h in range(H_A):
        vt_ref[h * V_ROWS:h * V_ROWS + D_V_A, :] = v_t[h * D_V_A:(h + 1) * D_V_A]
        vt_ref[h * V_ROWS + D_V_A:(h + 1) * V_ROWS, :] = jnp.ones((V_ROWS - D_V_A, v_t.shape[1]), BF16)

    ckv = ckv_t.T.astype(BF16)
    k_nope = jnp.dot(ckv, wuk_ref[...], preferred_element_type=F32)
    t = std[:, C_KR:C_KR + LANES]
    lane = lax.broadcasted_iota(jnp.int32, (1, LANES), 1)
    unswapped = jnp.where((lane // 32) % 2 == 0, 1.0, 0.0)
    roped = (t * cb_ref[...] + pltpu.roll(t, LANES - 32, axis=1) * sb_ref[...]) * unswapped
    moved = pltpu.roll(roped, 64, axis=1)
    k_rope_tile = jnp.where(lane >= 64, moved, 0.0)
    for h in range(H_A):
        k_ref[:, h * HEAD_PAD:(h + 1) * HEAD_PAD] = (k_nope[:, h * HEAD_PAD:(h + 1) * HEAD_PAD]
                                                      + k_rope_tile).astype(BF16)
    ki_ref[...] = jnp.where(lane < 32, moved, 0.0).astype(BF16)

    cos_a, sin_a = ca_ref[...], sa_ref[...]
    tp = x.shape[0]
    for g, (out_ref, (_, dilation)) in enumerate(zip((dq0_ref, dq1_ref, dq2_ref), DIL_PAIRS)):
        for which in range(2):
            scale = (D_HEAD_B ** -0.5 * LOG2_E) if which == 0 else 1.0
            c0 = C_QKVB + (which * N_GROUPS_B + g) * WIDTH_B
            x1, x2 = std[:, c0:c0 + 128], std[:, c0 + 128:c0 + 256]
            dil_scr[2 * which] = (x1 * cos_a - x2 * sin_a) * scale
            dil_scr[2 * which + 1] = (x2 * cos_a + x1 * sin_a) * scale
        c0 = C_QKVB + (2 * N_GROUPS_B + g) * WIDTH_B
        dil_scr[4] = std[:, c0:c0 + 128]
        dil_scr[5] = std[:, c0 + 128:c0 + 256]
        for c in range(3 * WIDTH_B // LANES):
            if dilation <= 4:
                for r in range(dilation):
                    out_ref[r, :, c * LANES:(c + 1) * LANES] = (
                        dil_scr[c, pl.ds(r, tp // dilation, stride=dilation), :].astype(BF16))
            else:
                quarter = tp // 4
                for r4 in range(4):
                    dil_tmp[c, r4 * quarter:(r4 + 1) * quarter, :] = dil_scr[c, pl.ds(r4, quarter, stride=4), :]
                for r in range(dilation):
                    out_ref[r, :, c * LANES:(c + 1) * LANES] = dil_tmp[
                        c, pl.ds((r % 4) * quarter + r // 4, tp // dilation, stride=dilation // 4), :].astype(BF16)
    gates_ref[...] = jax.nn.sigmoid(std[:, C_GATES:]).astype(BF16)


def _proj_call(x, mod, g_pre, w, tabs):
    b, s, d = x.shape
    tt, ca, sa, cb, sb = tabs
    tp = TP_PROJ
    const = lambda a: pl.BlockSpec(a.shape, lambda i, j: (0,) * a.ndim)
    rows = lambda width: pl.BlockSpec((None, tp, width), lambda i, j: (i, j, 0))
    colsT = lambda height: pl.BlockSpec((None, height, tp), lambda i, j: (i, 0, j))
    consts = [g_pre, w["w_std"], w["w_tr"], w["g_cq_col"], w["g_ckv_col"],
              w["w_uq_t"], w["w_iq_t"], w["w_uk_p"], w["w_uv_t"]]
    hq = H_A * HEAD_PAD
    dil_shapes = [jax.ShapeDtypeStruct((b, dil, s // dil, 3 * WIDTH_B), BF16) for _, dil in DIL_PAIRS]
    dil_specs = [pl.BlockSpec((None, dil, tp // dil, 3 * WIDTH_B), lambda i, j: (i, 0, j, 0))
                 for _, dil in DIL_PAIRS]
    return pl.pallas_call(
        _proj_kernel,
        out_shape=(jax.ShapeDtypeStruct((b, hq, s), BF16),
                   jax.ShapeDtypeStruct((b, H_IDX * D_IDX, s), BF16),
                   jax.ShapeDtypeStruct((b, H_IDX, s), F32),
                   jax.ShapeDtypeStruct((b, s, hq), BF16),
                   jax.ShapeDtypeStruct((b, s, LANES), BF16),
                   jax.ShapeDtypeStruct((b, H_A * V_ROWS, s), BF16),
                   *dil_shapes,
                   jax.ShapeDtypeStruct((b, s, 2048), BF16)),
        grid=(b, s // tp),
        in_specs=[rows(d), pl.BlockSpec((None, N_MOD, d), lambda i, j: (i, 0, 0))]
                 + [const(a) for a in consts]
                 + [pl.BlockSpec((None, 2, 16, tp), lambda i, j: (i, 0, 0, j)),
                    rows(LANES), rows(LANES), rows(LANES), rows(LANES)],
        out_specs=(colsT(hq), colsT(H_IDX * D_IDX), colsT(H_IDX), rows(hq), rows(LANES), colsT(H_A * V_ROWS),
                   *dil_specs, rows(2048)),
        scratch_shapes=[pltpu.VMEM((3 * WIDTH_B // LANES, tp, LANES), F32)] * 2,
        compiler_params=_params(2),
        name="proj",
    )(x, mod, *consts, tt, ca, sa, cb, sb)


def _dsa_kernel(qit_ref, wit_ref, qt_ref, ki_ref, k_ref, vt_ref, o_ref,
                sc_ref, m_ref, acc_ref, s_ref, s2_ref, smax_ref, smax2_ref, qpad_ref, *, topk):
    tq = qt_ref.shape[1]
    tk = s_ref.shape[1]
    i = pl.program_id(1)
    nkb = (i + 1) * (tq // tk)
    npair = (nkb + 1) // 2
    qpos = i * tq + lax.broadcasted_iota(jnp.int32, (1, tq), 1)

    def fold(v):
        return jnp.sum(v.reshape(v.shape[0] // FOLD_ROWS, FOLD_ROWS, tq), axis=0)

    qpad_ref[...] = jnp.zeros(qpad_ref.shape, BF16)
    for h in range(H_IDX):
        qpad_ref[h * HEAD_PAD:h * HEAD_PAD + D_IDX, :] = qit_ref[h * D_IDX:(h + 1) * D_IDX, :]

    def park_logits(kb, park_ref, heads=range(H_IDX)):
        ki = ki_ref[pl.ds(pl.multiple_of(kb * tk, tk), tk), :]
        for h in heads:
            park_ref[h] = jnp.dot(ki, qpad_ref[h * HEAD_PAD:(h + 1) * HEAD_PAD, :], preferred_element_type=F32)

    def rows8(v, op):
        return op(v.reshape(IDX_ROWS // 8, 8, LANES), axis=0)

    piece_iota = lax.broadcasted_iota(jnp.int32, (IDX_ROWS, LANES), 0)
    n_col = tq // LANES
    heads_per_col = H_IDX // n_col

    def reduce_block(kb, park_ref, stats, on_diagonal, ahead=None):
        k0 = pl.multiple_of(kb * tk, tk)
        columns = []
        for c in range(n_col):
            if ahead is not None:
                park_logits(ahead[0], ahead[1], range(c * heads_per_col, (c + 1) * heads_per_col))
            lanes = slice(c * LANES, (c + 1) * LANES)
            smax, smin, c_pos, c_nn = [v[:, lanes] for v in stats]
            w = [wit_ref[h:h + 1, lanes] for h in range(H_IDX)]
            for j in range(tk // IDX_ROWS):
                r0 = j * IDX_ROWS
                acc = jnp.zeros((IDX_ROWS, LANES), F32)
                for h in range(H_IDX):
                    acc = acc + w[h] * jnp.maximum(park_ref[h, r0:r0 + IDX_ROWS, lanes], 0.0)
                if on_diagonal:
                    causal = (k0 + r0 + piece_iota) <= qpos[:, lanes]
                    sv = jnp.where(causal, acc, -jnp.inf)
                    smin = jnp.minimum(smin, rows8(jnp.where(causal, acc, jnp.inf), jnp.min))
                else:
                    sv = acc
                    smin = jnp.minimum(smin, rows8(acc, jnp.min))
                sc_ref[pl.ds(k0 + r0, IDX_ROWS), lanes] = sv
                smax = jnp.maximum(smax, rows8(sv, jnp.max))
                c_pos = c_pos + rows8(jnp.where(sv > 0.0, 1.0, 0.0), jnp.sum)
                c_nn = c_nn + rows8(jnp.where(sv >= 0.0, 1.0, 0.0), jnp.sum)
            columns.append((smax, smin, c_pos, c_nn))
        return tuple(jnp.concatenate(parts, axis=1) for parts in zip(*columns))

    def idx_pair(kp, stats):
        stats = reduce_block(2 * kp, s_ref, stats, False, ahead=(2 * kp + 1, s2_ref))
        return reduce_block(2 * kp + 1, s2_ref, stats, False, ahead=(2 * kp + 2, s_ref))

    park_logits(0, s_ref)
    zero8 = jnp.zeros((8, tq), F32)
    stats = lax.fori_loop(0, i, idx_pair,
                          (jnp.full((8, tq), -jnp.inf, F32), jnp.full((8, tq), jnp.inf, F32), zero8, zero8))
    stats = reduce_block(nkb - 2, s_ref, stats, True, ahead=(nkb - 1, s2_ref))
    stats = reduce_block(nkb - 1, s2_ref, stats, True)
    smax = jnp.max(stats[0], axis=0, keepdims=True)
    smin = jnp.min(stats[1], axis=0, keepdims=True)
    c_pos, c_nn = [jnp.sum(v, axis=0, keepdims=True) for v in stats[2:]]

    @pl.when(nkb % 2 == 1)
    def _pad_block():
        sc_ref[pl.ds(pl.multiple_of(nkb * tk, tk), tk), :] = jnp.full((tk, tq), -jnp.inf, F32)

    n_causal = (qpos + 1).astype(F32)
    kf = jnp.minimum(qpos + 1, topk).astype(F32)

    def count_gt(thr):
        def body(kp, c):
            k0 = pl.multiple_of(kp * (2 * tk), 2 * tk)
            for j in range(2 * tk // COUNT_ROWS):
                rows = pl.ds(k0 + j * COUNT_ROWS, COUNT_ROWS)
                c = c + fold(jnp.where(sc_ref[rows, :] > thr, 1.0, 0.0))
            return c
        part = lax.fori_loop(0, npair, body, jnp.zeros((FOLD_ROWS, tq), F32))
        return jnp.sum(part, axis=0, keepdims=True)

    below = smin - (1.0 + jnp.abs(smin))
    zero_tie = jnp.logical_and(c_pos < kf, kf <= c_nn)
    positive = c_pos >= kf
    searching = jnp.where(zero_tie, 0.0, 1.0)

    def bisect_cond(carry):
        return jnp.logical_and(carry[0] < BISECT_MAX, carry[1] > 0)

    def bisect_body(carry):
        it, _, lo, hi, c_lo, c_hi = carry
        for _ in range(BISECT_CHECK):
            mid = 0.5 * (lo + hi)
            c_mid = count_gt(mid)
            ge = c_mid >= kf
            up = jnp.logical_and(ge, searching > 0.0)
            down = jnp.logical_and(jnp.logical_not(ge), searching > 0.0)
            lo, c_lo = jnp.where(up, mid, lo), jnp.where(up, c_mid, c_lo)
            hi, c_hi = jnp.where(down, mid, hi), jnp.where(down, c_mid, c_hi)
        open_queries = (jnp.max((c_lo - kf) * searching) > 0.0).astype(jnp.int32)
        return it + BISECT_CHECK, open_queries, lo, hi, c_lo, c_hi

    lo0 = jnp.where(positive, 0.0, jnp.where(zero_tie, 0.0, below))
    hi0 = jnp.where(positive, smax, 0.0)
    c_lo0 = jnp.where(positive, c_pos, jnp.where(zero_tie, c_nn, n_causal))
    c_hi0 = jnp.where(positive, 0.0, c_pos)
    first_open = (jnp.max((c_lo0 - kf) * searching) > 0.0).astype(jnp.int32)
    _, _, lo, hi, c_lo, c_hi = lax.while_loop(
        bisect_cond, bisect_body, (jnp.int32(0), first_open, lo0, hi0, c_lo0, c_hi0))

    @pl.when(jnp.max(jnp.maximum(c_lo - kf, 1.0 - searching)) > 0.0)
    def _break_ties():
        need = kf - c_hi
        closed = 1.0 - searching
        prefix = jnp.where(lax.broadcasted_iota(jnp.int32, (tk, tk), 0)
                           >= lax.broadcasted_iota(jnp.int32, (tk, tk), 1), 1.0, 0.0).astype(BF16)

        def body(kp, seen):
            blocks = []
            for j in range(2):
                k0 = pl.multiple_of((2 * kp + j) * tk, tk)
                sv = sc_ref[pl.ds(k0, tk), :]
                above_lo = jnp.where(sv > lo, 1.0, jnp.where(sv >= lo, closed, 0.0))
                member = jnp.where(sv > hi, 0.0, above_lo)
                blocks.append((k0, member, jnp.dot(prefix, member.astype(BF16), preferred_element_type=F32)))
            for k0, member, within in blocks:
                rank = within + seen
                sv = sc_ref[pl.ds(k0, tk), :]
                sc_ref[pl.ds(k0, tk), :] = jnp.where(member > 0.0, jnp.where(rank > need, -jnp.inf, jnp.inf), sv)
                seen = rank[tk - 1:tk, :]
            return seen

        lax.fori_loop(0, npair, body, jnp.zeros((1, tq), F32))

    m_ref[...] = jnp.full(m_ref.shape, NEG_BIG, F32)
    acc_ref[...] = jnp.zeros(acc_ref.shape, F32)

    def selection_bias(kb):
        return jnp.where(sc_ref[pl.ds(pl.multiple_of(kb * tk, tk), tk), :] > lo, 0.0, NEG_BIG)

    def score_head(kb, h, bias, park_ref, max_ref):
        kh = k_ref[pl.ds(pl.multiple_of(kb * tk, tk), tk), h * HEAD_PAD:(h + 1) * HEAD_PAD]
        s = jnp.dot(kh, qt_ref[h * HEAD_PAD:(h + 1) * HEAD_PAD, :], preferred_element_type=F32) + bias
        park_ref[h] = s
        max_ref[h:h + 1, :] = jnp.max(s, axis=0, keepdims=True)

    def softmax_head(kb, h, park_ref, max_ref):
        m_old = m_ref[h:h + 1, :]
        m_new = jnp.maximum(m_old, max_ref[h:h + 1, :])
        alpha = jnp.exp2(m_old - m_new)
        p = jnp.exp2((park_ref[h] - m_new).astype(BF16))
        rows = slice(h * V_ROWS, (h + 1) * V_ROWS)
        pv = jnp.dot(vt_ref[rows, pl.ds(pl.multiple_of(kb * tk, tk), tk)], p, preferred_element_type=F32)
        acc_ref[rows, :] = alpha * acc_ref[rows, :] + pv
        m_ref[h:h + 1, :] = m_new

    last_block = 2 * npair - 1
    bias0 = selection_bias(0)
    for h in range(H_A):
        score_head(0, h, bias0, s_ref, smax_ref)

    def att_body(kp, carry):
        odd, nxt = 2 * kp + 1, jnp.minimum(2 * kp + 2, last_block)
        bias = selection_bias(odd)
        for h in range(H_A):
            score_head(odd, h, bias, s2_ref, smax2_ref)
            softmax_head(2 * kp, h, s_ref, smax_ref)
        bias = selection_bias(nxt)
        for h in range(H_A):
            score_head(nxt, h, bias, s_ref, smax_ref)
            softmax_head(odd, h, s2_ref, smax2_ref)
        return carry

    lax.fori_loop(0, npair, att_body, 0)
    out = [acc_ref[h * V_ROWS:h * V_ROWS + D_V_A, :] / acc_ref[h * V_ROWS + D_V_A:h * V_ROWS + D_V_A + 1, :]
           for h in range(H_A)]
    o_ref[...] = jnp.concatenate(out, axis=0).T.astype(BF16)


def _dsa_call(qt, qit, wit, k, ki, vt):
    b, hq, s = qt.shape
    tq = TQ_DSA
    topk = min(TOPK_MAX, s // 4)
    colsT = lambda height: pl.BlockSpec((None, height, tq), lambda i, j: (i, 0, j))
    whole = lambda a: pl.BlockSpec((None,) + a.shape[1:], lambda i, j: (i, 0, 0))
    return pl.pallas_call(
        functools.partial(_dsa_kernel, topk=topk),
        out_shape=jax.ShapeDtypeStruct((b, s, WIDTH_A), BF16),
        grid=(b, s // tq),
        in_specs=[colsT(H_IDX * D_IDX), colsT(H_IDX), colsT(hq), whole(ki), whole(k), whole(vt)],
        out_specs=pl.BlockSpec((None, tq, WIDTH_A), lambda i, j: (i, j, 0)),
        scratch_shapes=[pltpu.VMEM((s, tq), F32),
                        pltpu.VMEM((H_A, tq), F32),
                        pltpu.VMEM((H_A * V_ROWS, tq), F32),
                        pltpu.VMEM((H_A, TK_DSA, tq), F32),
                        pltpu.VMEM((H_A, TK_DSA, tq), F32),
                        pltpu.VMEM((H_A, tq), F32),
                        pltpu.VMEM((H_A, tq), F32),
                        pltpu.VMEM((H_IDX * HEAD_PAD, tq), BF16)],
        compiler_params=_params(2),
        name="dsa",
    )(qit, wit, qt, ki, k, vt)


def _dil_kernel(q_ref, kc_ref, kp_ref, vc_ref, vp_ref, o_ref, lse_ref, s_ref):
    for res in range(q_ref.shape[0]):
        _dil_residue(q_ref.at[res], kc_ref.at[res], kp_ref.at[res], vc_ref.at[res], vp_ref.at[res],
                     o_ref.at[res], lse_ref.at[res], s_ref.at[res])


def _dil_residue(q_ref, kc_ref, kp_ref, vc_ref, vp_ref, o_ref, lse_ref, s_ref):
    tb = kp_ref.shape[0]
    n_sub = q_ref.shape[0] // tb
    first_step = pl.program_id(2) == 0
    k = jnp.concatenate([kp_ref[...], kc_ref[...]], axis=0)
    v = jnp.concatenate([vp_ref[...], vc_ref[...]], axis=0)
    r = lax.broadcasted_iota(jnp.int32, (tb, 2 * tb), 0)
    c = lax.broadcasted_iota(jnp.int32, (tb, 2 * tb), 1)
    behind = (tb + r - c).astype(jnp.uint32)
    bias = jnp.where(behind <= tb, 0.0, -jnp.inf)
    no_prev = jnp.where(first_step, r, tb).astype(jnp.uint32)
    bias0 = jnp.where(behind <= no_prev, 0.0, -jnp.inf)
    lane = lax.broadcasted_iota(jnp.int32, (1, WIDTH_B), 1)
    head_qk = (lane % 128) // (D_HEAD_B // 2)
    head_v = lane // D_HEAD_B
    nt = (((1,), (1,)), ((), ()))
    for j in range(n_sub):
        q = q_ref[j * tb:(j + 1) * tb, :]
        for h in range(H_B):
            qh = jnp.where(head_qk == h, q, jnp.zeros_like(q))
            s_ref[j * H_B + h] = (lax.dot_general(qh, k[j * tb:(j + 2) * tb], nt, preferred_element_type=F32)
                                  + (bias0 if j == 0 else bias))
    for j in range(n_sub):
        vj = v[j * tb:(j + 2) * tb]
        num = jnp.zeros((tb, WIDTH_B), F32)
        den = jnp.zeros((tb, WIDTH_B), F32)
        top = jnp.zeros((tb, WIDTH_B), F32)
        for h in range(H_B):
            s = s_ref[j * H_B + h]
            m = jnp.max(s, axis=1, keepdims=True)
            p = jnp.exp2(s - m).astype(BF16)
            pv = jnp.dot(p, jnp.where(head_v == h, vj, jnp.ones_like(vj)), preferred_element_type=F32)
            mine = head_v == h
            num = jnp.where(mine, pv, num)
            den = jnp.where(mine, pltpu.roll(pv, WIDTH_B // 2, axis=1), den)
            top = jnp.where(mine, m, top)
        o_ref[j * tb:(j + 1) * tb, :] = (num / den).astype(o_ref.dtype)
        lse_ref[j * tb:(j + 1) * tb, :] = top + jnp.log2(den)


def _dil_call(qkv):
    b, dilation, n, _ = qkv.shape
    tb, tq = TB_DIL, min(TQ_DIL, n)
    per = tq // tb
    n_res = max(1, min(dilation, TQ_DIL // tq))
    cur = lambda which: pl.BlockSpec((None, n_res, tq, WIDTH_B), lambda i, r, j: (i, r, j, which))
    prev = lambda which: pl.BlockSpec((None, n_res, tb, WIDTH_B),
                                      lambda i, r, j: (i, r, jnp.maximum(j * per - 1, 0), which))
    out_spec = pl.BlockSpec((None, n_res, tq, WIDTH_B), lambda i, r, j: (i, r, j, 0))
    out = lambda dtype: jax.ShapeDtypeStruct((b, dilation, n, WIDTH_B), dtype)
    return pl.pallas_call(
        _dil_kernel,
        out_shape=(out(BF16), out(F32)),
        grid=(b, dilation // n_res, n // tq),
        in_specs=[cur(0), cur(1), prev(1), cur(2), prev(2)],
        out_specs=(out_spec, out_spec),
        scratch_shapes=[pltpu.VMEM((n_res, per * H_B, tb, 2 * tb), F32)],
        compiler_params=_params(3),
        name="dil",
    )(qkv, qkv, qkv, qkv, qkv)


def _merge_ffn_kernel(x_ref, mod_ref, oa_ref, o0_ref, o1_ref, o2_ref, l0_ref, l1_ref, l2_ref, gates_ref,
                      wua_ref, wub_ref, wo_ref, gpost_ref, gpre2_ref, wg_ref, wu_ref, wd_ref, gpost2_ref,
                      out_ref, acc_ref, *scr):
    tm = x_ref.shape[0]
    tmp = scr[4]

    def token_major(ref, buf):
        dilation = ref.shape[0]
        if dilation == 1:
            return ref[0]
        n_slab = ref.shape[2] // LANES
        for c in range(n_slab):
            if dilation <= 4:
                for r in range(dilation):
                    buf[c, pl.ds(r, tm // dilation, stride=dilation), :] = (
                        ref[r, :, c * LANES:(c + 1) * LANES].astype(F32))
            else:
                quarter = tm // 4
                for r in range(dilation):
                    tmp[c, pl.ds((r % 4) * quarter + r // 4, tm // dilation, stride=dilation // 4), :] = (
                        ref[r, :, c * LANES:(c + 1) * LANES].astype(F32))
                for r4 in range(4):
                    buf[c, pl.ds(r4, quarter, stride=4), :] = tmp[c, r4 * quarter:(r4 + 1) * quarter, :]
        return jnp.concatenate([buf[c] for c in range(n_slab)], axis=1)

    o0, l0 = token_major(o0_ref, None).astype(F32), token_major(l0_ref, None)
    o1, l1 = token_major(o1_ref, scr[0]), token_major(l1_ref, scr[1])
    o2, l2 = token_major(o2_ref, scr[2]), token_major(l2_ref, scr[3])
    m = jnp.maximum(jnp.maximum(l0, l1), l2)
    e0, e1, e2 = jnp.exp2(l0 - m), jnp.exp2(l1 - m), jnp.exp2(l2 - m)
    ob = (e0 * o0 + e1 * o1 + e2 * o2) / (e0 + e1 + e2)
    za = jnp.dot(oa_ref[...], wua_ref[...], preferred_element_type=F32)
    zb = jnp.dot(ob.astype(BF16), wub_ref[...], preferred_element_type=F32)
    d = za.shape[1]
    z = gates_ref[:, 0:d].astype(F32) * za + gates_ref[:, d:2 * d].astype(F32) * zb
    y = jnp.dot(z.astype(BF16), wo_ref[...], preferred_element_type=F32)
    x_mixed = x_ref[...] + mod_ref[5:6, :] * (_rms_rows(y) * gpost_ref[...])
    _ffn_tile(x_mixed, mod_ref, gpre2_ref, wg_ref, wu_ref, wd_ref, gpost2_ref, out_ref, acc_ref, 6)


def _merge_ffn_call(x, mod, oa, dil, gates, w_up_a, w_up_b, w_o, g_post, g_pre2, w_gate, w_up, w_down, g_post2):
    b, s, d = x.shape
    tm = TM_FFN
    rows = lambda width: pl.BlockSpec((None, tm, width), lambda i, j: (i, j, 0))
    const = lambda a: pl.BlockSpec(a.shape, lambda i, j: (0,) * a.ndim, pipeline_mode=pl.Buffered(1))
    (o0, l0), (o1, l1), (o2, l2) = dil
    res = lambda a: pl.BlockSpec((None, a.shape[1], tm // a.shape[1], WIDTH_B), lambda i, j: (i, 0, j, 0))
    consts = [w_up_a, w_up_b, w_o, g_post, g_pre2, w_gate, w_up, w_down, g_post2]
    return pl.pallas_call(
        _merge_ffn_kernel,
        out_shape=jax.ShapeDtypeStruct((b, s, d), F32),
        grid=(b, s // tm),
        in_specs=[rows(d), pl.BlockSpec((None, N_MOD, d), lambda i, j: (i, 0, 0)), rows(WIDTH_A)]
                 + [res(a) for a in (o0, o1, o2, l0, l1, l2)]
                 + [rows(2 * d)] + [const(a) for a in consts],
        out_specs=rows(d),
        scratch_shapes=[pltpu.VMEM((tm, d), F32)] + [pltpu.VMEM((WIDTH_B // LANES, tm, LANES), F32)] * 5,
        compiler_params=_params(2),
        name="merge_ffn",
    )(x, mod, oa, o0, o1, o2, l0, l1, l2, gates, *consts)


def _mixer_weights(w_in, g_cq, g_ckv, w_uq, w_uk, w_uv, w_iq):
    d = w_in.shape[0]
    o_cq, o_ckv, o_kr, o_ki, o_wi = 0, Q_LORA, Q_LORA + KV_LORA, Q_LORA + KV_LORA + D_ROPE_A, \
        Q_LORA + KV_LORA + D_ROPE_A + D_IDX
    o_qkv = o_wi + H_IDX
    o_gates = o_qkv + N_QKV_B

    def swap_halves(w):
        half = w.shape[1] // 2
        return jnp.concatenate([w[:, half:], w[:, :half]], axis=1)

    w_kr, w_ki = w_in[:, o_kr:o_kr + D_ROPE_A], w_in[:, o_ki:o_ki + D_IDX]
    qkv = w_in[:, o_qkv:o_gates].reshape(d, 3, N_GROUPS_B, H_B, 2, D_HEAD_B // 2)
    qk_split = qkv[:, 0:2].transpose(0, 1, 2, 4, 3, 5).reshape(d, 2 * N_GROUPS_B * WIDTH_B)
    v_cols = qkv[:, 2].reshape(d, N_GROUPS_B * WIDTH_B)
    w_std = jnp.concatenate([w_kr, swap_halves(w_kr), w_ki, swap_halves(w_ki),
                             qk_split, v_cols, w_in[:, o_gates:]], axis=1)
    w_tr = jnp.concatenate([w_in[:, o_cq:o_cq + Q_LORA], w_in[:, o_ckv:o_ckv + KV_LORA],
                            w_in[:, o_wi:o_wi + H_IDX]], axis=1).T
    w_tr = jnp.pad(w_tr, ((0, N_TR - w_tr.shape[0]), (0, 0)))

    return {
        "w_std": w_std.astype(BF16),
        "w_tr": w_tr.astype(BF16),
        "g_cq_col": g_cq.reshape(Q_LORA, 1),
        "g_ckv_col": g_ckv.reshape(KV_LORA, 1),
        "w_uq_t": w_uq.transpose(1, 2, 0).reshape(H_A * (D_NOPE + D_ROPE_A), Q_LORA).astype(BF16),
        "w_iq_t": w_iq.transpose(1, 2, 0).reshape(H_IDX * D_IDX, Q_LORA).astype(BF16),
        "w_uk_p": jnp.pad(w_uk, ((0, 0), (0, 0), (0, HEAD_PAD - D_NOPE))).reshape(KV_LORA, H_A * HEAD_PAD).astype(BF16),
        "w_uv_t": w_uv.transpose(1, 2, 0).reshape(WIDTH_A, KV_LORA).astype(BF16),
    }


def kernel(x, c, positions, w_mod, b_mod, g_pre_ffn1, w_gate1, w_up1, w_down1, g_post_ffn1, g_pre_mix, w_in, g_cq, g_ckv, w_uq, w_uk, w_uv, w_iq, w_up_a, w_up_b, w_o, g_post_mix, g_pre_ffn2, w_gate2, w_up2, w_down2, g_post_ffn2):
    b, s, d = x.shape
    assert s % (16 * TB_DIL) == 0 and s % TM_FFN == 0 and b <= 8
    c8 = jnp.pad(c, ((0, 8 - b), (0, 0)))
    half_b, half_a = D_HEAD_B // 2, D_ROPE_A // 2
    inv = jnp.concatenate([ROPE_THETA ** (-jnp.arange(half_b, dtype=F32) / half_b),
                           ROPE_THETA ** (-jnp.arange(half_a, dtype=F32) / half_a)]).reshape(48, 1)
    tabs = _rope_call(positions.reshape(b, 1, s), inv)
    row = lambda g: g.reshape(1, -1)
    for l in range(w_mod.shape[0]):
        mod = _mod_call(c8, w_mod[l], b_mod[l].reshape(1, -1))[:b].reshape(b, N_MOD, d)
        x = _ffn_call(x, mod, row(g_pre_ffn1[l]), w_gate1[l].astype(BF16), w_up1[l].astype(BF16),
                      w_down1[l].astype(BF16), row(g_post_ffn1[l]), 0)
        w = _mixer_weights(w_in[l], g_cq[l], g_ckv[l], w_uq[l], w_uk[l], w_uv[l], w_iq[l])
        qt, qit, wit, k, ki, vt, dq0, dq1, dq2, gates = _proj_call(x, mod, row(g_pre_mix[l]), w, tabs)
        oa = _dsa_call(qt, qit, wit, k, ki, vt)
        dil = [_dil_call(dq) for dq in (dq0, dq1, dq2)]
        x = _merge_ffn_call(x, mod, oa, dil, gates, w_up_a[l].astype(BF16), w_up_b[l].astype(BF16),
                            w_o[l].astype(BF16), row(g_post_mix[l]), row(g_pre_ffn2[l]), w_gate2[l].astype(BF16),
                            w_up2[l].astype(BF16), w_down2[l].astype(BF16), row(g_post_ffn2[l]))
    return x
```

```python
import functools

import numpy as np
import jax
import jax.numpy as jnp
from jax import lax
from jax.experimental import pallas as pl
from jax.experimental.pallas import tpu as pltpu

F32 = jnp.float32
BF16 = jnp.bfloat16

D_FF = 2816
ROPE_THETA = 10000.0
NORM_EPS = 1e-6
H_A = 8
Q_LORA = 384
KV_LORA = 256
D_NOPE = 64
D_ROPE_A = 32
D_V_A = 64
H_IDX = 8
D_IDX = 32
TOPK_MAX = 256
DIL_PAIRS = ((128, 1), (512, 4), (2048, 16))
N_GROUPS_B = 3
H_B = 4
D_HEAD_B = 64
N_MOD = 9
WIDTH_A = H_A * D_V_A
WIDTH_B = H_B * D_HEAD_B
N_QKV_B = 3 * N_GROUPS_B * WIDTH_B

LANES = 128
HEAD_PAD = 128
VMEM_LIMIT = 56 * 1024 * 1024

TM_FFN = 512
CH_FFN = 256
TP_PROJ = 512
TQ_DSA = 512
TK_DSA = 256
V_ROWS = D_V_A + 16
TB_DIL = 128
TQ_DIL = 512
BISECT_MAX = 34
BISECT_CHECK = 2
FOLD_ROWS = 16
COUNT_ROWS = 64
IDX_ROWS = 64
NEG_BIG = -1e30
LOG2_E = 1.4426950408889634

C_KR = 0
C_QKVB = 128
C_GATES = C_QKVB + N_QKV_B
N_STD = C_GATES + 2048
R_CQ = 0
R_CKV = Q_LORA
R_WI = Q_LORA + KV_LORA
N_TR = 656


def _params(n_axes):
    return pltpu.CompilerParams(dimension_semantics=("arbitrary",) * n_axes,
                                vmem_limit_bytes=VMEM_LIMIT)


def _rms_rows(x):
    return x * lax.rsqrt(jnp.mean(x * x, axis=-1, keepdims=True) + NORM_EPS)


def _mod_kernel(c_ref, w_ref, b_ref, o_ref):
    c = c_ref[...]
    o_ref[...] = jnp.dot(c * jax.nn.sigmoid(c), w_ref[...], preferred_element_type=F32) + b_ref[...]


def _mod_call(c8, w_mod, b_mod):
    d = c8.shape[1]
    return pl.pallas_call(
        _mod_kernel,
        out_shape=jax.ShapeDtypeStruct((8, N_MOD * d), F32),
        grid=(N_MOD,),
        in_specs=[pl.BlockSpec((8, d), lambda j: (0, 0)),
                  pl.BlockSpec((d, d), lambda j: (0, j)),
                  pl.BlockSpec((1, d), lambda j: (0, j))],
        out_specs=pl.BlockSpec((8, d), lambda j: (0, j)),
        compiler_params=_params(1),
        name="mod",
    )(c8, w_mod, b_mod)


def _rope_kernel(pos_ref, inv_ref, tt_ref, ca_ref, sa_ref, cb_ref, sb_ref):
    s = pos_ref.shape[1]
    ang = inv_ref[...] * pos_ref[...].astype(F32)
    cos = jnp.cos(ang)
    sin = jnp.sin(ang)
    cos_b, sin_b = cos[32:48], sin[32:48]
    tt_ref[0] = cos_b
    tt_ref[1] = sin_b
    cos_a = jnp.concatenate([cos[0:32]] * 4, axis=0)
    sin_a = jnp.concatenate([sin[0:32]] * 4, axis=0)
    cos_bt = jnp.concatenate([cos_b] * 8, axis=0)
    sin_bt = jnp.concatenate([-sin_b, sin_b] * 4, axis=0)
    for j in range(s // LANES):
        cols = slice(j * LANES, (j + 1) * LANES)
        ca_ref[cols, :] = cos_a[:, cols].T
        sa_ref[cols, :] = sin_a[:, cols].T
        cb_ref[cols, :] = cos_bt[:, cols].T
        sb_ref[cols, :] = sin_bt[:, cols].T


def _mod_rope_kernel(c_ref, w_ref, b_ref, pos_ref, inv_ref, o_ref, tt_ref, ca_ref, sa_ref, cb_ref, sb_ref, *,
                     n_batch):
    _mod_kernel(c_ref, w_ref, b_ref, o_ref)

    @pl.when(pl.program_id(0) < n_batch)
    def _tables():
        _rope_kernel(pos_ref, inv_ref, tt_ref, ca_ref, sa_ref, cb_ref, sb_ref)


def _mod_rope_call(c8, w_mod, b_mod, pos3, inv):
    d = c8.shape[1]
    b, _, s = pos3.shape
    assert b <= N_MOD
    row = lambda j: jnp.minimum(j, b - 1)
    tab = jax.ShapeDtypeStruct((b, s, LANES), F32)
    tab_spec = pl.BlockSpec((None, s, LANES), lambda j: (row(j), 0, 0))
    return pl.pallas_call(
        functools.partial(_mod_rope_kernel, n_batch=b),
        out_shape=(jax.ShapeDtypeStruct((8, N_MOD * d), F32),
                   jax.ShapeDtypeStruct((b, 2, 16, s), F32), tab, tab, tab, tab),
        grid=(N_MOD,),
        in_specs=[pl.BlockSpec((8, d), lambda j: (0, 0)),
                  pl.BlockSpec((d, d), lambda j: (0, j)),
                  pl.BlockSpec((1, d), lambda j: (0, j)),
                  pl.BlockSpec((None, 1, s), lambda j: (row(j), 0, 0)),
                  pl.BlockSpec((48, 1), lambda j: (0, 0))],
        out_specs=(pl.BlockSpec((8, d), lambda j: (0, j)),
                   pl.BlockSpec((None, 2, 16, s), lambda j: (row(j), 0, 0, 0)),
                   tab_spec, tab_spec, tab_spec, tab_spec),
        compiler_params=_params(1),
        name="mod_rope",
    )(c8, w_mod, b_mod, pos3, inv)


def _ffn_kernel(x_ref, mod_ref, gpre_ref, wg_ref, wu_ref, wd_ref, gpost_ref, o_ref, acc_ref, *, j0):
    _ffn_tile(x_ref[...], mod_ref, gpre_ref, wg_ref, wu_ref, wd_ref, gpost_ref, o_ref, acc_ref, j0)


def _ffn_tile(x, mod_ref, gpre_ref, wg_ref, wu_ref, wd_ref, gpost_ref, o_ref, acc_ref, j0):
    sh, sc, gt = mod_ref[j0:j0 + 1, :], mod_ref[j0 + 1:j0 + 2, :], mod_ref[j0 + 2:j0 + 3, :]
    h = (_rms_rows(x) * gpre_ref[...] * (1.0 + sc) + sh).astype(BF16)
    n_ch = wg_ref.shape[1] // CH_FFN
    for ch in range(n_ch):
        cols = slice(ch * CH_FFN, (ch + 1) * CH_FFN)
        g = jnp.dot(h, wg_ref[:, cols], preferred_element_type=F32)
        u = jnp.dot(h, wu_ref[:, cols], preferred_element_type=F32)
        a = (g * jax.nn.sigmoid(g) * u).astype(BF16)
        part = jnp.dot(a, wd_ref[cols, :], preferred_element_type=F32)
        if ch == 0:
            acc_ref[...] = part
        else:
            acc_ref[...] += part
    o_ref[...] = x + 0.5 * gt * (_rms_rows(acc_ref[...]) * gpost_ref[...])


def _ffn_call(x, mod, g_pre, w_gate, w_up, w_down, g_post, j0):
    b, s, d = x.shape
    f = w_gate.shape[1]
    const = lambda shape: pl.BlockSpec(shape, lambda i, j: (0,) * len(shape), pipeline_mode=pl.Buffered(1))
    tile = pl.BlockSpec((None, TM_FFN, d), lambda i, j: (i, j, 0))
    return pl.pallas_call(
        functools.partial(_ffn_kernel, j0=j0),
        out_shape=jax.ShapeDtypeStruct((b, s, d), F32),
        grid=(b, s // TM_FFN),
        in_specs=[tile,
                  pl.BlockSpec((None, N_MOD, d), lambda i, j: (i, 0, 0)),
                  const((1, d)), const((d, f)), const((d, f)), const((f, d)), const((1, d))],
        out_specs=tile,
        scratch_shapes=[pltpu.VMEM((TM_FFN, d), F32)],
        compiler_params=_params(2),
        name="ffn",
    )(x, mod, g_pre, w_gate, w_up, w_down, g_post)


def _proj_kernel(x_ref, mod_ref, gpre_ref, wstd_ref, wtr_ref, gcq_ref, gckv_col_ref,
                 wuq_ref, wiq_ref, wuk_ref, wuv_ref,
                 tt_ref, ca_ref, sa_ref, cb_ref, sb_ref,
                 qt_ref, qit_ref, wit_ref, k_ref, ki_ref, vt_ref, dq0_ref, dq1_ref, dq2_ref, gates_ref,
                 dil_scr, dil_tmp):
    x = x_ref[...]
    sh, sc = mod_ref[3:4, :], mod_ref[4:5, :]
    u = (_rms_rows(x) * gpre_ref[...] * (1.0 + sc) + sh).astype(BF16)
    std = jnp.dot(u, wstd_ref[...], preferred_element_type=F32)
    tr = lax.dot_general(wtr_ref[...], u, (((1,), (1,)), ((), ())),
                         preferred_element_type=F32)

    cq = tr[R_CQ:R_CQ + Q_LORA]
    cq = cq * lax.rsqrt(jnp.mean(cq * cq, axis=0, keepdims=True) + NORM_EPS) * gcq_ref[...]
    cq = cq.astype(BF16)
    ckv_t = tr[R_CKV:R_CKV + KV_LORA]
    ckv_t = ckv_t * lax.rsqrt(jnp.mean(ckv_t * ckv_t, axis=0, keepdims=True) + NORM_EPS) * gckv_col_ref[...]
    cos_t, sin_t = tt_ref[0], tt_ref[1]

    attn_scale = (D_NOPE + D_ROPE_A) ** -0.5 * LOG2_E
    q_t = jnp.dot(wuq_ref[...], cq, preferred_element_type=F32) * attn_scale
    idx_scale = D_IDX ** -0.5
    qi_t = jnp.dot(wiq_ref[...], cq, preferred_element_type=F32) * idx_scale
    d_qk = D_NOPE + D_ROPE_A
    for h in range(H_A):
        r0, s0 = h * HEAD_PAD, h * d_qk
        x1, x2 = q_t[s0 + 64:s0 + 80], q_t[s0 + 80:s0 + 96]
        qt_ref[r0:r0 + 64, :] = q_t[s0:s0 + 64].astype(BF16)
        qt_ref[r0 + 64:r0 + 80, :] = (x1 * cos_t - x2 * sin_t).astype(BF16)
        qt_ref[r0 + 80:r0 + 96, :] = (x2 * cos_t + x1 * sin_t).astype(BF16)
        qt_ref[r0 + 96:r0 + 128, :] = jnp.zeros((HEAD_PAD - d_qk, q_t.shape[1]), BF16)
        i0 = h * D_IDX
        y1, y2 = qi_t[i0:i0 + 16], qi_t[i0 + 16:i0 + 32]
        qit_ref[i0:i0 + 16, :] = (y1 * cos_t - y2 * sin_t).astype(BF16)
        qit_ref[i0 + 16:i0 + 32, :] = (y2 * cos_t + y1 * sin_t).astype(BF16)
    wit_ref[...] = tr[R_WI:R_WI + H_IDX] * (H_IDX ** -0.5)
    v_t = jnp.dot(wuv_ref[...], ckv_t.astype(BF16), preferred_element_type=F32).astype(BF16)
    for h in range(H_A):
        vt_ref[h * V_ROWS:h * V_ROWS + D_V_A, :] = v_t[h * D_V_A:(h + 1) * D_V_A]
        vt_ref[h * V_ROWS + D_V_A:(h + 1) * V_ROWS, :] = jnp.ones((V_ROWS - D_V_A, v_t.shape[1]), BF16)

    ckv = ckv_t.T.astype(BF16)
    k_nope = jnp.dot(ckv, wuk_ref[...], preferred_element_type=F32)
    t = std[:, C_KR:C_KR + LANES]
    lane = lax.broadcasted_iota(jnp.int32, (1, LANES), 1)
    unswapped = jnp.where((lane // 32) % 2 == 0, 1.0, 0.0)
    roped = (t * cb_ref[...] + pltpu.roll(t, LANES - 32, axis=1) * sb_ref[...]) * unswapped
    moved = pltpu.roll(roped, 64, axis=1)
    k_rope_tile = jnp.where(lane >= 64, moved, 0.0)
    for h in range(H_A):
        k_ref[:, h * HEAD_PAD:(h + 1) * HEAD_PAD] = (k_nope[:, h * HEAD_PAD:(h + 1) * HEAD_PAD]
                                                      + k_rope_tile).astype(BF16)
    ki_ref[...] = jnp.where(lane < 32, moved, 0.0).astype(BF16)

    cos_a, sin_a = ca_ref[...], sa_ref[...]
    tp = x.shape[0]
    for g, (out_ref, (_, dilation)) in enumerate(zip((dq0_ref, dq1_ref, dq2_ref), DIL_PAIRS)):
        for which in range(2):
            scale = (D_HEAD_B ** -0.5 * LOG2_E) if which == 0 else 1.0
            c0 = C_QKVB + (which * N_GROUPS_B + g) * WIDTH_B
            x1, x2 = std[:, c0:c0 + 128], std[:, c0 + 128:c0 + 256]
            dil_scr[2 * which] = (x1 * cos_a - x2 * sin_a) * scale
            dil_scr[2 * which + 1] = (x2 * cos_a + x1 * sin_a) * scale
        c0 = C_QKVB + (2 * N_GROUPS_B + g) * WIDTH_B
        dil_scr[4] = std[:, c0:c0 + 128]
        dil_scr[5] = std[:, c0 + 128:c0 + 256]
        for c in range(3 * WIDTH_B // LANES):
            if dilation <= 4:
                for r in range(dilation):
                    out_ref[r, :, c * LANES:(c + 1) * LANES] = (
                        dil_scr[c, pl.ds(r, tp // dilation, stride=dilation), :].astype(BF16))
            else:
                quarter = tp // 4
                for r4 in range(4):
                    dil_tmp[c, r4 * quarter:(r4 + 1) * quarter, :] = dil_scr[c, pl.ds(r4, quarter, stride=4), :]
                for r in range(dilation):
                    out_ref[r, :, c * LANES:(c + 1) * LANES] = dil_tmp[
                        c, pl.ds((r % 4) * quarter + r // 4, tp // dilation, stride=dilation // 4), :].astype(BF16)
    gates_ref[...] = jax.nn.sigmoid(std[:, C_GATES:]).astype(BF16)


def _proj_call(x, mod, g_pre, w, tabs):
    b, s, d = x.shape
    tt, ca, sa, cb, sb = tabs
    tp = TP_PROJ
    const = lambda a: pl.BlockSpec(a.shape, lambda i, j: (0,) * a.ndim)
    rows = lambda width: pl.BlockSpec((None, tp, width), lambda i, j: (i, j, 0))
    colsT = lambda height: pl.BlockSpec((None, height, tp), lambda i, j: (i, 0, j))
    consts = [g_pre, w["w_std"], w["w_tr"], w["g_cq_col"], w["g_ckv_col"],
              w["w_uq_t"], w["w_iq_t"], w["w_uk_p"], w["w_uv_t"]]
    hq = H_A * HEAD_PAD
    dil_shapes = [jax.ShapeDtypeStruct((b, dil, s // dil, 3 * WIDTH_B), BF16) for _, dil in DIL_PAIRS]
    dil_specs = [pl.BlockSpec((None, dil, tp // dil, 3 * WIDTH_B), lambda i, j: (i, 0, j, 0))
                 for _, dil in DIL_PAIRS]
    return pl.pallas_call(
        _proj_kernel,
        out_shape=(jax.ShapeDtypeStruct((b, hq, s), BF16),
                   jax.ShapeDtypeStruct((b, H_IDX * D_IDX, s), BF16),
                   jax.ShapeDtypeStruct((b, H_IDX, s), F32),
                   jax.ShapeDtypeStruct((b, s, hq), BF16),
                   jax.ShapeDtypeStruct((b, s, LANES), BF16),
                   jax.ShapeDtypeStruct((b, H_A * V_ROWS, s), BF16),
                   *dil_shapes,
                   jax.ShapeDtypeStruct((b, s, 2048), BF16)),
        grid=(b, s // tp),
        in_specs=[rows(d), pl.BlockSpec((None, N_MOD, d), lambda i, j: (i, 0, 0))]
                 + [const(a) for a in consts]
                 + [pl.BlockSpec((None, 2, 16, tp), lambda i, j: (i, 0, 0, j)),
                    rows(LANES), rows(LANES), rows(LANES), rows(LANES)],
        out_specs=(colsT(hq), colsT(H_IDX * D_IDX), colsT(H_IDX), rows(hq), rows(LANES), colsT(H_A * V_ROWS),
                   *dil_specs, rows(2048)),
        scratch_shapes=[pltpu.VMEM((3 * WIDTH_B // LANES, tp, LANES), F32)] * 2,
        compiler_params=_params(2),
        name="proj",
    )(x, mod, *consts, tt, ca, sa, cb, sb)


def _dsa_kernel(qit_ref, wit_ref, qt_ref, ki_ref, k_ref, vt_ref, o_ref,
                sc_ref, m_ref, acc_ref, s_ref, s2_ref, smax_ref, smax2_ref, qpad_ref, *, topk):
    tq = qt_ref.shape[1]
    tk = s_ref.shape[1]
    i = pl.program_id(1)
    nkb = (i + 1) * (tq // tk)
    npair = (nkb + 1) // 2
    qpos = i * tq + lax.broadcasted_iota(jnp.int32, (1, tq), 1)

    def fold(v):
        return jnp.sum(v.reshape(v.shape[0] // FOLD_ROWS, FOLD_ROWS, tq), axis=0)

    qpad_ref[...] = jnp.zeros(qpad_ref.shape, BF16)
    for h in range(H_IDX):
        qpad_ref[h * HEAD_PAD:h * HEAD_PAD + D_IDX, :] = qit_ref[h * D_IDX:(h + 1) * D_IDX, :]

    def park_logits(kb, park_ref, heads=range(H_IDX)):
        ki = ki_ref[pl.ds(pl.multiple_of(kb * tk, tk), tk), :]
        for h in heads:
            park_ref[h] = jnp.dot(ki, qpad_ref[h * HEAD_PAD:(h + 1) * HEAD_PAD, :], preferred_element_type=F32)

    def rows8(v, op):
        return op(v.reshape(IDX_ROWS // 8, 8, LANES), axis=0)

    piece_iota = lax.broadcasted_iota(jnp.int32, (IDX_ROWS, LANES), 0)
    n_col = tq // LANES
    heads_per_col = H_IDX // n_col

    def reduce_block(kb, park_ref, stats, on_diagonal, ahead=None):
        k0 = pl.multiple_of(kb * tk, tk)
        columns = []
        for c in range(n_col):
            if ahead is not None:
                park_logits(ahead[0], ahead[1], range(c * heads_per_col, (c + 1) * heads_per_col))
            lanes = slice(c * LANES, (c + 1) * LANES)
            smax, smin, c_pos, c_nn = [v[:, lanes] for v in stats]
            w = [wit_ref[h:h + 1, lanes] for h in range(H_IDX)]
            for j in range(tk // IDX_ROWS):
                r0 = j * IDX_ROWS
                acc = jnp.zeros((IDX_ROWS, LANES), F32)
                for h in range(H_IDX):
                    acc = acc + w[h] * jnp.maximum(park_ref[h, r0:r0 + IDX_ROWS, lanes], 0.0)
                if on_diagonal:
                    causal = (k0 + r0 + piece_iota) <= qpos[:, lanes]
                    sv = jnp.where(causal, acc, -jnp.inf)
                    smin = jnp.minimum(smin, rows8(jnp.where(causal, acc, jnp.inf), jnp.min))
                else:
                    sv = acc
                    smin = jnp.minimum(smin, rows8(acc, jnp.min))
                sc_ref[pl.ds(k0 + r0, IDX_ROWS), lanes] = sv
                smax = jnp.maximum(smax, rows8(sv, jnp.max))
                c_pos = c_pos + rows8(jnp.where(sv > 0.0, 1.0, 0.0), jnp.sum)
                c_nn = c_nn + rows8(jnp.where(sv >= 0.0, 1.0, 0.0), jnp.sum)
            columns.append((smax, smin, c_pos, c_nn))
        return tuple(jnp.concatenate(parts, axis=1) for parts in zip(*columns))

    def idx_pair(kp, stats):
        stats = reduce_block(2 * kp, s_ref, stats, False, ahead=(2 * kp + 1, s2_ref))
        return reduce_block(2 * kp + 1, s2_ref, stats, False, ahead=(2 * kp + 2, s_ref))

    park_logits(0, s_ref)
    zero8 = jnp.zeros((8, tq), F32)
    stats = lax.fori_loop(0, i, idx_pair,
                          (jnp.full((8, tq), -jnp.inf, F32), jnp.full((8, tq), jnp.inf, F32), zero8, zero8))
    stats = reduce_block(nkb - 2, s_ref, stats, True, ahead=(nkb - 1, s2_ref))
    stats = reduce_block(nkb - 1, s2_ref, stats, True)
    smax = jnp.max(stats[0], axis=0, keepdims=True)
    smin = jnp.min(stats[1], axis=0, keepdims=True)
    c_pos, c_nn = [jnp.sum(v, axis=0, keepdims=True) for v in stats[2:]]

    @pl.when(nkb % 2 == 1)
    def _pad_block():
        sc_ref[pl.ds(pl.multiple_of(nkb * tk, tk), tk), :] = jnp.full((tk, tq), -jnp.inf, F32)

    n_causal = (qpos + 1).astype(F32)
    kf = jnp.minimum(qpos + 1, topk).astype(F32)

    def count_gt(thr):
        def body(kp, c):
            k0 = pl.multiple_of(kp * (2 * tk), 2 * tk)
            for j in range(2 * tk // COUNT_ROWS):
                rows = pl.ds(k0 + j * COUNT_ROWS, COUNT_ROWS)
                c = c + fold(jnp.where(sc_ref[rows, :] > thr, 1.0, 0.0))
            return c
        part = lax.fori_loop(0, npair, body, jnp.zeros((FOLD_ROWS, tq), F32))
        return jnp.sum(part, axis=0, keepdims=True)

    below = smin - (1.0 + jnp.abs(smin))
    zero_tie = jnp.logical_and(c_pos < kf, kf <= c_nn)
    positive = c_pos >= kf
    searching = jnp.where(zero_tie, 0.0, 1.0)

    def bisect_cond(carry):
        return jnp.logical_and(carry[0] < BISECT_MAX, carry[1] > 0)

    def bisect_body(carry):
        it, _, lo, hi, c_lo, c_hi = carry
        for _ in range(BISECT_CHECK):
            mid = 0.5 * (lo + hi)
            c_mid = count_gt(mid)
            ge = c_mid >= kf
            up = jnp.logical_and(ge, searching > 0.0)
            down = jnp.logical_and(jnp.logical_not(ge), searching > 0.0)
            lo, c_lo = jnp.where(up, mid, lo), jnp.where(up, c_mid, c_lo)
            hi, c_hi = jnp.where(down, mid, hi), jnp.where(down, c_mid, c_hi)
        open_queries = (jnp.max((c_lo - kf) * searching) > 0.0).astype(jnp.int32)
        return it + BISECT_CHECK, open_queries, lo, hi, c_lo, c_hi

    lo0 = jnp.where(positive, 0.0, jnp.where(zero_tie, 0.0, below))
    hi0 = jnp.where(positive, smax, 0.0)
    c_lo0 = jnp.where(positive, c_pos, jnp.where(zero_tie, c_nn, n_causal))
    c_hi0 = jnp.where(positive, 0.0, c_pos)
    first_open = (jnp.max((c_lo0 - kf) * searching) > 0.0).astype(jnp.int32)
    _, _, lo, hi, c_lo, c_hi = lax.while_loop(
        bisect_cond, bisect_body, (jnp.int32(0), first_open, lo0, hi0, c_lo0, c_hi0))

    @pl.when(jnp.max(jnp.maximum(c_lo - kf, 1.0 - searching)) > 0.0)
    def _break_ties():
        need = kf - c_hi
        closed = 1.0 - searching
        prefix = jnp.where(lax.broadcasted_iota(jnp.int32, (tk, tk), 0)
                           >= lax.broadcasted_iota(jnp.int32, (tk, tk), 1), 1.0, 0.0).astype(BF16)

        def body(kp, seen):
            blocks = []
            for j in range(2):
                k0 = pl.multiple_of((2 * kp + j) * tk, tk)
                sv = sc_ref[pl.ds(k0, tk), :]
                above_lo = jnp.where(sv > lo, 1.0, jnp.where(sv >= lo, closed, 0.0))
                member = jnp.where(sv > hi, 0.0, above_lo)
                blocks.append((k0, member, jnp.dot(prefix, member.astype(BF16), preferred_element_type=F32)))
            for k0, member, within in blocks:
                rank = within + seen
                sv = sc_ref[pl.ds(k0, tk), :]
                sc_ref[pl.ds(k0, tk), :] = jnp.where(member > 0.0, jnp.where(rank > need, -jnp.inf, jnp.inf), sv)
                seen = rank[tk - 1:tk, :]
            return seen

        lax.fori_loop(0, npair, body, jnp.zeros((1, tq), F32))

    m_ref[...] = jnp.full(m_ref.shape, NEG_BIG, F32)
    acc_ref[...] = jnp.zeros(acc_ref.shape, F32)

    def selection_bias(kb):
        return jnp.where(sc_ref[pl.ds(pl.multiple_of(kb * tk, tk), tk), :] > lo, 0.0, NEG_BIG)

    def score_head(kb, h, bias, park_ref, max_ref):
        kh = k_ref[pl.ds(pl.multiple_of(kb * tk, tk), tk), h * HEAD_PAD:(h + 1) * HEAD_PAD]
        s = jnp.dot(kh, qt_ref[h * HEAD_PAD:(h + 1) * HEAD_PAD, :], preferred_element_type=F32) + bias
        park_ref[h] = s
        max_ref[h:h + 1, :] = jnp.max(s, axis=0, keepdims=True)

    def softmax_head(kb, h, park_ref, max_ref):
        m_old = m_ref[h:h + 1, :]
        m_new = jnp.maximum(m_old, max_ref[h:h + 1, :])
        alpha = jnp.exp2(m_old - m_new)
        p = jnp.exp2((park_ref[h] - m_new).astype(BF16))
        rows = slice(h * V_ROWS, (h + 1) * V_ROWS)
        pv = jnp.dot(vt_ref[rows, pl.ds(pl.multiple_of(kb * tk, tk), tk)], p, preferred_element_type=F32)
        acc_ref[rows, :] = alpha * acc_ref[rows, :] + pv
        m_ref[h:h + 1, :] = m_new

    last_block = 2 * npair - 1
    bias0 = selection_bias(0)
    for h in range(H_A):
        score_head(0, h, bias0, s_ref, smax_ref)

    def att_body(kp, carry):
        odd, nxt = 2 * kp + 1, jnp.minimum(2 * kp + 2, last_block)
        bias = selection_bias(odd)
        for h in range(H_A):
            score_head(odd, h, bias, s2_ref, smax2_ref)
            softmax_head(2 * kp, h, s_ref, smax_ref)
        bias = selection_bias(nxt)
        for h in range(H_A):
            score_head(nxt, h, bias, s_ref, smax_ref)
            softmax_head(odd, h, s2_ref, smax2_ref)
        return carry

    lax.fori_loop(0, npair, att_body, 0)
    out = [acc_ref[h * V_ROWS:h * V_ROWS + D_V_A, :] / acc_ref[h * V_ROWS + D_V_A:h * V_ROWS + D_V_A + 1, :]
           for h in range(H_A)]
    o_ref[...] = jnp.concatenate(out, axis=0).T.astype(BF16)


def _dsa_call(qt, qit, wit, k, ki, vt):
    b, hq, s = qt.shape
    tq = TQ_DSA
    topk = min(TOPK_MAX, s // 4)
    colsT = lambda height: pl.BlockSpec((None, height, tq), lambda i, j: (i, 0, j))
    whole = lambda a: pl.BlockSpec((None,) + a.shape[1:], lambda i, j: (i, 0, 0))
    return pl.pallas_call(
        functools.partial(_dsa_kernel, topk=topk),
        out_shape=jax.ShapeDtypeStruct((b, s, WIDTH_A), BF16),
        grid=(b, s // tq),
        in_specs=[colsT(H_IDX * D_IDX), colsT(H_IDX), colsT(hq), whole(ki), whole(k), whole(vt)],
        out_specs=pl.BlockSpec((None, tq, WIDTH_A), lambda i, j: (i, j, 0)),
        scratch_shapes=[pltpu.VMEM((s, tq), F32),
                        pltpu.VMEM((H_A, tq), F32),
                        pltpu.VMEM((H_A * V_ROWS, tq), F32),
                        pltpu.VMEM((H_A, TK_DSA, tq), F32),
                        pltpu.VMEM((H_A, TK_DSA, tq), F32),
                        pltpu.VMEM((H_A, tq), F32),
                        pltpu.VMEM((H_A, tq), F32),
                        pltpu.VMEM((H_IDX * HEAD_PAD, tq), BF16)],
        compiler_params=_params(2),
        name="dsa",
    )(qit, wit, qt, ki, k, vt)


def _dil_kernel(q_ref, kc_ref, kp_ref, vc_ref, vp_ref, o_ref, lse_ref, s_ref):
    for res in range(q_ref.shape[0]):
        _dil_residue(q_ref.at[res], kc_ref.at[res], kp_ref.at[res], vc_ref.at[res], vp_ref.at[res],
                     o_ref.at[res], lse_ref.at[res], s_ref.at[res])


def _dil_residue(q_ref, kc_ref, kp_ref, vc_ref, vp_ref, o_ref, lse_ref, s_ref):
    tb = kp_ref.shape[0]
    n_sub = q_ref.shape[0] // tb
    first_step = pl.program_id(2) == 0
    k = jnp.concatenate([kp_ref[...], kc_ref[...]], axis=0)
    v = jnp.concatenate([vp_ref[...], vc_ref[...]], axis=0)
    r = lax.broadcasted_iota(jnp.int32, (tb, 2 * tb), 0)
    c = lax.broadcasted_iota(jnp.int32, (tb, 2 * tb), 1)
    behind = (tb + r - c).astype(jnp.uint32)
    bias = jnp.where(behind <= tb, 0.0, -jnp.inf)
    no_prev = jnp.where(first_step, r, tb).astype(jnp.uint32)
    bias0 = jnp.where(behind <= no_prev, 0.0, -jnp.inf)
    lane = lax.broadcasted_iota(jnp.int32, (1, WIDTH_B), 1)
    head_qk = (lane % 128) // (D_HEAD_B // 2)
    head_v = lane // D_HEAD_B
    nt = (((1,), (1,)), ((), ()))
    for j in range(n_sub):
        q = q_ref[j * tb:(j + 1) * tb, :]
        for h in range(H_B):
            qh = jnp.where(head_qk == h, q, jnp.zeros_like(q))
            s_ref[j * H_B + h] = (lax.dot_general(qh, k[j * tb:(j + 2) * tb], nt, preferred_element_type=F32)
                                  + (bias0 if j == 0 else bias))
    for j in range(n_sub):
        vj = v[j * tb:(j + 2) * tb]
        num = jnp.zeros((tb, WIDTH_B), F32)
        den = jnp.zeros((tb, WIDTH_B), F32)
        top = jnp.zeros((tb, WIDTH_B), F32)
        for h in range(H_B):
            s = s_ref[j * H_B + h]
            m = jnp.max(s, axis=1, keepdims=True)
            p = jnp.exp2(s - m).astype(BF16)
            pv = jnp.dot(p, jnp.where(head_v == h, vj, jnp.ones_like(vj)), preferred_element_type=F32)
            mine = head_v == h
            num = jnp.where(mine, pv, num)
            den = jnp.where(mine, pltpu.roll(pv, WIDTH_B // 2, axis=1), den)
            top = jnp.where(mine, m, top)
        o_ref[j * tb:(j + 1) * tb, :] = (num / den).astype(o_ref.dtype)
        lse_ref[j * tb:(j + 1) * tb, :] = top + jnp.log2(den)


def _dil_call(qkv):
    b, dilation, n, _ = qkv.shape
    tb, tq = TB_DIL, min(TQ_DIL, n)
    per = tq // tb
    n_res = max(1, min(dilation, TQ_DIL // tq))
    cur = lambda which: pl.BlockSpec((None, n_res, tq, WIDTH_B), lambda i, r, j: (i, r, j, which))
    prev = lambda which: pl.BlockSpec((None, n_res, tb, WIDTH_B),
                                      lambda i, r, j: (i, r, jnp.maximum(j * per - 1, 0), which))
    out_spec = pl.BlockSpec((None, n_res, tq, WIDTH_B), lambda i, r, j: (i, r, j, 0))
    out = lambda dtype: jax.ShapeDtypeStruct((b, dilation, n, WIDTH_B), dtype)
    return pl.pallas_call(
        _dil_kernel,
        out_shape=(out(BF16), out(F32)),
        grid=(b, dilation // n_res, n // tq),
        in_specs=[cur(0), cur(1), prev(1), cur(2), prev(2)],
        out_specs=(out_spec, out_spec),
        scratch_shapes=[pltpu.VMEM((n_res, per * H_B, tb, 2 * tb), F32)],
        compiler_params=_params(3),
        name="dil",
    )(qkv, qkv, qkv, qkv, qkv)


def _merge_ffn_kernel(x_ref, mod_ref, oa_ref, o0_ref, o1_ref, o2_ref, l0_ref, l1_ref, l2_ref, gates_ref,
                      wua_ref, wub_ref, wo_ref, gpost_ref, gpre2_ref, wg_ref, wu_ref, wd_ref, gpost2_ref,
                      out_ref, acc_ref, *scr):
    tm = x_ref.shape[0]
    tmp = scr[4]

    def token_major(ref, buf):
        dilation = ref.shape[0]
        if dilation == 1:
            return ref[0]
        n_slab = ref.shape[2] // LANES
        for c in range(n_slab):
            if dilation <= 4:
                for r in range(dilation):
                    buf[c, pl.ds(r, tm // dilation, stride=dilation), :] = (
                        ref[r, :, c * LANES:(c + 1) * LANES].astype(F32))
            else:
                quarter = tm // 4
                for r in range(dilation):
                    tmp[c, pl.ds((r % 4) * quarter + r // 4, tm // dilation, stride=dilation // 4), :] = (
                        ref[r, :, c * LANES:(c + 1) * LANES].astype(F32))
                for r4 in range(4):
                    buf[c, pl.ds(r4, quarter, stride=4), :] = tmp[c, r4 * quarter:(r4 + 1) * quarter, :]
        return jnp.concatenate([buf[c] for c in range(n_slab)], axis=1)

    o0, l0 = token_major(o0_ref, None).astype(F32), token_major(l0_ref, None)
    o1, l1 = token_major(o1_ref, scr[0]), token_major(l1_ref, scr[1])
    o2, l2 = token_major(o2_ref, scr[2]), token_major(l2_ref, scr[3])
    m = jnp.maximum(jnp.maximum(l0, l1), l2)
    e0, e1, e2 = jnp.exp2(l0 - m), jnp.exp2(l1 - m), jnp.exp2(l2 - m)
    ob = (e0 * o0 + e1 * o1 + e2 * o2) / (e0 + e1 + e2)
    za = jnp.dot(oa_ref[...], wua_ref[...], preferred_element_type=F32)
    zb = jnp.dot(ob.astype(BF16), wub_ref[...], preferred_element_type=F32)
    d = za.shape[1]
    z = gates_ref[:, 0:d].astype(F32) * za + gates_ref[:, d:2 * d].astype(F32) * zb
    y = jnp.dot(z.astype(BF16), wo_ref[...], preferred_element_type=F32)
    x_mixed = x_ref[...] + mod_ref[5:6, :] * (_rms_rows(y) * gpost_ref[...])
    _ffn_tile(x_mixed, mod_ref, gpre2_ref, wg_ref, wu_ref, wd_ref, gpost2_ref, out_ref, acc_ref, 6)


def _merge_ffn_call(x, mod, oa, dil, gates, w_up_a, w_up_b, w_o, g_post, g_pre2, w_gate, w_up, w_down, g_post2):
    b, s, d = x.shape
    tm = TM_FFN
    rows = lambda width: pl.BlockSpec((None, tm, width), lambda i, j: (i, j, 0))
    const = lambda a: pl.BlockSpec(a.shape, lambda i, j: (0,) * a.ndim, pipeline_mode=pl.Buffered(1))
    (o0, l0), (o1, l1), (o2, l2) = dil
    res = lambda a: pl.BlockSpec((None, a.shape[1], tm // a.shape[1], WIDTH_B), lambda i, j: (i, 0, j, 0))
    consts = [w_up_a, w_up_b, w_o, g_post, g_pre2, w_gate, w_up, w_down, g_post2]
    return pl.pallas_call(
        _merge_ffn_kernel,
        out_shape=jax.ShapeDtypeStruct((b, s, d), F32),
        grid=(b, s // tm),
        in_specs=[rows(d), pl.BlockSpec((None, N_MOD, d), lambda i, j: (i, 0, 0)), rows(WIDTH_A)]
                 + [res(a) for a in (o0, o1, o2, l0, l1, l2)]
                 + [rows(2 * d)] + [const(a) for a in consts],
        out_specs=rows(d),
        scratch_shapes=[pltpu.VMEM((tm, d), F32)] + [pltpu.VMEM((WIDTH_B // LANES, tm, LANES), F32)] * 5,
        compiler_params=_params(2),
        name="merge_ffn",
    )(x, mod, oa, o0, o1, o2, l0, l1, l2, gates, *consts)


def _mixer_weights(w_in, g_cq, g_ckv, w_uq, w_uk, w_uv, w_iq):
    d = w_in.shape[0]
    w_in = w_in.astype(BF16)
    o_cq, o_ckv, o_kr, o_ki, o_wi = 0, Q_LORA, Q_LORA + KV_LORA, Q_LORA + KV_LORA + D_ROPE_A, \
        Q_LORA + KV_LORA + D_ROPE_A + D_IDX
    o_qkv = o_wi + H_IDX
    o_gates = o_qkv + N_QKV_B

    def swap_halves(w):
        half = w.shape[1] // 2
        return jnp.concatenate([w[:, half:], w[:, :half]], axis=1)

    w_kr, w_ki = w_in[:, o_kr:o_kr + D_ROPE_A], w_in[:, o_ki:o_ki + D_IDX]
    qkv = w_in[:, o_qkv:o_gates].reshape(d, 3, N_GROUPS_B, H_B, 2, D_HEAD_B // 2)
    qk_split = qkv[:, 0:2].transpose(0, 1, 2, 4, 3, 5).reshape(d, 2 * N_GROUPS_B * WIDTH_B)
    v_cols = qkv[:, 2].reshape(d, N_GROUPS_B * WIDTH_B)
    w_std = jnp.concatenate([w_kr, swap_halves(w_kr), w_ki, swap_halves(w_ki),
                             qk_split, v_cols, w_in[:, o_gates:]], axis=1)
    w_tr = jnp.concatenate([w_in[:, o_cq:o_cq + Q_LORA], w_in[:, o_ckv:o_ckv + KV_LORA],
                            w_in[:, o_wi:o_wi + H_IDX]], axis=1).T
    w_tr = jnp.pad(w_tr, ((0, N_TR - w_tr.shape[0]), (0, 0)))

    return {
        "w_std": w_std.astype(BF16),
        "w_tr": w_tr.astype(BF16),
        "g_cq_col": g_cq.reshape(Q_LORA, 1),
        "g_ckv_col": g_ckv.reshape(KV_LORA, 1),
        "w_uq_t": w_uq.transpose(1, 2, 0).reshape(H_A * (D_NOPE + D_ROPE_A), Q_LORA).astype(BF16),
        "w_iq_t": w_iq.transpose(1, 2, 0).reshape(H_IDX * D_IDX, Q_LORA).astype(BF16),
        "w_uk_p": jnp.pad(w_uk, ((0, 0), (0, 0), (0, HEAD_PAD - D_NOPE))).reshape(KV_LORA, H_A * HEAD_PAD).astype(BF16),
        "w_uv_t": w_uv.transpose(1, 2, 0).reshape(WIDTH_A, KV_LORA).astype(BF16),
    }


def kernel(x, c, positions, w_mod, b_mod, g_pre_ffn1, w_gate1, w_up1, w_down1, g_post_ffn1, g_pre_mix, w_in, g_cq, g_ckv, w_uq, w_uk, w_uv, w_iq, w_up_a, w_up_b, w_o, g_post_mix, g_pre_ffn2, w_gate2, w_up2, w_down2, g_post_ffn2):
    b, s, d = x.shape
    assert s % (16 * TB_DIL) == 0 and s % TM_FFN == 0 and b <= 8
    c8 = jnp.pad(c, ((0, 8 - b), (0, 0)))
    half_b, half_a = D_HEAD_B // 2, D_ROPE_A // 2
    inv = jnp.concatenate([ROPE_THETA ** (-jnp.arange(half_b, dtype=F32) / half_b),
                           ROPE_THETA ** (-jnp.arange(half_a, dtype=F32) / half_a)]).reshape(48, 1)
    row = lambda g: g.reshape(1, -1)
    for l in range(w_mod.shape[0]):
        if l == 0:
            mod, *tabs = _mod_rope_call(c8, w_mod[l], row(b_mod[l]), positions.reshape(b, 1, s), inv)
        else:
            mod = _mod_call(c8, w_mod[l], row(b_mod[l]))
        mod = mod[:b].reshape(b, N_MOD, d)
        x = _ffn_call(x, mod, row(g_pre_ffn1[l]), w_gate1[l].astype(BF16), w_up1[l].astype(BF16),
                      w_down1[l].astype(BF16), row(g_post_ffn1[l]), 0)
        w = _mixer_weights(w_in[l], g_cq[l], g_ckv[l], w_uq[l], w_uk[l], w_uv[l], w_iq[l])
        qt, qit, wit, k, ki, vt, dq0, dq1, dq2, gates = _proj_call(x, mod, row(g_pre_mix[l]), w, tabs)
        oa = _dsa_call(qt, qit, wit, k, ki, vt)
        dil = [_dil_call(dq) for dq in (dq0, dq1, dq2)]
        x = _merge_ffn_call(x, mod, oa, dil, gates, w_up_a[l].astype(BF16), w_up_b[l].astype(BF16),
                            w_o[l].astype(BF16), row(g_post_mix[l]), row(g_pre_ffn2[l]), w_gate2[l].astype(BF16),
                            w_up2[l].astype(BF16), w_down2[l].astype(BF16), row(g_post_ffn2[l]))
    return x
```

```python
import functools

import numpy as np
import jax
import jax.numpy as jnp
from jax import lax
from jax.experimental import pallas as pl
from jax.experimental.pallas import tpu as pltpu

F32 = jnp.float32
BF16 = jnp.bfloat16

D_FF = 2816
ROPE_THETA = 10000.0
NORM_EPS = 1e-6
H_A = 8
Q_LORA = 384
KV_LORA = 256
D_NOPE = 64
D_ROPE_A = 32
D_V_A = 64
H_IDX = 8
D_IDX = 32
TOPK_MAX = 256
DIL_PAIRS = ((128, 1), (512, 4), (2048, 16))
N_GROUPS_B = 3
H_B = 4
D_HEAD_B = 64
N_MOD = 9
WIDTH_A = H_A * D_V_A
WIDTH_B = H_B * D_HEAD_B
N_QKV_B = 3 * N_GROUPS_B * WIDTH_B

LANES = 128
HEAD_PAD = 128
VMEM_LIMIT = 56 * 1024 * 1024

TM_FFN = 512
CH_FFN = 256
TP_PROJ = 512
TQ_DSA = 512
TK_DSA = 256
V_ROWS = D_V_A + 16
TB_DIL = 128
TQ_DIL = 512
BISECT_MAX = 34
BISECT_CHECK = 2
FOLD_ROWS = 16
COUNT_ROWS = 64
IDX_ROWS = 64
NEG_BIG = -1e30
LOG2_E = 1.4426950408889634

C_KR = 0
C_QKVB = 128
R_CQ = 0
R_CKV = Q_LORA
R_WI = Q_LORA + KV_LORA
N_TR = 656


def _params(n_axes):
    return pltpu.CompilerParams(dimension_semantics=("arbitrary",) * n_axes,
                                vmem_limit_bytes=VMEM_LIMIT)


def _rms_rows(x):
    return x * lax.rsqrt(jnp.mean(x * x, axis=-1, keepdims=True) + NORM_EPS)


def _mod_kernel(c_ref, w_ref, b_ref, o_ref):
    c = c_ref[...]
    o_ref[...] = jnp.dot(c * jax.nn.sigmoid(c), w_ref[...], preferred_element_type=F32) + b_ref[...]


def _mod_call(c8, w_mod, b_mod):
    d = c8.shape[1]
    return pl.pallas_call(
        _mod_kernel,
        out_shape=jax.ShapeDtypeStruct((8, N_MOD * d), F32),
        grid=(N_MOD,),
        in_specs=[pl.BlockSpec((8, d), lambda j: (0, 0)),
                  pl.BlockSpec((d, d), lambda j: (0, j)),
                  pl.BlockSpec((1, d), lambda j: (0, j))],
        out_specs=pl.BlockSpec((8, d), lambda j: (0, j)),
        compiler_params=_params(1),
        name="mod",
    )(c8, w_mod, b_mod)


def _rope_kernel(pos_ref, inv_ref, tt_ref, ca_ref, sa_ref, cb_ref, sb_ref):
    s = pos_ref.shape[1]
    ang = inv_ref[...] * pos_ref[...].astype(F32)
    cos = jnp.cos(ang)
    sin = jnp.sin(ang)
    cos_b, sin_b = cos[32:48], sin[32:48]
    tt_ref[0] = cos_b
    tt_ref[1] = sin_b
    cos_a = jnp.concatenate([cos[0:32]] * 4, axis=0)
    sin_a = jnp.concatenate([sin[0:32]] * 4, axis=0)
    cos_bt = jnp.concatenate([cos_b] * 8, axis=0)
    sin_bt = jnp.concatenate([-sin_b, sin_b] * 4, axis=0)
    for j in range(s // LANES):
        cols = slice(j * LANES, (j + 1) * LANES)
        ca_ref[cols, :] = cos_a[:, cols].T
        sa_ref[cols, :] = sin_a[:, cols].T
        cb_ref[cols, :] = cos_bt[:, cols].T
        sb_ref[cols, :] = sin_bt[:, cols].T


def _mod_rope_kernel(c_ref, w_ref, b_ref, pos_ref, inv_ref, o_ref, tt_ref, ca_ref, sa_ref, cb_ref, sb_ref, *,
                     n_batch):
    _mod_kernel(c_ref, w_ref, b_ref, o_ref)

    @pl.when(pl.program_id(0) < n_batch)
    def _tables():
        _rope_kernel(pos_ref, inv_ref, tt_ref, ca_ref, sa_ref, cb_ref, sb_ref)


def _mod_rope_call(c8, w_mod, b_mod, pos3, inv):
    d = c8.shape[1]
    b, _, s = pos3.shape
    assert b <= N_MOD
    row = lambda j: jnp.minimum(j, b - 1)
    tab = jax.ShapeDtypeStruct((b, s, LANES), F32)
    tab_spec = pl.BlockSpec((None, s, LANES), lambda j: (row(j), 0, 0))
    return pl.pallas_call(
        functools.partial(_mod_rope_kernel, n_batch=b),
        out_shape=(jax.ShapeDtypeStruct((8, N_MOD * d), F32),
                   jax.ShapeDtypeStruct((b, 2, 16, s), F32), tab, tab, tab, tab),
        grid=(N_MOD,),
        in_specs=[pl.BlockSpec((8, d), lambda j: (0, 0)),
                  pl.BlockSpec((d, d), lambda j: (0, j)),
                  pl.BlockSpec((1, d), lambda j: (0, j)),
                  pl.BlockSpec((None, 1, s), lambda j: (row(j), 0, 0)),
                  pl.BlockSpec((48, 1), lambda j: (0, 0))],
        out_specs=(pl.BlockSpec((8, d), lambda j: (0, j)),
                   pl.BlockSpec((None, 2, 16, s), lambda j: (row(j), 0, 0, 0)),
                   tab_spec, tab_spec, tab_spec, tab_spec),
        compiler_params=_params(1),
        name="mod_rope",
    )(c8, w_mod, b_mod, pos3, inv)


def _ffn_kernel(x_ref, mod_ref, gpre_ref, wg_ref, wu_ref, wd_ref, gpost_ref, o_ref, acc_ref, *, j0):
    _ffn_tile(x_ref[...], mod_ref, gpre_ref, wg_ref, wu_ref, wd_ref, gpost_ref, o_ref, acc_ref, j0)


def _ffn_tile(x, mod_ref, gpre_ref, wg_ref, wu_ref, wd_ref, gpost_ref, o_ref, acc_ref, j0):
    sh, sc, gt = mod_ref[j0:j0 + 1, :], mod_ref[j0 + 1:j0 + 2, :], mod_ref[j0 + 2:j0 + 3, :]
    h = (_rms_rows(x) * gpre_ref[...] * (1.0 + sc) + sh).astype(BF16)
    n_ch = wg_ref.shape[1] // CH_FFN
    for ch in range(n_ch):
        cols = slice(ch * CH_FFN, (ch + 1) * CH_FFN)
        g = jnp.dot(h, wg_ref[:, cols], preferred_element_type=F32)
        u = jnp.dot(h, wu_ref[:, cols], preferred_element_type=F32)
        a = (g * jax.nn.sigmoid(g) * u).astype(BF16)
        part = jnp.dot(a, wd_ref[cols, :], preferred_element_type=F32)
        if ch == 0:
            acc_ref[...] = part
        else:
            acc_ref[...] += part
    o_ref[...] = x + 0.5 * gt * (_rms_rows(acc_ref[...]) * gpost_ref[...])


def _ffn_call(x, mod, g_pre, w_gate, w_up, w_down, g_post, j0):
    b, s, d = x.shape
    f = w_gate.shape[1]
    const = lambda shape: pl.BlockSpec(shape, lambda i, j: (0,) * len(shape), pipeline_mode=pl.Buffered(1))
    tile = pl.BlockSpec((None, TM_FFN, d), lambda i, j: (i, j, 0))
    return pl.pallas_call(
        functools.partial(_ffn_kernel, j0=j0),
        out_shape=jax.ShapeDtypeStruct((b, s, d), F32),
        grid=(b, s // TM_FFN),
        in_specs=[tile,
                  pl.BlockSpec((None, N_MOD, d), lambda i, j: (i, 0, 0)),
                  const((1, d)), const((d, f)), const((d, f)), const((f, d)), const((1, d))],
        out_specs=tile,
        scratch_shapes=[pltpu.VMEM((TM_FFN, d), F32)],
        compiler_params=_params(2),
        name="ffn",
    )(x, mod, g_pre, w_gate, w_up, w_down, g_post)


def _proj_kernel(x_ref, mod_ref, gpre_ref, wstd_ref, wtail_ref, wtr_ref, gcq_ref, gckv_col_ref,
                 wuq_ref, wiq_ref, wuk_ref, wuv_ref,
                 tt_ref, ca_ref, sa_ref, cb_ref, sb_ref,
                 qt_ref, qit_ref, wit_ref, k_ref, ki_ref, vt_ref, dq0_ref, dq1_ref, dq2_ref, gates_ref,
                 dil_scr, dil_tmp):
    x = x_ref[...]
    sh, sc = mod_ref[3:4, :], mod_ref[4:5, :]
    u = (_rms_rows(x) * gpre_ref[...] * (1.0 + sc) + sh).astype(BF16)
    std = jnp.dot(u, wstd_ref[...], preferred_element_type=F32)
    tail = jnp.dot(u, wtail_ref[...], preferred_element_type=F32)
    tr = lax.dot_general(wtr_ref[...], u, (((1,), (1,)), ((), ())),
                         preferred_element_type=F32)

    cq = tr[R_CQ:R_CQ + Q_LORA]
    cq = cq * lax.rsqrt(jnp.mean(cq * cq, axis=0, keepdims=True) + NORM_EPS) * gcq_ref[...]
    cq = cq.astype(BF16)
    ckv_t = tr[R_CKV:R_CKV + KV_LORA]
    ckv_t = ckv_t * lax.rsqrt(jnp.mean(ckv_t * ckv_t, axis=0, keepdims=True) + NORM_EPS) * gckv_col_ref[...]
    cos_t, sin_t = tt_ref[0], tt_ref[1]

    attn_scale = (D_NOPE + D_ROPE_A) ** -0.5 * LOG2_E
    q_t = jnp.dot(wuq_ref[...], cq, preferred_element_type=F32) * attn_scale
    idx_scale = D_IDX ** -0.5
    qi_t = jnp.dot(wiq_ref[...], cq, preferred_element_type=F32) * idx_scale
    d_qk = D_NOPE + D_ROPE_A
    for h in range(H_A):
        r0, s0 = h * HEAD_PAD, h * d_qk
        x1, x2 = q_t[s0 + 64:s0 + 80], q_t[s0 + 80:s0 + 96]
        qt_ref[r0:r0 + 64, :] = q_t[s0:s0 + 64].astype(BF16)
        qt_ref[r0 + 64:r0 + 80, :] = (x1 * cos_t - x2 * sin_t).astype(BF16)
        qt_ref[r0 + 80:r0 + 96, :] = (x2 * cos_t + x1 * sin_t).astype(BF16)
        qt_ref[r0 + 96:r0 + 128, :] = jnp.zeros((HEAD_PAD - d_qk, q_t.shape[1]), BF16)
        i0 = h * D_IDX
        y1, y2 = qi_t[i0:i0 + 16], qi_t[i0 + 16:i0 + 32]
        qit_ref[i0:i0 + 16, :] = (y1 * cos_t - y2 * sin_t).astype(BF16)
        qit_ref[i0 + 16:i0 + 32, :] = (y2 * cos_t + y1 * sin_t).astype(BF16)
    wit_ref[...] = tr[R_WI:R_WI + H_IDX] * (H_IDX ** -0.5)
    v_t = jnp.dot(wuv_ref[...], ckv_t.astype(BF16), preferred_element_type=F32).astype(BF16)
    for h in range(H_A):
        vt_ref[h * V_ROWS:h * V_ROWS + D_V_A, :] = v_t[h * D_V_A:(h + 1) * D_V_A]
        vt_ref[h * V_ROWS + D_V_A:(h + 1) * V_ROWS, :] = jnp.ones((V_ROWS - D_V_A, v_t.shape[1]), BF16)

    ckv = ckv_t.T.astype(BF16)
    k_nope = jnp.dot(ckv, wuk_ref[...], preferred_element_type=F32)
    t = std[:, C_KR:C_KR + LANES]
    lane = lax.broadcasted_iota(jnp.int32, (1, LANES), 1)
    unswapped = jnp.where((lane // 32) % 2 == 0, 1.0, 0.0)
    roped = (t * cb_ref[...] + pltpu.roll(t, LANES - 32, axis=1) * sb_ref[...]) * unswapped
    moved = pltpu.roll(roped, 64, axis=1)
    k_rope_tile = jnp.where(lane >= 64, moved, 0.0)
    for h in range(H_A):
        k_ref[:, h * HEAD_PAD:(h + 1) * HEAD_PAD] = (k_nope[:, h * HEAD_PAD:(h + 1) * HEAD_PAD]
                                                      + k_rope_tile).astype(BF16)
    ki_ref[...] = jnp.where(lane < 32, moved, 0.0).astype(BF16)

    cos_a, sin_a = ca_ref[...], sa_ref[...]
    tp = x.shape[0]
    for g, (out_ref, (_, dilation)) in enumerate(zip((dq0_ref, dq1_ref, dq2_ref), DIL_PAIRS)):
        for which in range(2):
            scale = (D_HEAD_B ** -0.5 * LOG2_E) if which == 0 else 1.0
            c0 = C_QKVB + (which * N_GROUPS_B + g) * WIDTH_B
            x1, x2 = std[:, c0:c0 + 128], std[:, c0 + 128:c0 + 256]
            dil_scr[2 * which] = (x1 * cos_a - x2 * sin_a) * scale
            dil_scr[2 * which + 1] = (x2 * cos_a + x1 * sin_a) * scale
        dil_scr[4] = tail[:, g * WIDTH_B:g * WIDTH_B + 128]
        dil_scr[5] = tail[:, g * WIDTH_B + 128:(g + 1) * WIDTH_B]
        for c in range(3 * WIDTH_B // LANES):
            if dilation <= 4:
                for r in range(dilation):
                    out_ref[r, :, c * LANES:(c + 1) * LANES] = (
                        dil_scr[c, pl.ds(r, tp // dilation, stride=dilation), :].astype(BF16))
            else:
                quarter = tp // 4
                for r4 in range(4):
                    dil_tmp[c, r4 * quarter:(r4 + 1) * quarter, :] = dil_scr[c, pl.ds(r4, quarter, stride=4), :]
                for r in range(dilation):
                    out_ref[r, :, c * LANES:(c + 1) * LANES] = dil_tmp[
                        c, pl.ds((r % 4) * quarter + r // 4, tp // dilation, stride=dilation // 4), :].astype(BF16)
    gates_ref[...] = jax.nn.sigmoid(tail[:, N_GROUPS_B * WIDTH_B:]).astype(BF16)


def _proj_call(x, mod, g_pre, w, tabs):
    b, s, d = x.shape
    tt, ca, sa, cb, sb = tabs
    tp = TP_PROJ
    const = lambda a: pl.BlockSpec(a.shape, lambda i, j: (0,) * a.ndim)
    rows = lambda width: pl.BlockSpec((None, tp, width), lambda i, j: (i, j, 0))
    colsT = lambda height: pl.BlockSpec((None, height, tp), lambda i, j: (i, 0, j))
    consts = [g_pre, w["w_std"], w["w_tail"], w["w_tr"], w["g_cq_col"], w["g_ckv_col"],
              w["w_uq_t"], w["w_iq_t"], w["w_uk_p"], w["w_uv_t"]]
    hq = H_A * HEAD_PAD
    dil_shapes = [jax.ShapeDtypeStruct((b, dil, s // dil, 3 * WIDTH_B), BF16) for _, dil in DIL_PAIRS]
    dil_specs = [pl.BlockSpec((None, dil, tp // dil, 3 * WIDTH_B), lambda i, j: (i, 0, j, 0))
                 for _, dil in DIL_PAIRS]
    return pl.pallas_call(
        _proj_kernel,
        out_shape=(jax.ShapeDtypeStruct((b, hq, s), BF16),
                   jax.ShapeDtypeStruct((b, H_IDX * D_IDX, s), BF16),
                   jax.ShapeDtypeStruct((b, H_IDX, s), F32),
                   jax.ShapeDtypeStruct((b, s, hq), BF16),
                   jax.ShapeDtypeStruct((b, s, LANES), BF16),
                   jax.ShapeDtypeStruct((b, H_A * V_ROWS, s), BF16),
                   *dil_shapes,
                   jax.ShapeDtypeStruct((b, s, 2048), BF16)),
        grid=(b, s // tp),
        in_specs=[rows(d), pl.BlockSpec((None, N_MOD, d), lambda i, j: (i, 0, 0))]
                 + [const(a) for a in consts]
                 + [pl.BlockSpec((None, 2, 16, tp), lambda i, j: (i, 0, 0, j)),
                    rows(LANES), rows(LANES), rows(LANES), rows(LANES)],
        out_specs=(colsT(hq), colsT(H_IDX * D_IDX), colsT(H_IDX), rows(hq), rows(LANES), colsT(H_A * V_ROWS),
                   *dil_specs, rows(2048)),
        scratch_shapes=[pltpu.VMEM((3 * WIDTH_B // LANES, tp, LANES), F32)] * 2,
        compiler_params=_params(2),
        name="proj",
    )(x, mod, *consts, tt, ca, sa, cb, sb)


def _dsa_kernel(qit_ref, wit_ref, qt_ref, ki_ref, k_ref, vt_ref, o_ref,
                sc_ref, m_ref, acc_ref, s_ref, s2_ref, smax_ref, smax2_ref, qpad_ref, *, topk):
    tq = qt_ref.shape[1]
    tk = s_ref.shape[1]
    i = pl.program_id(1)
    assert tq == 2 * tk
    nkb = 2 * (i + 1)
    npair = i + 1
    qpos = i * tq + lax.broadcasted_iota(jnp.int32, (1, tq), 1)

    def fold(v):
        return jnp.sum(v.reshape(v.shape[0] // FOLD_ROWS, FOLD_ROWS, tq), axis=0)

    qpad_ref[...] = jnp.zeros(qpad_ref.shape, BF16)
    for h in range(H_IDX):
        qpad_ref[h * HEAD_PAD:h * HEAD_PAD + D_IDX, :] = qit_ref[h * D_IDX:(h + 1) * D_IDX, :]

    def park_logits(kb, park_ref, heads=range(H_IDX)):
        ki = ki_ref[pl.ds(pl.multiple_of(kb * tk, tk), tk), :]
        for h in heads:
            park_ref[h] = jnp.dot(ki, qpad_ref[h * HEAD_PAD:(h + 1) * HEAD_PAD, :], preferred_element_type=F32)

    def rows8(v, op):
        return op(v.reshape(IDX_ROWS // 8, 8, LANES), axis=0)

    piece_iota = lax.broadcasted_iota(jnp.int32, (IDX_ROWS, LANES), 0)
    n_col = tq // LANES
    heads_per_col = H_IDX // n_col

    def reduce_block(kb, park_ref, stats, on_diagonal, ahead=None, first_col=0):
        k0 = pl.multiple_of(kb * tk, tk)
        columns = []
        for c in range(n_col):
            if ahead is not None:
                park_logits(ahead[0], ahead[1], range(c * heads_per_col, (c + 1) * heads_per_col))
            lanes = slice(c * LANES, (c + 1) * LANES)
            if c < first_col:
                sc_ref[pl.ds(k0, tk), lanes] = jnp.full((tk, LANES), -jnp.inf, F32)
                columns.append(tuple(v[:, lanes] for v in stats))
                continue
            smax, smin, c_pos, c_nn = [v[:, lanes] for v in stats]
            w = [wit_ref[h:h + 1, lanes] for h in range(H_IDX)]
            for j in range(tk // IDX_ROWS):
                r0 = j * IDX_ROWS
                acc = jnp.zeros((IDX_ROWS, LANES), F32)
                for h in range(H_IDX):
                    acc = acc + w[h] * jnp.maximum(park_ref[h, r0:r0 + IDX_ROWS, lanes], 0.0)
                if on_diagonal:
                    causal = (k0 + r0 + piece_iota) <= qpos[:, lanes]
                    sv = jnp.where(causal, acc, -jnp.inf)
                    smin = jnp.minimum(smin, rows8(jnp.where(causal, acc, jnp.inf), jnp.min))
                else:
                    sv = acc
                    smin = jnp.minimum(smin, rows8(acc, jnp.min))
                sc_ref[pl.ds(k0 + r0, IDX_ROWS), lanes] = sv
                smax = jnp.maximum(smax, rows8(sv, jnp.max))
                c_pos = c_pos + rows8(jnp.where(sv > 0.0, 1.0, 0.0), jnp.sum)
                c_nn = c_nn + rows8(jnp.where(sv >= 0.0, 1.0, 0.0), jnp.sum)
            columns.append((smax, smin, c_pos, c_nn))
        return tuple(jnp.concatenate(parts, axis=1) for parts in zip(*columns))

    def idx_pair(kp, stats):
        stats = reduce_block(2 * kp, s_ref, stats, False, ahead=(2 * kp + 1, s2_ref))
        return reduce_block(2 * kp + 1, s2_ref, stats, False, ahead=(2 * kp + 2, s_ref))

    park_logits(0, s_ref)
    zero8 = jnp.zeros((8, tq), F32)
    stats = lax.fori_loop(0, i, idx_pair,
                          (jnp.full((8, tq), -jnp.inf, F32), jnp.full((8, tq), jnp.inf, F32), zero8, zero8))
    stats = reduce_block(nkb - 2, s_ref, stats, True, ahead=(nkb - 1, s2_ref))
    stats = reduce_block(nkb - 1, s2_ref, stats, True, first_col=(tq - tk) // LANES)
    smax = jnp.max(stats[0], axis=0, keepdims=True)
    smin = jnp.min(stats[1], axis=0, keepdims=True)
    c_pos, c_nn = [jnp.sum(v, axis=0, keepdims=True) for v in stats[2:]]

    n_causal = (qpos + 1).astype(F32)
    kf = jnp.minimum(qpos + 1, topk).astype(F32)

    def count_gt(thr):
        def pieces(k0, n_pieces, c):
            for j in range(n_pieces):
                rows = pl.ds(k0 + j * COUNT_ROWS, COUNT_ROWS)
                c = c + fold(jnp.where(sc_ref[rows, :] > thr, 1.0, 0.0))
            return c

        def body(kp, c):
            return pieces(pl.multiple_of(kp * (2 * tk), 2 * tk), 2 * tk // COUNT_ROWS, c)

        part = lax.fori_loop(0, npair - 1, body, jnp.zeros((FOLD_ROWS, tq), F32))
        k0 = pl.multiple_of((nkb - 2) * tk, tk)
        part = pieces(k0, tk // COUNT_ROWS, part)
        half = jnp.zeros((FOLD_ROWS, tk), F32)
        for j in range(tk // COUNT_ROWS):
            sv = sc_ref[pl.ds(k0 + tk + j * COUNT_ROWS, COUNT_ROWS), tq - tk:]
            half = half + jnp.sum(jnp.where(sv > thr[:, tq - tk:], 1.0, 0.0)
                                  .reshape(COUNT_ROWS // FOLD_ROWS, FOLD_ROWS, tk), axis=0)
        part = jnp.concatenate([part[:, :tq - tk], part[:, tq - tk:] + half], axis=1)
        return jnp.sum(part, axis=0, keepdims=True)

    below = smin - (1.0 + jnp.abs(smin))
    zero_tie = jnp.logical_and(c_pos < kf, kf <= c_nn)
    positive = c_pos >= kf
    searching = jnp.where(zero_tie, 0.0, 1.0)

    def bisect_cond(carry):
        return jnp.logical_and(carry[0] < BISECT_MAX, carry[1] > 0)

    def bisect_body(carry):
        it, _, lo, hi, c_lo, c_hi = carry
        for _ in range(BISECT_CHECK):
            mid = 0.5 * (lo + hi)
            c_mid = count_gt(mid)
            ge = c_mid >= kf
            up = jnp.logical_and(ge, searching > 0.0)
            down = jnp.logical_and(jnp.logical_not(ge), searching > 0.0)
            lo, c_lo = jnp.where(up, mid, lo), jnp.where(up, c_mid, c_lo)
            hi, c_hi = jnp.where(down, mid, hi), jnp.where(down, c_mid, c_hi)
        open_queries = (jnp.max((c_lo - kf) * searching) > 0.0).astype(jnp.int32)
        return it + BISECT_CHECK, open_queries, lo, hi, c_lo, c_hi

    lo0 = jnp.where(positive, 0.0, jnp.where(zero_tie, 0.0, below))
    hi0 = jnp.where(positive, smax, 0.0)
    c_lo0 = jnp.where(positive, c_pos, jnp.where(zero_tie, c_nn, n_causal))
    c_hi0 = jnp.where(positive, 0.0, c_pos)
    first_open = (jnp.max((c_lo0 - kf) * searching) > 0.0).astype(jnp.int32)
    _, _, lo, hi, c_lo, c_hi = lax.while_loop(
        bisect_cond, bisect_body, (jnp.int32(0), first_open, lo0, hi0, c_lo0, c_hi0))

    @pl.when(jnp.max(jnp.maximum(c_lo - kf, 1.0 - searching)) > 0.0)
    def _break_ties():
        need = kf - c_hi
        closed = 1.0 - searching
        prefix = jnp.where(lax.broadcasted_iota(jnp.int32, (tk, tk), 0)
                           >= lax.broadcasted_iota(jnp.int32, (tk, tk), 1), 1.0, 0.0).astype(BF16)

        def body(kp, seen):
            blocks = []
            for j in range(2):
                k0 = pl.multiple_of((2 * kp + j) * tk, tk)
                sv = sc_ref[pl.ds(k0, tk), :]
                above_lo = jnp.where(sv > lo, 1.0, jnp.where(sv >= lo, closed, 0.0))
                member = jnp.where(sv > hi, 0.0, above_lo)
                blocks.append((k0, member, jnp.dot(prefix, member.astype(BF16), preferred_element_type=F32)))
            for k0, member, within in blocks:
                rank = within + seen
                sv = sc_ref[pl.ds(k0, tk), :]
                sc_ref[pl.ds(k0, tk), :] = jnp.where(member > 0.0, jnp.where(rank > need, -jnp.inf, jnp.inf), sv)
                seen = rank[tk - 1:tk, :]
            return seen

        lax.fori_loop(0, npair, body, jnp.zeros((1, tq), F32))

    m_ref[...] = jnp.full(m_ref.shape, NEG_BIG, F32)
    acc_ref[...] = jnp.zeros(acc_ref.shape, F32)

    every = slice(0, tq)
    late = slice(tq - tk, tq)

    def selection_bias(kb, cols=every):
        return jnp.where(sc_ref[pl.ds(pl.multiple_of(kb * tk, tk), tk), cols] > lo[:, cols], 0.0, NEG_BIG)

    def score_head(kb, h, bias, park_ref, max_ref, cols=every):
        kh = k_ref[pl.ds(pl.multiple_of(kb * tk, tk), tk), h * HEAD_PAD:(h + 1) * HEAD_PAD]
        s = jnp.dot(kh, qt_ref[h * HEAD_PAD:(h + 1) * HEAD_PAD, cols], preferred_element_type=F32) + bias
        park_ref[h, :, cols] = s
        max_ref[h:h + 1, cols] = jnp.max(s, axis=0, keepdims=True)

    def softmax_head(kb, h, park_ref, max_ref, cols=every):
        m_old = m_ref[h:h + 1, cols]
        m_new = jnp.maximum(m_old, max_ref[h:h + 1, cols])
        alpha = jnp.exp2(m_old - m_new)
        p = jnp.exp2((park_ref[h, :, cols] - m_new).astype(BF16))
        rows = slice(h * V_ROWS, (h + 1) * V_ROWS)
        pv = jnp.dot(vt_ref[rows, pl.ds(pl.multiple_of(kb * tk, tk), tk)], p, preferred_element_type=F32)
        acc_ref[rows, cols] = alpha * acc_ref[rows, cols] + pv
        m_ref[h:h + 1, cols] = m_new

    bias0 = selection_bias(0)
    for h in range(H_A):
        score_head(0, h, bias0, s_ref, smax_ref)

    def att_body(kp, carry):
        odd, nxt = 2 * kp + 1, 2 * kp + 2
        bias = selection_bias(odd)
        for h in range(H_A):
            score_head(odd, h, bias, s2_ref, smax2_ref)
            softmax_head(2 * kp, h, s_ref, smax_ref)
        bias = selection_bias(nxt)
        for h in range(H_A):
            score_head(nxt, h, bias, s_ref, smax_ref)
            softmax_head(odd, h, s2_ref, smax2_ref)
        return carry

    lax.fori_loop(0, npair - 1, att_body, 0)
    bias = selection_bias(nkb - 1, late)
    for h in range(H_A):
        score_head(nkb - 1, h, bias, s2_ref, smax2_ref, late)
        softmax_head(nkb - 2, h, s_ref, smax_ref)
    for h in range(H_A):
        softmax_head(nkb - 1, h, s2_ref, smax2_ref, late)
    out = [acc_ref[h * V_ROWS:h * V_ROWS + D_V_A, :] / acc_ref[h * V_ROWS + D_V_A:h * V_ROWS + D_V_A + 1, :]
           for h in range(H_A)]
    o_ref[...] = jnp.concatenate(out, axis=0).T.astype(BF16)


def _dsa_call(qt, qit, wit, k, ki, vt):
    b, hq, s = qt.shape
    tq = TQ_DSA
    topk = min(TOPK_MAX, s // 4)
    colsT = lambda height: pl.BlockSpec((None, height, tq), lambda i, j: (i, 0, j))
    whole = lambda a: pl.BlockSpec((None,) + a.shape[1:], lambda i, j: (i, 0, 0))
    return pl.pallas_call(
        functools.partial(_dsa_kernel, topk=topk),
        out_shape=jax.ShapeDtypeStruct((b, s, WIDTH_A), BF16),
        grid=(b, s // tq),
        in_specs=[colsT(H_IDX * D_IDX), colsT(H_IDX), colsT(hq), whole(ki), whole(k), whole(vt)],
        out_specs=pl.BlockSpec((None, tq, WIDTH_A), lambda i, j: (i, j, 0)),
        scratch_shapes=[pltpu.VMEM((s, tq), F32),
                        pltpu.VMEM((H_A, tq), F32),
                        pltpu.VMEM((H_A * V_ROWS, tq), F32),
                        pltpu.VMEM((H_A, TK_DSA, tq), F32),
                        pltpu.VMEM((H_A, TK_DSA, tq), F32),
                        pltpu.VMEM((H_A, tq), F32),
                        pltpu.VMEM((H_A, tq), F32),
                        pltpu.VMEM((H_IDX * HEAD_PAD, tq), BF16)],
        compiler_params=_params(2),
        name="dsa",
    )(qit, wit, qt, ki, k, vt)


def _dil_kernel(q_ref, kc_ref, kp_ref, vc_ref, vp_ref, o_ref, lse_ref, s_ref):
    for res in range(q_ref.shape[0]):
        _dil_residue(q_ref.at[res], kc_ref.at[res], kp_ref.at[res], vc_ref.at[res], vp_ref.at[res],
                     o_ref.at[res], lse_ref.at[res], s_ref.at[res])


def _dil_residue(q_ref, kc_ref, kp_ref, vc_ref, vp_ref, o_ref, lse_ref, s_ref):
    tb = kp_ref.shape[0]
    n_sub = q_ref.shape[0] // tb
    first_step = pl.program_id(2) == 0
    k = jnp.concatenate([kp_ref[...], kc_ref[...]], axis=0)
    v = jnp.concatenate([vp_ref[...], vc_ref[...]], axis=0)
    r = lax.broadcasted_iota(jnp.int32, (tb, 2 * tb), 0)
    c = lax.broadcasted_iota(jnp.int32, (tb, 2 * tb), 1)
    behind = (tb + r - c).astype(jnp.uint32)
    bias = jnp.where(behind <= tb, 0.0, -jnp.inf)
    no_prev = jnp.where(first_step, r, tb).astype(jnp.uint32)
    bias0 = jnp.where(behind <= no_prev, 0.0, -jnp.inf)
    lane = lax.broadcasted_iota(jnp.int32, (1, WIDTH_B), 1)
    head_qk = (lane % 128) // (D_HEAD_B // 2)
    head_v = lane // D_HEAD_B
    nt = (((1,), (1,)), ((), ()))
    for j in range(n_sub):
        q = q_ref[j * tb:(j + 1) * tb, :]
        for h in range(H_B):
            qh = jnp.where(head_qk == h, q, jnp.zeros_like(q))
            s_ref[j * H_B + h] = (lax.dot_general(qh, k[j * tb:(j + 2) * tb], nt, preferred_element_type=F32)
                                  + (bias0 if j == 0 else bias))
    for j in range(n_sub):
        vj = v[j * tb:(j + 2) * tb]
        num = jnp.zeros((tb, WIDTH_B), F32)
        den = jnp.zeros((tb, WIDTH_B), F32)
        top = jnp.zeros((tb, WIDTH_B), F32)
        for h in range(H_B):
            s = s_ref[j * H_B + h]
            m = jnp.max(s, axis=1, keepdims=True)
            p = jnp.exp2(s - m).astype(BF16)
            pv = jnp.dot(p, jnp.where(head_v == h, vj, jnp.ones_like(vj)), preferred_element_type=F32)
            mine = head_v == h
            num = jnp.where(mine, pv, num)
            den = jnp.where(mine, pltpu.roll(pv, WIDTH_B // 2, axis=1), den)
            top = jnp.where(mine, m, top)
        o_ref[j * tb:(j + 1) * tb, :] = (num / den).astype(o_ref.dtype)
        lse_ref[j * tb:(j + 1) * tb, :] = top + jnp.log2(den)


def _dil_call(qkv):
    b, dilation, n, _ = qkv.shape
    tb, tq = TB_DIL, min(TQ_DIL, n)
    per = tq // tb
    n_res = max(1, min(dilation, TQ_DIL // tq))
    cur = lambda which: pl.BlockSpec((None, n_res, tq, WIDTH_B), lambda i, r, j: (i, r, j, which))
    prev = lambda which: pl.BlockSpec((None, n_res, tb, WIDTH_B),
                                      lambda i, r, j: (i, r, jnp.maximum(j * per - 1, 0), which))
    out_spec = pl.BlockSpec((None, n_res, tq, WIDTH_B), lambda i, r, j: (i, r, j, 0))
    out = lambda dtype: jax.ShapeDtypeStruct((b, dilation, n, WIDTH_B), dtype)
    return pl.pallas_call(
        _dil_kernel,
        out_shape=(out(BF16), out(F32)),
        grid=(b, dilation // n_res, n // tq),
        in_specs=[cur(0), cur(1), prev(1), cur(2), prev(2)],
        out_specs=(out_spec, out_spec),
        scratch_shapes=[pltpu.VMEM((n_res, per * H_B, tb, 2 * tb), F32)],
        compiler_params=_params(3),
        name="dil",
    )(qkv, qkv, qkv, qkv, qkv)


def _merge_ffn_kernel(x_ref, mod_ref, oa_ref, o0_ref, o1_ref, o2_ref, l0_ref, l1_ref, l2_ref, gates_ref,
                      wua_ref, wub_ref, wo_ref, gpost_ref, gpre2_ref, wg_ref, wu_ref, wd_ref, gpost2_ref,
                      out_ref, acc_ref, *scr):
    tm = x_ref.shape[0]
    tmp = scr[4]

    def token_major(ref, buf):
        dilation = ref.shape[0]
        if dilation == 1:
            return ref[0]
        n_slab = ref.shape[2] // LANES
        for c in range(n_slab):
            if dilation <= 4:
                for r in range(dilation):
                    buf[c, pl.ds(r, tm // dilation, stride=dilation), :] = (
                        ref[r, :, c * LANES:(c + 1) * LANES].astype(F32))
            else:
                quarter = tm // 4
                for r in range(dilation):
                    tmp[c, pl.ds((r % 4) * quarter + r // 4, tm // dilation, stride=dilation // 4), :] = (
                        ref[r, :, c * LANES:(c + 1) * LANES].astype(F32))
                for r4 in range(4):
                    buf[c, pl.ds(r4, quarter, stride=4), :] = tmp[c, r4 * quarter:(r4 + 1) * quarter, :]
        return jnp.concatenate([buf[c] for c in range(n_slab)], axis=1)

    o0, l0 = token_major(o0_ref, None).astype(F32), token_major(l0_ref, None)
    o1, l1 = token_major(o1_ref, scr[0]), token_major(l1_ref, scr[1])
    o2, l2 = token_major(o2_ref, scr[2]), token_major(l2_ref, scr[3])
    m = jnp.maximum(jnp.maximum(l0, l1), l2)
    e0, e1, e2 = jnp.exp2(l0 - m), jnp.exp2(l1 - m), jnp.exp2(l2 - m)
    ob = (e0 * o0 + e1 * o1 + e2 * o2) / (e0 + e1 + e2)
    za = jnp.dot(oa_ref[...], wua_ref[...], preferred_element_type=F32)
    zb = jnp.dot(ob.astype(BF16), wub_ref[...], preferred_element_type=F32)
    d = za.shape[1]
    z = gates_ref[:, 0:d].astype(F32) * za + gates_ref[:, d:2 * d].astype(F32) * zb
    y = jnp.dot(z.astype(BF16), wo_ref[...], preferred_element_type=F32)
    x_mixed = x_ref[...] + mod_ref[5:6, :] * (_rms_rows(y) * gpost_ref[...])
    _ffn_tile(x_mixed, mod_ref, gpre2_ref, wg_ref, wu_ref, wd_ref, gpost2_ref, out_ref, acc_ref, 6)


def _merge_ffn_call(x, mod, oa, dil, gates, w_up_a, w_up_b, w_o, g_post, g_pre2, w_gate, w_up, w_down, g_post2):
    b, s, d = x.shape
    tm = TM_FFN
    rows = lambda width: pl.BlockSpec((None, tm, width), lambda i, j: (i, j, 0))
    const = lambda a: pl.BlockSpec(a.shape, lambda i, j: (0,) * a.ndim, pipeline_mode=pl.Buffered(1))
    (o0, l0), (o1, l1), (o2, l2) = dil
    res = lambda a: pl.BlockSpec((None, a.shape[1], tm // a.shape[1], WIDTH_B), lambda i, j: (i, 0, j, 0))
    consts = [w_up_a, w_up_b, w_o, g_post, g_pre2, w_gate, w_up, w_down, g_post2]
    return pl.pallas_call(
        _merge_ffn_kernel,
        out_shape=jax.ShapeDtypeStruct((b, s, d), F32),
        grid=(b, s // tm),
        in_specs=[rows(d), pl.BlockSpec((None, N_MOD, d), lambda i, j: (i, 0, 0)), rows(WIDTH_A)]
                 + [res(a) for a in (o0, o1, o2, l0, l1, l2)]
                 + [rows(2 * d)] + [const(a) for a in consts],
        out_specs=rows(d),
        scratch_shapes=[pltpu.VMEM((tm, d), F32)] + [pltpu.VMEM((WIDTH_B // LANES, tm, LANES), F32)] * 5,
        compiler_params=_params(2),
        name="merge_ffn",
    )(x, mod, oa, o0, o1, o2, l0, l1, l2, gates, *consts)


def _mixer_weights(w_in, g_cq, g_ckv, w_uq, w_uk, w_uv, w_iq):
    d = w_in.shape[0]
    w_in = w_in.astype(BF16)
    o_cq, o_ckv, o_kr, o_ki, o_wi = 0, Q_LORA, Q_LORA + KV_LORA, Q_LORA + KV_LORA + D_ROPE_A, \
        Q_LORA + KV_LORA + D_ROPE_A + D_IDX
    o_qkv = o_wi + H_IDX
    o_gates = o_qkv + N_QKV_B

    def swap_halves(w):
        half = w.shape[1] // 2
        return jnp.concatenate([w[:, half:], w[:, :half]], axis=1)

    w_kr, w_ki = w_in[:, o_kr:o_kr + D_ROPE_A], w_in[:, o_ki:o_ki + D_IDX]
    qkv = w_in[:, o_qkv:o_gates].reshape(d, 3, N_GROUPS_B, H_B, 2, D_HEAD_B // 2)
    qk_split = qkv[:, 0:2].transpose(0, 1, 2, 4, 3, 5).reshape(d, 2 * N_GROUPS_B * WIDTH_B)
    w_std = jnp.concatenate([w_kr, swap_halves(w_kr), w_ki, swap_halves(w_ki), qk_split], axis=1)
    w_tail = w_in[:, o_qkv + 2 * N_GROUPS_B * WIDTH_B:]
    w_tr = jnp.concatenate([w_in[:, o_cq:o_cq + Q_LORA], w_in[:, o_ckv:o_ckv + KV_LORA],
                            w_in[:, o_wi:o_wi + H_IDX]], axis=1).T
    w_tr = jnp.pad(w_tr, ((0, N_TR - w_tr.shape[0]), (0, 0)))

    return {
        "w_std": w_std.astype(BF16),
        "w_tail": w_tail.astype(BF16),
        "w_tr": w_tr.astype(BF16),
        "g_cq_col": g_cq.reshape(Q_LORA, 1),
        "g_ckv_col": g_ckv.reshape(KV_LORA, 1),
        "w_uq_t": w_uq.transpose(1, 2, 0).reshape(H_A * (D_NOPE + D_ROPE_A), Q_LORA).astype(BF16),
        "w_iq_t": w_iq.transpose(1, 2, 0).reshape(H_IDX * D_IDX, Q_LORA).astype(BF16),
        "w_uk_p": jnp.pad(w_uk, ((0, 0), (0, 0), (0, HEAD_PAD - D_NOPE))).reshape(KV_LORA, H_A * HEAD_PAD).astype(BF16),
        "w_uv_t": w_uv.transpose(1, 2, 0).reshape(WIDTH_A, KV_LORA).astype(BF16),
    }


def kernel(x, c, positions, w_mod, b_mod, g_pre_ffn1, w_gate1, w_up1, w_down1, g_post_ffn1, g_pre_mix, w_in, g_cq, g_ckv, w_uq, w_uk, w_uv, w_iq, w_up_a, w_up_b, w_o, g_post_mix, g_pre_ffn2, w_gate2, w_up2, w_down2, g_post_ffn2):
    b, s, d = x.shape
    assert s % (16 * TB_DIL) == 0 and s % TM_FFN == 0 and b <= 8
    c8 = jnp.pad(c, ((0, 8 - b), (0, 0)))
    half_b, half_a = D_HEAD_B // 2, D_ROPE_A // 2
    inv = jnp.concatenate([ROPE_THETA ** (-jnp.arange(half_b, dtype=F32) / half_b),
                           ROPE_THETA ** (-jnp.arange(half_a, dtype=F32) / half_a)]).reshape(48, 1)
    row = lambda g: g.reshape(1, -1)
    for l in range(w_mod.shape[0]):
        if l == 0:
            mod, *tabs = _mod_rope_call(c8, w_mod[l], row(b_mod[l]), positions.reshape(b, 1, s), inv)
        else:
            mod = _mod_call(c8, w_mod[l], row(b_mod[l]))
        mod = mod[:b].reshape(b, N_MOD, d)
        x = _ffn_call(x, mod, row(g_pre_ffn1[l]), w_gate1[l].astype(BF16), w_up1[l].astype(BF16),
                      w_down1[l].astype(BF16), row(g_post_ffn1[l]), 0)
        w = _mixer_weights(w_in[l], g_cq[l], g_ckv[l], w_uq[l], w_uk[l], w_uv[l], w_iq[l])
        qt, qit, wit, k, ki, vt, dq0, dq1, dq2, gates = _proj_call(x, mod, row(g_pre_mix[l]), w, tabs)
        oa = _dsa_call(qt, qit, wit, k, ki, vt)
        dil = [_dil_call(dq) for dq in (dq0, dq1, dq2)]
        x = _merge_ffn_call(x, mod, oa, dil, gates, w_up_a[l].astype(BF16), w_up_b[l].astype(BF16),
                            w_o[l].astype(BF16), row(g_post_mix[l]), row(g_pre_ffn2[l]), w_gate2[l].astype(BF16),
                            w_up2[l].astype(BF16), w_down2[l].astype(BF16), row(g_post_ffn2[l]))
    return x
```

```python
import functools

import jax
import jax.numpy as jnp
from jax import lax
from jax.experimental import pallas as pl
from jax.experimental.pallas import tpu as pltpu

F32 = jnp.float32
BF16 = jnp.bfloat16

ROPE_THETA = 10000.0
NORM_EPS = 1e-6
H_A = 8
Q_LORA = 384
KV_LORA = 256
D_NOPE = 64
D_ROPE_A = 32
D_V_A = 64
H_IDX = 8
D_IDX = 32
TOPK_MAX = 256
DIL_PAIRS = ((128, 1), (512, 4), (2048, 16))
N_GROUPS_B = 3
H_B = 4
D_HEAD_B = 64
N_MOD = 9
WIDTH_A = H_A * D_V_A
WIDTH_B = H_B * D_HEAD_B
N_QKV_B = 3 * N_GROUPS_B * WIDTH_B

LANES = 128
HEAD_PAD = 128
VMEM_LIMIT = 56 * 1024 * 1024

TM_FFN = 512
CH_FFN = 256
TP_PROJ = 512
TQ_DSA = 512
TK_DSA = 256
V_ROWS = D_V_A + 16
TB_DIL = 128
TQ_DIL = 512
BISECT_MAX = 64
BISECT_CHECK = 2
FOLD_ROWS = 16
COUNT_ROWS = 64
IDX_ROWS = 64
NEG_BIG = -1e30
LOG2_E = 1.4426950408889634

C_KR = 0
C_QKVB = 128
R_CQ = 0
R_CKV = Q_LORA
R_WI = Q_LORA + KV_LORA
N_TR = 656


def _params(n_axes):
    return pltpu.CompilerParams(dimension_semantics=("arbitrary",) * n_axes,
                                vmem_limit_bytes=VMEM_LIMIT)


def _rms_rows(x):
    return x * lax.rsqrt(jnp.mean(x * x, axis=-1, keepdims=True) + NORM_EPS)


def _mod_kernel(c_ref, w_ref, b_ref, o_ref):
    c = c_ref[...]
    o_ref[...] = jnp.dot(c * jax.nn.sigmoid(c), w_ref[...], preferred_element_type=F32) + b_ref[...]


def _mod_call(c8, w_mod, b_mod):
    d = c8.shape[1]
    return pl.pallas_call(
        _mod_kernel,
        out_shape=jax.ShapeDtypeStruct((8, N_MOD * d), F32),
        grid=(N_MOD,),
        in_specs=[pl.BlockSpec((8, d), lambda j: (0, 0)),
                  pl.BlockSpec((d, d), lambda j: (0, j)),
                  pl.BlockSpec((1, d), lambda j: (0, j))],
        out_specs=pl.BlockSpec((8, d), lambda j: (0, j)),
        compiler_params=_params(1),
        name="mod",
    )(c8, w_mod, b_mod)


def _rope_kernel(pos_ref, inv_ref, tt_ref, ca_ref, sa_ref, cb_ref, sb_ref):
    s = pos_ref.shape[1]
    ang = inv_ref[...] * pos_ref[...].astype(F32)
    cos = jnp.cos(ang)
    sin = jnp.sin(ang)
    cos_b, sin_b = cos[32:48], sin[32:48]
    tt_ref[0] = cos_b
    tt_ref[1] = sin_b
    cos_a = jnp.concatenate([cos[0:32]] * 4, axis=0)
    sin_a = jnp.concatenate([sin[0:32]] * 4, axis=0)
    cos_bt = jnp.concatenate([cos_b] * 8, axis=0)
    sin_bt = jnp.concatenate([-sin_b, sin_b] * 4, axis=0)
    for j in range(s // LANES):
        cols = slice(j * LANES, (j + 1) * LANES)
        ca_ref[cols, :] = cos_a[:, cols].T
        sa_ref[cols, :] = sin_a[:, cols].T
        cb_ref[cols, :] = cos_bt[:, cols].T
        sb_ref[cols, :] = sin_bt[:, cols].T


def _mod_rope_kernel(c_ref, w_ref, b_ref, pos_ref, inv_ref, o_ref, tt_ref, ca_ref, sa_ref, cb_ref, sb_ref, *,
                     n_batch):
    _mod_kernel(c_ref, w_ref, b_ref, o_ref)

    @pl.when(pl.program_id(0) < n_batch)
    def _tables():
        _rope_kernel(pos_ref, inv_ref, tt_ref, ca_ref, sa_ref, cb_ref, sb_ref)


def _mod_rope_call(c8, w_mod, b_mod, pos3, inv):
    d = c8.shape[1]
    b, _, s = pos3.shape
    assert b <= N_MOD
    row = lambda j: jnp.minimum(j, b - 1)
    tab = jax.ShapeDtypeStruct((b, s, LANES), F32)
    tab_spec = pl.BlockSpec((None, s, LANES), lambda j: (row(j), 0, 0))
    return pl.pallas_call(
        functools.partial(_mod_rope_kernel, n_batch=b),
        out_shape=(jax.ShapeDtypeStruct((8, N_MOD * d), F32),
                   jax.ShapeDtypeStruct((b, 2, 16, s), F32), tab, tab, tab, tab),
        grid=(N_MOD,),
        in_specs=[pl.BlockSpec((8, d), lambda j: (0, 0)),
                  pl.BlockSpec((d, d), lambda j: (0, j)),
                  pl.BlockSpec((1, d), lambda j: (0, j)),
                  pl.BlockSpec((None, 1, s), lambda j: (row(j), 0, 0)),
                  pl.BlockSpec((48, 1), lambda j: (0, 0))],
        out_specs=(pl.BlockSpec((8, d), lambda j: (0, j)),
                   pl.BlockSpec((None, 2, 16, s), lambda j: (row(j), 0, 0, 0)),
                   tab_spec, tab_spec, tab_spec, tab_spec),
        compiler_params=_params(1),
        name="mod_rope",
    )(c8, w_mod, b_mod, pos3, inv)


def _ffn_kernel(x_ref, mod_ref, gpre_ref, wg_ref, wu_ref, wd_ref, gpost_ref, o_ref, acc_ref, *, j0):
    _ffn_tile(x_ref[...], mod_ref, gpre_ref, wg_ref, wu_ref, wd_ref, gpost_ref, o_ref, acc_ref, j0)


def _ffn_tile(x, mod_ref, gpre_ref, wg_ref, wu_ref, wd_ref, gpost_ref, o_ref, acc_ref, j0):
    sh, sc, gt = mod_ref[j0:j0 + 1, :], mod_ref[j0 + 1:j0 + 2, :], mod_ref[j0 + 2:j0 + 3, :]
    h = (_rms_rows(x) * gpre_ref[...] * (1.0 + sc) + sh).astype(BF16)
    n_ch = wg_ref.shape[1] // CH_FFN
    for ch in range(n_ch):
        cols = slice(ch * CH_FFN, (ch + 1) * CH_FFN)
        g = jnp.dot(h, wg_ref[:, cols], preferred_element_type=F32)
        u = jnp.dot(h, wu_ref[:, cols], preferred_element_type=F32)
        a = (g * jax.nn.sigmoid(g) * u).astype(BF16)
        part = jnp.dot(a, wd_ref[cols, :], preferred_element_type=F32)
        if ch == 0:
            acc_ref[...] = part
        else:
            acc_ref[...] += part
    o_ref[...] = x + 0.5 * gt * (_rms_rows(acc_ref[...]) * gpost_ref[...])


def _ffn_call(x, mod, g_pre, w_gate, w_up, w_down, g_post, j0):
    b, s, d = x.shape
    f = w_gate.shape[1]
    const = lambda shape: pl.BlockSpec(shape, lambda i, j: (0,) * len(shape), pipeline_mode=pl.Buffered(1))
    tile = pl.BlockSpec((None, TM_FFN, d), lambda i, j: (i, j, 0))
    return pl.pallas_call(
        functools.partial(_ffn_kernel, j0=j0),
        out_shape=jax.ShapeDtypeStruct((b, s, d), F32),
        grid=(b, s // TM_FFN),
        in_specs=[tile,
                  pl.BlockSpec((None, N_MOD, d), lambda i, j: (i, 0, 0)),
                  const((1, d)), const((d, f)), const((d, f)), const((f, d)), const((1, d))],
        out_specs=tile,
        scratch_shapes=[pltpu.VMEM((TM_FFN, d), F32)],
        compiler_params=_params(2),
        name="ffn",
    )(x, mod, g_pre, w_gate, w_up, w_down, g_post)


def _proj_kernel(x_ref, mod_ref, gpre_ref, wstd_ref, wtail_ref, wtr_ref, gcq_ref, gckv_col_ref,
                 wuq_ref, wiq_ref, wuk_ref, wuv_ref,
                 tt_ref, ca_ref, sa_ref, cb_ref, sb_ref,
                 qt_ref, qit_ref, wit_ref, k_ref, ki_ref, vt_ref, dq0_ref, dq1_ref, dq2_ref, gates_ref,
                 dil_scr, dil_tmp):
    x = x_ref[...]
    sh, sc = mod_ref[3:4, :], mod_ref[4:5, :]
    u = (_rms_rows(x) * gpre_ref[...] * (1.0 + sc) + sh).astype(BF16)
    std = jnp.dot(u, wstd_ref[...], preferred_element_type=F32)
    tail = jnp.dot(u, wtail_ref[...], preferred_element_type=F32)
    tr = lax.dot_general(wtr_ref[...], u, (((1,), (1,)), ((), ())),
                         preferred_element_type=F32)

    cq = tr[R_CQ:R_CQ + Q_LORA]
    cq = cq * lax.rsqrt(jnp.mean(cq * cq, axis=0, keepdims=True) + NORM_EPS) * gcq_ref[...]
    cq = cq.astype(BF16)
    ckv_t = tr[R_CKV:R_CKV + KV_LORA]
    ckv_t = ckv_t * lax.rsqrt(jnp.mean(ckv_t * ckv_t, axis=0, keepdims=True) + NORM_EPS) * gckv_col_ref[...]
    cos_t, sin_t = tt_ref[0], tt_ref[1]

    attn_scale = (D_NOPE + D_ROPE_A) ** -0.5 * LOG2_E
    q_t = jnp.dot(wuq_ref[...], cq, preferred_element_type=F32) * attn_scale
    idx_scale = D_IDX ** -0.5
    qi_t = jnp.dot(wiq_ref[...], cq, preferred_element_type=F32) * idx_scale
    d_qk = D_NOPE + D_ROPE_A
    for h in range(H_A):
        r0, s0 = h * HEAD_PAD, h * d_qk
        x1, x2 = q_t[s0 + 64:s0 + 80], q_t[s0 + 80:s0 + 96]
        qt_ref[r0:r0 + 64, :] = q_t[s0:s0 + 64].astype(BF16)
        qt_ref[r0 + 64:r0 + 80, :] = (x1 * cos_t - x2 * sin_t).astype(BF16)
        qt_ref[r0 + 80:r0 + 96, :] = (x2 * cos_t + x1 * sin_t).astype(BF16)
        qt_ref[r0 + 96:r0 + 128, :] = jnp.zeros((HEAD_PAD - d_qk, q_t.shape[1]), BF16)
        i0 = h * D_IDX
        y1, y2 = qi_t[i0:i0 + 16], qi_t[i0 + 16:i0 + 32]
        qit_ref[i0:i0 + 16, :] = (y1 * cos_t - y2 * sin_t).astype(BF16)
        qit_ref[i0 + 16:i0 + 32, :] = (y2 * cos_t + y1 * sin_t).astype(BF16)
    wit_ref[...] = tr[R_WI:R_WI + H_IDX] * (H_IDX ** -0.5)
    v_t = jnp.dot(wuv_ref[...], ckv_t.astype(BF16), preferred_element_type=F32).astype(BF16)
    for h in range(H_A):
        vt_ref[h * V_ROWS:h * V_ROWS + D_V_A, :] = v_t[h * D_V_A:(h + 1) * D_V_A]
        vt_ref[h * V_ROWS + D_V_A:(h + 1) * V_ROWS, :] = jnp.ones((V_ROWS - D_V_A, v_t.shape[1]), BF16)

    ckv = ckv_t.T.astype(BF16)
    k_nope = jnp.dot(ckv, wuk_ref[...], preferred_element_type=F32)
    t = std[:, C_KR:C_KR + LANES]
    lane = lax.broadcasted_iota(jnp.int32, (1, LANES), 1)
    unswapped = jnp.where((lane // 32) % 2 == 0, 1.0, 0.0)
    roped = (t * cb_ref[...] + pltpu.roll(t, LANES - 32, axis=1) * sb_ref[...]) * unswapped
    moved = pltpu.roll(roped, 64, axis=1)
    k_rope_tile = jnp.where(lane >= 64, moved, 0.0)
    for h in range(H_A):
        k_ref[:, h * HEAD_PAD:(h + 1) * HEAD_PAD] = (k_nope[:, h * HEAD_PAD:(h + 1) * HEAD_PAD]
                                                      + k_rope_tile).astype(BF16)
    ki_ref[...] = jnp.where(lane < 32, moved, 0.0).astype(BF16)

    cos_a, sin_a = ca_ref[...], sa_ref[...]
    tp = x.shape[0]
    for g, (out_ref, (_, dilation)) in enumerate(zip((dq0_ref, dq1_ref, dq2_ref), DIL_PAIRS)):
        for which in range(2):
            scale = (D_HEAD_B ** -0.5 * LOG2_E) if which == 0 else 1.0
            c0 = C_QKVB + (which * N_GROUPS_B + g) * WIDTH_B
            x1, x2 = std[:, c0:c0 + 128], std[:, c0 + 128:c0 + 256]
            dil_scr[2 * which] = (x1 * cos_a - x2 * sin_a) * scale
            dil_scr[2 * which + 1] = (x2 * cos_a + x1 * sin_a) * scale
        dil_scr[4] = tail[:, g * WIDTH_B:g * WIDTH_B + 128]
        dil_scr[5] = tail[:, g * WIDTH_B + 128:(g + 1) * WIDTH_B]
        for c in range(3 * WIDTH_B // LANES):
            if dilation <= 4:
                for r in range(dilation):
                    out_ref[r, :, c * LANES:(c + 1) * LANES] = (
                        dil_scr[c, pl.ds(r, tp // dilation, stride=dilation), :].astype(BF16))
            else:
                quarter = tp // 4
                for r4 in range(4):
                    dil_tmp[c, r4 * quarter:(r4 + 1) * quarter, :] = dil_scr[c, pl.ds(r4, quarter, stride=4), :]
                for r in range(dilation):
                    out_ref[r, :, c * LANES:(c + 1) * LANES] = dil_tmp[
                        c, pl.ds((r % 4) * quarter + r // 4, tp // dilation, stride=dilation // 4), :].astype(BF16)
    gates_ref[...] = jax.nn.sigmoid(tail[:, N_GROUPS_B * WIDTH_B:]).astype(BF16)


def _proj_call(x, mod, g_pre, w, tabs):
    b, s, d = x.shape
    tt, ca, sa, cb, sb = tabs
    tp = TP_PROJ
    const = lambda a: pl.BlockSpec(a.shape, lambda i, j: (0,) * a.ndim)
    rows = lambda width: pl.BlockSpec((None, tp, width), lambda i, j: (i, j, 0))
    colsT = lambda height: pl.BlockSpec((None, height, tp), lambda i, j: (i, 0, j))
    consts = [g_pre, w["w_std"], w["w_tail"], w["w_tr"], w["g_cq_col"], w["g_ckv_col"],
              w["w_uq_t"], w["w_iq_t"], w["w_uk_p"], w["w_uv_t"]]
    hq = H_A * HEAD_PAD
    dil_shapes = [jax.ShapeDtypeStruct((b, dil, s // dil, 3 * WIDTH_B), BF16) for _, dil in DIL_PAIRS]
    dil_specs = [pl.BlockSpec((None, dil, tp // dil, 3 * WIDTH_B), lambda i, j: (i, 0, j, 0))
                 for _, dil in DIL_PAIRS]
    return pl.pallas_call(
        _proj_kernel,
        out_shape=(jax.ShapeDtypeStruct((b, hq, s), BF16),
                   jax.ShapeDtypeStruct((b, H_IDX * D_IDX, s), BF16),
                   jax.ShapeDtypeStruct((b, H_IDX, s), F32),
                   jax.ShapeDtypeStruct((b, s, hq), BF16),
                   jax.ShapeDtypeStruct((b, s, LANES), BF16),
                   jax.ShapeDtypeStruct((b, H_A * V_ROWS, s), BF16),
                   *dil_shapes,
                   jax.ShapeDtypeStruct((b, s, 2048), BF16)),
        grid=(b, s // tp),
        in_specs=[rows(d), pl.BlockSpec((None, N_MOD, d), lambda i, j: (i, 0, 0))]
                 + [const(a) for a in consts]
                 + [pl.BlockSpec((None, 2, 16, tp), lambda i, j: (i, 0, 0, j)),
                    rows(LANES), rows(LANES), rows(LANES), rows(LANES)],
        out_specs=(colsT(hq), colsT(H_IDX * D_IDX), colsT(H_IDX), rows(hq), rows(LANES), colsT(H_A * V_ROWS),
                   *dil_specs, rows(2048)),
        scratch_shapes=[pltpu.VMEM((3 * WIDTH_B // LANES, tp, LANES), F32)] * 2,
        compiler_params=_params(2),
        name="proj",
    )(x, mod, *consts, tt, ca, sa, cb, sb)


def _dsa_kernel(qit_ref, wit_ref, qt_ref, ki_ref, k_ref, vt_ref, o_ref,
                sc_ref, m_ref, acc_ref, s_ref, s2_ref, smax_ref, smax2_ref, qpad_ref, *, topk):
    tq = qt_ref.shape[1]
    tk = s_ref.shape[1]
    i = pl.program_id(1)
    assert tq == 2 * tk
    nkb = 2 * (i + 1)
    npair = i + 1
    qpos = i * tq + lax.broadcasted_iota(jnp.int32, (1, tq), 1)

    def fold(v):
        return jnp.sum(v.reshape(v.shape[0] // FOLD_ROWS, FOLD_ROWS, tq), axis=0)

    qpad_ref[...] = jnp.zeros(qpad_ref.shape, BF16)
    for h in range(H_IDX):
        qpad_ref[h * HEAD_PAD:h * HEAD_PAD + D_IDX, :] = qit_ref[h * D_IDX:(h + 1) * D_IDX, :]

    def park_logits(kb, park_ref, heads=range(H_IDX)):
        ki = ki_ref[pl.ds(pl.multiple_of(kb * tk, tk), tk), :]
        for h in heads:
            park_ref[h] = jnp.dot(ki, qpad_ref[h * HEAD_PAD:(h + 1) * HEAD_PAD, :], preferred_element_type=F32)

    def rows8(v, op):
        return op(v.reshape(IDX_ROWS // 8, 8, LANES), axis=0)

    piece_iota = lax.broadcasted_iota(jnp.int32, (IDX_ROWS, LANES), 0)
    n_col = tq // LANES
    heads_per_col = H_IDX // n_col

    def reduce_block(kb, park_ref, stats, on_diagonal, ahead=None, first_col=0):
        k0 = pl.multiple_of(kb * tk, tk)
        columns = []
        for c in range(n_col):
            if ahead is not None:
                park_logits(ahead[0], ahead[1], range(c * heads_per_col, (c + 1) * heads_per_col))
            lanes = slice(c * LANES, (c + 1) * LANES)
            if c < first_col:
                sc_ref[pl.ds(k0, tk), lanes] = jnp.full((tk, LANES), -jnp.inf, F32)
                columns.append(tuple(v[:, lanes] for v in stats))
                continue
            smax, smin, c_pos, c_nn = [v[:, lanes] for v in stats]
            w = [wit_ref[h:h + 1, lanes] for h in range(H_IDX)]
            for j in range(tk // IDX_ROWS):
                r0 = j * IDX_ROWS
                acc = jnp.zeros((IDX_ROWS, LANES), F32)
                for h in range(H_IDX):
                    acc = acc + w[h] * jnp.maximum(park_ref[h, r0:r0 + IDX_ROWS, lanes], 0.0)
                if on_diagonal:
                    causal = (k0 + r0 + piece_iota) <= qpos[:, lanes]
                    sv = jnp.where(causal, acc, -jnp.inf)
                    smin = jnp.minimum(smin, rows8(jnp.where(causal, acc, jnp.inf), jnp.min))
                else:
                    sv = acc
                    smin = jnp.minimum(smin, rows8(acc, jnp.min))
                sc_ref[pl.ds(k0 + r0, IDX_ROWS), lanes] = sv
                smax = jnp.maximum(smax, rows8(sv, jnp.max))
                c_pos = c_pos + rows8(jnp.where(sv > 0.0, 1.0, 0.0), jnp.sum)
                c_nn = c_nn + rows8(jnp.where(sv >= 0.0, 1.0, 0.0), jnp.sum)
            columns.append((smax, smin, c_pos, c_nn))
        return tuple(jnp.concatenate(parts, axis=1) for parts in zip(*columns))

    def idx_pair(kp, stats):
        stats = reduce_block(2 * kp, s_ref, stats, False, ahead=(2 * kp + 1, s2_ref))
        return reduce_block(2 * kp + 1, s2_ref, stats, False, ahead=(2 * kp + 2, s_ref))

    park_logits(0, s_ref)
    zero8 = jnp.zeros((8, tq), F32)
    stats = lax.fori_loop(0, i, idx_pair,
                          (jnp.full((8, tq), -jnp.inf, F32), jnp.full((8, tq), jnp.inf, F32), zero8, zero8))
    stats = reduce_block(nkb - 2, s_ref, stats, True, ahead=(nkb - 1, s2_ref))
    stats = reduce_block(nkb - 1, s2_ref, stats, True, first_col=(tq - tk) // LANES)
    smax = jnp.max(stats[0], axis=0, keepdims=True)
    smin = jnp.min(stats[1], axis=0, keepdims=True)
    c_pos, c_nn = [jnp.sum(v, axis=0, keepdims=True) for v in stats[2:]]

    n_causal = (qpos + 1).astype(F32)
    kf = jnp.minimum(qpos + 1, topk).astype(F32)

    def count_gt(thr):
        def pieces(k0, n_pieces, c):
            for j in range(n_pieces):
                rows = pl.ds(k0 + j * COUNT_ROWS, COUNT_ROWS)
                c = c + fold(jnp.where(sc_ref[rows, :] > thr, 1.0, 0.0))
            return c

        def body(kp, c):
            return pieces(pl.multiple_of(kp * (2 * tk), 2 * tk), 2 * tk // COUNT_ROWS, c)

        part = lax.fori_loop(0, npair - 1, body, jnp.zeros((FOLD_ROWS, tq), F32))
        k0 = pl.multiple_of((nkb - 2) * tk, tk)
        part = pieces(k0, tk // COUNT_ROWS, part)
        half = jnp.zeros((FOLD_ROWS, tk), F32)
        for j in range(tk // COUNT_ROWS):
            sv = sc_ref[pl.ds(k0 + tk + j * COUNT_ROWS, COUNT_ROWS), tq - tk:]
            half = half + jnp.sum(jnp.where(sv > thr[:, tq - tk:], 1.0, 0.0)
                                  .reshape(COUNT_ROWS // FOLD_ROWS, FOLD_ROWS, tk), axis=0)
        part = jnp.concatenate([part[:, :tq - tk], part[:, tq - tk:] + half], axis=1)
        return jnp.sum(part, axis=0, keepdims=True)

    below = smin - (1.0 + jnp.abs(smin))
    zero_tie = jnp.logical_and(c_pos < kf, kf <= c_nn)
    positive = c_pos >= kf
    searching = jnp.where(zero_tie, 0.0, 1.0)

    def bisect_cond(carry):
        return jnp.logical_and(carry[0] < BISECT_MAX, carry[1] > 0)

    def bisect_body(carry):
        it, _, lo, hi, c_lo, c_hi = carry
        for _ in range(BISECT_CHECK):
            mid = 0.5 * (lo + hi)
            c_mid = count_gt(mid)
            ge = c_mid >= kf
            up = jnp.logical_and(ge, searching > 0.0)
            down = jnp.logical_and(jnp.logical_not(ge), searching > 0.0)
            lo, c_lo = jnp.where(up, mid, lo), jnp.where(up, c_mid, c_lo)
            hi, c_hi = jnp.where(down, mid, hi), jnp.where(down, c_mid, c_hi)
        open_queries = (jnp.max((c_lo - kf) * searching) > 0.0).astype(jnp.int32)
        return it + BISECT_CHECK, open_queries, lo, hi, c_lo, c_hi

    lo0 = jnp.where(positive, 0.0, jnp.where(zero_tie, 0.0, below))
    hi0 = jnp.where(positive, smax, 0.0)
    c_lo0 = jnp.where(positive, c_pos, jnp.where(zero_tie, c_nn, n_causal))
    c_hi0 = jnp.where(positive, 0.0, c_pos)
    first_open = (jnp.max((c_lo0 - kf) * searching) > 0.0).astype(jnp.int32)
    _, _, lo, hi, c_lo, c_hi = lax.while_loop(
        bisect_cond, bisect_body, (jnp.int32(0), first_open, lo0, hi0, c_lo0, c_hi0))

    @pl.when(jnp.max(jnp.maximum(c_lo - kf, 1.0 - searching)) > 0.0)
    def _break_ties():
        need = kf - c_hi
        closed = 1.0 - searching
        prefix = jnp.where(lax.broadcasted_iota(jnp.int32, (tk, tk), 0)
                           >= lax.broadcasted_iota(jnp.int32, (tk, tk), 1), 1.0, 0.0).astype(BF16)

        def body(kp, seen):
            blocks = []
            for j in range(2):
                k0 = pl.multiple_of((2 * kp + j) * tk, tk)
                sv = sc_ref[pl.ds(k0, tk), :]
                above_lo = jnp.where(sv > lo, 1.0, jnp.where(sv >= lo, closed, 0.0))
                member = jnp.where(sv > hi, 0.0, above_lo)
                blocks.append((k0, member, jnp.dot(prefix, member.astype(BF16), preferred_element_type=F32)))
            for k0, member, within in blocks:
                rank = within + seen
                sv = sc_ref[pl.ds(k0, tk), :]
                sc_ref[pl.ds(k0, tk), :] = jnp.where(member > 0.0, jnp.where(rank > need, -jnp.inf, jnp.inf), sv)
                seen = rank[tk - 1:tk, :]
            return seen

        lax.fori_loop(0, npair, body, jnp.zeros((1, tq), F32))

    m_ref[...] = jnp.full(m_ref.shape, NEG_BIG, F32)
    acc_ref[...] = jnp.zeros(acc_ref.shape, F32)

    every = slice(0, tq)
    late = slice(tq - tk, tq)

    def selection_bias(kb, cols=every):
        return jnp.where(sc_ref[pl.ds(pl.multiple_of(kb * tk, tk), tk), cols] > lo[:, cols], 0.0, NEG_BIG)

    def score_head(kb, h, bias, park_ref, max_ref, cols=every):
        kh = k_ref[pl.ds(pl.multiple_of(kb * tk, tk), tk), h * HEAD_PAD:(h + 1) * HEAD_PAD]
        s = jnp.dot(kh, qt_ref[h * HEAD_PAD:(h + 1) * HEAD_PAD, cols], preferred_element_type=F32) + bias
        park_ref[h, :, cols] = s
        max_ref[h:h + 1, cols] = jnp.max(s, axis=0, keepdims=True)

    def softmax_head(kb, h, park_ref, max_ref, cols=every):
        m_old = m_ref[h:h + 1, cols]
        m_new = jnp.maximum(m_old, max_ref[h:h + 1, cols])
        alpha = jnp.exp2(m_old - m_new)
        p = jnp.exp2((park_ref[h, :, cols] - m_new).astype(BF16))
        rows = slice(h * V_ROWS, (h + 1) * V_ROWS)
        pv = jnp.dot(vt_ref[rows, pl.ds(pl.multiple_of(kb * tk, tk), tk)], p, preferred_element_type=F32)
        acc_ref[rows, cols] = alpha * acc_ref[rows, cols] + pv
        m_ref[h:h + 1, cols] = m_new

    bias0 = selection_bias(0)
    for h in range(H_A):
        score_head(0, h, bias0, s_ref, smax_ref)

    def att_body(kp, carry):
        odd, nxt = 2 * kp + 1, 2 * kp + 2
        bias = selection_bias(odd)
        for h in range(H_A):
            score_head(odd, h, bias, s2_ref, smax2_ref)
            softmax_head(2 * kp, h, s_ref, smax_ref)
        bias = selection_bias(nxt)
        for h in range(H_A):
            score_head(nxt, h, bias, s_ref, smax_ref)
            softmax_head(odd, h, s2_ref, smax2_ref)
        return carry

    lax.fori_loop(0, npair - 1, att_body, 0)
    bias = selection_bias(nkb - 1, late)
    for h in range(H_A):
        score_head(nkb - 1, h, bias, s2_ref, smax2_ref, late)
        softmax_head(nkb - 2, h, s_ref, smax_ref)
    for h in range(H_A):
        softmax_head(nkb - 1, h, s2_ref, smax2_ref, late)
    out = [acc_ref[h * V_ROWS:h * V_ROWS + D_V_A, :] / acc_ref[h * V_ROWS + D_V_A:h * V_ROWS + D_V_A + 1, :]
           for h in range(H_A)]
    o_ref[...] = jnp.concatenate(out, axis=0).T.astype(BF16)


def _dsa_call(qt, qit, wit, k, ki, vt):
    b, hq, s = qt.shape
    tq = TQ_DSA
    topk = min(TOPK_MAX, s // 4)
    colsT = lambda height: pl.BlockSpec((None, height, tq), lambda i, j: (i, 0, j))
    whole = lambda a: pl.BlockSpec((None,) + a.shape[1:], lambda i, j: (i, 0, 0))
    return pl.pallas_call(
        functools.partial(_dsa_kernel, topk=topk),
        out_shape=jax.ShapeDtypeStruct((b, s, WIDTH_A), BF16),
        grid=(b, s // tq),
        in_specs=[colsT(H_IDX * D_IDX), colsT(H_IDX), colsT(hq), whole(ki), whole(k), whole(vt)],
        out_specs=pl.BlockSpec((None, tq, WIDTH_A), lambda i, j: (i, j, 0)),
        scratch_shapes=[pltpu.VMEM((s, tq), F32),
                        pltpu.VMEM((H_A, tq), F32),
                        pltpu.VMEM((H_A * V_ROWS, tq), F32),
                        pltpu.VMEM((H_A, TK_DSA, tq), F32),
                        pltpu.VMEM((H_A, TK_DSA, tq), F32),
                        pltpu.VMEM((H_A, tq), F32),
                        pltpu.VMEM((H_A, tq), F32),
                        pltpu.VMEM((H_IDX * HEAD_PAD, tq), BF16)],
        compiler_params=_params(2),
        name="dsa",
    )(qit, wit, qt, ki, k, vt)


def _dil_kernel(q_ref, kc_ref, kp_ref, vc_ref, vp_ref, o_ref, lse_ref, s_ref):
    for res in range(q_ref.shape[0]):
        _dil_residue(q_ref.at[res], kc_ref.at[res], kp_ref.at[res], vc_ref.at[res], vp_ref.at[res],
                     o_ref.at[res], lse_ref.at[res], s_ref.at[res])


def _dil_residue(q_ref, kc_ref, kp_ref, vc_ref, vp_ref, o_ref, lse_ref, s_ref):
    tb = kp_ref.shape[0]
    n_sub = q_ref.shape[0] // tb
    first_step = pl.program_id(2) == 0
    k = jnp.concatenate([kp_ref[...], kc_ref[...]], axis=0)
    v = jnp.concatenate([vp_ref[...], vc_ref[...]], axis=0)
    r = lax.broadcasted_iota(jnp.int32, (tb, 2 * tb), 0)
    c = lax.broadcasted_iota(jnp.int32, (tb, 2 * tb), 1)
    behind = (tb + r - c).astype(jnp.uint32)
    bias = jnp.where(behind <= tb, 0.0, -jnp.inf)
    no_prev = jnp.where(first_step, r, tb).astype(jnp.uint32)
    bias0 = jnp.where(behind <= no_prev, 0.0, -jnp.inf)
    lane = lax.broadcasted_iota(jnp.int32, (1, WIDTH_B), 1)
    head_qk = (lane % 128) // (D_HEAD_B // 2)
    head_v = lane // D_HEAD_B
    nt = (((1,), (1,)), ((), ()))
    for j in range(n_sub):
        q = q_ref[j * tb:(j + 1) * tb, :]
        for h in range(H_B):
            qh = jnp.where(head_qk == h, q, jnp.zeros_like(q))
            s_ref[j * H_B + h] = (lax.dot_general(qh, k[j * tb:(j + 2) * tb], nt, preferred_element_type=F32)
                                  + (bias0 if j == 0 else bias))
    for j in range(n_sub):
        vj = v[j * tb:(j + 2) * tb]
        num = jnp.zeros((tb, WIDTH_B), F32)
        den = jnp.zeros((tb, WIDTH_B), F32)
        top = jnp.zeros((tb, WIDTH_B), F32)
        for h in range(H_B):
            s = s_ref[j * H_B + h]
            m = jnp.max(s, axis=1, keepdims=True)
            p = jnp.exp2(s - m).astype(BF16)
            pv = jnp.dot(p, jnp.where(head_v == h, vj, jnp.ones_like(vj)), preferred_element_type=F32)
            mine = head_v == h
            num = jnp.where(mine, pv, num)
            den = jnp.where(mine, pltpu.roll(pv, WIDTH_B // 2, axis=1), den)
            top = jnp.where(mine, m, top)
        o_ref[j * tb:(j + 1) * tb, :] = (num / den).astype(o_ref.dtype)
        lse_ref[j * tb:(j + 1) * tb, :] = top + jnp.log2(den)


def _dil_call(qkv):
    b, dilation, n, _ = qkv.shape
    tb, tq = TB_DIL, min(TQ_DIL, n)
    per = tq // tb
    n_res = max(1, min(dilation, TQ_DIL // tq))
    cur = lambda which: pl.BlockSpec((None, n_res, tq, WIDTH_B), lambda i, r, j: (i, r, j, which))
    prev = lambda which: pl.BlockSpec((None, n_res, tb, WIDTH_B),
                                      lambda i, r, j: (i, r, jnp.maximum(j * per - 1, 0), which))
    out_spec = pl.BlockSpec((None, n_res, tq, WIDTH_B), lambda i, r, j: (i, r, j, 0))
    out = lambda dtype: jax.ShapeDtypeStruct((b, dilation, n, WIDTH_B), dtype)
    return pl.pallas_call(
        _dil_kernel,
        out_shape=(out(BF16), out(F32)),
        grid=(b, dilation // n_res, n // tq),
        in_specs=[cur(0), cur(1), prev(1), cur(2), prev(2)],
        out_specs=(out_spec, out_spec),
        scratch_shapes=[pltpu.VMEM((n_res, per * H_B, tb, 2 * tb), F32)],
        compiler_params=_params(3),
        name="dil",
    )(qkv, qkv, qkv, qkv, qkv)


def _merge_ffn_kernel(x_ref, mod_ref, oa_ref, o0_ref, o1_ref, o2_ref, l0_ref, l1_ref, l2_ref, gates_ref,
                      wua_ref, wub_ref, wo_ref, gpost_ref, gpre2_ref, wg_ref, wu_ref, wd_ref, gpost2_ref,
                      out_ref, acc_ref, *scr):
    tm = x_ref.shape[0]
    tmp = scr[4]

    def token_major(ref, buf):
        dilation = ref.shape[0]
        if dilation == 1:
            return ref[0]
        n_slab = ref.shape[2] // LANES
        for c in range(n_slab):
            if dilation <= 4:
                for r in range(dilation):
                    buf[c, pl.ds(r, tm // dilation, stride=dilation), :] = (
                        ref[r, :, c * LANES:(c + 1) * LANES].astype(F32))
            else:
                quarter = tm // 4
                for r in range(dilation):
                    tmp[c, pl.ds((r % 4) * quarter + r // 4, tm // dilation, stride=dilation // 4), :] = (
                        ref[r, :, c * LANES:(c + 1) * LANES].astype(F32))
                for r4 in range(4):
                    buf[c, pl.ds(r4, quarter, stride=4), :] = tmp[c, r4 * quarter:(r4 + 1) * quarter, :]
        return jnp.concatenate([buf[c] for c in range(n_slab)], axis=1)

    o0, l0 = token_major(o0_ref, None).astype(F32), token_major(l0_ref, None)
    o1, l1 = token_major(o1_ref, scr[0]), token_major(l1_ref, scr[1])
    o2, l2 = token_major(o2_ref, scr[2]), token_major(l2_ref, scr[3])
    m = jnp.maximum(jnp.maximum(l0, l1), l2)
    e0, e1, e2 = jnp.exp2(l0 - m), jnp.exp2(l1 - m), jnp.exp2(l2 - m)
    ob = (e0 * o0 + e1 * o1 + e2 * o2) / (e0 + e1 + e2)
    za = jnp.dot(oa_ref[...], wua_ref[...], preferred_element_type=F32)
    zb = jnp.dot(ob.astype(BF16), wub_ref[...], preferred_element_type=F32)
    d = za.shape[1]
    z = gates_ref[:, 0:d].astype(F32) * za + gates_ref[:, d:2 * d].astype(F32) * zb
    y = jnp.dot(z.astype(BF16), wo_ref[...], preferred_element_type=F32)
    x_mixed = x_ref[...] + mod_ref[5:6, :] * (_rms_rows(y) * gpost_ref[...])
    _ffn_tile(x_mixed, mod_ref, gpre2_ref, wg_ref, wu_ref, wd_ref, gpost2_ref, out_ref, acc_ref, 6)


def _merge_ffn_call(x, mod, oa, dil, gates, w_up_a, w_up_b, w_o, g_post, g_pre2, w_gate, w_up, w_down, g_post2):
    b, s, d = x.shape
    tm = TM_FFN
    rows = lambda width: pl.BlockSpec((None, tm, width), lambda i, j: (i, j, 0))
    const = lambda a: pl.BlockSpec(a.shape, lambda i, j: (0,) * a.ndim, pipeline_mode=pl.Buffered(1))
    (o0, l0), (o1, l1), (o2, l2) = dil
    res = lambda a: pl.BlockSpec((None, a.shape[1], tm // a.shape[1], WIDTH_B), lambda i, j: (i, 0, j, 0))
    consts = [w_up_a, w_up_b, w_o, g_post, g_pre2, w_gate, w_up, w_down, g_post2]
    return pl.pallas_call(
        _merge_ffn_kernel,
        out_shape=jax.ShapeDtypeStruct((b, s, d), F32),
        grid=(b, s // tm),
        in_specs=[rows(d), pl.BlockSpec((None, N_MOD, d), lambda i, j: (i, 0, 0)), rows(WIDTH_A)]
                 + [res(a) for a in (o0, o1, o2, l0, l1, l2)]
                 + [rows(2 * d)] + [const(a) for a in consts],
        out_specs=rows(d),
        scratch_shapes=[pltpu.VMEM((tm, d), F32)] + [pltpu.VMEM((WIDTH_B // LANES, tm, LANES), F32)] * 5,
        compiler_params=_params(2),
        name="merge_ffn",
    )(x, mod, oa, o0, o1, o2, l0, l1, l2, gates, *consts)


def _mixer_weights(w_in, g_cq, g_ckv, w_uq, w_uk, w_uv, w_iq):
    d = w_in.shape[0]
    w_in = w_in.astype(BF16)
    o_cq, o_ckv, o_kr, o_ki, o_wi = 0, Q_LORA, Q_LORA + KV_LORA, Q_LORA + KV_LORA + D_ROPE_A, \
        Q_LORA + KV_LORA + D_ROPE_A + D_IDX
    o_qkv = o_wi + H_IDX
    o_gates = o_qkv + N_QKV_B

    def swap_halves(w):
        half = w.shape[1] // 2
        return jnp.concatenate([w[:, half:], w[:, :half]], axis=1)

    w_kr, w_ki = w_in[:, o_kr:o_kr + D_ROPE_A], w_in[:, o_ki:o_ki + D_IDX]
    qkv = w_in[:, o_qkv:o_gates].reshape(d, 3, N_GROUPS_B, H_B, 2, D_HEAD_B // 2)
    qk_split = qkv[:, 0:2].transpose(0, 1, 2, 4, 3, 5).reshape(d, 2 * N_GROUPS_B * WIDTH_B)
    w_std = jnp.concatenate([w_kr, swap_halves(w_kr), w_ki, swap_halves(w_ki), qk_split], axis=1)
    w_tail = w_in[:, o_qkv + 2 * N_GROUPS_B * WIDTH_B:]
    w_tr = jnp.concatenate([w_in[:, o_cq:o_cq + Q_LORA], w_in[:, o_ckv:o_ckv + KV_LORA],
                            w_in[:, o_wi:o_wi + H_IDX]], axis=1).T
    w_tr = jnp.pad(w_tr, ((0, N_TR - w_tr.shape[0]), (0, 0)))

    return {
        "w_std": w_std.astype(BF16),
        "w_tail": w_tail.astype(BF16),
        "w_tr": w_tr.astype(BF16),
        "g_cq_col": g_cq.reshape(Q_LORA, 1),
        "g_ckv_col": g_ckv.reshape(KV_LORA, 1),
        "w_uq_t": w_uq.transpose(1, 2, 0).reshape(H_A * (D_NOPE + D_ROPE_A), Q_LORA).astype(BF16),
        "w_iq_t": w_iq.transpose(1, 2, 0).reshape(H_IDX * D_IDX, Q_LORA).astype(BF16),
        "w_uk_p": jnp.pad(w_uk, ((0, 0), (0, 0), (0, HEAD_PAD - D_NOPE))).reshape(KV_LORA, H_A * HEAD_PAD).astype(BF16),
        "w_uv_t": w_uv.transpose(1, 2, 0).reshape(WIDTH_A, KV_LORA).astype(BF16),
    }


def kernel(x, c, positions, w_mod, b_mod, g_pre_ffn1, w_gate1, w_up1, w_down1, g_post_ffn1, g_pre_mix, w_in, g_cq, g_ckv, w_uq, w_uk, w_uv, w_iq, w_up_a, w_up_b, w_o, g_post_mix, g_pre_ffn2, w_gate2, w_up2, w_down2, g_post_ffn2):
    b, s, d = x.shape
    assert s % (16 * TB_DIL) == 0 and s % TM_FFN == 0 and b <= 8
    c8 = jnp.pad(c, ((0, 8 - b), (0, 0)))
    half_b, half_a = D_HEAD_B // 2, D_ROPE_A // 2
    inv = jnp.concatenate([ROPE_THETA ** (-jnp.arange(half_b, dtype=F32) / half_b),
                           ROPE_THETA ** (-jnp.arange(half_a, dtype=F32) / half_a)]).reshape(48, 1)
    row = lambda g: g.reshape(1, -1)
    for l in range(w_mod.shape[0]):
        if l == 0:
            mod, *tabs = _mod_rope_call(c8, w_mod[l], row(b_mod[l]), positions.reshape(b, 1, s), inv)
        else:
            mod = _mod_call(c8, w_mod[l], row(b_mod[l]))
        mod = mod[:b].reshape(b, N_MOD, d)
        x = _ffn_call(x, mod, row(g_pre_ffn1[l]), w_gate1[l].astype(BF16), w_up1[l].astype(BF16),
                      w_down1[l].astype(BF16), row(g_post_ffn1[l]), 0)
        w = _mixer_weights(w_in[l], g_cq[l], g_ckv[l], w_uq[l], w_uk[l], w_uv[l], w_iq[l])
        qt, qit, wit, k, ki, vt, dq0, dq1, dq2, gates = _proj_call(x, mod, row(g_pre_mix[l]), w, tabs)
        oa = _dsa_call(qt, qit, wit, k, ki, vt)
        dil = [_dil_call(dq) for dq in (dq0, dq1, dq2)]
        x = _merge_ffn_call(x, mod, oa, dil, gates, w_up_a[l].astype(BF16), w_up_b[l].astype(BF16),
                            w_o[l].astype(BF16), row(g_post_mix[l]), row(g_pre_ffn2[l]), w_gate2[l].astype(BF16),
                            w_up2[l].astype(BF16), w_down2[l].astype(BF16), row(g_post_ffn2[l]))
    return x
```

```python
import functools

import jax
import jax.numpy as jnp
from jax import lax
from jax.experimental import pallas as pl
from jax.experimental.pallas import tpu as pltpu

F32 = jnp.float32
BF16 = jnp.bfloat16

ROPE_THETA = 10000.0
NORM_EPS = 1e-6
H_A = 8
Q_LORA = 384
KV_LORA = 256
D_NOPE = 64
D_ROPE_A = 32
D_V_A = 64
H_IDX = 8
D_IDX = 32
TOPK_MAX = 256
DIL_PAIRS = ((128, 1), (512, 4), (2048, 16))
N_GROUPS_B = 3
H_B = 4
D_HEAD_B = 64
N_MOD = 9
WIDTH_A = H_A * D_V_A
WIDTH_B = H_B * D_HEAD_B
N_QKV_B = 3 * N_GROUPS_B * WIDTH_B

LANES = 128
HEAD_PAD = 128
VMEM_LIMIT = 56 * 1024 * 1024

TM_FFN = 512
CH_FFN = 256
TP_PROJ = 512
TQ_DSA = 512
TK_DSA = 256
V_ROWS = D_V_A + 16
TB_DIL = 128
TQ_DIL = 2048
BISECT_MAX = 64
BISECT_CHECK = 2
FOLD_ROWS = 16
COUNT_ROWS = 64
IDX_ROWS = 64
NEG_BIG = -1e30
LOG2_E = 1.4426950408889634

C_KR = 0
C_QKVB = 128
R_CQ = 0
R_CKV = Q_LORA
R_WI = Q_LORA + KV_LORA
N_TR = 656


def _params(n_axes):
    return pltpu.CompilerParams(dimension_semantics=("arbitrary",) * n_axes,
                                vmem_limit_bytes=VMEM_LIMIT)


def _rms_rows(x):
    return x * lax.rsqrt(jnp.mean(x * x, axis=-1, keepdims=True) + NORM_EPS)


def _mod_kernel(c_ref, w_ref, b_ref, o_ref):
    c = c_ref[...]
    o_ref[...] = jnp.dot(c * jax.nn.sigmoid(c), w_ref[...], preferred_element_type=F32) + b_ref[...]


def _mod_call(c8, w_mod, b_mod):
    d = c8.shape[1]
    return pl.pallas_call(
        _mod_kernel,
        out_shape=jax.ShapeDtypeStruct((8, N_MOD * d), F32),
        grid=(N_MOD,),
        in_specs=[pl.BlockSpec((8, d), lambda j: (0, 0)),
                  pl.BlockSpec((d, d), lambda j: (0, j)),
                  pl.BlockSpec((1, d), lambda j: (0, j))],
        out_specs=pl.BlockSpec((8, d), lambda j: (0, j)),
        compiler_params=_params(1),
        name="mod",
    )(c8, w_mod, b_mod)


def _rope_kernel(pos_ref, inv_ref, tt_ref, ca_ref, sa_ref, cb_ref, sb_ref):
    s = pos_ref.shape[1]
    ang = inv_ref[...] * pos_ref[...].astype(F32)
    cos = jnp.cos(ang)
    sin = jnp.sin(ang)
    cos_b, sin_b = cos[32:48], sin[32:48]
    tt_ref[0] = cos_b
    tt_ref[1] = sin_b
    cos_a = jnp.concatenate([cos[0:32]] * 4, axis=0)
    sin_a = jnp.concatenate([sin[0:32]] * 4, axis=0)
    cos_bt = jnp.concatenate([cos_b] * 8, axis=0)
    sin_bt = jnp.concatenate([-sin_b, sin_b] * 4, axis=0)
    for j in range(s // LANES):
        cols = slice(j * LANES, (j + 1) * LANES)
        ca_ref[cols, :] = cos_a[:, cols].T
        sa_ref[cols, :] = sin_a[:, cols].T
        cb_ref[cols, :] = cos_bt[:, cols].T
        sb_ref[cols, :] = sin_bt[:, cols].T


def _mod_rope_kernel(c_ref, w_ref, b_ref, pos_ref, inv_ref, o_ref, tt_ref, ca_ref, sa_ref, cb_ref, sb_ref, *,
                     n_batch):
    _mod_kernel(c_ref, w_ref, b_ref, o_ref)

    @pl.when(pl.program_id(0) < n_batch)
    def _tables():
        _rope_kernel(pos_ref, inv_ref, tt_ref, ca_ref, sa_ref, cb_ref, sb_ref)


def _mod_rope_call(c8, w_mod, b_mod, pos3, inv):
    d = c8.shape[1]
    b, _, s = pos3.shape
    assert b <= N_MOD
    row = lambda j: jnp.minimum(j, b - 1)
    tab = jax.ShapeDtypeStruct((b, s, LANES), F32)
    tab_spec = pl.BlockSpec((None, s, LANES), lambda j: (row(j), 0, 0))
    return pl.pallas_call(
        functools.partial(_mod_rope_kernel, n_batch=b),
        out_shape=(jax.ShapeDtypeStruct((8, N_MOD * d), F32),
                   jax.ShapeDtypeStruct((b, 2, 16, s), F32), tab, tab, tab, tab),
        grid=(N_MOD,),
        in_specs=[pl.BlockSpec((8, d), lambda j: (0, 0)),
                  pl.BlockSpec((d, d), lambda j: (0, j)),
                  pl.BlockSpec((1, d), lambda j: (0, j)),
                  pl.BlockSpec((None, 1, s), lambda j: (row(j), 0, 0)),
                  pl.BlockSpec((48, 1), lambda j: (0, 0))],
        out_specs=(pl.BlockSpec((8, d), lambda j: (0, j)),
                   pl.BlockSpec((None, 2, 16, s), lambda j: (row(j), 0, 0, 0)),
                   tab_spec, tab_spec, tab_spec, tab_spec),
        compiler_params=_params(1),
        name="mod_rope",
    )(c8, w_mod, b_mod, pos3, inv)


def _ffn_kernel(x_ref, mod_ref, gpre_ref, wg_ref, wu_ref, wd_ref, gpost_ref, o_ref, acc_ref, *, j0):
    _ffn_tile(x_ref[...], mod_ref, gpre_ref, wg_ref, wu_ref, wd_ref, gpost_ref, o_ref, acc_ref, j0)


def _ffn_tile(x, mod_ref, gpre_ref, wg_ref, wu_ref, wd_ref, gpost_ref, o_ref, acc_ref, j0):
    sh, sc, gt = mod_ref[j0:j0 + 1, :], mod_ref[j0 + 1:j0 + 2, :], mod_ref[j0 + 2:j0 + 3, :]
    h = (_rms_rows(x) * gpre_ref[...] * (1.0 + sc) + sh).astype(BF16)
    n_ch = wg_ref.shape[1] // CH_FFN
    for ch in range(n_ch):
        cols = slice(ch * CH_FFN, (ch + 1) * CH_FFN)
        g = jnp.dot(h, wg_ref[:, cols], preferred_element_type=F32)
        u = jnp.dot(h, wu_ref[:, cols], preferred_element_type=F32)
        a = (g * jax.nn.sigmoid(g) * u).astype(BF16)
        part = jnp.dot(a, wd_ref[cols, :], preferred_element_type=F32)
        if ch == 0:
            acc_ref[...] = part
        else:
            acc_ref[...] += part
    o_ref[...] = x + 0.5 * gt * (_rms_rows(acc_ref[...]) * gpost_ref[...])


def _ffn_call(x, mod, g_pre, w_gate, w_up, w_down, g_post, j0):
    b, s, d = x.shape
    f = w_gate.shape[1]
    const = lambda shape: pl.BlockSpec(shape, lambda i, j: (0,) * len(shape), pipeline_mode=pl.Buffered(1))
    tile = pl.BlockSpec((None, TM_FFN, d), lambda i, j: (i, j, 0))
    return pl.pallas_call(
        functools.partial(_ffn_kernel, j0=j0),
        out_shape=jax.ShapeDtypeStruct((b, s, d), F32),
        grid=(b, s // TM_FFN),
        in_specs=[tile,
                  pl.BlockSpec((None, N_MOD, d), lambda i, j: (i, 0, 0)),
                  const((1, d)), const((d, f)), const((d, f)), const((f, d)), const((1, d))],
        out_specs=tile,
        scratch_shapes=[pltpu.VMEM((TM_FFN, d), F32)],
        compiler_params=_params(2),
        name="ffn",
    )(x, mod, g_pre, w_gate, w_up, w_down, g_post)


def _proj_kernel(x_ref, mod_ref, gpre_ref, wstd_ref, wtail_ref, wtr_ref, gcq_ref, gckv_col_ref,
                 wuq_ref, wiq_ref, wuk_ref, wuv_ref,
                 tt_ref, ca_ref, sa_ref, cb_ref, sb_ref,
                 qt_ref, qit_ref, wit_ref, k_ref, ki_ref, vt_ref, dq0_ref, dq1_ref, dq2_ref, gates_ref,
                 dil_scr, dil_tmp):
    x = x_ref[...]
    sh, sc = mod_ref[3:4, :], mod_ref[4:5, :]
    u = (_rms_rows(x) * gpre_ref[...] * (1.0 + sc) + sh).astype(BF16)
    std = jnp.dot(u, wstd_ref[...], preferred_element_type=F32)
    tail = jnp.dot(u, wtail_ref[...], preferred_element_type=F32)
    tr = lax.dot_general(wtr_ref[...], u, (((1,), (1,)), ((), ())),
                         preferred_element_type=F32)

    cq = tr[R_CQ:R_CQ + Q_LORA]
    cq = cq * lax.rsqrt(jnp.mean(cq * cq, axis=0, keepdims=True) + NORM_EPS) * gcq_ref[...]
    cq = cq.astype(BF16)
    ckv_t = tr[R_CKV:R_CKV + KV_LORA]
    ckv_t = ckv_t * lax.rsqrt(jnp.mean(ckv_t * ckv_t, axis=0, keepdims=True) + NORM_EPS) * gckv_col_ref[...]
    cos_t, sin_t = tt_ref[0], tt_ref[1]

    attn_scale = (D_NOPE + D_ROPE_A) ** -0.5 * LOG2_E
    q_t = jnp.dot(wuq_ref[...], cq, preferred_element_type=F32) * attn_scale
    idx_scale = D_IDX ** -0.5
    qi_t = jnp.dot(wiq_ref[...], cq, preferred_element_type=F32) * idx_scale
    d_qk = D_NOPE + D_ROPE_A
    for h in range(H_A):
        r0, s0 = h * HEAD_PAD, h * d_qk
        x1, x2 = q_t[s0 + 64:s0 + 80], q_t[s0 + 80:s0 + 96]
        qt_ref[r0:r0 + 64, :] = q_t[s0:s0 + 64].astype(BF16)
        qt_ref[r0 + 64:r0 + 80, :] = (x1 * cos_t - x2 * sin_t).astype(BF16)
        qt_ref[r0 + 80:r0 + 96, :] = (x2 * cos_t + x1 * sin_t).astype(BF16)
        qt_ref[r0 + 96:r0 + 128, :] = jnp.zeros((HEAD_PAD - d_qk, q_t.shape[1]), BF16)
        i0 = h * D_IDX
        y1, y2 = qi_t[i0:i0 + 16], qi_t[i0 + 16:i0 + 32]
        qit_ref[i0:i0 + 16, :] = (y1 * cos_t - y2 * sin_t).astype(BF16)
        qit_ref[i0 + 16:i0 + 32, :] = (y2 * cos_t + y1 * sin_t).astype(BF16)
    wit_ref[...] = tr[R_WI:R_WI + H_IDX] * (H_IDX ** -0.5)
    v_t = jnp.dot(wuv_ref[...], ckv_t.astype(BF16), preferred_element_type=F32).astype(BF16)
    for h in range(H_A):
        vt_ref[h * V_ROWS:h * V_ROWS + D_V_A, :] = v_t[h * D_V_A:(h + 1) * D_V_A]
        vt_ref[h * V_ROWS + D_V_A:(h + 1) * V_ROWS, :] = jnp.ones((V_ROWS - D_V_A, v_t.shape[1]), BF16)

    ckv = ckv_t.T.astype(BF16)
    k_nope = jnp.dot(ckv, wuk_ref[...], preferred_element_type=F32)
    t = std[:, C_KR:C_KR + LANES]
    lane = lax.broadcasted_iota(jnp.int32, (1, LANES), 1)
    unswapped = jnp.where((lane // 32) % 2 == 0, 1.0, 0.0)
    roped = (t * cb_ref[...] + pltpu.roll(t, LANES - 32, axis=1) * sb_ref[...]) * unswapped
    moved = pltpu.roll(roped, 64, axis=1)
    k_rope_tile = jnp.where(lane >= 64, moved, 0.0)
    for h in range(H_A):
        k_ref[:, h * HEAD_PAD:(h + 1) * HEAD_PAD] = (k_nope[:, h * HEAD_PAD:(h + 1) * HEAD_PAD]
                                                      + k_rope_tile).astype(BF16)
    ki_ref[...] = jnp.where(lane < 32, moved, 0.0).astype(BF16)

    cos_a, sin_a = ca_ref[...], sa_ref[...]
    tp = x.shape[0]
    for g, (out_ref, (_, dilation)) in enumerate(zip((dq0_ref, dq1_ref, dq2_ref), DIL_PAIRS)):
        for which in range(2):
            scale = (D_HEAD_B ** -0.5 * LOG2_E) if which == 0 else 1.0
            c0 = C_QKVB + (which * N_GROUPS_B + g) * WIDTH_B
            x1, x2 = std[:, c0:c0 + 128], std[:, c0 + 128:c0 + 256]
            dil_scr[2 * which] = (x1 * cos_a - x2 * sin_a) * scale
            dil_scr[2 * which + 1] = (x2 * cos_a + x1 * sin_a) * scale
        dil_scr[4] = tail[:, g * WIDTH_B:g * WIDTH_B + 128]
        dil_scr[5] = tail[:, g * WIDTH_B + 128:(g + 1) * WIDTH_B]
        for c in range(3 * WIDTH_B // LANES):
            if dilation <= 4:
                for r in range(dilation):
                    out_ref[r, :, c * LANES:(c + 1) * LANES] = (
                        dil_scr[c, pl.ds(r, tp // dilation, stride=dilation), :].astype(BF16))
            else:
                quarter = tp // 4
                for r4 in range(4):
                    dil_tmp[c, r4 * quarter:(r4 + 1) * quarter, :] = dil_scr[c, pl.ds(r4, quarter, stride=4), :]
                for r in range(dilation):
                    out_ref[r, :, c * LANES:(c + 1) * LANES] = dil_tmp[
                        c, pl.ds((r % 4) * quarter + r // 4, tp // dilation, stride=dilation // 4), :].astype(BF16)
    gates_ref[...] = jax.nn.sigmoid(tail[:, N_GROUPS_B * WIDTH_B:]).astype(BF16)


def _proj_call(x, mod, g_pre, w, tabs):
    b, s, d = x.shape
    tt, ca, sa, cb, sb = tabs
    tp = TP_PROJ
    const = lambda a: pl.BlockSpec(a.shape, lambda i, j: (0,) * a.ndim)
    rows = lambda width: pl.BlockSpec((None, tp, width), lambda i, j: (i, j, 0))
    colsT = lambda height: pl.BlockSpec((None, height, tp), lambda i, j: (i, 0, j))
    consts = [g_pre, w["w_std"], w["w_tail"], w["w_tr"], w["g_cq_col"], w["g_ckv_col"],
              w["w_uq_t"], w["w_iq_t"], w["w_uk_p"], w["w_uv_t"]]
    hq = H_A * HEAD_PAD
    dil_shapes = [jax.ShapeDtypeStruct((b, dil, s // dil, 3 * WIDTH_B), BF16) for _, dil in DIL_PAIRS]
    dil_specs = [pl.BlockSpec((None, dil, tp // dil, 3 * WIDTH_B), lambda i, j: (i, 0, j, 0))
                 for _, dil in DIL_PAIRS]
    return pl.pallas_call(
        _proj_kernel,
        out_shape=(jax.ShapeDtypeStruct((b, hq, s), BF16),
                   jax.ShapeDtypeStruct((b, H_IDX * D_IDX, s), BF16),
                   jax.ShapeDtypeStruct((b, H_IDX, s), F32),
                   jax.ShapeDtypeStruct((b, s, hq), BF16),
                   jax.ShapeDtypeStruct((b, s, LANES), BF16),
                   jax.ShapeDtypeStruct((b, H_A * V_ROWS, s), BF16),
                   *dil_shapes,
                   jax.ShapeDtypeStruct((b, s, 2048), BF16)),
        grid=(b, s // tp),
        in_specs=[rows(d), pl.BlockSpec((None, N_MOD, d), lambda i, j: (i, 0, 0))]
                 + [const(a) for a in consts]
                 + [pl.BlockSpec((None, 2, 16, tp), lambda i, j: (i, 0, 0, j)),
                    rows(LANES), rows(LANES), rows(LANES), rows(LANES)],
        out_specs=(colsT(hq), colsT(H_IDX * D_IDX), colsT(H_IDX), rows(hq), rows(LANES), colsT(H_A * V_ROWS),
                   *dil_specs, rows(2048)),
        scratch_shapes=[pltpu.VMEM((3 * WIDTH_B // LANES, tp, LANES), F32)] * 2,
        compiler_params=_params(2),
        name="proj",
    )(x, mod, *consts, tt, ca, sa, cb, sb)


def _dsa_kernel(qit_ref, wit_ref, qt_ref, ki_ref, k_ref, vt_ref, o_ref,
                sc_ref, m_ref, acc_ref, s_ref, s2_ref, smax_ref, smax2_ref, qpad_ref, *, topk):
    tq = qt_ref.shape[1]
    tk = s_ref.shape[1]
    i = pl.program_id(1)
    assert tq == 2 * tk
    nkb = 2 * (i + 1)
    npair = i + 1
    qpos = i * tq + lax.broadcasted_iota(jnp.int32, (1, tq), 1)

    def fold(v):
        return jnp.sum(v.reshape(v.shape[0] // FOLD_ROWS, FOLD_ROWS, tq), axis=0)

    qpad_ref[...] = jnp.zeros(qpad_ref.shape, BF16)
    for h in range(H_IDX):
        qpad_ref[h * HEAD_PAD:h * HEAD_PAD + D_IDX, :] = qit_ref[h * D_IDX:(h + 1) * D_IDX, :]

    def park_logits(kb, park_ref, heads=range(H_IDX)):
        ki = ki_ref[pl.ds(pl.multiple_of(kb * tk, tk), tk), :]
        for h in heads:
            park_ref[h] = jnp.dot(ki, qpad_ref[h * HEAD_PAD:(h + 1) * HEAD_PAD, :], preferred_element_type=F32)

    def rows8(v, op):
        return op(v.reshape(IDX_ROWS // 8, 8, LANES), axis=0)

    piece_iota = lax.broadcasted_iota(jnp.int32, (IDX_ROWS, LANES), 0)
    n_col = tq // LANES
    heads_per_col = H_IDX // n_col

    def reduce_block(kb, park_ref, stats, on_diagonal, ahead=None, first_col=0):
        k0 = pl.multiple_of(kb * tk, tk)
        columns = []
        for c in range(n_col):
            if ahead is not None:
                park_logits(ahead[0], ahead[1], range(c * heads_per_col, (c + 1) * heads_per_col))
            lanes = slice(c * LANES, (c + 1) * LANES)
            if c < first_col:
                sc_ref[pl.ds(k0, tk), lanes] = jnp.full((tk, LANES), -jnp.inf, F32)
                columns.append(tuple(v[:, lanes] for v in stats))
                continue
            smax, smin, c_pos, c_nn = [v[:, lanes] for v in stats]
            w = [wit_ref[h:h + 1, lanes] for h in range(H_IDX)]
            for j in range(tk // IDX_ROWS):
                r0 = j * IDX_ROWS
                acc = jnp.zeros((IDX_ROWS, LANES), F32)
                for h in range(H_IDX):
                    acc = acc + w[h] * jnp.maximum(park_ref[h, r0:r0 + IDX_ROWS, lanes], 0.0)
                if on_diagonal:
                    causal = (k0 + r0 + piece_iota) <= qpos[:, lanes]
                    sv = jnp.where(causal, acc, -jnp.inf)
                    smin = jnp.minimum(smin, rows8(jnp.where(causal, acc, jnp.inf), jnp.min))
                else:
                    sv = acc
                    smin = jnp.minimum(smin, rows8(acc, jnp.min))
                sc_ref[pl.ds(k0 + r0, IDX_ROWS), lanes] = sv
                smax = jnp.maximum(smax, rows8(sv, jnp.max))
                c_pos = c_pos + rows8(jnp.where(sv > 0.0, 1.0, 0.0), jnp.sum)
                c_nn = c_nn + rows8(jnp.where(sv >= 0.0, 1.0, 0.0), jnp.sum)
            columns.append((smax, smin, c_pos, c_nn))
        return tuple(jnp.concatenate(parts, axis=1) for parts in zip(*columns))

    def idx_pair(kp, stats):
        stats = reduce_block(2 * kp, s_ref, stats, False, ahead=(2 * kp + 1, s2_ref))
        return reduce_block(2 * kp + 1, s2_ref, stats, False, ahead=(2 * kp + 2, s_ref))

    park_logits(0, s_ref)
    zero8 = jnp.zeros((8, tq), F32)
    stats = lax.fori_loop(0, i, idx_pair,
                          (jnp.full((8, tq), -jnp.inf, F32), jnp.full((8, tq), jnp.inf, F32), zero8, zero8))
    stats = reduce_block(nkb - 2, s_ref, stats, True, ahead=(nkb - 1, s2_ref))
    stats = reduce_block(nkb - 1, s2_ref, stats, True, first_col=(tq - tk) // LANES)
    smax = jnp.max(stats[0], axis=0, keepdims=True)
    smin = jnp.min(stats[1], axis=0, keepdims=True)
    c_pos, c_nn = [jnp.sum(v, axis=0, keepdims=True) for v in stats[2:]]

    n_causal = (qpos + 1).astype(F32)
    kf = jnp.minimum(qpos + 1, topk).astype(F32)

    def count_gt(thr):
        def pieces(k0, n_pieces, c):
            for j in range(n_pieces):
                rows = pl.ds(k0 + j * COUNT_ROWS, COUNT_ROWS)
                c = c + fold(jnp.where(sc_ref[rows, :] > thr, 1.0, 0.0))
            return c

        def body(kp, c):
            return pieces(pl.multiple_of(kp * (2 * tk), 2 * tk), 2 * tk // COUNT_ROWS, c)

        part = lax.fori_loop(0, npair - 1, body, jnp.zeros((FOLD_ROWS, tq), F32))
        k0 = pl.multiple_of((nkb - 2) * tk, tk)
        part = pieces(k0, tk // COUNT_ROWS, part)
        half = jnp.zeros((FOLD_ROWS, tk), F32)
        for j in range(tk // COUNT_ROWS):
            sv = sc_ref[pl.ds(k0 + tk + j * COUNT_ROWS, COUNT_ROWS), tq - tk:]
            half = half + jnp.sum(jnp.where(sv > thr[:, tq - tk:], 1.0, 0.0)
                                  .reshape(COUNT_ROWS // FOLD_ROWS, FOLD_ROWS, tk), axis=0)
        part = jnp.concatenate([part[:, :tq - tk], part[:, tq - tk:] + half], axis=1)
        return jnp.sum(part, axis=0, keepdims=True)

    below = smin - (1.0 + jnp.abs(smin))
    zero_tie = jnp.logical_and(c_pos < kf, kf <= c_nn)
    positive = c_pos >= kf
    searching = jnp.where(zero_tie, 0.0, 1.0)

    def bisect_cond(carry):
        return jnp.logical_and(carry[0] < BISECT_MAX, carry[1] > 0)

    def bisect_body(carry):
        it, _, lo, hi, c_lo, c_hi = carry
        for _ in range(BISECT_CHECK):
            mid = 0.5 * (lo + hi)
            c_mid = count_gt(mid)
            ge = c_mid >= kf
            up = jnp.logical_and(ge, searching > 0.0)
            down = jnp.logical_and(jnp.logical_not(ge), searching > 0.0)
            lo, c_lo = jnp.where(up, mid, lo), jnp.where(up, c_mid, c_lo)
            hi, c_hi = jnp.where(down, mid, hi), jnp.where(down, c_mid, c_hi)
        open_queries = (jnp.max((c_lo - kf) * searching) > 0.0).astype(jnp.int32)
        return it + BISECT_CHECK, open_queries, lo, hi, c_lo, c_hi

    lo0 = jnp.where(positive, 0.0, jnp.where(zero_tie, 0.0, below))
    hi0 = jnp.where(positive, smax, 0.0)
    c_lo0 = jnp.where(positive, c_pos, jnp.where(zero_tie, c_nn, n_causal))
    c_hi0 = jnp.where(positive, 0.0, c_pos)
    first_open = (jnp.max((c_lo0 - kf) * searching) > 0.0).astype(jnp.int32)
    _, _, lo, hi, c_lo, c_hi = lax.while_loop(
        bisect_cond, bisect_body, (jnp.int32(0), first_open, lo0, hi0, c_lo0, c_hi0))

    @pl.when(jnp.max(jnp.maximum(c_lo - kf, 1.0 - searching)) > 0.0)
    def _break_ties():
        need = kf - c_hi
        closed = 1.0 - searching
        prefix = jnp.where(lax.broadcasted_iota(jnp.int32, (tk, tk), 0)
                           >= lax.broadcasted_iota(jnp.int32, (tk, tk), 1), 1.0, 0.0).astype(BF16)

        def body(kp, seen):
            blocks = []
            for j in range(2):
                k0 = pl.multiple_of((2 * kp + j) * tk, tk)
                sv = sc_ref[pl.ds(k0, tk), :]
                above_lo = jnp.where(sv > lo, 1.0, jnp.where(sv >= lo, closed, 0.0))
                member = jnp.where(sv > hi, 0.0, above_lo)
                blocks.append((k0, member, jnp.dot(prefix, member.astype(BF16), preferred_element_type=F32)))
            for k0, member, within in blocks:
                rank = within + seen
                sv = sc_ref[pl.ds(k0, tk), :]
                sc_ref[pl.ds(k0, tk), :] = jnp.where(member > 0.0, jnp.where(rank > need, -jnp.inf, jnp.inf), sv)
                seen = rank[tk - 1:tk, :]
            return seen

        lax.fori_loop(0, npair, body, jnp.zeros((1, tq), F32))

    m_ref[...] = jnp.full(m_ref.shape, NEG_BIG, F32)
    acc_ref[...] = jnp.zeros(acc_ref.shape, F32)

    every = slice(0, tq)
    late = slice(tq - tk, tq)

    def selection_bias(kb, cols=every):
        return jnp.where(sc_ref[pl.ds(pl.multiple_of(kb * tk, tk), tk), cols] > lo[:, cols], 0.0, NEG_BIG)

    def score_head(kb, h, bias, park_ref, max_ref, cols=every):
        kh = k_ref[pl.ds(pl.multiple_of(kb * tk, tk), tk), h * HEAD_PAD:(h + 1) * HEAD_PAD]
        s = jnp.dot(kh, qt_ref[h * HEAD_PAD:(h + 1) * HEAD_PAD, cols], preferred_element_type=F32) + bias
        park_ref[h, :, cols] = s
        max_ref[h:h + 1, cols] = jnp.max(s, axis=0, keepdims=True)

    def softmax_head(kb, h, park_ref, max_ref, cols=every):
        m_old = m_ref[h:h + 1, cols]
        m_new = jnp.maximum(m_old, max_ref[h:h + 1, cols])
        alpha = jnp.exp2(m_old - m_new)
        p = jnp.exp2((park_ref[h, :, cols] - m_new).astype(BF16))
        rows = slice(h * V_ROWS, (h + 1) * V_ROWS)
        pv = jnp.dot(vt_ref[rows, pl.ds(pl.multiple_of(kb * tk, tk), tk)], p, preferred_element_type=F32)
        acc_ref[rows, cols] = alpha * acc_ref[rows, cols] + pv
        m_ref[h:h + 1, cols] = m_new

    bias0 = selection_bias(0)
    for h in range(H_A):
        score_head(0, h, bias0, s_ref, smax_ref)

    def att_body(kp, carry):
        odd, nxt = 2 * kp + 1, 2 * kp + 2
        bias = selection_bias(odd)
        for h in range(H_A):
            score_head(odd, h, bias, s2_ref, smax2_ref)
            softmax_head(2 * kp, h, s_ref, smax_ref)
        bias = selection_bias(nxt)
        for h in range(H_A):
            score_head(nxt, h, bias, s_ref, smax_ref)
            softmax_head(odd, h, s2_ref, smax2_ref)
        return carry

    lax.fori_loop(0, npair - 1, att_body, 0)
    bias = selection_bias(nkb - 1, late)
    for h in range(H_A):
        score_head(nkb - 1, h, bias, s2_ref, smax2_ref, late)
        softmax_head(nkb - 2, h, s_ref, smax_ref)
    for h in range(H_A):
        softmax_head(nkb - 1, h, s2_ref, smax2_ref, late)
    out = [acc_ref[h * V_ROWS:h * V_ROWS + D_V_A, :] / acc_ref[h * V_ROWS + D_V_A:h * V_ROWS + D_V_A + 1, :]
           for h in range(H_A)]
    o_ref[...] = jnp.concatenate(out, axis=0).T.astype(BF16)


def _dsa_call(qt, qit, wit, k, ki, vt):
    b, hq, s = qt.shape
    tq = TQ_DSA
    topk = min(TOPK_MAX, s // 4)
    colsT = lambda height: pl.BlockSpec((None, height, tq), lambda i, j: (i, 0, j))
    whole = lambda a: pl.BlockSpec((None,) + a.shape[1:], lambda i, j: (i, 0, 0))
    return pl.pallas_call(
        functools.partial(_dsa_kernel, topk=topk),
        out_shape=jax.ShapeDtypeStruct((b, s, WIDTH_A), BF16),
        grid=(b, s // tq),
        in_specs=[colsT(H_IDX * D_IDX), colsT(H_IDX), colsT(hq), whole(ki), whole(k), whole(vt)],
        out_specs=pl.BlockSpec((None, tq, WIDTH_A), lambda i, j: (i, j, 0)),
        scratch_shapes=[pltpu.VMEM((s, tq), F32),
                        pltpu.VMEM((H_A, tq), F32),
                        pltpu.VMEM((H_A * V_ROWS, tq), F32),
                        pltpu.VMEM((H_A, TK_DSA, tq), F32),
                        pltpu.VMEM((H_A, TK_DSA, tq), F32),
                        pltpu.VMEM((H_A, tq), F32),
                        pltpu.VMEM((H_A, tq), F32),
                        pltpu.VMEM((H_IDX * HEAD_PAD, tq), BF16)],
        compiler_params=_params(2),
        name="dsa",
    )(qit, wit, qt, ki, k, vt)


def _dil_kernel(q_ref, kc_ref, kp_ref, vc_ref, vp_ref, o_ref, lse_ref, s_ref):
    for res in range(q_ref.shape[0]):
        _dil_residue(q_ref.at[res], kc_ref.at[res], kp_ref.at[res], vc_ref.at[res], vp_ref.at[res],
                     o_ref.at[res], lse_ref.at[res], s_ref.at[res])


def _dil_residue(q_ref, kc_ref, kp_ref, vc_ref, vp_ref, o_ref, lse_ref, s_ref):
    tb = kp_ref.shape[0]
    n_sub = q_ref.shape[0] // tb
    first_step = pl.program_id(2) == 0
    k = jnp.concatenate([kp_ref[...], kc_ref[...]], axis=0)
    v = jnp.concatenate([vp_ref[...], vc_ref[...]], axis=0)
    r = lax.broadcasted_iota(jnp.int32, (tb, 2 * tb), 0)
    c = lax.broadcasted_iota(jnp.int32, (tb, 2 * tb), 1)
    behind = (tb + r - c).astype(jnp.uint32)
    bias = jnp.where(behind <= tb, 0.0, -jnp.inf)
    no_prev = jnp.where(first_step, r, tb).astype(jnp.uint32)
    bias0 = jnp.where(behind <= no_prev, 0.0, -jnp.inf)
    lane = lax.broadcasted_iota(jnp.int32, (1, WIDTH_B), 1)
    head_qk = (lane % 128) // (D_HEAD_B // 2)
    head_v = lane // D_HEAD_B
    nt = (((1,), (1,)), ((), ()))
    for j in range(n_sub):
        q = q_ref[j * tb:(j + 1) * tb, :]
        for h in range(H_B):
            qh = jnp.where(head_qk == h, q, jnp.zeros_like(q))
            s_ref[j * H_B + h] = (lax.dot_general(qh, k[j * tb:(j + 2) * tb], nt, preferred_element_type=F32)
                                  + (bias0 if j == 0 else bias))
    for j in range(n_sub):
        vj = v[j * tb:(j + 2) * tb]
        num = jnp.zeros((tb, WIDTH_B), F32)
        den = jnp.zeros((tb, WIDTH_B), F32)
        top = jnp.zeros((tb, WIDTH_B), F32)
        for h in range(H_B):
            s = s_ref[j * H_B + h]
            m = jnp.max(s, axis=1, keepdims=True)
            p = jnp.exp2(s - m).astype(BF16)
            pv = jnp.dot(p, jnp.where(head_v == h, vj, jnp.ones_like(vj)), preferred_element_type=F32)
            mine = head_v == h
            num = jnp.where(mine, pv, num)
            den = jnp.where(mine, pltpu.roll(pv, WIDTH_B // 2, axis=1), den)
            top = jnp.where(mine, m, top)
        o_ref[j * tb:(j + 1) * tb, :] = (num / den).astype(o_ref.dtype)
        lse_ref[j * tb:(j + 1) * tb, :] = top + jnp.log2(den)


def _dil_call(qkv):
    b, dilation, n, _ = qkv.shape
    tb, tq = TB_DIL, min(TQ_DIL, n)
    per = tq // tb
    n_res = max(1, min(dilation, TQ_DIL // tq))
    cur = lambda which: pl.BlockSpec((None, n_res, tq, WIDTH_B), lambda i, r, j: (i, r, j, which))
    prev = lambda which: pl.BlockSpec((None, n_res, tb, WIDTH_B),
                                      lambda i, r, j: (i, r, jnp.maximum(j * per - 1, 0), which))
    out_spec = pl.BlockSpec((None, n_res, tq, WIDTH_B), lambda i, r, j: (i, r, j, 0))
    out = lambda dtype: jax.ShapeDtypeStruct((b, dilation, n, WIDTH_B), dtype)
    return pl.pallas_call(
        _dil_kernel,
        out_shape=(out(BF16), out(F32)),
        grid=(b, dilation // n_res, n // tq),
        in_specs=[cur(0), cur(1), prev(1), cur(2), prev(2)],
        out_specs=(out_spec, out_spec),
        scratch_shapes=[pltpu.VMEM((n_res, per * H_B, tb, 2 * tb), F32)],
        compiler_params=_params(3),
        name="dil",
    )(qkv, qkv, qkv, qkv, qkv)


def _merge_ffn_kernel(x_ref, mod_ref, oa_ref, o0_ref, o1_ref, o2_ref, l0_ref, l1_ref, l2_ref, gates_ref,
                      wua_ref, wub_ref, wo_ref, gpost_ref, gpre2_ref, wg_ref, wu_ref, wd_ref, gpost2_ref,
                      out_ref, acc_ref, *scr):
    tm = x_ref.shape[0]
    tmp = scr[4]

    def token_major(ref, buf):
        dilation = ref.shape[0]
        if dilation == 1:
            return ref[0]
        n_slab = ref.shape[2] // LANES
        for c in range(n_slab):
            if dilation <= 4:
                for r in range(dilation):
                    buf[c, pl.ds(r, tm // dilation, stride=dilation), :] = (
                        ref[r, :, c * LANES:(c + 1) * LANES].astype(F32))
            else:
                quarter = tm // 4
                for r in range(dilation):
                    tmp[c, pl.ds((r % 4) * quarter + r // 4, tm // dilation, stride=dilation // 4), :] = (
                        ref[r, :, c * LANES:(c + 1) * LANES].astype(F32))
                for r4 in range(4):
                    buf[c, pl.ds(r4, quarter, stride=4), :] = tmp[c, r4 * quarter:(r4 + 1) * quarter, :]
        return jnp.concatenate([buf[c] for c in range(n_slab)], axis=1)

    o0, l0 = token_major(o0_ref, None).astype(F32), token_major(l0_ref, None)
    o1, l1 = token_major(o1_ref, scr[0]), token_major(l1_ref, scr[1])
    o2, l2 = token_major(o2_ref, scr[2]), token_major(l2_ref, scr[3])
    m = jnp.maximum(jnp.maximum(l0, l1), l2)
    e0, e1, e2 = jnp.exp2(l0 - m), jnp.exp2(l1 - m), jnp.exp2(l2 - m)
    ob = (e0 * o0 + e1 * o1 + e2 * o2) / (e0 + e1 + e2)
    za = jnp.dot(oa_ref[...], wua_ref[...], preferred_element_type=F32)
    zb = jnp.dot(ob.astype(BF16), wub_ref[...], preferred_element_type=F32)
    d = za.shape[1]
    z = gates_ref[:, 0:d].astype(F32) * za + gates_ref[:, d:2 * d].astype(F32) * zb
    y = jnp.dot(z.astype(BF16), wo_ref[...], preferred_element_type=F32)
    x_mixed = x_ref[...] + mod_ref[5:6, :] * (_rms_rows(y) * gpost_ref[...])
    _ffn_tile(x_mixed, mod_ref, gpre2_ref, wg_ref, wu_ref, wd_ref, gpost2_ref, out_ref, acc_ref, 6)


def _merge_ffn_call(x, mod, oa, dil, gates, w_up_a, w_up_b, w_o, g_post, g_pre2, w_gate, w_up, w_down, g_post2):
    b, s, d = x.shape
    tm = TM_FFN
    rows = lambda width: pl.BlockSpec((None, tm, width), lambda i, j: (i, j, 0))
    const = lambda a: pl.BlockSpec(a.shape, lambda i, j: (0,) * a.ndim, pipeline_mode=pl.Buffered(1))
    (o0, l0), (o1, l1), (o2, l2) = dil
    res = lambda a: pl.BlockSpec((None, a.shape[1], tm // a.shape[1], WIDTH_B), lambda i, j: (i, 0, j, 0))
    consts = [w_up_a, w_up_b, w_o, g_post, g_pre2, w_gate, w_up, w_down, g_post2]
    return pl.pallas_call(
        _merge_ffn_kernel,
        out_shape=jax.ShapeDtypeStruct((b, s, d), F32),
        grid=(b, s // tm),
        in_specs=[rows(d), pl.BlockSpec((None, N_MOD, d), lambda i, j: (i, 0, 0)), rows(WIDTH_A)]
                 + [res(a) for a in (o0, o1, o2, l0, l1, l2)]
                 + [rows(2 * d)] + [const(a) for a in consts],
        out_specs=rows(d),
        scratch_shapes=[pltpu.VMEM((tm, d), F32)] + [pltpu.VMEM((WIDTH_B // LANES, tm, LANES), F32)] * 5,
        compiler_params=_params(2),
        name="merge_ffn",
    )(x, mod, oa, o0, o1, o2, l0, l1, l2, gates, *consts)


def _mixer_weights(w_in, g_cq, g_ckv, w_uq, w_uk, w_uv, w_iq):
    d = w_in.shape[0]
    w_in = w_in.astype(BF16)
    o_cq, o_ckv, o_kr, o_ki, o_wi = 0, Q_LORA, Q_LORA + KV_LORA, Q_LORA + KV_LORA + D_ROPE_A, \
        Q_LORA + KV_LORA + D_ROPE_A + D_IDX
    o_qkv = o_wi + H_IDX
    o_gates = o_qkv + N_QKV_B

    def swap_halves(w):
        half = w.shape[1] // 2
        return jnp.concatenate([w[:, half:], w[:, :half]], axis=1)

    w_kr, w_ki = w_in[:, o_kr:o_kr + D_ROPE_A], w_in[:, o_ki:o_ki + D_IDX]
    qkv = w_in[:, o_qkv:o_gates].reshape(d, 3, N_GROUPS_B, H_B, 2, D_HEAD_B // 2)
    qk_split = qkv[:, 0:2].transpose(0, 1, 2, 4, 3, 5).reshape(d, 2 * N_GROUPS_B * WIDTH_B)
    w_std = jnp.concatenate([w_kr, swap_halves(w_kr), w_ki, swap_halves(w_ki), qk_split], axis=1)
    w_tail = w_in[:, o_qkv + 2 * N_GROUPS_B * WIDTH_B:]
    w_tr = jnp.concatenate([w_in[:, o_cq:o_cq + Q_LORA], w_in[:, o_ckv:o_ckv + KV_LORA],
                            w_in[:, o_wi:o_wi + H_IDX]], axis=1).T
    w_tr = jnp.pad(w_tr, ((0, N_TR - w_tr.shape[0]), (0, 0)))

    return {
        "w_std": w_std.astype(BF16),
        "w_tail": w_tail.astype(BF16),
        "w_tr": w_tr.astype(BF16),
        "g_cq_col": g_cq.reshape(Q_LORA, 1),
        "g_ckv_col": g_ckv.reshape(KV_LORA, 1),
        "w_uq_t": w_uq.transpose(1, 2, 0).reshape(H_A * (D_NOPE + D_ROPE_A), Q_LORA).astype(BF16),
        "w_iq_t": w_iq.transpose(1, 2, 0).reshape(H_IDX * D_IDX, Q_LORA).astype(BF16),
        "w_uk_p": jnp.pad(w_uk, ((0, 0), (0, 0), (0, HEAD_PAD - D_NOPE))).reshape(KV_LORA, H_A * HEAD_PAD).astype(BF16),
        "w_uv_t": w_uv.transpose(1, 2, 0).reshape(WIDTH_A, KV_LORA).astype(BF16),
    }


def kernel(x, c, positions, w_mod, b_mod, g_pre_ffn1, w_gate1, w_up1, w_down1, g_post_ffn1, g_pre_mix, w_in, g_cq, g_ckv, w_uq, w_uk, w_uv, w_iq, w_up_a, w_up_b, w_o, g_post_mix, g_pre_ffn2, w_gate2, w_up2, w_down2, g_post_ffn2):
    b, s, d = x.shape
    assert s % (16 * TB_DIL) == 0 and s % TM_FFN == 0 and b <= 8
    c8 = jnp.pad(c, ((0, 8 - b), (0, 0)))
    half_b, half_a = D_HEAD_B // 2, D_ROPE_A // 2
    inv = jnp.concatenate([ROPE_THETA ** (-jnp.arange(half_b, dtype=F32) / half_b),
                           ROPE_THETA ** (-jnp.arange(half_a, dtype=F32) / half_a)]).reshape(48, 1)
    row = lambda g: g.reshape(1, -1)
    for l in range(w_mod.shape[0]):
        if l == 0:
            mod, *tabs = _mod_rope_call(c8, w_mod[l], row(b_mod[l]), positions.reshape(b, 1, s), inv)
        else:
            mod = _mod_call(c8, w_mod[l], row(b_mod[l]))
        mod = mod[:b].reshape(b, N_MOD, d)
        x = _ffn_call(x, mod, row(g_pre_ffn1[l]), w_gate1[l].astype(BF16), w_up1[l].astype(BF16),
                      w_down1[l].astype(BF16), row(g_post_ffn1[l]), 0)
        w = _mixer_weights(w_in[l], g_cq[l], g_ckv[l], w_uq[l], w_uk[l], w_uv[l], w_iq[l])
        qt, qit, wit, k, ki, vt, dq0, dq1, dq2, gates = _proj_call(x, mod, row(g_pre_mix[l]), w, tabs)
        oa = _dsa_call(qt, qit, wit, k, ki, vt)
        dil = [_dil_call(dq) for dq in (dq0, dq1, dq2)]
        x = _merge_ffn_call(x, mod, oa, dil, gates, w_up_a[l].astype(BF16), w_up_b[l].astype(BF16),
                            w_o[l].astype(BF16), row(g_post_mix[l]), row(g_pre_ffn2[l]), w_gate2[l].astype(BF16),
                            w_up2[l].astype(BF16), w_down2[l].astype(BF16), row(g_post_ffn2[l]))
    return x
```

```python
import functools

import jax
import jax.numpy as jnp
from jax import lax
from jax.experimental import pallas as pl
from jax.experimental.pallas import tpu as pltpu

F32 = jnp.float32
BF16 = jnp.bfloat16

ROPE_THETA = 10000.0
NORM_EPS = 1e-6
H_A = 8
Q_LORA = 384
KV_LORA = 256
D_NOPE = 64
D_ROPE_A = 32
D_V_A = 64
H_IDX = 8
D_IDX = 32
TOPK_MAX = 256
DIL_PAIRS = ((128, 1), (512, 4), (2048, 16))
N_GROUPS_B = 3
H_B = 4
D_HEAD_B = 64
N_MOD = 9
WIDTH_A = H_A * D_V_A
WIDTH_B = H_B * D_HEAD_B
N_QKV_B = 3 * N_GROUPS_B * WIDTH_B

LANES = 128
HEAD_PAD = 128
VMEM_LIMIT = 56 * 1024 * 1024

TM_FFN = 512
CH_FFN = 256
TP_PROJ = 512
TQ_DSA = 512
TK_DSA = 256
V_ROWS = D_V_A + 16
TB_DIL = 128
TQ_DIL = 2048
BISECT_MAX = 64
BISECT_CHECK = 2
FOLD_ROWS = 16
COUNT_ROWS = 64
IDX_ROWS = 64
NEG_BIG = -1e30
LOG2_E = 1.4426950408889634

C_KR = 0
C_QKVB = 128
R_CQ = 0
R_CKV = Q_LORA
R_WI = Q_LORA + KV_LORA
N_TR = 656


def _params(n_axes):
    return pltpu.CompilerParams(dimension_semantics=("arbitrary",) * n_axes,
                                vmem_limit_bytes=VMEM_LIMIT)


def _rms_rows(x):
    return x * lax.rsqrt(jnp.mean(x * x, axis=-1, keepdims=True) + NORM_EPS)


def _mod_kernel(c_ref, w_ref, b_ref, o_ref):
    c = c_ref[...]
    o_ref[...] = jnp.dot(c * jax.nn.sigmoid(c), w_ref[...], preferred_element_type=F32) + b_ref[...]


def _mod_call(c8, w_mod, b_mod):
    d = c8.shape[1]
    return pl.pallas_call(
        _mod_kernel,
        out_shape=jax.ShapeDtypeStruct((8, N_MOD * d), F32),
        grid=(N_MOD,),
        in_specs=[pl.BlockSpec((8, d), lambda j: (0, 0)),
                  pl.BlockSpec((d, d), lambda j: (0, j)),
                  pl.BlockSpec((1, d), lambda j: (0, j))],
        out_specs=pl.BlockSpec((8, d), lambda j: (0, j)),
        compiler_params=_params(1),
        name="mod",
    )(c8, w_mod, b_mod)


def _rope_kernel(pos_ref, inv_ref, tt_ref, ca_ref, sa_ref, cb_ref, sb_ref):
    s = pos_ref.shape[1]
    ang = inv_ref[...] * pos_ref[...].astype(F32)
    cos = jnp.cos(ang)
    sin = jnp.sin(ang)
    cos_b, sin_b = cos[32:48], sin[32:48]
    tt_ref[0] = cos_b
    tt_ref[1] = sin_b
    cos_a = jnp.concatenate([cos[0:32]] * 4, axis=0)
    sin_a = jnp.concatenate([sin[0:32]] * 4, axis=0)
    cos_bt = jnp.concatenate([cos_b] * 8, axis=0)
    sin_bt = jnp.concatenate([-sin_b, sin_b] * 4, axis=0)
    for j in range(s // LANES):
        cols = slice(j * LANES, (j + 1) * LANES)
        ca_ref[cols, :] = cos_a[:, cols].T
        sa_ref[cols, :] = sin_a[:, cols].T
        cb_ref[cols, :] = cos_bt[:, cols].T
        sb_ref[cols, :] = sin_bt[:, cols].T


def _mod_rope_kernel(c_ref, w_ref, b_ref, pos_ref, inv_ref, o_ref, tt_ref, ca_ref, sa_ref, cb_ref, sb_ref, *,
                     n_table_steps):
    _mod_kernel(c_ref, w_ref, b_ref, o_ref)

    @pl.when(pl.program_id(0) < n_table_steps)
    def _tables():
        _rope_kernel(pos_ref, inv_ref, tt_ref, ca_ref, sa_ref, cb_ref, sb_ref)


def _mod_rope_call(c8, w_mod, b_mod, pos3, inv):
    d = c8.shape[1]
    b, _, s = pos3.shape
    assert b <= N_MOD
    parts = max(1, (N_MOD - 1) // b)
    sp = s // parts
    assert sp % LANES == 0
    n_table_steps = b * parts
    step = lambda j: jnp.minimum(j, n_table_steps - 1)
    tab = jax.ShapeDtypeStruct((b, s, LANES), F32)
    tab_spec = pl.BlockSpec((None, sp, LANES), lambda j: (step(j) // parts, step(j) % parts, 0))
    return pl.pallas_call(
        functools.partial(_mod_rope_kernel, n_table_steps=n_table_steps),
        out_shape=(jax.ShapeDtypeStruct((8, N_MOD * d), F32),
                   jax.ShapeDtypeStruct((b, 2, 16, s), F32), tab, tab, tab, tab),
        grid=(N_MOD,),
        in_specs=[pl.BlockSpec((8, d), lambda j: (0, 0)),
                  pl.BlockSpec((d, d), lambda j: (0, j)),
                  pl.BlockSpec((1, d), lambda j: (0, j)),
                  pl.BlockSpec((None, 1, sp), lambda j: (step(j) // parts, 0, step(j) % parts)),
                  pl.BlockSpec((48, 1), lambda j: (0, 0))],
        out_specs=(pl.BlockSpec((8, d), lambda j: (0, j)),
                   pl.BlockSpec((None, 2, 16, sp), lambda j: (step(j) // parts, 0, 0, step(j) % parts)),
                   tab_spec, tab_spec, tab_spec, tab_spec),
        compiler_params=_params(1),
        name="mod_rope",
    )(c8, w_mod, b_mod, pos3, inv)


def _ffn_kernel(x_ref, mod_ref, gpre_ref, wg_ref, wu_ref, wd_ref, gpost_ref, o_ref, acc_ref, *, j0):
    _ffn_tile(x_ref[...], mod_ref, gpre_ref, wg_ref, wu_ref, wd_ref, gpost_ref, o_ref, acc_ref, j0)


def _ffn_tile(x, mod_ref, gpre_ref, wg_ref, wu_ref, wd_ref, gpost_ref, o_ref, acc_ref, j0):
    sh, sc, gt = mod_ref[j0:j0 + 1, :], mod_ref[j0 + 1:j0 + 2, :], mod_ref[j0 + 2:j0 + 3, :]
    h = (_rms_rows(x) * gpre_ref[...] * (1.0 + sc) + sh).astype(BF16)
    n_ch = wg_ref.shape[1] // CH_FFN
    for ch in range(n_ch):
        cols = slice(ch * CH_FFN, (ch + 1) * CH_FFN)
        g = jnp.dot(h, wg_ref[:, cols], preferred_element_type=F32)
        u = jnp.dot(h, wu_ref[:, cols], preferred_element_type=F32)
        a = (g * jax.nn.sigmoid(g) * u).astype(BF16)
        part = jnp.dot(a, wd_ref[cols, :], preferred_element_type=F32)
        if ch == 0:
            acc_ref[...] = part
        else:
            acc_ref[...] += part
    o_ref[...] = x + 0.5 * gt * (_rms_rows(acc_ref[...]) * gpost_ref[...])


def _ffn_call(x, mod, g_pre, w_gate, w_up, w_down, g_post, j0):
    b, s, d = x.shape
    f = w_gate.shape[1]
    const = lambda shape: pl.BlockSpec(shape, lambda i, j: (0,) * len(shape), pipeline_mode=pl.Buffered(1))
    tile = pl.BlockSpec((None, TM_FFN, d), lambda i, j: (i, j, 0))
    return pl.pallas_call(
        functools.partial(_ffn_kernel, j0=j0),
        out_shape=jax.ShapeDtypeStruct((b, s, d), F32),
        grid=(b, s // TM_FFN),
        in_specs=[tile,
                  pl.BlockSpec((None, N_MOD, d), lambda i, j: (i, 0, 0)),
                  const((1, d)), const((d, f)), const((d, f)), const((f, d)), const((1, d))],
        out_specs=tile,
        scratch_shapes=[pltpu.VMEM((TM_FFN, d), F32)],
        compiler_params=_params(2),
        name="ffn",
    )(x, mod, g_pre, w_gate, w_up, w_down, g_post)


def _proj_kernel(x_ref, mod_ref, gpre_ref, wstd_ref, wtail_ref, wtr_ref, gcq_ref, gckv_col_ref,
                 wuq_ref, wiq_ref, wuk_ref, wuv_ref,
                 tt_ref, ca_ref, sa_ref, cb_ref, sb_ref,
                 qt_ref, qit_ref, wit_ref, k_ref, ki_ref, vt_ref, dq0_ref, dq1_ref, dq2_ref, gates_ref,
                 dil_scr, dil_tmp):
    x = x_ref[...]
    sh, sc = mod_ref[3:4, :], mod_ref[4:5, :]
    u = (_rms_rows(x) * gpre_ref[...] * (1.0 + sc) + sh).astype(BF16)
    std = jnp.dot(u, wstd_ref[...], preferred_element_type=F32)
    tail = jnp.dot(u, wtail_ref[...], preferred_element_type=F32)
    tr = lax.dot_general(wtr_ref[...], u, (((1,), (1,)), ((), ())),
                         preferred_element_type=F32)

    cq = tr[R_CQ:R_CQ + Q_LORA]
    cq = cq * lax.rsqrt(jnp.mean(cq * cq, axis=0, keepdims=True) + NORM_EPS) * gcq_ref[...]
    cq = cq.astype(BF16)
    ckv_t = tr[R_CKV:R_CKV + KV_LORA]
    ckv_t = ckv_t * lax.rsqrt(jnp.mean(ckv_t * ckv_t, axis=0, keepdims=True) + NORM_EPS) * gckv_col_ref[...]
    cos_t, sin_t = tt_ref[0], tt_ref[1]

    attn_scale = (D_NOPE + D_ROPE_A) ** -0.5 * LOG2_E
    q_t = jnp.dot(wuq_ref[...], cq, preferred_element_type=F32) * attn_scale
    idx_scale = D_IDX ** -0.5
    qi_t = jnp.dot(wiq_ref[...], cq, preferred_element_type=F32) * idx_scale
    d_qk = D_NOPE + D_ROPE_A
    for h in range(H_A):
        r0, s0 = h * HEAD_PAD, h * d_qk
        x1, x2 = q_t[s0 + 64:s0 + 80], q_t[s0 + 80:s0 + 96]
        qt_ref[r0:r0 + 64, :] = q_t[s0:s0 + 64].astype(BF16)
        qt_ref[r0 + 64:r0 + 80, :] = (x1 * cos_t - x2 * sin_t).astype(BF16)
        qt_ref[r0 + 80:r0 + 96, :] = (x2 * cos_t + x1 * sin_t).astype(BF16)
        qt_ref[r0 + 96:r0 + 128, :] = jnp.zeros((HEAD_PAD - d_qk, q_t.shape[1]), BF16)
        i0 = h * D_IDX
        y1, y2 = qi_t[i0:i0 + 16], qi_t[i0 + 16:i0 + 32]
        qit_ref[i0:i0 + 16, :] = (y1 * cos_t - y2 * sin_t).astype(BF16)
        qit_ref[i0 + 16:i0 + 32, :] = (y2 * cos_t + y1 * sin_t).astype(BF16)
    wit_ref[...] = tr[R_WI:R_WI + H_IDX] * (H_IDX ** -0.5)
    v_t = jnp.dot(wuv_ref[...], ckv_t.astype(BF16), preferred_element_type=F32).astype(BF16)
    for h in range(H_A):
        vt_ref[h * V_ROWS:h * V_ROWS + D_V_A, :] = v_t[h * D_V_A:(h + 1) * D_V_A]
        vt_ref[h * V_ROWS + D_V_A:(h + 1) * V_ROWS, :] = jnp.ones((V_ROWS - D_V_A, v_t.shape[1]), BF16)

    ckv = ckv_t.T.astype(BF16)
    k_nope = jnp.dot(ckv, wuk_ref[...], preferred_element_type=F32)
    t = std[:, C_KR:C_KR + LANES]
    lane = lax.broadcasted_iota(jnp.int32, (1, LANES), 1)
    unswapped = jnp.where((lane // 32) % 2 == 0, 1.0, 0.0)
    roped = (t * cb_ref[...] + pltpu.roll(t, LANES - 32, axis=1) * sb_ref[...]) * unswapped
    moved = pltpu.roll(roped, 64, axis=1)
    k_rope_tile = jnp.where(lane >= 64, moved, 0.0)
    for h in range(H_A):
        k_ref[:, h * HEAD_PAD:(h + 1) * HEAD_PAD] = (k_nope[:, h * HEAD_PAD:(h + 1) * HEAD_PAD]
                                                      + k_rope_tile).astype(BF16)
    ki_ref[...] = jnp.where(lane < 32, moved, 0.0).astype(BF16)

    cos_a, sin_a = ca_ref[...], sa_ref[...]
    tp = x.shape[0]
    for g, (out_ref, (_, dilation)) in enumerate(zip((dq0_ref, dq1_ref, dq2_ref), DIL_PAIRS)):
        for which in range(2):
            scale = (D_HEAD_B ** -0.5 * LOG2_E) if which == 0 else 1.0
            c0 = C_QKVB + (which * N_GROUPS_B + g) * WIDTH_B
            x1, x2 = std[:, c0:c0 + 128], std[:, c0 + 128:c0 + 256]
            dil_scr[2 * which] = (x1 * cos_a - x2 * sin_a) * scale
            dil_scr[2 * which + 1] = (x2 * cos_a + x1 * sin_a) * scale
        dil_scr[4] = tail[:, g * WIDTH_B:g * WIDTH_B + 128]
        dil_scr[5] = tail[:, g * WIDTH_B + 128:(g + 1) * WIDTH_B]
        for c in range(3 * WIDTH_B // LANES):
            if dilation <= 4:
                for r in range(dilation):
                    out_ref[r, :, c * LANES:(c + 1) * LANES] = (
                        dil_scr[c, pl.ds(r, tp // dilation, stride=dilation), :].astype(BF16))
            else:
                quarter = tp // 4
                for r4 in range(4):
                    dil_tmp[c, r4 * quarter:(r4 + 1) * quarter, :] = dil_scr[c, pl.ds(r4, quarter, stride=4), :]
                for r in range(dilation):
                    out_ref[r, :, c * LANES:(c + 1) * LANES] = dil_tmp[
                        c, pl.ds((r % 4) * quarter + r // 4, tp // dilation, stride=dilation // 4), :].astype(BF16)
    gates_ref[...] = jax.nn.sigmoid(tail[:, N_GROUPS_B * WIDTH_B:]).astype(BF16)


def _proj_call(x, mod, g_pre, w, tabs):
    b, s, d = x.shape
    tt, ca, sa, cb, sb = tabs
    tp = TP_PROJ
    const = lambda a: pl.BlockSpec(a.shape, lambda i, j: (0,) * a.ndim)
    rows = lambda width: pl.BlockSpec((None, tp, width), lambda i, j: (i, j, 0))
    colsT = lambda height: pl.BlockSpec((None, height, tp), lambda i, j: (i, 0, j))
    consts = [g_pre, w["w_std"], w["w_tail"], w["w_tr"], w["g_cq_col"], w["g_ckv_col"],
              w["w_uq_t"], w["w_iq_t"], w["w_uk_p"], w["w_uv_t"]]
    hq = H_A * HEAD_PAD
    dil_shapes = [jax.ShapeDtypeStruct((b, dil, s // dil, 3 * WIDTH_B), BF16) for _, dil in DIL_PAIRS]
    dil_specs = [pl.BlockSpec((None, dil, tp // dil, 3 * WIDTH_B), lambda i, j: (i, 0, j, 0))
                 for _, dil in DIL_PAIRS]
    return pl.pallas_call(
        _proj_kernel,
        out_shape=(jax.ShapeDtypeStruct((b, hq, s), BF16),
                   jax.ShapeDtypeStruct((b, H_IDX * D_IDX, s), BF16),
                   jax.ShapeDtypeStruct((b, H_IDX, s), F32),
                   jax.ShapeDtypeStruct((b, s, hq), BF16),
                   jax.ShapeDtypeStruct((b, s, LANES), BF16),
                   jax.ShapeDtypeStruct((b, H_A * V_ROWS, s), BF16),
                   *dil_shapes,
                   jax.ShapeDtypeStruct((b, s, 2048), BF16)),
        grid=(b, s // tp),
        in_specs=[rows(d), pl.BlockSpec((None, N_MOD, d), lambda i, j: (i, 0, 0))]
                 + [const(a) for a in consts]
                 + [pl.BlockSpec((None, 2, 16, tp), lambda i, j: (i, 0, 0, j)),
                    rows(LANES), rows(LANES), rows(LANES), rows(LANES)],
        out_specs=(colsT(hq), colsT(H_IDX * D_IDX), colsT(H_IDX), rows(hq), rows(LANES), colsT(H_A * V_ROWS),
                   *dil_specs, rows(2048)),
        scratch_shapes=[pltpu.VMEM((3 * WIDTH_B // LANES, tp, LANES), F32)] * 2,
        compiler_params=_params(2),
        name="proj",
    )(x, mod, *consts, tt, ca, sa, cb, sb)


def _dsa_kernel(qit_ref, wit_ref, qt_ref, ki_ref, k_ref, vt_ref, o_ref,
                sc_ref, m_ref, acc_ref, s_ref, s2_ref, smax_ref, smax2_ref, qpad_ref, *, topk):
    tq = qt_ref.shape[1]
    tk = s_ref.shape[1]
    i = pl.program_id(1)
    assert tq == 2 * tk
    nkb = 2 * (i + 1)
    npair = i + 1
    qpos = i * tq + lax.broadcasted_iota(jnp.int32, (1, tq), 1)

    def fold(v):
        return jnp.sum(v.reshape(v.shape[0] // FOLD_ROWS, FOLD_ROWS, tq), axis=0)

    qpad_ref[...] = jnp.zeros(qpad_ref.shape, BF16)
    for h in range(H_IDX):
        qpad_ref[h * HEAD_PAD:h * HEAD_PAD + D_IDX, :] = qit_ref[h * D_IDX:(h + 1) * D_IDX, :]

    def park_logits(kb, park_ref, heads=range(H_IDX)):
        ki = ki_ref[pl.ds(pl.multiple_of(kb * tk, tk), tk), :]
        for h in heads:
            park_ref[h] = jnp.dot(ki, qpad_ref[h * HEAD_PAD:(h + 1) * HEAD_PAD, :], preferred_element_type=F32)

    def rows8(v, op):
        return op(v.reshape(IDX_ROWS // 8, 8, LANES), axis=0)

    piece_iota = lax.broadcasted_iota(jnp.int32, (IDX_ROWS, LANES), 0)
    n_col = tq // LANES
    heads_per_col = H_IDX // n_col

    def reduce_block(kb, park_ref, stats, on_diagonal, ahead=None, first_col=0):
        k0 = pl.multiple_of(kb * tk, tk)
        columns = []
        for c in range(n_col):
            if ahead is not None:
                park_logits(ahead[0], ahead[1], range(c * heads_per_col, (c + 1) * heads_per_col))
            lanes = slice(c * LANES, (c + 1) * LANES)
            if c < first_col:
                sc_ref[pl.ds(k0, tk), lanes] = jnp.full((tk, LANES), -jnp.inf, F32)
                columns.append(tuple(v[:, lanes] for v in stats))
                continue
            smax, smin, c_pos, c_nn = [v[:, lanes] for v in stats]
            w = [wit_ref[h:h + 1, lanes] for h in range(H_IDX)]
            for j in range(tk // IDX_ROWS):
                r0 = j * IDX_ROWS
                acc = jnp.zeros((IDX_ROWS, LANES), F32)
                for h in range(H_IDX):
                    acc = acc + w[h] * jnp.maximum(park_ref[h, r0:r0 + IDX_ROWS, lanes], 0.0)
                if on_diagonal:
                    causal = (k0 + r0 + piece_iota) <= qpos[:, lanes]
                    sv = jnp.where(causal, acc, -jnp.inf)
                    smin = jnp.minimum(smin, rows8(jnp.where(causal, acc, jnp.inf), jnp.min))
                else:
                    sv = acc
                    smin = jnp.minimum(smin, rows8(acc, jnp.min))
                sc_ref[pl.ds(k0 + r0, IDX_ROWS), lanes] = sv
                smax = jnp.maximum(smax, rows8(sv, jnp.max))
                c_pos = c_pos + rows8(jnp.where(sv > 0.0, 1.0, 0.0), jnp.sum)
                c_nn = c_nn + rows8(jnp.where(sv >= 0.0, 1.0, 0.0), jnp.sum)
            columns.append((smax, smin, c_pos, c_nn))
        return tuple(jnp.concatenate(parts, axis=1) for parts in zip(*columns))

    def idx_pair(kp, stats):
        stats = reduce_block(2 * kp, s_ref, stats, False, ahead=(2 * kp + 1, s2_ref))
        return reduce_block(2 * kp + 1, s2_ref, stats, False, ahead=(2 * kp + 2, s_ref))

    park_logits(0, s_ref)
    zero8 = jnp.zeros((8, tq), F32)
    stats = lax.fori_loop(0, i, idx_pair,
                          (jnp.full((8, tq), -jnp.inf, F32), jnp.full((8, tq), jnp.inf, F32), zero8, zero8))
    stats = reduce_block(nkb - 2, s_ref, stats, True, ahead=(nkb - 1, s2_ref))
    stats = reduce_block(nkb - 1, s2_ref, stats, True, first_col=(tq - tk) // LANES)
    smax = jnp.max(stats[0], axis=0, keepdims=True)
    smin = jnp.min(stats[1], axis=0, keepdims=True)
    c_pos, c_nn = [jnp.sum(v, axis=0, keepdims=True) for v in stats[2:]]

    n_causal = (qpos + 1).astype(F32)
    kf = jnp.minimum(qpos + 1, topk).astype(F32)

    def count_gt(thr):
        def pieces(k0, n_pieces, c):
            for j in range(n_pieces):
                rows = pl.ds(k0 + j * COUNT_ROWS, COUNT_ROWS)
                c = c + fold(jnp.where(sc_ref[rows, :] > thr, 1.0, 0.0))
            return c

        def body(kp, c):
            return pieces(pl.multiple_of(kp * (2 * tk), 2 * tk), 2 * tk // COUNT_ROWS, c)

        part = lax.fori_loop(0, npair - 1, body, jnp.zeros((FOLD_ROWS, tq), F32))
        k0 = pl.multiple_of((nkb - 2) * tk, tk)
        part = pieces(k0, tk // COUNT_ROWS, part)
        half = jnp.zeros((FOLD_ROWS, tk), F32)
        for j in range(tk // COUNT_ROWS):
            sv = sc_ref[pl.ds(k0 + tk + j * COUNT_ROWS, COUNT_ROWS), tq - tk:]
            half = half + jnp.sum(jnp.where(sv > thr[:, tq - tk:], 1.0, 0.0)
                                  .reshape(COUNT_ROWS // FOLD_ROWS, FOLD_ROWS, tk), axis=0)
        part = jnp.concatenate([part[:, :tq - tk], part[:, tq - tk:] + half], axis=1)
        return jnp.sum(part, axis=0, keepdims=True)

    below = smin - (1.0 + jnp.abs(smin))
    zero_tie = jnp.logical_and(c_pos < kf, kf <= c_nn)
    positive = c_pos >= kf
    searching = jnp.where(zero_tie, 0.0, 1.0)

    def bisect_cond(carry):
        return jnp.logical_and(carry[0] < BISECT_MAX, carry[1] > 0)

    def bisect_body(carry):
        it, _, lo, hi, c_lo, c_hi = carry
        for _ in range(BISECT_CHECK):
            mid = 0.5 * (lo + hi)
            c_mid = count_gt(mid)
            ge = c_mid >= kf
            up = jnp.logical_and(ge, searching > 0.0)
            down = jnp.logical_and(jnp.logical_not(ge), searching > 0.0)
            lo, c_lo = jnp.where(up, mid, lo), jnp.where(up, c_mid, c_lo)
            hi, c_hi = jnp.where(down, mid, hi), jnp.where(down, c_mid, c_hi)
        open_queries = (jnp.max((c_lo - kf) * searching) > 0.0).astype(jnp.int32)
        return it + BISECT_CHECK, open_queries, lo, hi, c_lo, c_hi

    lo0 = jnp.where(positive, 0.0, jnp.where(zero_tie, 0.0, below))
    hi0 = jnp.where(positive, smax, 0.0)
    c_lo0 = jnp.where(positive, c_pos, jnp.where(zero_tie, c_nn, n_causal))
    c_hi0 = jnp.where(positive, 0.0, c_pos)
    first_open = (jnp.max((c_lo0 - kf) * searching) > 0.0).astype(jnp.int32)
    _, _, lo, hi, c_lo, c_hi = lax.while_loop(
        bisect_cond, bisect_body, (jnp.int32(0), first_open, lo0, hi0, c_lo0, c_hi0))

    @pl.when(jnp.max(jnp.maximum(c_lo - kf, 1.0 - searching)) > 0.0)
    def _break_ties():
        need = kf - c_hi
        closed = 1.0 - searching
        prefix = jnp.where(lax.broadcasted_iota(jnp.int32, (tk, tk), 0)
                           >= lax.broadcasted_iota(jnp.int32, (tk, tk), 1), 1.0, 0.0).astype(BF16)

        def body(kp, seen):
            blocks = []
            for j in range(2):
                k0 = pl.multiple_of((2 * kp + j) * tk, tk)
                sv = sc_ref[pl.ds(k0, tk), :]
                above_lo = jnp.where(sv > lo, 1.0, jnp.where(sv >= lo, closed, 0.0))
                member = jnp.where(sv > hi, 0.0, above_lo)
                blocks.append((k0, member, jnp.dot(prefix, member.astype(BF16), preferred_element_type=F32)))
            for k0, member, within in blocks:
                rank = within + seen
                sv = sc_ref[pl.ds(k0, tk), :]
                sc_ref[pl.ds(k0, tk), :] = jnp.where(member > 0.0, jnp.where(rank > need, -jnp.inf, jnp.inf), sv)
                seen = rank[tk - 1:tk, :]
            return seen

        lax.fori_loop(0, npair, body, jnp.zeros((1, tq), F32))

    m_ref[...] = jnp.full(m_ref.shape, NEG_BIG, F32)
    acc_ref[...] = jnp.zeros(acc_ref.shape, F32)

    every = slice(0, tq)
    late = slice(tq - tk, tq)

    def selection_bias(kb, cols=every):
        return jnp.where(sc_ref[pl.ds(pl.multiple_of(kb * tk, tk), tk), cols] > lo[:, cols], 0.0, NEG_BIG)

    def score_head(kb, h, bias, park_ref, max_ref, cols=every):
        kh = k_ref[pl.ds(pl.multiple_of(kb * tk, tk), tk), h * HEAD_PAD:(h + 1) * HEAD_PAD]
        s = jnp.dot(kh, qt_ref[h * HEAD_PAD:(h + 1) * HEAD_PAD, cols], preferred_element_type=F32) + bias
        park_ref[h, :, cols] = s
        max_ref[h:h + 1, cols] = jnp.max(s, axis=0, keepdims=True)

    def softmax_head(kb, h, park_ref, max_ref, cols=every):
        m_old = m_ref[h:h + 1, cols]
        m_new = jnp.maximum(m_old, max_ref[h:h + 1, cols])
        alpha = jnp.exp2(m_old - m_new)
        p = jnp.exp2((park_ref[h, :, cols] - m_new).astype(BF16))
        rows = slice(h * V_ROWS, (h + 1) * V_ROWS)
        pv = jnp.dot(vt_ref[rows, pl.ds(pl.multiple_of(kb * tk, tk), tk)], p, preferred_element_type=F32)
        acc_ref[rows, cols] = alpha * acc_ref[rows, cols] + pv
        m_ref[h:h + 1, cols] = m_new

    bias0 = selection_bias(0)
    for h in range(H_A):
        score_head(0, h, bias0, s_ref, smax_ref)

    def att_body(kp, carry):
        odd, nxt = 2 * kp + 1, 2 * kp + 2
        bias = selection_bias(odd)
        for h in range(H_A):
            score_head(odd, h, bias, s2_ref, smax2_ref)
            softmax_head(2 * kp, h, s_ref, smax_ref)
        bias = selection_bias(nxt)
        for h in range(H_A):
            score_head(nxt, h, bias, s_ref, smax_ref)
            softmax_head(odd, h, s2_ref, smax2_ref)
        return carry

    lax.fori_loop(0, npair - 1, att_body, 0)
    bias = selection_bias(nkb - 1, late)
    for h in range(H_A):
        score_head(nkb - 1, h, bias, s2_ref, smax2_ref, late)
        softmax_head(nkb - 2, h, s_ref, smax_ref)
    for h in range(H_A):
        softmax_head(nkb - 1, h, s2_ref, smax2_ref, late)
    out = [acc_ref[h * V_ROWS:h * V_ROWS + D_V_A, :] / acc_ref[h * V_ROWS + D_V_A:h * V_ROWS + D_V_A + 1, :]
           for h in range(H_A)]
    o_ref[...] = jnp.concatenate(out, axis=0).T.astype(BF16)


def _dsa_call(qt, qit, wit, k, ki, vt):
    b, hq, s = qt.shape
    tq = TQ_DSA
    topk = min(TOPK_MAX, s // 4)
    colsT = lambda height: pl.BlockSpec((None, height, tq), lambda i, j: (i, 0, j))
    whole = lambda a: pl.BlockSpec((None,) + a.shape[1:], lambda i, j: (i, 0, 0))
    return pl.pallas_call(
        functools.partial(_dsa_kernel, topk=topk),
        out_shape=jax.ShapeDtypeStruct((b, s, WIDTH_A), BF16),
        grid=(b, s // tq),
        in_specs=[colsT(H_IDX * D_IDX), colsT(H_IDX), colsT(hq), whole(ki), whole(k), whole(vt)],
        out_specs=pl.BlockSpec((None, tq, WIDTH_A), lambda i, j: (i, j, 0)),
        scratch_shapes=[pltpu.VMEM((s, tq), F32),
                        pltpu.VMEM((H_A, tq), F32),
                        pltpu.VMEM((H_A * V_ROWS, tq), F32),
                        pltpu.VMEM((H_A, TK_DSA, tq), F32),
                        pltpu.VMEM((H_A, TK_DSA, tq), F32),
                        pltpu.VMEM((H_A, tq), F32),
                        pltpu.VMEM((H_A, tq), F32),
                        pltpu.VMEM((H_IDX * HEAD_PAD, tq), BF16)],
        compiler_params=_params(2),
        name="dsa",
    )(qit, wit, qt, ki, k, vt)


def _dil_kernel(q_ref, kc_ref, kp_ref, vc_ref, vp_ref, o_ref, lse_ref, s_ref):
    for res in range(q_ref.shape[0]):
        _dil_residue(q_ref.at[res], kc_ref.at[res], kp_ref.at[res], vc_ref.at[res], vp_ref.at[res],
                     o_ref.at[res], lse_ref.at[res], s_ref.at[res])


def _dil_residue(q_ref, kc_ref, kp_ref, vc_ref, vp_ref, o_ref, lse_ref, s_ref):
    tb = kp_ref.shape[0]
    n_sub = q_ref.shape[0] // tb
    first_step = pl.program_id(2) == 0
    k = jnp.concatenate([kp_ref[...], kc_ref[...]], axis=0)
    v = jnp.concatenate([vp_ref[...], vc_ref[...]], axis=0)
    r = lax.broadcasted_iota(jnp.int32, (tb, 2 * tb), 0)
    c = lax.broadcasted_iota(jnp.int32, (tb, 2 * tb), 1)
    behind = (tb + r - c).astype(jnp.uint32)
    bias = jnp.where(behind <= tb, 0.0, -jnp.inf)
    no_prev = jnp.where(first_step, r, tb).astype(jnp.uint32)
    bias0 = jnp.where(behind <= no_prev, 0.0, -jnp.inf)
    lane = lax.broadcasted_iota(jnp.int32, (1, WIDTH_B), 1)
    head_qk = (lane % 128) // (D_HEAD_B // 2)
    head_v = lane // D_HEAD_B
    nt = (((1,), (1,)), ((), ()))
    for j in range(n_sub):
        q = q_ref[j * tb:(j + 1) * tb, :]
        for h in range(H_B):
            qh = jnp.where(head_qk == h, q, jnp.zeros_like(q))
            s_ref[j * H_B + h] = (lax.dot_general(qh, k[j * tb:(j + 2) * tb], nt, preferred_element_type=F32)
                                  + (bias0 if j == 0 else bias))
    for j in range(n_sub):
        vj = v[j * tb:(j + 2) * tb]
        num = jnp.zeros((tb, WIDTH_B), F32)
        den = jnp.zeros((tb, WIDTH_B), F32)
        top = jnp.zeros((tb, WIDTH_B), F32)
        for h in range(H_B):
            s = s_ref[j * H_B + h]
            m = jnp.max(s, axis=1, keepdims=True)
            p = jnp.exp2(s - m).astype(BF16)
            pv = jnp.dot(p, jnp.where(head_v == h, vj, jnp.ones_like(vj)), preferred_element_type=F32)
            mine = head_v == h
            num = jnp.where(mine, pv, num)
            den = jnp.where(mine, pltpu.roll(pv, WIDTH_B // 2, axis=1), den)
            top = jnp.where(mine, m, top)
        o_ref[j * tb:(j + 1) * tb, :] = (num / den).astype(o_ref.dtype)
        lse_ref[j * tb:(j + 1) * tb, :] = top + jnp.log2(den)


def _dil_call(qkv):
    b, dilation, n, _ = qkv.shape
    tb, tq = TB_DIL, min(TQ_DIL, n)
    per = tq // tb
    n_res = max(1, min(dilation, TQ_DIL // tq))
    cur = lambda which: pl.BlockSpec((None, n_res, tq, WIDTH_B), lambda i, r, j: (i, r, j, which))
    prev = lambda which: pl.BlockSpec((None, n_res, tb, WIDTH_B),
                                      lambda i, r, j: (i, r, jnp.maximum(j * per - 1, 0), which))
    out_spec = pl.BlockSpec((None, n_res, tq, WIDTH_B), lambda i, r, j: (i, r, j, 0))
    out = lambda dtype: jax.ShapeDtypeStruct((b, dilation, n, WIDTH_B), dtype)
    return pl.pallas_call(
        _dil_kernel,
        out_shape=(out(BF16), out(F32)),
        grid=(b, dilation // n_res, n // tq),
        in_specs=[cur(0), cur(1), prev(1), cur(2), prev(2)],
        out_specs=(out_spec, out_spec),
        scratch_shapes=[pltpu.VMEM((n_res, per * H_B, tb, 2 * tb), F32)],
        compiler_params=_params(3),
        name="dil",
    )(qkv, qkv, qkv, qkv, qkv)


def _merge_ffn_kernel(x_ref, mod_ref, oa_ref, o0_ref, o1_ref, o2_ref, l0_ref, l1_ref, l2_ref, gates_ref,
                      wua_ref, wub_ref, wo_ref, gpost_ref, gpre2_ref, wg_ref, wu_ref, wd_ref, gpost2_ref,
                      out_ref, acc_ref, *scr):
    tm = x_ref.shape[0]
    tmp = scr[4]

    def token_major(ref, buf):
        dilation = ref.shape[0]
        if dilation == 1:
            return ref[0]
        n_slab = ref.shape[2] // LANES
        for c in range(n_slab):
            if dilation <= 4:
                for r in range(dilation):
                    buf[c, pl.ds(r, tm // dilation, stride=dilation), :] = (
                        ref[r, :, c * LANES:(c + 1) * LANES].astype(F32))
            else:
                quarter = tm // 4
                for r in range(dilation):
                    tmp[c, pl.ds((r % 4) * quarter + r // 4, tm // dilation, stride=dilation // 4), :] = (
                        ref[r, :, c * LANES:(c + 1) * LANES].astype(F32))
                for r4 in range(4):
                    buf[c, pl.ds(r4, quarter, stride=4), :] = tmp[c, r4 * quarter:(r4 + 1) * quarter, :]
        return jnp.concatenate([buf[c] for c in range(n_slab)], axis=1)

    o0, l0 = token_major(o0_ref, None).astype(F32), token_major(l0_ref, None)
    o1, l1 = token_major(o1_ref, scr[0]), token_major(l1_ref, scr[1])
    o2, l2 = token_major(o2_ref, scr[2]), token_major(l2_ref, scr[3])
    m = jnp.maximum(jnp.maximum(l0, l1), l2)
    e0, e1, e2 = jnp.exp2(l0 - m), jnp.exp2(l1 - m), jnp.exp2(l2 - m)
    ob = (e0 * o0 + e1 * o1 + e2 * o2) / (e0 + e1 + e2)
    za = jnp.dot(oa_ref[...], wua_ref[...], preferred_element_type=F32)
    zb = jnp.dot(ob.astype(BF16), wub_ref[...], preferred_element_type=F32)
    d = za.shape[1]
    z = gates_ref[:, 0:d].astype(F32) * za + gates_ref[:, d:2 * d].astype(F32) * zb
    y = jnp.dot(z.astype(BF16), wo_ref[...], preferred_element_type=F32)
    x_mixed = x_ref[...] + mod_ref[5:6, :] * (_rms_rows(y) * gpost_ref[...])
    _ffn_tile(x_mixed, mod_ref, gpre2_ref, wg_ref, wu_ref, wd_ref, gpost2_ref, out_ref, acc_ref, 6)


def _merge_ffn_call(x, mod, oa, dil, gates, w_up_a, w_up_b, w_o, g_post, g_pre2, w_gate, w_up, w_down, g_post2):
    b, s, d = x.shape
    tm = TM_FFN
    rows = lambda width: pl.BlockSpec((None, tm, width), lambda i, j: (i, j, 0))
    const = lambda a: pl.BlockSpec(a.shape, lambda i, j: (0,) * a.ndim, pipeline_mode=pl.Buffered(1))
    (o0, l0), (o1, l1), (o2, l2) = dil
    res = lambda a: pl.BlockSpec((None, a.shape[1], tm // a.shape[1], WIDTH_B), lambda i, j: (i, 0, j, 0))
    consts = [w_up_a, w_up_b, w_o, g_post, g_pre2, w_gate, w_up, w_down, g_post2]
    return pl.pallas_call(
        _merge_ffn_kernel,
        out_shape=jax.ShapeDtypeStruct((b, s, d), F32),
        grid=(b, s // tm),
        in_specs=[rows(d), pl.BlockSpec((None, N_MOD, d), lambda i, j: (i, 0, 0)), rows(WIDTH_A)]
                 + [res(a) for a in (o0, o1, o2, l0, l1, l2)]
                 + [rows(2 * d)] + [const(a) for a in consts],
        out_specs=rows(d),
        scratch_shapes=[pltpu.VMEM((tm, d), F32)] + [pltpu.VMEM((WIDTH_B // LANES, tm, LANES), F32)] * 5,
        compiler_params=_params(2),
        name="merge_ffn",
    )(x, mod, oa, o0, o1, o2, l0, l1, l2, gates, *consts)


def _mixer_weights(w_in, g_cq, g_ckv, w_uq, w_uk, w_uv, w_iq):
    d = w_in.shape[0]
    w_in = w_in.astype(BF16)
    o_cq, o_ckv, o_kr, o_ki, o_wi = 0, Q_LORA, Q_LORA + KV_LORA, Q_LORA + KV_LORA + D_ROPE_A, \
        Q_LORA + KV_LORA + D_ROPE_A + D_IDX
    o_qkv = o_wi + H_IDX
    o_gates = o_qkv + N_QKV_B

    def swap_halves(w):
        half = w.shape[1] // 2
        return jnp.concatenate([w[:, half:], w[:, :half]], axis=1)

    w_kr, w_ki = w_in[:, o_kr:o_kr + D_ROPE_A], w_in[:, o_ki:o_ki + D_IDX]
    qkv = w_in[:, o_qkv:o_gates].reshape(d, 3, N_GROUPS_B, H_B, 2, D_HEAD_B // 2)
    qk_split = qkv[:, 0:2].transpose(0, 1, 2, 4, 3, 5).reshape(d, 2 * N_GROUPS_B * WIDTH_B)
    w_std = jnp.concatenate([w_kr, swap_halves(w_kr), w_ki, swap_halves(w_ki), qk_split], axis=1)
    w_tail = w_in[:, o_qkv + 2 * N_GROUPS_B * WIDTH_B:]
    w_tr = jnp.concatenate([w_in[:, o_cq:o_cq + Q_LORA], w_in[:, o_ckv:o_ckv + KV_LORA],
                            w_in[:, o_wi:o_wi + H_IDX]], axis=1).T
    w_tr = jnp.pad(w_tr, ((0, N_TR - w_tr.shape[0]), (0, 0)))

    return {
        "w_std": w_std.astype(BF16),
        "w_tail": w_tail.astype(BF16),
        "w_tr": w_tr.astype(BF16),
        "g_cq_col": g_cq.reshape(Q_LORA, 1),
        "g_ckv_col": g_ckv.reshape(KV_LORA, 1),
        "w_uq_t": w_uq.transpose(1, 2, 0).reshape(H_A * (D_NOPE + D_ROPE_A), Q_LORA).astype(BF16),
        "w_iq_t": w_iq.transpose(1, 2, 0).reshape(H_IDX * D_IDX, Q_LORA).astype(BF16),
        "w_uk_p": jnp.pad(w_uk, ((0, 0), (0, 0), (0, HEAD_PAD - D_NOPE))).reshape(KV_LORA, H_A * HEAD_PAD).astype(BF16),
        "w_uv_t": w_uv.transpose(1, 2, 0).reshape(WIDTH_A, KV_LORA).astype(BF16),
    }


def kernel(x, c, positions, w_mod, b_mod, g_pre_ffn1, w_gate1, w_up1, w_down1, g_post_ffn1, g_pre_mix, w_in, g_cq, g_ckv, w_uq, w_uk, w_uv, w_iq, w_up_a, w_up_b, w_o, g_post_mix, g_pre_ffn2, w_gate2, w_up2, w_down2, g_post_ffn2):
    b, s, d = x.shape
    assert s % (16 * TB_DIL) == 0 and s % TM_FFN == 0 and b <= 8
    c8 = jnp.pad(c, ((0, 8 - b), (0, 0)))
    half_b, half_a = D_HEAD_B // 2, D_ROPE_A // 2
    inv = jnp.concatenate([ROPE_THETA ** (-jnp.arange(half_b, dtype=F32) / half_b),
                           ROPE_THETA ** (-jnp.arange(half_a, dtype=F32) / half_a)]).reshape(48, 1)
    row = lambda g: g.reshape(1, -1)
    for l in range(w_mod.shape[0]):
        if l == 0:
            mod, *tabs = _mod_rope_call(c8, w_mod[l], row(b_mod[l]), positions.reshape(b, 1, s), inv)
        else:
            mod = _mod_call(c8, w_mod[l], row(b_mod[l]))
        mod = mod[:b].reshape(b, N_MOD, d)
        x = _ffn_call(x, mod, row(g_pre_ffn1[l]), w_gate1[l].astype(BF16), w_up1[l].astype(BF16),
                      w_down1[l].astype(BF16), row(g_post_ffn1[l]), 0)
        w = _mixer_weights(w_in[l], g_cq[l], g_ckv[l], w_uq[l], w_uk[l], w_uv[l], w_iq[l])
        qt, qit, wit, k, ki, vt, dq0, dq1, dq2, gates = _proj_call(x, mod, row(g_pre_mix[l]), w, tabs)
        oa = _dsa_call(qt, qit, wit, k, ki, vt)
        dil = [_dil_call(dq) for dq in (dq0, dq1, dq2)]
        x = _merge_ffn_call(x, mod, oa, dil, gates, w_up_a[l].astype(BF16), w_up_b[l].astype(BF16),
                            w_o[l].astype(BF16), row(g_post_mix[l]), row(g_pre_ffn2[l]), w_gate2[l].astype(BF16),
                            w_up2[l].astype(BF16), w_down2[l].astype(BF16), row(g_post_ffn2[l]))
    return x
```

```python
import functools

import jax
import jax.numpy as jnp
from jax import lax
from jax.experimental import pallas as pl
from jax.experimental.pallas import tpu as pltpu

F32 = jnp.float32
BF16 = jnp.bfloat16

ROPE_THETA = 10000.0
NORM_EPS = 1e-6
H_A = 8
Q_LORA = 384
KV_LORA = 256
D_NOPE = 64
D_ROPE_A = 32
D_V_A = 64
H_IDX = 8
D_IDX = 32
TOPK_MAX = 256
DIL_PAIRS = ((128, 1), (512, 4), (2048, 16))
N_GROUPS_B = 3
H_B = 4
D_HEAD_B = 64
N_MOD = 9
WIDTH_A = H_A * D_V_A
WIDTH_B = H_B * D_HEAD_B
N_QKV_B = 3 * N_GROUPS_B * WIDTH_B

LANES = 128
HEAD_PAD = 128
VMEM_LIMIT = 56 * 1024 * 1024

TM_FFN = 512
CH_FFN = 256
TP_PROJ = 512
TQ_DSA = 512
TK_DSA = 256
V_ROWS = D_V_A + 16
TB_DIL = 128
TQ_DIL = 1024
BISECT_MAX = 64
BISECT_CHECK = 2
FOLD_ROWS = 16
COUNT_ROWS = 64
IDX_ROWS = 32
NEG_BIG = -1e30
LOG2_E = 1.4426950408889634

C_KR = 0
C_QKVB = 128
R_CQ = 0
R_CKV = Q_LORA
R_WI = Q_LORA + KV_LORA
N_TR = 656


def _params(n_axes):
    return pltpu.CompilerParams(dimension_semantics=("arbitrary",) * n_axes,
                                vmem_limit_bytes=VMEM_LIMIT)


def _rms_rows(x):
    return x * lax.rsqrt(jnp.mean(x * x, axis=-1, keepdims=True) + NORM_EPS)


def _mod_kernel(c_ref, w_ref, b_ref, o_ref):
    c = c_ref[...]
    o_ref[...] = jnp.dot(c * jax.nn.sigmoid(c), w_ref[...], preferred_element_type=F32) + b_ref[...]


def _mod_call(c8, w_mod, b_mod):
    d = c8.shape[1]
    return pl.pallas_call(
        _mod_kernel,
        out_shape=jax.ShapeDtypeStruct((8, N_MOD * d), F32),
        grid=(N_MOD,),
        in_specs=[pl.BlockSpec((8, d), lambda j: (0, 0)),
                  pl.BlockSpec((d, d), lambda j: (0, j)),
                  pl.BlockSpec((1, d), lambda j: (0, j))],
        out_specs=pl.BlockSpec((8, d), lambda j: (0, j)),
        compiler_params=_params(1),
        name="mod",
    )(c8, w_mod, b_mod)


def _rope_kernel(pos_ref, inv_ref, tt_ref, ca_ref, sa_ref, cb_ref, sb_ref):
    s = pos_ref.shape[1]
    ang = inv_ref[...] * pos_ref[...].astype(F32)
    cos = jnp.cos(ang)
    sin = jnp.sin(ang)
    cos_b, sin_b = cos[32:48], sin[32:48]
    tt_ref[0] = cos_b
    tt_ref[1] = sin_b
    cos_a = jnp.concatenate([cos[0:32]] * 4, axis=0)
    sin_a = jnp.concatenate([sin[0:32]] * 4, axis=0)
    cos_bt = jnp.concatenate([cos_b] * 8, axis=0)
    sin_bt = jnp.concatenate([-sin_b, sin_b] * 4, axis=0)
    for j in range(s // LANES):
        cols = slice(j * LANES, (j + 1) * LANES)
        ca_ref[cols, :] = cos_a[:, cols].T
        sa_ref[cols, :] = sin_a[:, cols].T
        cb_ref[cols, :] = cos_bt[:, cols].T
        sb_ref[cols, :] = sin_bt[:, cols].T


def _mod_rope_kernel(c_ref, w_ref, b_ref, pos_ref, inv_ref, o_ref, tt_ref, ca_ref, sa_ref, cb_ref, sb_ref, *,
                     n_batch):
    _mod_kernel(c_ref, w_ref, b_ref, o_ref)

    @pl.when(pl.program_id(0) < n_batch)
    def _tables():
        _rope_kernel(pos_ref, inv_ref, tt_ref, ca_ref, sa_ref, cb_ref, sb_ref)


def _mod_rope_call(c8, w_mod, b_mod, pos3, inv):
    d = c8.shape[1]
    b, _, s = pos3.shape
    assert b <= N_MOD
    row = lambda j: jnp.minimum(j, b - 1)
    tab = jax.ShapeDtypeStruct((b, s, LANES), F32)
    tab_spec = pl.BlockSpec((None, s, LANES), lambda j: (row(j), 0, 0))
    return pl.pallas_call(
        functools.partial(_mod_rope_kernel, n_batch=b),
        out_shape=(jax.ShapeDtypeStruct((8, N_MOD * d), F32),
                   jax.ShapeDtypeStruct((b, 2, 16, s), F32), tab, tab, tab, tab),
        grid=(N_MOD,),
        in_specs=[pl.BlockSpec((8, d), lambda j: (0, 0)),
                  pl.BlockSpec((d, d), lambda j: (0, j)),
                  pl.BlockSpec((1, d), lambda j: (0, j)),
                  pl.BlockSpec((None, 1, s), lambda j: (row(j), 0, 0)),
                  pl.BlockSpec((48, 1), lambda j: (0, 0))],
        out_specs=(pl.BlockSpec((8, d), lambda j: (0, j)),
                   pl.BlockSpec((None, 2, 16, s), lambda j: (row(j), 0, 0, 0)),
                   tab_spec, tab_spec, tab_spec, tab_spec),
        compiler_params=_params(1),
        name="mod_rope",
    )(c8, w_mod, b_mod, pos3, inv)


def _ffn_kernel(x_ref, mod_ref, gpre_ref, wg_ref, wu_ref, wd_ref, gpost_ref, o_ref, acc_ref, *, j0):
    _ffn_tile(x_ref[...], mod_ref, gpre_ref, wg_ref, wu_ref, wd_ref, gpost_ref, o_ref, acc_ref, j0)


def _ffn_tile(x, mod_ref, gpre_ref, wg_ref, wu_ref, wd_ref, gpost_ref, o_ref, acc_ref, j0):
    sh, sc, gt = mod_ref[j0:j0 + 1, :], mod_ref[j0 + 1:j0 + 2, :], mod_ref[j0 + 2:j0 + 3, :]
    h = (_rms_rows(x) * gpre_ref[...] * (1.0 + sc) + sh).astype(BF16)
    n_ch = wg_ref.shape[1] // CH_FFN
    for ch in range(n_ch):
        cols = slice(ch * CH_FFN, (ch + 1) * CH_FFN)
        g = jnp.dot(h, wg_ref[:, cols], preferred_element_type=F32)
        u = jnp.dot(h, wu_ref[:, cols], preferred_element_type=F32)
        a = (g * jax.nn.sigmoid(g) * u).astype(BF16)
        part = jnp.dot(a, wd_ref[cols, :], preferred_element_type=F32)
        if ch == 0:
            acc_ref[...] = part
        else:
            acc_ref[...] += part
    o_ref[...] = x + 0.5 * gt * (_rms_rows(acc_ref[...]) * gpost_ref[...])


def _ffn_call(x, mod, g_pre, w_gate, w_up, w_down, g_post, j0):
    b, s, d = x.shape
    f = w_gate.shape[1]
    const = lambda shape: pl.BlockSpec(shape, lambda i, j: (0,) * len(shape), pipeline_mode=pl.Buffered(1))
    tile = pl.BlockSpec((None, TM_FFN, d), lambda i, j: (i, j, 0))
    return pl.pallas_call(
        functools.partial(_ffn_kernel, j0=j0),
        out_shape=jax.ShapeDtypeStruct((b, s, d), F32),
        grid=(b, s // TM_FFN),
        in_specs=[tile,
                  pl.BlockSpec((None, N_MOD, d), lambda i, j: (i, 0, 0)),
                  const((1, d)), const((d, f)), const((d, f)), const((f, d)), const((1, d))],
        out_specs=tile,
        scratch_shapes=[pltpu.VMEM((TM_FFN, d), F32)],
        compiler_params=_params(2),
        name="ffn",
    )(x, mod, g_pre, w_gate, w_up, w_down, g_post)


def _proj_kernel(x_ref, mod_ref, gpre_ref, wstd_ref, wtail_ref, wtr_ref, gcq_ref, gckv_col_ref,
                 wuq_ref, wiq_ref, wuk_ref, wuv_ref,
                 tt_ref, ca_ref, sa_ref, cb_ref, sb_ref,
                 qt_ref, qit_ref, wit_ref, k_ref, ki_ref, vt_ref, dq0_ref, dq1_ref, dq2_ref, gates_ref,
                 dil_scr, dil_tmp):
    x = x_ref[...]
    sh, sc = mod_ref[3:4, :], mod_ref[4:5, :]
    u = (_rms_rows(x) * gpre_ref[...] * (1.0 + sc) + sh).astype(BF16)
    std = jnp.dot(u, wstd_ref[...], preferred_element_type=F32)
    tail = jnp.dot(u, wtail_ref[...], preferred_element_type=F32)
    tr = lax.dot_general(wtr_ref[...], u, (((1,), (1,)), ((), ())),
                         preferred_element_type=F32)

    cq = tr[R_CQ:R_CQ + Q_LORA]
    cq = cq * lax.rsqrt(jnp.mean(cq * cq, axis=0, keepdims=True) + NORM_EPS) * gcq_ref[...]
    cq = cq.astype(BF16)
    ckv_t = tr[R_CKV:R_CKV + KV_LORA]
    ckv_t = ckv_t * lax.rsqrt(jnp.mean(ckv_t * ckv_t, axis=0, keepdims=True) + NORM_EPS) * gckv_col_ref[...]
    cos_t, sin_t = tt_ref[0], tt_ref[1]

    attn_scale = (D_NOPE + D_ROPE_A) ** -0.5 * LOG2_E
    q_t = jnp.dot(wuq_ref[...], cq, preferred_element_type=F32) * attn_scale
    idx_scale = D_IDX ** -0.5
    qi_t = jnp.dot(wiq_ref[...], cq, preferred_element_type=F32) * idx_scale
    d_qk = D_NOPE + D_ROPE_A
    for h in range(H_A):
        r0, s0 = h * HEAD_PAD, h * d_qk
        x1, x2 = q_t[s0 + 64:s0 + 80], q_t[s0 + 80:s0 + 96]
        qt_ref[r0:r0 + 64, :] = q_t[s0:s0 + 64].astype(BF16)
        qt_ref[r0 + 64:r0 + 80, :] = (x1 * cos_t - x2 * sin_t).astype(BF16)
        qt_ref[r0 + 80:r0 + 96, :] = (x2 * cos_t + x1 * sin_t).astype(BF16)
        qt_ref[r0 + 96:r0 + 128, :] = jnp.zeros((HEAD_PAD - d_qk, q_t.shape[1]), BF16)
        i0 = h * D_IDX
        y1, y2 = qi_t[i0:i0 + 16], qi_t[i0 + 16:i0 + 32]
        qit_ref[i0:i0 + 16, :] = (y1 * cos_t - y2 * sin_t).astype(BF16)
        qit_ref[i0 + 16:i0 + 32, :] = (y2 * cos_t + y1 * sin_t).astype(BF16)
    wit_ref[...] = tr[R_WI:R_WI + H_IDX] * (H_IDX ** -0.5)
    v_t = jnp.dot(wuv_ref[...], ckv_t.astype(BF16), preferred_element_type=F32).astype(BF16)
    for h in range(H_A):
        vt_ref[h * V_ROWS:h * V_ROWS + D_V_A, :] = v_t[h * D_V_A:(h + 1) * D_V_A]
        vt_ref[h * V_ROWS + D_V_A:(h + 1) * V_ROWS, :] = jnp.ones((V_ROWS - D_V_A, v_t.shape[1]), BF16)

    ckv = ckv_t.T.astype(BF16)
    k_nope = jnp.dot(ckv, wuk_ref[...], preferred_element_type=F32)
    t = std[:, C_KR:C_KR + LANES]
    lane = lax.broadcasted_iota(jnp.int32, (1, LANES), 1)
    unswapped = jnp.where((lane // 32) % 2 == 0, 1.0, 0.0)
    roped = (t * cb_ref[...] + pltpu.roll(t, LANES - 32, axis=1) * sb_ref[...]) * unswapped
    moved = pltpu.roll(roped, 64, axis=1)
    k_rope_tile = jnp.where(lane >= 64, moved, 0.0)
    for h in range(H_A):
        k_ref[:, h * HEAD_PAD:(h + 1) * HEAD_PAD] = (k_nope[:, h * HEAD_PAD:(h + 1) * HEAD_PAD]
                                                      + k_rope_tile).astype(BF16)
    ki_ref[...] = jnp.where(lane < 32, moved, 0.0).astype(BF16)

    cos_a, sin_a = ca_ref[...], sa_ref[...]
    tp = x.shape[0]
    for g, (out_ref, (_, dilation)) in enumerate(zip((dq0_ref, dq1_ref, dq2_ref), DIL_PAIRS)):
        for which in range(2):
            scale = (D_HEAD_B ** -0.5 * LOG2_E) if which == 0 else 1.0
            c0 = C_QKVB + (which * N_GROUPS_B + g) * WIDTH_B
            x1, x2 = std[:, c0:c0 + 128], std[:, c0 + 128:c0 + 256]
            dil_scr[2 * which] = (x1 * cos_a - x2 * sin_a) * scale
            dil_scr[2 * which + 1] = (x2 * cos_a + x1 * sin_a) * scale
        dil_scr[4] = tail[:, g * WIDTH_B:g * WIDTH_B + 128]
        dil_scr[5] = tail[:, g * WIDTH_B + 128:(g + 1) * WIDTH_B]
        for c in range(3 * WIDTH_B // LANES):
            if dilation <= 4:
                for r in range(dilation):
                    out_ref[r, :, c * LANES:(c + 1) * LANES] = (
                        dil_scr[c, pl.ds(r, tp // dilation, stride=dilation), :].astype(BF16))
            else:
                quarter = tp // 4
                for r4 in range(4):
                    dil_tmp[c, r4 * quarter:(r4 + 1) * quarter, :] = dil_scr[c, pl.ds(r4, quarter, stride=4), :]
                for r in range(dilation):
                    out_ref[r, :, c * LANES:(c + 1) * LANES] = dil_tmp[
                        c, pl.ds((r % 4) * quarter + r // 4, tp // dilation, stride=dilation // 4), :].astype(BF16)
    gates_ref[...] = jax.nn.sigmoid(tail[:, N_GROUPS_B * WIDTH_B:]).astype(BF16)


def _proj_call(x, mod, g_pre, w, tabs):
    b, s, d = x.shape
    tt, ca, sa, cb, sb = tabs
    tp = TP_PROJ
    const = lambda a: pl.BlockSpec(a.shape, lambda i, j: (0,) * a.ndim)
    rows = lambda width: pl.BlockSpec((None, tp, width), lambda i, j: (i, j, 0))
    colsT = lambda height: pl.BlockSpec((None, height, tp), lambda i, j: (i, 0, j))
    consts = [g_pre, w["w_std"], w["w_tail"], w["w_tr"], w["g_cq_col"], w["g_ckv_col"],
              w["w_uq_t"], w["w_iq_t"], w["w_uk_p"], w["w_uv_t"]]
    hq = H_A * HEAD_PAD
    dil_shapes = [jax.ShapeDtypeStruct((b, dil, s // dil, 3 * WIDTH_B), BF16) for _, dil in DIL_PAIRS]
    dil_specs = [pl.BlockSpec((None, dil, tp // dil, 3 * WIDTH_B), lambda i, j: (i, 0, j, 0))
                 for _, dil in DIL_PAIRS]
    return pl.pallas_call(
        _proj_kernel,
        out_shape=(jax.ShapeDtypeStruct((b, hq, s), BF16),
                   jax.ShapeDtypeStruct((b, H_IDX * D_IDX, s), BF16),
                   jax.ShapeDtypeStruct((b, H_IDX, s), F32),
                   jax.ShapeDtypeStruct((b, s, hq), BF16),
                   jax.ShapeDtypeStruct((b, s, LANES), BF16),
                   jax.ShapeDtypeStruct((b, H_A * V_ROWS, s), BF16),
                   *dil_shapes,
                   jax.ShapeDtypeStruct((b, s, 2048), BF16)),
        grid=(b, s // tp),
        in_specs=[rows(d), pl.BlockSpec((None, N_MOD, d), lambda i, j: (i, 0, 0))]
                 + [const(a) for a in consts]
                 + [pl.BlockSpec((None, 2, 16, tp), lambda i, j: (i, 0, 0, j)),
                    rows(LANES), rows(LANES), rows(LANES), rows(LANES)],
        out_specs=(colsT(hq), colsT(H_IDX * D_IDX), colsT(H_IDX), rows(hq), rows(LANES), colsT(H_A * V_ROWS),
                   *dil_specs, rows(2048)),
        scratch_shapes=[pltpu.VMEM((3 * WIDTH_B // LANES, tp, LANES), F32)] * 2,
        compiler_params=_params(2),
        name="proj",
    )(x, mod, *consts, tt, ca, sa, cb, sb)


def _dsa_kernel(qit_ref, wit_ref, qt_ref, ki_ref, k_ref, vt_ref, o_ref,
                sc_ref, m_ref, acc_ref, s_ref, s2_ref, smax_ref, smax2_ref, qpad_ref, *, topk):
    tq = qt_ref.shape[1]
    tk = s_ref.shape[1]
    i = pl.program_id(1)
    assert tq == 2 * tk
    nkb = 2 * (i + 1)
    npair = i + 1
    qpos = i * tq + lax.broadcasted_iota(jnp.int32, (1, tq), 1)

    def fold(v):
        return jnp.sum(v.reshape(v.shape[0] // FOLD_ROWS, FOLD_ROWS, tq), axis=0)

    qpad_ref[...] = jnp.zeros(qpad_ref.shape, BF16)
    for h in range(H_IDX):
        qpad_ref[h * HEAD_PAD:h * HEAD_PAD + D_IDX, :] = qit_ref[h * D_IDX:(h + 1) * D_IDX, :]

    def park_logits(kb, park_ref, heads=range(H_IDX)):
        ki = ki_ref[pl.ds(pl.multiple_of(kb * tk, tk), tk), :]
        for h in heads:
            park_ref[h] = jnp.dot(ki, qpad_ref[h * HEAD_PAD:(h + 1) * HEAD_PAD, :], preferred_element_type=F32)

    def rows8(v, op):
        return op(v.reshape(IDX_ROWS // 8, 8, LANES), axis=0)

    piece_iota = lax.broadcasted_iota(jnp.int32, (IDX_ROWS, LANES), 0)
    n_col = tq // LANES
    heads_per_col = H_IDX // n_col

    def reduce_block(kb, park_ref, stats, on_diagonal, ahead=None, first_col=0):
        k0 = pl.multiple_of(kb * tk, tk)
        columns = []
        for c in range(n_col):
            if ahead is not None:
                park_logits(ahead[0], ahead[1], range(c * heads_per_col, (c + 1) * heads_per_col))
            lanes = slice(c * LANES, (c + 1) * LANES)
            if c < first_col:
                sc_ref[pl.ds(k0, tk), lanes] = jnp.full((tk, LANES), -jnp.inf, F32)
                columns.append(tuple(v[:, lanes] for v in stats))
                continue
            smax, smin, c_pos, c_nn = [v[:, lanes] for v in stats]
            w = [wit_ref[h:h + 1, lanes] for h in range(H_IDX)]
            for j in range(tk // IDX_ROWS):
                r0 = j * IDX_ROWS
                acc = jnp.zeros((IDX_ROWS, LANES), F32)
                for h in range(H_IDX):
                    acc = acc + w[h] * jnp.maximum(park_ref[h, r0:r0 + IDX_ROWS, lanes], 0.0)
                if on_diagonal:
                    causal = (k0 + r0 + piece_iota) <= qpos[:, lanes]
                    sv = jnp.where(causal, acc, -jnp.inf)
                    smin = jnp.minimum(smin, rows8(jnp.where(causal, acc, jnp.inf), jnp.min))
                else:
                    sv = acc
                    smin = jnp.minimum(smin, rows8(acc, jnp.min))
                sc_ref[pl.ds(k0 + r0, IDX_ROWS), lanes] = sv
                smax = jnp.maximum(smax, rows8(sv, jnp.max))
                c_pos = c_pos + rows8(jnp.where(sv > 0.0, 1.0, 0.0), jnp.sum)
                c_nn = c_nn + rows8(jnp.where(sv >= 0.0, 1.0, 0.0), jnp.sum)
            columns.append((smax, smin, c_pos, c_nn))
        return tuple(jnp.concatenate(parts, axis=1) for parts in zip(*columns))

    def idx_pair(kp, stats):
        stats = reduce_block(2 * kp, s_ref, stats, False, ahead=(2 * kp + 1, s2_ref))
        return reduce_block(2 * kp + 1, s2_ref, stats, False, ahead=(2 * kp + 2, s_ref))

    park_logits(0, s_ref)
    zero8 = jnp.zeros((8, tq), F32)
    stats = lax.fori_loop(0, i, idx_pair,
                          (jnp.full((8, tq), -jnp.inf, F32), jnp.full((8, tq), jnp.inf, F32), zero8, zero8))
    stats = reduce_block(nkb - 2, s_ref, stats, True, ahead=(nkb - 1, s2_ref))
    stats = reduce_block(nkb - 1, s2_ref, stats, True, first_col=(tq - tk) // LANES)
    smax = jnp.max(stats[0], axis=0, keepdims=True)
    smin = jnp.min(stats[1], axis=0, keepdims=True)
    c_pos, c_nn = [jnp.sum(v, axis=0, keepdims=True) for v in stats[2:]]

    n_causal = (qpos + 1).astype(F32)
    kf = jnp.minimum(qpos + 1, topk).astype(F32)

    def count_gt(thr):
        def pieces(k0, n_pieces, c):
            for j in range(n_pieces):
                rows = pl.ds(k0 + j * COUNT_ROWS, COUNT_ROWS)
                c = c + fold(jnp.where(sc_ref[rows, :] > thr, 1.0, 0.0))
            return c

        def body(kp, c):
            return pieces(pl.multiple_of(kp * (2 * tk), 2 * tk), 2 * tk // COUNT_ROWS, c)

        part = lax.fori_loop(0, npair - 1, body, jnp.zeros((FOLD_ROWS, tq), F32))
        k0 = pl.multiple_of((nkb - 2) * tk, tk)
        part = pieces(k0, tk // COUNT_ROWS, part)
        half = jnp.zeros((FOLD_ROWS, tk), F32)
        for j in range(tk // COUNT_ROWS):
            sv = sc_ref[pl.ds(k0 + tk + j * COUNT_ROWS, COUNT_ROWS), tq - tk:]
            half = half + jnp.sum(jnp.where(sv > thr[:, tq - tk:], 1.0, 0.0)
                                  .reshape(COUNT_ROWS // FOLD_ROWS, FOLD_ROWS, tk), axis=0)
        part = jnp.concatenate([part[:, :tq - tk], part[:, tq - tk:] + half], axis=1)
        return jnp.sum(part, axis=0, keepdims=True)

    below = smin - (1.0 + jnp.abs(smin))
    zero_tie = jnp.logical_and(c_pos < kf, kf <= c_nn)
    positive = c_pos >= kf
    searching = jnp.where(zero_tie, 0.0, 1.0)

    def bisect_cond(carry):
        return jnp.logical_and(carry[0] < BISECT_MAX, carry[1] > 0)

    def bisect_body(carry):
        it, _, lo, hi, c_lo, c_hi = carry
        for _ in range(BISECT_CHECK):
            mid = 0.5 * (lo + hi)
            c_mid = count_gt(mid)
            ge = c_mid >= kf
            up = jnp.logical_and(ge, searching > 0.0)
            down = jnp.logical_and(jnp.logical_not(ge), searching > 0.0)
            lo, c_lo = jnp.where(up, mid, lo), jnp.where(up, c_mid, c_lo)
            hi, c_hi = jnp.where(down, mid, hi), jnp.where(down, c_mid, c_hi)
        open_queries = (jnp.max((c_lo - kf) * searching) > 0.0).astype(jnp.int32)
        return it + BISECT_CHECK, open_queries, lo, hi, c_lo, c_hi

    lo0 = jnp.where(positive, 0.0, jnp.where(zero_tie, 0.0, below))
    hi0 = jnp.where(positive, smax, 0.0)
    c_lo0 = jnp.where(positive, c_pos, jnp.where(zero_tie, c_nn, n_causal))
    c_hi0 = jnp.where(positive, 0.0, c_pos)
    first_open = (jnp.max((c_lo0 - kf) * searching) > 0.0).astype(jnp.int32)
    _, _, lo, hi, c_lo, c_hi = lax.while_loop(
        bisect_cond, bisect_body, (jnp.int32(0), first_open, lo0, hi0, c_lo0, c_hi0))

    @pl.when(jnp.max(jnp.maximum(c_lo - kf, 1.0 - searching)) > 0.0)
    def _break_ties():
        need = kf - c_hi
        closed = 1.0 - searching
        prefix = jnp.where(lax.broadcasted_iota(jnp.int32, (tk, tk), 0)
                           >= lax.broadcasted_iota(jnp.int32, (tk, tk), 1), 1.0, 0.0).astype(BF16)

        def body(kp, seen):
            blocks = []
            for j in range(2):
                k0 = pl.multiple_of((2 * kp + j) * tk, tk)
                sv = sc_ref[pl.ds(k0, tk), :]
                above_lo = jnp.where(sv > lo, 1.0, jnp.where(sv >= lo, closed, 0.0))
                member = jnp.where(sv > hi, 0.0, above_lo)
                blocks.append((k0, member, jnp.dot(prefix, member.astype(BF16), preferred_element_type=F32)))
            for k0, member, within in blocks:
                rank = within + seen
                sv = sc_ref[pl.ds(k0, tk), :]
                sc_ref[pl.ds(k0, tk), :] = jnp.where(member > 0.0, jnp.where(rank > need, -jnp.inf, jnp.inf), sv)
                seen = rank[tk - 1:tk, :]
            return seen

        lax.fori_loop(0, npair, body, jnp.zeros((1, tq), F32))

    m_ref[...] = jnp.full(m_ref.shape, NEG_BIG, F32)
    acc_ref[...] = jnp.zeros(acc_ref.shape, F32)

    every = slice(0, tq)
    late = slice(tq - tk, tq)

    def selection_bias(kb, cols=every):
        return jnp.where(sc_ref[pl.ds(pl.multiple_of(kb * tk, tk), tk), cols] > lo[:, cols], 0.0, NEG_BIG)

    def score_head(kb, h, bias, park_ref, max_ref, cols=every):
        kh = k_ref[pl.ds(pl.multiple_of(kb * tk, tk), tk), h * HEAD_PAD:(h + 1) * HEAD_PAD]
        s = jnp.dot(kh, qt_ref[h * HEAD_PAD:(h + 1) * HEAD_PAD, cols], preferred_element_type=F32) + bias
        park_ref[h, :, cols] = s
        max_ref[h:h + 1, cols] = jnp.max(s, axis=0, keepdims=True)

    def softmax_head(kb, h, park_ref, max_ref, cols=every):
        m_old = m_ref[h:h + 1, cols]
        m_new = jnp.maximum(m_old, max_ref[h:h + 1, cols])
        alpha = jnp.exp2(m_old - m_new)
        p = jnp.exp2((park_ref[h, :, cols] - m_new).astype(BF16))
        rows = slice(h * V_ROWS, (h + 1) * V_ROWS)
        pv = jnp.dot(vt_ref[rows, pl.ds(pl.multiple_of(kb * tk, tk), tk)], p, preferred_element_type=F32)
        acc_ref[rows, cols] = alpha * acc_ref[rows, cols] + pv
        m_ref[h:h + 1, cols] = m_new

    bias0 = selection_bias(0)
    for h in range(H_A):
        score_head(0, h, bias0, s_ref, smax_ref)

    def att_body(kp, carry):
        odd, nxt = 2 * kp + 1, 2 * kp + 2
        bias = selection_bias(odd)
        for h in range(H_A):
            score_head(odd, h, bias, s2_ref, smax2_ref)
            softmax_head(2 * kp, h, s_ref, smax_ref)
        bias = selection_bias(nxt)
        for h in range(H_A):
            score_head(nxt, h, bias, s_ref, smax_ref)
            softmax_head(odd, h, s2_ref, smax2_ref)
        return carry

    lax.fori_loop(0, npair - 1, att_body, 0)
    bias = selection_bias(nkb - 1, late)
    for h in range(H_A):
        score_head(nkb - 1, h, bias, s2_ref, smax2_ref, late)
        softmax_head(nkb - 2, h, s_ref, smax_ref)
    for h in range(H_A):
        softmax_head(nkb - 1, h, s2_ref, smax2_ref, late)
    out = [acc_ref[h * V_ROWS:h * V_ROWS + D_V_A, :] / acc_ref[h * V_ROWS + D_V_A:h * V_ROWS + D_V_A + 1, :]
           for h in range(H_A)]
    o_ref[...] = jnp.concatenate(out, axis=0).T.astype(BF16)


def _dsa_call(qt, qit, wit, k, ki, vt):
    b, hq, s = qt.shape
    tq = TQ_DSA
    topk = min(TOPK_MAX, s // 4)
    colsT = lambda height: pl.BlockSpec((None, height, tq), lambda i, j: (i, 0, j))
    whole = lambda a: pl.BlockSpec((None,) + a.shape[1:], lambda i, j: (i, 0, 0))
    return pl.pallas_call(
        functools.partial(_dsa_kernel, topk=topk),
        out_shape=jax.ShapeDtypeStruct((b, s, WIDTH_A), BF16),
        grid=(b, s // tq),
        in_specs=[colsT(H_IDX * D_IDX), colsT(H_IDX), colsT(hq), whole(ki), whole(k), whole(vt)],
        out_specs=pl.BlockSpec((None, tq, WIDTH_A), lambda i, j: (i, j, 0)),
        scratch_shapes=[pltpu.VMEM((s, tq), F32),
                        pltpu.VMEM((H_A, tq), F32),
                        pltpu.VMEM((H_A * V_ROWS, tq), F32),
                        pltpu.VMEM((H_A, TK_DSA, tq), F32),
                        pltpu.VMEM((H_A, TK_DSA, tq), F32),
                        pltpu.VMEM((H_A, tq), F32),
                        pltpu.VMEM((H_A, tq), F32),
                        pltpu.VMEM((H_IDX * HEAD_PAD, tq), BF16)],
        compiler_params=_params(2),
        name="dsa",
    )(qit, wit, qt, ki, k, vt)


def _dil_kernel(q_ref, kc_ref, kp_ref, vc_ref, vp_ref, o_ref, lse_ref, s_ref):
    for res in range(q_ref.shape[0]):
        _dil_residue(q_ref.at[res], kc_ref.at[res], kp_ref.at[res], vc_ref.at[res], vp_ref.at[res],
                     o_ref.at[res], lse_ref.at[res], s_ref.at[res])


def _dil_residue(q_ref, kc_ref, kp_ref, vc_ref, vp_ref, o_ref, lse_ref, s_ref):
    tb = kp_ref.shape[0]
    n_sub = q_ref.shape[0] // tb
    first_step = pl.program_id(2) == 0
    k = jnp.concatenate([kp_ref[...], kc_ref[...]], axis=0)
    v = jnp.concatenate([vp_ref[...], vc_ref[...]], axis=0)
    r = lax.broadcasted_iota(jnp.int32, (tb, 2 * tb), 0)
    c = lax.broadcasted_iota(jnp.int32, (tb, 2 * tb), 1)
    behind = (tb + r - c).astype(jnp.uint32)
    bias = jnp.where(behind <= tb, 0.0, -jnp.inf)
    no_prev = jnp.where(first_step, r, tb).astype(jnp.uint32)
    bias0 = jnp.where(behind <= no_prev, 0.0, -jnp.inf)
    lane = lax.broadcasted_iota(jnp.int32, (1, WIDTH_B), 1)
    head_qk = (lane % 128) // (D_HEAD_B // 2)
    head_v = lane // D_HEAD_B
    nt = (((1,), (1,)), ((), ()))
    for j in range(n_sub):
        q = q_ref[j * tb:(j + 1) * tb, :]
        for h in range(H_B):
            qh = jnp.where(head_qk == h, q, jnp.zeros_like(q))
            s_ref[j * H_B + h] = (lax.dot_general(qh, k[j * tb:(j + 2) * tb], nt, preferred_element_type=F32)
                                  + (bias0 if j == 0 else bias))
    for j in range(n_sub):
        vj = v[j * tb:(j + 2) * tb]
        num = jnp.zeros((tb, WIDTH_B), F32)
        den = jnp.zeros((tb, WIDTH_B), F32)
        top = jnp.zeros((tb, WIDTH_B), F32)
        for h in range(H_B):
            s = s_ref[j * H_B + h]
            m = jnp.max(s, axis=1, keepdims=True)
            p = jnp.exp2(s - m).astype(BF16)
            pv = jnp.dot(p, jnp.where(head_v == h, vj, jnp.ones_like(vj)), preferred_element_type=F32)
            mine = head_v == h
            num = jnp.where(mine, pv, num)
            den = jnp.where(mine, pltpu.roll(pv, WIDTH_B // 2, axis=1), den)
            top = jnp.where(mine, m, top)
        o_ref[j * tb:(j + 1) * tb, :] = (num / den).astype(o_ref.dtype)
        lse_ref[j * tb:(j + 1) * tb, :] = top + jnp.log2(den)


def _dil_call(qkv):
    b, dilation, n, _ = qkv.shape
    tb, tq = TB_DIL, min(TQ_DIL, n)
    per = tq // tb
    n_res = max(1, min(dilation, TQ_DIL // tq))
    cur = lambda which: pl.BlockSpec((None, n_res, tq, WIDTH_B), lambda i, r, j: (i, r, j, which))
    prev = lambda which: pl.BlockSpec((None, n_res, tb, WIDTH_B),
                                      lambda i, r, j: (i, r, jnp.maximum(j * per - 1, 0), which))
    out_spec = pl.BlockSpec((None, n_res, tq, WIDTH_B), lambda i, r, j: (i, r, j, 0))
    out = lambda dtype: jax.ShapeDtypeStruct((b, dilation, n, WIDTH_B), dtype)
    return pl.pallas_call(
        _dil_kernel,
        out_shape=(out(BF16), out(F32)),
        grid=(b, dilation // n_res, n // tq),
        in_specs=[cur(0), cur(1), prev(1), cur(2), prev(2)],
        out_specs=(out_spec, out_spec),
        scratch_shapes=[pltpu.VMEM((n_res, per * H_B, tb, 2 * tb), F32)],
        compiler_params=_params(3),
        name="dil",
    )(qkv, qkv, qkv, qkv, qkv)


def _merge_ffn_kernel(x_ref, mod_ref, oa_ref, o0_ref, o1_ref, o2_ref, l0_ref, l1_ref, l2_ref, gates_ref,
                      wua_ref, wub_ref, wo_ref, gpost_ref, gpre2_ref, wg_ref, wu_ref, wd_ref, gpost2_ref,
                      out_ref, acc_ref, *scr):
    tm = x_ref.shape[0]
    tmp = scr[4]

    def token_major(ref, buf):
        dilation = ref.shape[0]
        if dilation == 1:
            return ref[0]
        n_slab = ref.shape[2] // LANES
        for c in range(n_slab):
            if dilation <= 4:
                for r in range(dilation):
                    buf[c, pl.ds(r, tm // dilation, stride=dilation), :] = (
                        ref[r, :, c * LANES:(c + 1) * LANES].astype(F32))
            else:
                quarter = tm // 4
                for r in range(dilation):
                    tmp[c, pl.ds((r % 4) * quarter + r // 4, tm // dilation, stride=dilation // 4), :] = (
                        ref[r, :, c * LANES:(c + 1) * LANES].astype(F32))
                for r4 in range(4):
                    buf[c, pl.ds(r4, quarter, stride=4), :] = tmp[c, r4 * quarter:(r4 + 1) * quarter, :]
        return jnp.concatenate([buf[c] for c in range(n_slab)], axis=1)

    o0, l0 = token_major(o0_ref, None).astype(F32), token_major(l0_ref, None)
    o1, l1 = token_major(o1_ref, scr[0]), token_major(l1_ref, scr[1])
    o2, l2 = token_major(o2_ref, scr[2]), token_major(l2_ref, scr[3])
    m = jnp.maximum(jnp.maximum(l0, l1), l2)
    e0, e1, e2 = jnp.exp2(l0 - m), jnp.exp2(l1 - m), jnp.exp2(l2 - m)
    ob = (e0 * o0 + e1 * o1 + e2 * o2) / (e0 + e1 + e2)
    za = jnp.dot(oa_ref[...], wua_ref[...], preferred_element_type=F32)
    zb = jnp.dot(ob.astype(BF16), wub_ref[...], preferred_element_type=F32)
    d = za.shape[1]
    z = gates_ref[:, 0:d].astype(F32) * za + gates_ref[:, d:2 * d].astype(F32) * zb
    y = jnp.dot(z.astype(BF16), wo_ref[...], preferred_element_type=F32)
    x_mixed = x_ref[...] + mod_ref[5:6, :] * (_rms_rows(y) * gpost_ref[...])
    _ffn_tile(x_mixed, mod_ref, gpre2_ref, wg_ref, wu_ref, wd_ref, gpost2_ref, out_ref, acc_ref, 6)


def _merge_ffn_call(x, mod, oa, dil, gates, w_up_a, w_up_b, w_o, g_post, g_pre2, w_gate, w_up, w_down, g_post2):
    b, s, d = x.shape
    tm = TM_FFN
    rows = lambda width: pl.BlockSpec((None, tm, width), lambda i, j: (i, j, 0))
    const = lambda a: pl.BlockSpec(a.shape, lambda i, j: (0,) * a.ndim, pipeline_mode=pl.Buffered(1))
    (o0, l0), (o1, l1), (o2, l2) = dil
    res = lambda a: pl.BlockSpec((None, a.shape[1], tm // a.shape[1], WIDTH_B), lambda i, j: (i, 0, j, 0))
    consts = [w_up_a, w_up_b, w_o, g_post, g_pre2, w_gate, w_up, w_down, g_post2]
    return pl.pallas_call(
        _merge_ffn_kernel,
        out_shape=jax.ShapeDtypeStruct((b, s, d), F32),
        grid=(b, s // tm),
        in_specs=[rows(d), pl.BlockSpec((None, N_MOD, d), lambda i, j: (i, 0, 0)), rows(WIDTH_A)]
                 + [res(a) for a in (o0, o1, o2, l0, l1, l2)]
                 + [rows(2 * d)] + [const(a) for a in consts],
        out_specs=rows(d),
        scratch_shapes=[pltpu.VMEM((tm, d), F32)] + [pltpu.VMEM((WIDTH_B // LANES, tm, LANES), F32)] * 5,
        compiler_params=_params(2),
        name="merge_ffn",
    )(x, mod, oa, o0, o1, o2, l0, l1, l2, gates, *consts)


def _mixer_weights(w_in, g_cq, g_ckv, w_uq, w_uk, w_uv, w_iq):
    d = w_in.shape[0]
    w_in = w_in.astype(BF16)
    o_cq, o_ckv, o_kr, o_ki, o_wi = 0, Q_LORA, Q_LORA + KV_LORA, Q_LORA + KV_LORA + D_ROPE_A, \
        Q_LORA + KV_LORA + D_ROPE_A + D_IDX
    o_qkv = o_wi + H_IDX
    o_gates = o_qkv + N_QKV_B

    def swap_halves(w):
        half = w.shape[1] // 2
        return jnp.concatenate([w[:, half:], w[:, :half]], axis=1)

    w_kr, w_ki = w_in[:, o_kr:o_kr + D_ROPE_A], w_in[:, o_ki:o_ki + D_IDX]
    qkv = w_in[:, o_qkv:o_gates].reshape(d, 3, N_GROUPS_B, H_B, 2, D_HEAD_B // 2)
    qk_split = qkv[:, 0:2].transpose(0, 1, 2, 4, 3, 5).reshape(d, 2 * N_GROUPS_B * WIDTH_B)
    w_std = jnp.concatenate([w_kr, swap_halves(w_kr), w_ki, swap_halves(w_ki), qk_split], axis=1)
    w_tail = w_in[:, o_qkv + 2 * N_GROUPS_B * WIDTH_B:]
    w_tr = jnp.concatenate([w_in[:, o_cq:o_cq + Q_LORA], w_in[:, o_ckv:o_ckv + KV_LORA],
                            w_in[:, o_wi:o_wi + H_IDX]], axis=1).T
    w_tr = jnp.pad(w_tr, ((0, N_TR - w_tr.shape[0]), (0, 0)))

    return {
        "w_std": w_std.astype(BF16),
        "w_tail": w_tail.astype(BF16),
        "w_tr": w_tr.astype(BF16),
        "g_cq_col": g_cq.reshape(Q_LORA, 1),
        "g_ckv_col": g_ckv.reshape(KV_LORA, 1),
        "w_uq_t": w_uq.transpose(1, 2, 0).reshape(H_A * (D_NOPE + D_ROPE_A), Q_LORA).astype(BF16),
        "w_iq_t": w_iq.transpose(1, 2, 0).reshape(H_IDX * D_IDX, Q_LORA).astype(BF16),
        "w_uk_p": jnp.pad(w_uk, ((0, 0), (0, 0), (0, HEAD_PAD - D_NOPE))).reshape(KV_LORA, H_A * HEAD_PAD).astype(BF16),
        "w_uv_t": w_uv.transpose(1, 2, 0).reshape(WIDTH_A, KV_LORA).astype(BF16),
    }


def kernel(x, c, positions, w_mod, b_mod, g_pre_ffn1, w_gate1, w_up1, w_down1, g_post_ffn1, g_pre_mix, w_in, g_cq, g_ckv, w_uq, w_uk, w_uv, w_iq, w_up_a, w_up_b, w_o, g_post_mix, g_pre_ffn2, w_gate2, w_up2, w_down2, g_post_ffn2):
    b, s, d = x.shape
    assert s % (16 * TB_DIL) == 0 and s % TM_FFN == 0 and b <= 8
    c8 = jnp.pad(c, ((0, 8 - b), (0, 0)))
    half_b, half_a = D_HEAD_B // 2, D_ROPE_A // 2
    inv = jnp.concatenate([ROPE_THETA ** (-jnp.arange(half_b, dtype=F32) / half_b),
                           ROPE_THETA ** (-jnp.arange(half_a, dtype=F32) / half_a)]).reshape(48, 1)
    row = lambda g: g.reshape(1, -1)
    for l in range(w_mod.shape[0]):
        if l == 0:
            mod, *tabs = _mod_rope_call(c8, w_mod[l], row(b_mod[l]), positions.reshape(b, 1, s), inv)
        else:
            mod = _mod_call(c8, w_mod[l], row(b_mod[l]))
        mod = mod[:b].reshape(b, N_MOD, d)
        x = _ffn_call(x, mod, row(g_pre_ffn1[l]), w_gate1[l].astype(BF16), w_up1[l].astype(BF16),
                      w_down1[l].astype(BF16), row(g_post_ffn1[l]), 0)
        w = _mixer_weights(w_in[l], g_cq[l], g_ckv[l], w_uq[l], w_uk[l], w_uv[l], w_iq[l])
        qt, qit, wit, k, ki, vt, dq0, dq1, dq2, gates = _proj_call(x, mod, row(g_pre_mix[l]), w, tabs)
        oa = _dsa_call(qt, qit, wit, k, ki, vt)
        dil = [_dil_call(dq) for dq in (dq0, dq1, dq2)]
        x = _merge_ffn_call(x, mod, oa, dil, gates, w_up_a[l].astype(BF16), w_up_b[l].astype(BF16),
                            w_o[l].astype(BF16), row(g_post_mix[l]), row(g_pre_ffn2[l]), w_gate2[l].astype(BF16),
                            w_up2[l].astype(BF16), w_down2[l].astype(BF16), row(g_post_ffn2[l]))
    return x
```

```python
import functools

import jax
import jax.numpy as jnp
from jax import lax
from jax.experimental import pallas as pl
from jax.experimental.pallas import tpu as pltpu

F32 = jnp.float32
BF16 = jnp.bfloat16

ROPE_THETA = 10000.0
NORM_EPS = 1e-6
H_A = 8
Q_LORA = 384
KV_LORA = 256
D_NOPE = 64
D_ROPE_A = 32
D_V_A = 64
H_IDX = 8
D_IDX = 32
TOPK_MAX = 256
DIL_PAIRS = ((128, 1), (512, 4), (2048, 16))
N_GROUPS_B = 3
H_B = 4
D_HEAD_B = 64
N_MOD = 9
WIDTH_A = H_A * D_V_A
WIDTH_B = H_B * D_HEAD_B
N_QKV_B = 3 * N_GROUPS_B * WIDTH_B

LANES = 128
HEAD_PAD = 128
VMEM_LIMIT = 56 * 1024 * 1024

TM_FFN = 512
CH_FFN = 256
TP_PROJ = 512
TQ_DSA = 512
TK_DSA = 256
V_ROWS = D_V_A + 16
TB_DIL = 128
TQ_DIL = 1024
BISECT_MAX = 64
BISECT_CHECK = 2
FOLD_ROWS = 16
COUNT_ROWS = 64
IDX_ROWS = 32
NEG_BIG = -1e30
LOG2_E = 1.4426950408889634

C_KR = 0
C_QKVB = 128
R_CQ = 0
R_CKV = Q_LORA
R_WI = Q_LORA + KV_LORA
N_TR = 656


def _params(n_axes):
    return pltpu.CompilerParams(dimension_semantics=("arbitrary",) * n_axes,
                                vmem_limit_bytes=VMEM_LIMIT)


def _rms_rows(x):
    return x * lax.rsqrt(jnp.mean(x * x, axis=-1, keepdims=True) + NORM_EPS)


def _mod_kernel(c_ref, w_ref, b_ref, o_ref):
    c = c_ref[...]
    o_ref[...] = jnp.dot(c * jax.nn.sigmoid(c), w_ref[...], preferred_element_type=F32) + b_ref[...]


def _mod_call(c8, w_mod, b_mod):
    d = c8.shape[1]
    return pl.pallas_call(
        _mod_kernel,
        out_shape=jax.ShapeDtypeStruct((8, N_MOD * d), F32),
        grid=(N_MOD,),
        in_specs=[pl.BlockSpec((8, d), lambda j: (0, 0)),
                  pl.BlockSpec((d, d), lambda j: (0, j)),
                  pl.BlockSpec((1, d), lambda j: (0, j))],
        out_specs=pl.BlockSpec((8, d), lambda j: (0, j)),
        compiler_params=_params(1),
        name="mod",
    )(c8, w_mod, b_mod)


def _rope_kernel(pos_ref, inv_ref, tt_ref, ca_ref, sa_ref, cb_ref, sb_ref):
    s = pos_ref.shape[1]
    ang = inv_ref[...] * pos_ref[...].astype(F32)
    cos = jnp.cos(ang)
    sin = jnp.sin(ang)
    cos_b, sin_b = cos[32:48], sin[32:48]
    tt_ref[0] = cos_b
    tt_ref[1] = sin_b
    cos_a = jnp.concatenate([cos[0:32]] * 4, axis=0)
    sin_a = jnp.concatenate([sin[0:32]] * 4, axis=0)
    cos_bt = jnp.concatenate([cos_b] * 8, axis=0)
    sin_bt = jnp.concatenate([-sin_b, sin_b] * 4, axis=0)
    for j in range(s // LANES):
        cols = slice(j * LANES, (j + 1) * LANES)
        ca_ref[cols, :] = cos_a[:, cols].T
        sa_ref[cols, :] = sin_a[:, cols].T
        cb_ref[cols, :] = cos_bt[:, cols].T
        sb_ref[cols, :] = sin_bt[:, cols].T


def _mod_rope_kernel(c_ref, w_ref, b_ref, pos_ref, inv_ref, o_ref, tt_ref, ca_ref, sa_ref, cb_ref, sb_ref, *,
                     n_batch):
    _mod_kernel(c_ref, w_ref, b_ref, o_ref)

    @pl.when(pl.program_id(0) < n_batch)
    def _tables():
        _rope_kernel(pos_ref, inv_ref, tt_ref, ca_ref, sa_ref, cb_ref, sb_ref)


def _mod_rope_call(c8, w_mod, b_mod, pos3, inv):
    d = c8.shape[1]
    b, _, s = pos3.shape
    assert b <= N_MOD
    row = lambda j: jnp.minimum(j, b - 1)
    tab = jax.ShapeDtypeStruct((b, s, LANES), F32)
    tab_spec = pl.BlockSpec((None, s, LANES), lambda j: (row(j), 0, 0))
    return pl.pallas_call(
        functools.partial(_mod_rope_kernel, n_batch=b),
        out_shape=(jax.ShapeDtypeStruct((8, N_MOD * d), F32),
                   jax.ShapeDtypeStruct((b, 2, 16, s), F32), tab, tab, tab, tab),
        grid=(N_MOD,),
        in_specs=[pl.BlockSpec((8, d), lambda j: (0, 0)),
                  pl.BlockSpec((d, d), lambda j: (0, j)),
                  pl.BlockSpec((1, d), lambda j: (0, j)),
                  pl.BlockSpec((None, 1, s), lambda j: (row(j), 0, 0)),
                  pl.BlockSpec((48, 1), lambda j: (0, 0))],
        out_specs=(pl.BlockSpec((8, d), lambda j: (0, j)),
                   pl.BlockSpec((None, 2, 16, s), lambda j: (row(j), 0, 0, 0)),
                   tab_spec, tab_spec, tab_spec, tab_spec),
        compiler_params=_params(1),
        name="mod_rope",
    )(c8, w_mod, b_mod, pos3, inv)


def _ffn_kernel(x_ref, mod_ref, gpre_ref, wg_ref, wu_ref, wd_ref, gpost_ref, o_ref, acc_ref, *, j0):
    _ffn_tile(x_ref[...], mod_ref, gpre_ref, wg_ref, wu_ref, wd_ref, gpost_ref, o_ref, acc_ref, j0)


def _ffn_tile(x, mod_ref, gpre_ref, wg_ref, wu_ref, wd_ref, gpost_ref, o_ref, acc_ref, j0):
    sh, sc, gt = mod_ref[j0:j0 + 1, :], mod_ref[j0 + 1:j0 + 2, :], mod_ref[j0 + 2:j0 + 3, :]
    h = (_rms_rows(x) * gpre_ref[...] * (1.0 + sc) + sh).astype(BF16)
    n_ch = wg_ref.shape[1] // CH_FFN
    for ch in range(n_ch):
        cols = slice(ch * CH_FFN, (ch + 1) * CH_FFN)
        g = jnp.dot(h, wg_ref[:, cols], preferred_element_type=F32)
        u = jnp.dot(h, wu_ref[:, cols], preferred_element_type=F32)
        a = (g * jax.nn.sigmoid(g) * u).astype(BF16)
        part = jnp.dot(a, wd_ref[cols, :], preferred_element_type=F32)
        if ch == 0:
            acc_ref[...] = part
        else:
            acc_ref[...] += part
    o_ref[...] = x + 0.5 * gt * (_rms_rows(acc_ref[...]) * gpost_ref[...])


def _ffn_call(x, mod, g_pre, w_gate, w_up, w_down, g_post, j0):
    b, s, d = x.shape
    f = w_gate.shape[1]
    const = lambda shape: pl.BlockSpec(shape, lambda i, j: (0,) * len(shape), pipeline_mode=pl.Buffered(1))
    tile = pl.BlockSpec((None, TM_FFN, d), lambda i, j: (i, j, 0))
    return pl.pallas_call(
        functools.partial(_ffn_kernel, j0=j0),
        out_shape=jax.ShapeDtypeStruct((b, s, d), F32),
        grid=(b, s // TM_FFN),
        in_specs=[tile,
                  pl.BlockSpec((None, N_MOD, d), lambda i, j: (i, 0, 0)),
                  const((1, d)), const((d, f)), const((d, f)), const((f, d)), const((1, d))],
        out_specs=tile,
        scratch_shapes=[pltpu.VMEM((TM_FFN, d), F32)],
        compiler_params=_params(2),
        name="ffn",
    )(x, mod, g_pre, w_gate, w_up, w_down, g_post)


def _proj_kernel(x_ref, mod_ref, gpre_ref, wstd_ref, wtail_ref, wtr_ref, gcq_ref, gckv_col_ref,
                 wuq_ref, wiq_ref, wuk_ref, wuv_ref,
                 tt_ref, ca_ref, sa_ref, cb_ref, sb_ref,
                 qt_ref, qit_ref, wit_ref, k_ref, ki_ref, vt_ref, dq0_ref, dq1_ref, dq2_ref, gates_ref,
                 dil_scr, dil_tmp):
    x = x_ref[...]
    sh, sc = mod_ref[3:4, :], mod_ref[4:5, :]
    u = (_rms_rows(x) * gpre_ref[...] * (1.0 + sc) + sh).astype(BF16)
    std = jnp.dot(u, wstd_ref[...], preferred_element_type=F32)
    tail = jnp.dot(u, wtail_ref[...], preferred_element_type=F32)
    tr = lax.dot_general(wtr_ref[...], u, (((1,), (1,)), ((), ())),
                         preferred_element_type=F32)

    cq = tr[R_CQ:R_CQ + Q_LORA]
    cq = cq * lax.rsqrt(jnp.mean(cq * cq, axis=0, keepdims=True) + NORM_EPS) * gcq_ref[...]
    cq = cq.astype(BF16)
    ckv_t = tr[R_CKV:R_CKV + KV_LORA]
    ckv_t = ckv_t * lax.rsqrt(jnp.mean(ckv_t * ckv_t, axis=0, keepdims=True) + NORM_EPS) * gckv_col_ref[...]
    cos_t, sin_t = tt_ref[0], tt_ref[1]

    attn_scale = (D_NOPE + D_ROPE_A) ** -0.5 * LOG2_E
    q_t = jnp.dot(wuq_ref[...], cq, preferred_element_type=F32) * attn_scale
    idx_scale = D_IDX ** -0.5
    qi_t = jnp.dot(wiq_ref[...], cq, preferred_element_type=F32) * idx_scale
    d_qk = D_NOPE + D_ROPE_A
    for h in range(H_A):
        r0, s0 = h * HEAD_PAD, h * d_qk
        x1, x2 = q_t[s0 + 64:s0 + 80], q_t[s0 + 80:s0 + 96]
        qt_ref[r0:r0 + 64, :] = q_t[s0:s0 + 64].astype(BF16)
        qt_ref[r0 + 64:r0 + 80, :] = (x1 * cos_t - x2 * sin_t).astype(BF16)
        qt_ref[r0 + 80:r0 + 96, :] = (x2 * cos_t + x1 * sin_t).astype(BF16)
        qt_ref[r0 + 96:r0 + 128, :] = jnp.zeros((HEAD_PAD - d_qk, q_t.shape[1]), BF16)
        i0 = h * D_IDX
        y1, y2 = qi_t[i0:i0 + 16], qi_t[i0 + 16:i0 + 32]
        qit_ref[i0:i0 + 16, :] = (y1 * cos_t - y2 * sin_t).astype(BF16)
        qit_ref[i0 + 16:i0 + 32, :] = (y2 * cos_t + y1 * sin_t).astype(BF16)
    wit_ref[...] = tr[R_WI:R_WI + H_IDX] * (H_IDX ** -0.5)
    v_t = jnp.dot(wuv_ref[...], ckv_t.astype(BF16), preferred_element_type=F32).astype(BF16)
    for h in range(H_A):
        vt_ref[h * V_ROWS:h * V_ROWS + D_V_A, :] = v_t[h * D_V_A:(h + 1) * D_V_A]
        vt_ref[h * V_ROWS + D_V_A:(h + 1) * V_ROWS, :] = jnp.ones((V_ROWS - D_V_A, v_t.shape[1]), BF16)

    ckv = ckv_t.T.astype(BF16)
    k_nope = jnp.dot(ckv, wuk_ref[...], preferred_element_type=F32)
    t = std[:, C_KR:C_KR + LANES]
    lane = lax.broadcasted_iota(jnp.int32, (1, LANES), 1)
    unswapped = jnp.where((lane // 32) % 2 == 0, 1.0, 0.0)
    roped = (t * cb_ref[...] + pltpu.roll(t, LANES - 32, axis=1) * sb_ref[...]) * unswapped
    moved = pltpu.roll(roped, 64, axis=1)
    k_rope_tile = jnp.where(lane >= 64, moved, 0.0)
    for h in range(H_A):
        k_ref[:, h * HEAD_PAD:(h + 1) * HEAD_PAD] = (k_nope[:, h * HEAD_PAD:(h + 1) * HEAD_PAD]
                                                      + k_rope_tile).astype(BF16)
    ki_ref[...] = jnp.where(lane < 32, moved, 0.0).astype(BF16)

    cos_a, sin_a = ca_ref[...], sa_ref[...]
    tp = x.shape[0]
    for g, (out_ref, (_, dilation)) in enumerate(zip((dq0_ref, dq1_ref, dq2_ref), DIL_PAIRS)):
        for which in range(2):
            scale = (D_HEAD_B ** -0.5 * LOG2_E) if which == 0 else 1.0
            c0 = C_QKVB + (which * N_GROUPS_B + g) * WIDTH_B
            x1, x2 = std[:, c0:c0 + 128], std[:, c0 + 128:c0 + 256]
            dil_scr[2 * which] = (x1 * cos_a - x2 * sin_a) * scale
            dil_scr[2 * which + 1] = (x2 * cos_a + x1 * sin_a) * scale
        dil_scr[4] = tail[:, g * WIDTH_B:g * WIDTH_B + 128]
        dil_scr[5] = tail[:, g * WIDTH_B + 128:(g + 1) * WIDTH_B]
        for c in range(3 * WIDTH_B // LANES):
            if dilation <= 4:
                for r in range(dilation):
                    out_ref[r, :, c * LANES:(c + 1) * LANES] = (
                        dil_scr[c, pl.ds(r, tp // dilation, stride=dilation), :].astype(BF16))
            else:
                quarter = tp // 4
                for r4 in range(4):
                    dil_tmp[c, r4 * quarter:(r4 + 1) * quarter, :] = dil_scr[c, pl.ds(r4, quarter, stride=4), :]
                for r in range(dilation):
                    out_ref[r, :, c * LANES:(c + 1) * LANES] = dil_tmp[
                        c, pl.ds((r % 4) * quarter + r // 4, tp // dilation, stride=dilation // 4), :].astype(BF16)
    gates_ref[...] = jax.nn.sigmoid(tail[:, N_GROUPS_B * WIDTH_B:]).astype(BF16)


def _proj_call(x, mod, g_pre, w, tabs):
    b, s, d = x.shape
    tt, ca, sa, cb, sb = tabs
    tp = TP_PROJ
    const = lambda a: pl.BlockSpec(a.shape, lambda i, j: (0,) * a.ndim)
    rows = lambda width: pl.BlockSpec((None, tp, width), lambda i, j: (i, j, 0))
    colsT = lambda height: pl.BlockSpec((None, height, tp), lambda i, j: (i, 0, j))
    consts = [g_pre, w["w_std"], w["w_tail"], w["w_tr"], w["g_cq_col"], w["g_ckv_col"],
              w["w_uq_t"], w["w_iq_t"], w["w_uk_p"], w["w_uv_t"]]
    hq = H_A * HEAD_PAD
    dil_shapes = [jax.ShapeDtypeStruct((b, dil, s // dil, 3 * WIDTH_B), BF16) for _, dil in DIL_PAIRS]
    dil_specs = [pl.BlockSpec((None, dil, tp // dil, 3 * WIDTH_B), lambda i, j: (i, 0, j, 0))
                 for _, dil in DIL_PAIRS]
    return pl.pallas_call(
        _proj_kernel,
        out_shape=(jax.ShapeDtypeStruct((b, hq, s), BF16),
                   jax.ShapeDtypeStruct((b, H_IDX * D_IDX, s), BF16),
                   jax.ShapeDtypeStruct((b, H_IDX, s), F32),
                   jax.ShapeDtypeStruct((b, s, hq), BF16),
                   jax.ShapeDtypeStruct((b, s, LANES), BF16),
                   jax.ShapeDtypeStruct((b, H_A * V_ROWS, s), BF16),
                   *dil_shapes,
                   jax.ShapeDtypeStruct((b, s, 2048), BF16)),
        grid=(b, s // tp),
        in_specs=[rows(d), pl.BlockSpec((None, N_MOD, d), lambda i, j: (i, 0, 0))]
                 + [const(a) for a in consts]
                 + [pl.BlockSpec((None, 2, 16, tp), lambda i, j: (i, 0, 0, j)),
                    rows(LANES), rows(LANES), rows(LANES), rows(LANES)],
        out_specs=(colsT(hq), colsT(H_IDX * D_IDX), colsT(H_IDX), rows(hq), rows(LANES), colsT(H_A * V_ROWS),
                   *dil_specs, rows(2048)),
        scratch_shapes=[pltpu.VMEM((3 * WIDTH_B // LANES, tp, LANES), F32)] * 2,
        compiler_params=_params(2),
        name="proj",
    )(x, mod, *consts, tt, ca, sa, cb, sb)


def _dsa_kernel(qit_ref, wit_ref, qt_ref, ki_ref, k_ref, vt_ref, o_ref,
                sc_ref, m_ref, acc_ref, s_ref, s2_ref, smax_ref, smax2_ref, qpad_ref, *, topk):
    tq = qt_ref.shape[1]
    tk = s_ref.shape[1]
    i = pl.program_id(1)
    assert tq == 2 * tk
    nkb = 2 * (i + 1)
    npair = i + 1
    qpos = i * tq + lax.broadcasted_iota(jnp.int32, (1, tq), 1)

    def fold(v):
        return jnp.sum(v.reshape(v.shape[0] // FOLD_ROWS, FOLD_ROWS, tq), axis=0)

    qpad_ref[...] = jnp.zeros(qpad_ref.shape, BF16)
    for h in range(H_IDX):
        qpad_ref[h * HEAD_PAD:h * HEAD_PAD + D_IDX, :] = qit_ref[h * D_IDX:(h + 1) * D_IDX, :]

    def park_logits(kb, park_ref, heads=range(H_IDX)):
        ki = ki_ref[pl.ds(pl.multiple_of(kb * tk, tk), tk), :]
        for h in heads:
            park_ref[h] = jnp.dot(ki, qpad_ref[h * HEAD_PAD:(h + 1) * HEAD_PAD, :], preferred_element_type=F32)

    def rows8(v, op):
        return op(v.reshape(IDX_ROWS // 8, 8, LANES), axis=0)

    piece_iota = lax.broadcasted_iota(jnp.int32, (IDX_ROWS, LANES), 0)
    n_col = tq // LANES
    heads_per_col = H_IDX // n_col

    def reduce_block(kb, park_ref, stats, on_diagonal, ahead=None, first_col=0):
        k0 = pl.multiple_of(kb * tk, tk)
        columns = []
        for c in range(n_col):
            if ahead is not None:
                park_logits(ahead[0], ahead[1], range(c * heads_per_col, (c + 1) * heads_per_col))
            lanes = slice(c * LANES, (c + 1) * LANES)
            if c < first_col:
                sc_ref[pl.ds(k0, tk), lanes] = jnp.full((tk, LANES), -jnp.inf, F32)
                columns.append(tuple(v[:, lanes] for v in stats))
                continue
            smax, smin, c_pos, c_nn = [v[:, lanes] for v in stats]
            w = [wit_ref[h:h + 1, lanes] for h in range(H_IDX)]
            for j in range(tk // IDX_ROWS):
                r0 = j * IDX_ROWS
                acc = jnp.zeros((IDX_ROWS, LANES), F32)
                for h in range(H_IDX):
                    acc = acc + w[h] * jnp.maximum(park_ref[h, r0:r0 + IDX_ROWS, lanes], 0.0)
                if on_diagonal:
                    causal = (k0 + r0 + piece_iota) <= qpos[:, lanes]
                    sv = jnp.where(causal, acc, -jnp.inf)
                    smin = jnp.minimum(smin, rows8(jnp.where(causal, acc, jnp.inf), jnp.min))
                else:
                    sv = acc
                    smin = jnp.minimum(smin, rows8(acc, jnp.min))
                sc_ref[pl.ds(k0 + r0, IDX_ROWS), lanes] = sv
                smax = jnp.maximum(smax, rows8(sv, jnp.max))
                c_pos = c_pos + rows8(jnp.where(sv > 0.0, 1.0, 0.0), jnp.sum)
                c_nn = c_nn + rows8(jnp.where(sv >= 0.0, 1.0, 0.0), jnp.sum)
            columns.append((smax, smin, c_pos, c_nn))
        return tuple(jnp.concatenate(parts, axis=1) for parts in zip(*columns))

    def idx_pair(kp, stats):
        stats = reduce_block(2 * kp, s_ref, stats, False, ahead=(2 * kp + 1, s2_ref))
        return reduce_block(2 * kp + 1, s2_ref, stats, False, ahead=(2 * kp + 2, s_ref))

    park_logits(0, s_ref)
    zero8 = jnp.zeros((8, tq), F32)
    stats = lax.fori_loop(0, i, idx_pair,
                          (jnp.full((8, tq), -jnp.inf, F32), jnp.full((8, tq), jnp.inf, F32), zero8, zero8))
    stats = reduce_block(nkb - 2, s_ref, stats, True, ahead=(nkb - 1, s2_ref))
    stats = reduce_block(nkb - 1, s2_ref, stats, True, first_col=(tq - tk) // LANES)
    smax = jnp.max(stats[0], axis=0, keepdims=True)
    smin = jnp.min(stats[1], axis=0, keepdims=True)
    c_pos, c_nn = [jnp.sum(v, axis=0, keepdims=True) for v in stats[2:]]

    n_causal = (qpos + 1).astype(F32)
    kf = jnp.minimum(qpos + 1, topk).astype(F32)

    def count_gt(thr):
        def pieces(k0, n_pieces, c):
            for j in range(n_pieces):
                rows = pl.ds(k0 + j * COUNT_ROWS, COUNT_ROWS)
                c = c + fold(jnp.where(sc_ref[rows, :] > thr, 1.0, 0.0))
            return c

        def body(kp, c):
            return pieces(pl.multiple_of(kp * (2 * tk), 2 * tk), 2 * tk // COUNT_ROWS, c)

        part = lax.fori_loop(0, npair - 1, body, jnp.zeros((FOLD_ROWS, tq), F32))
        k0 = pl.multiple_of((nkb - 2) * tk, tk)
        part = pieces(k0, tk // COUNT_ROWS, part)
        half = jnp.zeros((FOLD_ROWS, tk), F32)
        for j in range(tk // COUNT_ROWS):
            sv = sc_ref[pl.ds(k0 + tk + j * COUNT_ROWS, COUNT_ROWS), tq - tk:]
            half = half + jnp.sum(jnp.where(sv > thr[:, tq - tk:], 1.0, 0.0)
                                  .reshape(COUNT_ROWS // FOLD_ROWS, FOLD_ROWS, tk), axis=0)
        part = jnp.concatenate([part[:, :tq - tk], part[:, tq - tk:] + half], axis=1)
        return jnp.sum(part, axis=0, keepdims=True)

    below = smin - (1.0 + jnp.abs(smin))
    zero_tie = jnp.logical_and(c_pos < kf, kf <= c_nn)
    positive = c_pos >= kf
    searching = jnp.where(zero_tie, 0.0, 1.0)

    def bisect_cond(carry):
        return jnp.logical_and(carry[0] < BISECT_MAX, carry[1] > 0)

    def bisect_body(carry):
        it, _, lo, hi, c_lo, c_hi = carry
        for _ in range(BISECT_CHECK):
            mid = 0.5 * (lo + hi)
            c_mid = count_gt(mid)
            ge = c_mid >= kf
            up = jnp.logical_and(ge, searching > 0.0)
            down = jnp.logical_and(jnp.logical_not(ge), searching > 0.0)
            lo, c_lo = jnp.where(up, mid, lo), jnp.where(up, c_mid, c_lo)
            hi, c_hi = jnp.where(down, mid, hi), jnp.where(down, c_mid, c_hi)
        open_queries = (jnp.max((c_lo - kf) * searching) > 0.0).astype(jnp.int32)
        return it + BISECT_CHECK, open_queries, lo, hi, c_lo, c_hi

    lo0 = jnp.where(positive, 0.0, jnp.where(zero_tie, 0.0, below))
    hi0 = jnp.where(positive, smax, 0.0)
    c_lo0 = jnp.where(positive, c_pos, jnp.where(zero_tie, c_nn, n_causal))
    c_hi0 = jnp.where(positive, 0.0, c_pos)
    first_open = (jnp.max((c_lo0 - kf) * searching) > 0.0).astype(jnp.int32)
    _, _, lo, hi, c_lo, c_hi = lax.while_loop(
        bisect_cond, bisect_body, (jnp.int32(0), first_open, lo0, hi0, c_lo0, c_hi0))

    @pl.when(jnp.max(jnp.maximum(c_lo - kf, 1.0 - searching)) > 0.0)
    def _break_ties():
        need = kf - c_hi
        closed = 1.0 - searching
        prefix = jnp.where(lax.broadcasted_iota(jnp.int32, (tk, tk), 0)
                           >= lax.broadcasted_iota(jnp.int32, (tk, tk), 1), 1.0, 0.0).astype(BF16)

        def body(kp, seen):
            blocks = []
            for j in range(2):
                k0 = pl.multiple_of((2 * kp + j) * tk, tk)
                sv = sc_ref[pl.ds(k0, tk), :]
                above_lo = jnp.where(sv > lo, 1.0, jnp.where(sv >= lo, closed, 0.0))
                member = jnp.where(sv > hi, 0.0, above_lo)
                blocks.append((k0, member, jnp.dot(prefix, member.astype(BF16), preferred_element_type=F32)))
            for k0, member, within in blocks:
                rank = within + seen
                sv = sc_ref[pl.ds(k0, tk), :]
                sc_ref[pl.ds(k0, tk), :] = jnp.where(member > 0.0, jnp.where(rank > need, -jnp.inf, jnp.inf), sv)
                seen = rank[tk - 1:tk, :]
            return seen

        lax.fori_loop(0, npair, body, jnp.zeros((1, tq), F32))

    m_ref[...] = jnp.full(m_ref.shape, NEG_BIG, F32)
    acc_ref[...] = jnp.zeros(acc_ref.shape, F32)

    every = slice(0, tq)
    late = slice(tq - tk, tq)

    def selection_bias(kb, cols=every):
        return jnp.where(sc_ref[pl.ds(pl.multiple_of(kb * tk, tk), tk), cols] > lo[:, cols], 0.0, NEG_BIG)

    def score_head(kb, h, bias, park_ref, max_ref, cols=every):
        kh = k_ref[pl.ds(pl.multiple_of(kb * tk, tk), tk), h * HEAD_PAD:(h + 1) * HEAD_PAD]
        s = jnp.dot(kh, qt_ref[h * HEAD_PAD:(h + 1) * HEAD_PAD, cols], preferred_element_type=F32) + bias
        park_ref[h, :, cols] = s
        max_ref[h:h + 1, cols] = jnp.max(s, axis=0, keepdims=True)

    def softmax_head(kb, h, park_ref, max_ref, cols=every):
        m_old = m_ref[h:h + 1, cols]
        m_new = jnp.maximum(m_old, max_ref[h:h + 1, cols])
        alpha = jnp.exp2(m_old - m_new)
        p = jnp.exp2((park_ref[h, :, cols] - m_new).astype(BF16))
        rows = slice(h * V_ROWS, (h + 1) * V_ROWS)
        pv = jnp.dot(vt_ref[rows, pl.ds(pl.multiple_of(kb * tk, tk), tk)], p, preferred_element_type=F32)
        acc_ref[rows, cols] = alpha * acc_ref[rows, cols] + pv
        m_ref[h:h + 1, cols] = m_new

    bias0 = selection_bias(0)
    for h in range(H_A):
        score_head(0, h, bias0, s_ref, smax_ref)

    def att_body(kp, carry):
        odd, nxt = 2 * kp + 1, 2 * kp + 2
        bias = selection_bias(odd)
        for h in range(H_A):
            score_head(odd, h, bias, s2_ref, smax2_ref)
            softmax_head(2 * kp, h, s_ref, smax_ref)
        bias = selection_bias(nxt)
        for h in range(H_A):
            score_head(nxt, h, bias, s_ref, smax_ref)
            softmax_head(odd, h, s2_ref, smax2_ref)
        return carry

    lax.fori_loop(0, npair - 1, att_body, 0)
    bias = selection_bias(nkb - 1, late)
    for h in range(H_A):
        score_head(nkb - 1, h, bias, s2_ref, smax2_ref, late)
        softmax_head(nkb - 2, h, s_ref, smax_ref)
    for h in range(H_A):
        softmax_head(nkb - 1, h, s2_ref, smax2_ref, late)
    out = [acc_ref[h * V_ROWS:h * V_ROWS + D_V_A, :] / acc_ref[h * V_ROWS + D_V_A:h * V_ROWS + D_V_A + 1, :]
           for h in range(H_A)]
    o_ref[...] = jnp.concatenate(out, axis=0).T.astype(BF16)


def _dsa_call(qt, qit, wit, k, ki, vt):
    b, hq, s = qt.shape
    tq = TQ_DSA
    topk = min(TOPK_MAX, s // 4)
    colsT = lambda height: pl.BlockSpec((None, height, tq), lambda i, j: (i, 0, j))
    whole = lambda a: pl.BlockSpec((None,) + a.shape[1:], lambda i, j: (i, 0, 0),
                                   pipeline_mode=pl.Buffered(1))
    return pl.pallas_call(
        functools.partial(_dsa_kernel, topk=topk),
        out_shape=jax.ShapeDtypeStruct((b, s, WIDTH_A), BF16),
        grid=(b, s // tq),
        in_specs=[colsT(H_IDX * D_IDX), colsT(H_IDX), colsT(hq), whole(ki), whole(k), whole(vt)],
        out_specs=pl.BlockSpec((None, tq, WIDTH_A), lambda i, j: (i, j, 0)),
        scratch_shapes=[pltpu.VMEM((s, tq), F32),
                        pltpu.VMEM((H_A, tq), F32),
                        pltpu.VMEM((H_A * V_ROWS, tq), F32),
                        pltpu.VMEM((H_A, TK_DSA, tq), F32),
                        pltpu.VMEM((H_A, TK_DSA, tq), F32),
                        pltpu.VMEM((H_A, tq), F32),
                        pltpu.VMEM((H_A, tq), F32),
                        pltpu.VMEM((H_IDX * HEAD_PAD, tq), BF16)],
        compiler_params=_params(2),
        name="dsa",
    )(qit, wit, qt, ki, k, vt)


def _dil_kernel(q_ref, kc_ref, kp_ref, vc_ref, vp_ref, o_ref, lse_ref, s_ref):
    for res in range(q_ref.shape[0]):
        _dil_residue(q_ref.at[res], kc_ref.at[res], kp_ref.at[res], vc_ref.at[res], vp_ref.at[res],
                     o_ref.at[res], lse_ref.at[res], s_ref.at[res])


def _dil_residue(q_ref, kc_ref, kp_ref, vc_ref, vp_ref, o_ref, lse_ref, s_ref):
    tb = kp_ref.shape[0]
    n_sub = q_ref.shape[0] // tb
    first_step = pl.program_id(2) == 0
    k = jnp.concatenate([kp_ref[...], kc_ref[...]], axis=0)
    v = jnp.concatenate([vp_ref[...], vc_ref[...]], axis=0)
    r = lax.broadcasted_iota(jnp.int32, (tb, 2 * tb), 0)
    c = lax.broadcasted_iota(jnp.int32, (tb, 2 * tb), 1)
    behind = (tb + r - c).astype(jnp.uint32)
    bias = jnp.where(behind <= tb, 0.0, -jnp.inf)
    no_prev = jnp.where(first_step, r, tb).astype(jnp.uint32)
    bias0 = jnp.where(behind <= no_prev, 0.0, -jnp.inf)
    lane = lax.broadcasted_iota(jnp.int32, (1, WIDTH_B), 1)
    head_qk = (lane % 128) // (D_HEAD_B // 2)
    head_v = lane // D_HEAD_B
    nt = (((1,), (1,)), ((), ()))
    for j in range(n_sub):
        q = q_ref[j * tb:(j + 1) * tb, :]
        for h in range(H_B):
            qh = jnp.where(head_qk == h, q, jnp.zeros_like(q))
            s_ref[j * H_B + h] = (lax.dot_general(qh, k[j * tb:(j + 2) * tb], nt, preferred_element_type=F32)
                                  + (bias0 if j == 0 else bias))
    for j in range(n_sub):
        vj = v[j * tb:(j + 2) * tb]
        num = jnp.zeros((tb, WIDTH_B), F32)
        den = jnp.zeros((tb, WIDTH_B), F32)
        top = jnp.zeros((tb, WIDTH_B), F32)
        for h in range(H_B):
            s = s_ref[j * H_B + h]
            m = jnp.max(s, axis=1, keepdims=True)
            p = jnp.exp2(s - m).astype(BF16)
            pv = jnp.dot(p, jnp.where(head_v == h, vj, jnp.ones_like(vj)), preferred_element_type=F32)
            mine = head_v == h
            num = jnp.where(mine, pv, num)
            den = jnp.where(mine, pltpu.roll(pv, WIDTH_B // 2, axis=1), den)
            top = jnp.where(mine, m, top)
        o_ref[j * tb:(j + 1) * tb, :] = (num / den).astype(o_ref.dtype)
        lse_ref[j * tb:(j + 1) * tb, :] = top + jnp.log2(den)


def _dil_call(qkv):
    b, dilation, n, _ = qkv.shape
    tb, tq = TB_DIL, min(TQ_DIL, n)
    per = tq // tb
    n_res = max(1, min(dilation, TQ_DIL // tq))
    cur = lambda which: pl.BlockSpec((None, n_res, tq, WIDTH_B), lambda i, r, j: (i, r, j, which))
    prev = lambda which: pl.BlockSpec((None, n_res, tb, WIDTH_B),
                                      lambda i, r, j: (i, r, jnp.maximum(j * per - 1, 0), which))
    out_spec = pl.BlockSpec((None, n_res, tq, WIDTH_B), lambda i, r, j: (i, r, j, 0))
    out = lambda dtype: jax.ShapeDtypeStruct((b, dilation, n, WIDTH_B), dtype)
    return pl.pallas_call(
        _dil_kernel,
        out_shape=(out(BF16), out(F32)),
        grid=(b, dilation // n_res, n // tq),
        in_specs=[cur(0), cur(1), prev(1), cur(2), prev(2)],
        out_specs=(out_spec, out_spec),
        scratch_shapes=[pltpu.VMEM((n_res, per * H_B, tb, 2 * tb), F32)],
        compiler_params=_params(3),
        name="dil",
    )(qkv, qkv, qkv, qkv, qkv)


def _merge_ffn_kernel(x_ref, mod_ref, oa_ref, o0_ref, o1_ref, o2_ref, l0_ref, l1_ref, l2_ref, gates_ref,
                      wua_ref, wub_ref, wo_ref, gpost_ref, gpre2_ref, wg_ref, wu_ref, wd_ref, gpost2_ref,
                      out_ref, acc_ref, *scr):
    tm = x_ref.shape[0]
    tmp = scr[4]

    def token_major(ref, buf):
        dilation = ref.shape[0]
        if dilation == 1:
            return ref[0]
        n_slab = ref.shape[2] // LANES
        for c in range(n_slab):
            if dilation <= 4:
                for r in range(dilation):
                    buf[c, pl.ds(r, tm // dilation, stride=dilation), :] = (
                        ref[r, :, c * LANES:(c + 1) * LANES].astype(F32))
            else:
                quarter = tm // 4
                for r in range(dilation):
                    tmp[c, pl.ds((r % 4) * quarter + r // 4, tm // dilation, stride=dilation // 4), :] = (
                        ref[r, :, c * LANES:(c + 1) * LANES].astype(F32))
                for r4 in range(4):
                    buf[c, pl.ds(r4, quarter, stride=4), :] = tmp[c, r4 * quarter:(r4 + 1) * quarter, :]
        return jnp.concatenate([buf[c] for c in range(n_slab)], axis=1)

    o0, l0 = token_major(o0_ref, None).astype(F32), token_major(l0_ref, None)
    o1, l1 = token_major(o1_ref, scr[0]), token_major(l1_ref, scr[1])
    o2, l2 = token_major(o2_ref, scr[2]), token_major(l2_ref, scr[3])
    m = jnp.maximum(jnp.maximum(l0, l1), l2)
    e0, e1, e2 = jnp.exp2(l0 - m), jnp.exp2(l1 - m), jnp.exp2(l2 - m)
    ob = (e0 * o0 + e1 * o1 + e2 * o2) / (e0 + e1 + e2)
    za = jnp.dot(oa_ref[...], wua_ref[...], preferred_element_type=F32)
    zb = jnp.dot(ob.astype(BF16), wub_ref[...], preferred_element_type=F32)
    d = za.shape[1]
    z = gates_ref[:, 0:d].astype(F32) * za + gates_ref[:, d:2 * d].astype(F32) * zb
    y = jnp.dot(z.astype(BF16), wo_ref[...], preferred_element_type=F32)
    x_mixed = x_ref[...] + mod_ref[5:6, :] * (_rms_rows(y) * gpost_ref[...])
    _ffn_tile(x_mixed, mod_ref, gpre2_ref, wg_ref, wu_ref, wd_ref, gpost2_ref, out_ref, acc_ref, 6)


def _merge_ffn_call(x, mod, oa, dil, gates, w_up_a, w_up_b, w_o, g_post, g_pre2, w_gate, w_up, w_down, g_post2):
    b, s, d = x.shape
    tm = TM_FFN
    rows = lambda width: pl.BlockSpec((None, tm, width), lambda i, j: (i, j, 0))
    const = lambda a: pl.BlockSpec(a.shape, lambda i, j: (0,) * a.ndim, pipeline_mode=pl.Buffered(1))
    (o0, l0), (o1, l1), (o2, l2) = dil
    res = lambda a: pl.BlockSpec((None, a.shape[1], tm // a.shape[1], WIDTH_B), lambda i, j: (i, 0, j, 0))
    consts = [w_up_a, w_up_b, w_o, g_post, g_pre2, w_gate, w_up, w_down, g_post2]
    return pl.pallas_call(
        _merge_ffn_kernel,
        out_shape=jax.ShapeDtypeStruct((b, s, d), F32),
        grid=(b, s // tm),
        in_specs=[rows(d), pl.BlockSpec((None, N_MOD, d), lambda i, j: (i, 0, 0)), rows(WIDTH_A)]
                 + [res(a) for a in (o0, o1, o2, l0, l1, l2)]
                 + [rows(2 * d)] + [const(a) for a in consts],
        out_specs=rows(d),
        scratch_shapes=[pltpu.VMEM((tm, d), F32)] + [pltpu.VMEM((WIDTH_B // LANES, tm, LANES), F32)] * 5,
        compiler_params=_params(2),
        name="merge_ffn",
    )(x, mod, oa, o0, o1, o2, l0, l1, l2, gates, *consts)


def _mixer_weights(w_in, g_cq, g_ckv, w_uq, w_uk, w_uv, w_iq):
    d = w_in.shape[0]
    w_in = w_in.astype(BF16)
    o_cq, o_ckv, o_kr, o_ki, o_wi = 0, Q_LORA, Q_LORA + KV_LORA, Q_LORA + KV_LORA + D_ROPE_A, \
        Q_LORA + KV_LORA + D_ROPE_A + D_IDX
    o_qkv = o_wi + H_IDX
    o_gates = o_qkv + N_QKV_B

    def swap_halves(w):
        half = w.shape[1] // 2
        return jnp.concatenate([w[:, half:], w[:, :half]], axis=1)

    w_kr, w_ki = w_in[:, o_kr:o_kr + D_ROPE_A], w_in[:, o_ki:o_ki + D_IDX]
    qkv = w_in[:, o_qkv:o_gates].reshape(d, 3, N_GROUPS_B, H_B, 2, D_HEAD_B // 2)
    qk_split = qkv[:, 0:2].transpose(0, 1, 2, 4, 3, 5).reshape(d, 2 * N_GROUPS_B * WIDTH_B)
    w_std = jnp.concatenate([w_kr, swap_halves(w_kr), w_ki, swap_halves(w_ki), qk_split], axis=1)
    w_tail = w_in[:, o_qkv + 2 * N_GROUPS_B * WIDTH_B:]
    w_tr = jnp.concatenate([w_in[:, o_cq:o_cq + Q_LORA], w_in[:, o_ckv:o_ckv + KV_LORA],
                            w_in[:, o_wi:o_wi + H_IDX]], axis=1).T
    w_tr = jnp.pad(w_tr, ((0, N_TR - w_tr.shape[0]), (0, 0)))

    return {
        "w_std": w_std.astype(BF16),
        "w_tail": w_tail.astype(BF16),
        "w_tr": w_tr.astype(BF16),
        "g_cq_col": g_cq.reshape(Q_LORA, 1),
        "g_ckv_col": g_ckv.reshape(KV_LORA, 1),
        "w_uq_t": w_uq.transpose(1, 2, 0).reshape(H_A * (D_NOPE + D_ROPE_A), Q_LORA).astype(BF16),
        "w_iq_t": w_iq.transpose(1, 2, 0).reshape(H_IDX * D_IDX, Q_LORA).astype(BF16),
        "w_uk_p": jnp.pad(w_uk, ((0, 0), (0, 0), (0, HEAD_PAD - D_NOPE))).reshape(KV_LORA, H_A * HEAD_PAD).astype(BF16),
        "w_uv_t": w_uv.transpose(1, 2, 0).reshape(WIDTH_A, KV_LORA).astype(BF16),
    }


def kernel(x, c, positions, w_mod, b_mod, g_pre_ffn1, w_gate1, w_up1, w_down1, g_post_ffn1, g_pre_mix, w_in, g_cq, g_ckv, w_uq, w_uk, w_uv, w_iq, w_up_a, w_up_b, w_o, g_post_mix, g_pre_ffn2, w_gate2, w_up2, w_down2, g_post_ffn2):
    b, s, d = x.shape
    assert s % (16 * TB_DIL) == 0 and s % TM_FFN == 0 and b <= 8
    c8 = jnp.pad(c, ((0, 8 - b), (0, 0)))
    half_b, half_a = D_HEAD_B // 2, D_ROPE_A // 2
    inv = jnp.concatenate([ROPE_THETA ** (-jnp.arange(half_b, dtype=F32) / half_b),
                           ROPE_THETA ** (-jnp.arange(half_a, dtype=F32) / half_a)]).reshape(48, 1)
    row = lambda g: g.reshape(1, -1)
    for l in range(w_mod.shape[0]):
        if l == 0:
            mod, *tabs = _mod_rope_call(c8, w_mod[l], row(b_mod[l]), positions.reshape(b, 1, s), inv)
        else:
            mod = _mod_call(c8, w_mod[l], row(b_mod[l]))
        mod = mod[:b].reshape(b, N_MOD, d)
        x = _ffn_call(x, mod, row(g_pre_ffn1[l]), w_gate1[l].astype(BF16), w_up1[l].astype(BF16),
                      w_down1[l].astype(BF16), row(g_post_ffn1[l]), 0)
        w = _mixer_weights(w_in[l], g_cq[l], g_ckv[l], w_uq[l], w_uk[l], w_uv[l], w_iq[l])
        qt, qit, wit, k, ki, vt, dq0, dq1, dq2, gates = _proj_call(x, mod, row(g_pre_mix[l]), w, tabs)
        oa = _dsa_call(qt, qit, wit, k, ki, vt)
        dil = [_dil_call(dq) for dq in (dq0, dq1, dq2)]
        x = _merge_ffn_call(x, mod, oa, dil, gates, w_up_a[l].astype(BF16), w_up_b[l].astype(BF16),
                            w_o[l].astype(BF16), row(g_post_mix[l]), row(g_pre_ffn2[l]), w_gate2[l].astype(BF16),
                            w_up2[l].astype(BF16), w_down2[l].astype(BF16), row(g_post_ffn2[l]))
    return x
```
